```python
import jax
import jax.numpy as jnp
from jax import lax
import numpy as np

D_MODEL = 1024
BATCH = 32
SEQ = 256
DEPTH = 2
DEC_BATCH = 2
DEC_SEQ = 2048
PAST_LEN = 512

GRID_W = 64
Q_BLOCK = 128
ROPE_BASE = 10000.0
NORM_EPS = 1e-6
MASK_VALUE = -1e30
LOG_FLOOR = 1e-30
N_MOD = 6
N_BRANCH = 4
BRANCH_W = 512
D_FF = 4 * D_MODEL

HGRN_HEADS = 4
HGRN_DK = 128
HGRN_DV = 128
HGRN_CHUNK = 32

SWA_HEADS = 8
SWA_KV_HEADS = 2
SWA_HEAD_DIM = 64
SWA_WINDOW = 128

MLA_HEADS = 8
MLA_Q_RANK = 256
MLA_KV_RANK = 128
MLA_NOPE = 64
MLA_ROPE = 32
MLA_V = 64

GQA_HEADS = 8
GQA_KV_HEADS = 2
GQA_HEAD_DIM = 64

IN_SPLITS = (
    ('hgrn_q', HGRN_HEADS * HGRN_DK),
    ('hgrn_f_fwd', HGRN_HEADS * HGRN_DK),
    ('hgrn_f_bwd', HGRN_HEADS * HGRN_DK),
    ('hgrn_i', HGRN_HEADS * HGRN_DV),
    ('hgrn_g', HGRN_HEADS * HGRN_DV),
    ('swa_q', SWA_HEADS * SWA_HEAD_DIM),
    ('swa_k', SWA_KV_HEADS * SWA_HEAD_DIM),
    ('swa_v', SWA_KV_HEADS * SWA_HEAD_DIM),
    ('mla_cq', MLA_Q_RANK),
    ('mla_ckv', MLA_KV_RANK),
    ('mla_kr', MLA_ROPE),
    ('gqa_q', GQA_HEADS * GQA_HEAD_DIM),
    ('gqa_k', GQA_KV_HEADS * GQA_HEAD_DIM),
    ('gqa_v', GQA_KV_HEADS * GQA_HEAD_DIM),
    ('gates', N_BRANCH * D_MODEL),
)
D_IN = sum(w for _, w in IN_SPLITS)

kernel_name = 'hybrid_diffusion_prefix_trunk_step'


def _rmsnorm(x, g):
    xf = x.astype(jnp.float32)
    y = xf * lax.rsqrt(jnp.mean(xf * xf, axis=-1, keepdims=True) + NORM_EPS)
    return (y * g.astype(jnp.float32)).astype(x.dtype)


def _grid_positions(n_tokens):
    rows = n_tokens // GRID_W
    row = jnp.repeat(jnp.arange(rows, dtype=jnp.float32), GRID_W)
    col = jnp.tile(jnp.arange(GRID_W, dtype=jnp.float32), rows)
    return row, col


def _axial_rope(x, row, col):
    B, T, H, d = x.shape
    nf = d // 4
    inv = ROPE_BASE ** (-jnp.arange(nf, dtype=jnp.float32) / nf)
    ang = jnp.stack([row[:, None] * inv, col[:, None] * inv], axis=1)[:, None]
    cos, sin = jnp.cos(ang), jnp.sin(ang)
    xs = x.astype(jnp.float32).reshape(B, T, H, 2, 2, nf)
    x1, x2 = xs[..., 0, :], xs[..., 1, :]
    out = jnp.stack([x1 * cos - x2 * sin, x2 * cos + x1 * sin], axis=-2)
    return out.reshape(B, T, H, d).astype(x.dtype)


def _split_cols(u):
    parts, off = {}, 0
    for name, width in IN_SPLITS:
        parts[name] = u[..., off:off + width]
        off += width
    return parts


def _joint_attention(q, segments, scale, sink=None):
    n_kv, n_grp = q.shape[3], q.shape[4]
    scores = []
    for k, v, mask in segments:
        spec = 'bnqhgd,bnkhd->bnhgqk' if k.ndim == 5 else 'bnqhgd,bkhd->bnhgqk'
        s = jnp.einsum(spec, q, k).astype(jnp.float32) * scale
        if mask is not None:
            s = jnp.where(mask, s, MASK_VALUE)
        scores.append(s)
    s = jnp.concatenate(scores, axis=-1)
    m = jnp.max(s, axis=-1, keepdims=True)
    if sink is not None:
        sink_logit = sink.astype(jnp.float32).reshape(1, 1, n_kv, n_grp, 1, 1)
        m = jnp.maximum(m, sink_logit)
    e = jnp.exp(s - m)
    den = jnp.sum(e, axis=-1, keepdims=True)
    if sink is not None:
        den = den + jnp.exp(sink_logit - m)
    p = e / den
    out, off = None, 0
    for k, v, _ in segments:
        n_k = k.shape[-3]
        spec = 'bnhgqk,bnkhd->bnqhgd' if v.ndim == 5 else 'bnhgqk,bkhd->bnqhgd'
        o = jnp.einsum(spec, p[..., off:off + n_k].astype(v.dtype), v)
        out = o if out is None else out + o
        off += n_k
    return out


def _dense_blocked(q, segments, scale, sink=None):
    B, T = q.shape[:2]
    nb = T // Q_BLOCK
    qb = jnp.moveaxis(q.reshape(B, nb, Q_BLOCK, *q.shape[2:]), 1, 0)
    out = lax.map(lambda qi: _joint_attention(qi[:, None], segments, scale, sink)[:, 0], qb)
    return jnp.moveaxis(out, 0, 1).reshape(B, T, *out.shape[3:])


def _window_attention(q, k, v, ctx_k, ctx_v, scale, sink):
    B, T = q.shape[:2]
    nb = T // Q_BLOCK
    pad = ((0, 0), (Q_BLOCK, Q_BLOCK), (0, 0), (0, 0))

    def band(a):
        ab = jnp.pad(a, pad).reshape(B, nb + 2, Q_BLOCK, *a.shape[2:])
        return jnp.concatenate([ab[:, :-2], ab[:, 1:-1], ab[:, 2:]], axis=2)

    qpos = jnp.arange(T).reshape(nb, Q_BLOCK)
    kpos = (jnp.arange(nb)[:, None] - 1) * Q_BLOCK + jnp.arange(3 * Q_BLOCK)[None, :]
    valid = ((kpos >= 0) & (kpos < T))[:, None, :]
    mask = (jnp.abs(qpos[:, :, None] - kpos[:, None, :]) <= SWA_WINDOW) & valid
    mask = mask[None, :, None, None]
    qb = q.reshape(B, nb, Q_BLOCK, *q.shape[2:])
    out = _joint_attention(qb, [(ctx_k, ctx_v, None), (band(k), band(v), mask)], scale, sink)
    return out.reshape(B, T, *out.shape[3:])


def _lower_bounds(p):
    s = jax.nn.softmax(p.astype(jnp.float32), axis=0)
    return jnp.cumsum(s, axis=0) - s[0]


def _hgrn_forget(f_pre, lb):
    f_pre = f_pre.astype(jnp.float32)
    lb = lb.astype(jnp.float32)
    f = lb + (1.0 - lb) * jax.nn.sigmoid(f_pre)
    log_f = jnp.log(jnp.maximum(f, LOG_FLOOR))
    k = (1.0 - lb) * jax.nn.sigmoid(-f_pre)
    return k, log_f


def _gla_scan(q, k, v, log_f, s0):
    B, T, H, _ = q.shape
    C = HGRN_CHUNK
    n = T // C

    def chunks(a):
        return jnp.moveaxis(a.reshape(B, n, C, *a.shape[2:]), 1, 0)

    tri = jnp.tril(jnp.ones((C, C), dtype=bool))[None, :, :, None, None]

    def step(S, inp):
        qc, kc, vc, gc = inp
        b = jnp.cumsum(gc, axis=1)
        diff = jnp.where(tri, b[:, :, None] - b[:, None, :], MASK_VALUE)
        A = jnp.einsum('bthk,bshk,btshk->bhts', qc, kc, jnp.exp(diff))
        o = jnp.einsum('bhts,bshv->bthv', A, vc) + jnp.einsum('bthk,bhkv->bthv', qc * jnp.exp(b), S)
        b_last = b[:, -1]
        S = jnp.exp(b_last)[..., None] * S + jnp.einsum('bshk,bshv->bhkv', kc * jnp.exp(b_last[:, None] - b), vc)
        return S, o

    S, o = lax.scan(step, s0, (chunks(q), chunks(k), chunks(v), chunks(log_f)))
    return jnp.moveaxis(o, 0, 1).reshape(B, T, H, v.shape[-1]), S


def _hgrn_mixer(u, P, lb_f, lb_b, ctx):
    dtype = u['hgrn_q'].dtype
    B, T, _ = u['hgrn_q'].shape
    kshape = (B, T, HGRN_HEADS, HGRN_DK)
    vshape = (B, T, HGRN_HEADS, HGRN_DV)
    q = jax.nn.silu(u['hgrn_q'].astype(jnp.float32)).reshape(kshape)
    v = u['hgrn_i'].astype(jnp.float32).reshape(vshape)
    k_f, logf_f = _hgrn_forget(u['hgrn_f_fwd'].reshape(kshape), lb_f.reshape(HGRN_HEADS, HGRN_DK))
    k_b, logf_b = _hgrn_forget(u['hgrn_f_bwd'].reshape(kshape), lb_b.reshape(HGRN_HEADS, HGRN_DK))
    if ctx is None:
        s_f = jnp.zeros((B, HGRN_HEADS, HGRN_DK, HGRN_DV), jnp.float32)
        s_b = s_f
    else:
        s_f = ctx['hgrn'][:, 0].astype(jnp.float32)
        s_b = ctx['hgrn'][:, 1].astype(jnp.float32)
    o_f, fin_f = _gla_scan(q, k_f, v, logf_f, s_f)
    o_b, fin_b = _gla_scan(q[:, ::-1], k_b[:, ::-1], v[:, ::-1], logf_b[:, ::-1], s_b)
    o = _rmsnorm(o_f + o_b[:, ::-1], P['hgrn_norm'].reshape(HGRN_HEADS, HGRN_DV))
    o = o * jax.nn.silu(u['hgrn_g'].astype(jnp.float32).reshape(vshape))
    return o.reshape(B, T, HGRN_HEADS * HGRN_DV).astype(dtype), jnp.stack([fin_f, fin_b], axis=1)


def _swa_mixer(u, P, pos, ctx):
    B, T, _ = u['swa_q'].shape
    grp = SWA_HEADS // SWA_KV_HEADS
    q = u['swa_q'].reshape(B, T, SWA_HEADS, SWA_HEAD_DIM)
    k = u['swa_k'].reshape(B, T, SWA_KV_HEADS, SWA_HEAD_DIM)
    v = u['swa_v'].reshape(B, T, SWA_KV_HEADS, SWA_HEAD_DIM)
    scale = SWA_HEAD_DIM ** -0.5
    if ctx is None:
        o = _dense_blocked(q.reshape(B, T, SWA_KV_HEADS, grp, SWA_HEAD_DIM), [(k, v, None)], scale, P['swa_sink'])
    else:
        qr = _axial_rope(q, *pos).reshape(B, T, SWA_KV_HEADS, grp, SWA_HEAD_DIM)
        kr = _axial_rope(k, *pos)
        o = _window_attention(qr, kr, v, ctx['swa_k'], ctx['swa_v'], scale, P['swa_sink'])
    return o.reshape(B, T, SWA_HEADS * SWA_HEAD_DIM), (k, v)


def _mla_keys(c_kv, k_rope, w_ukv):
    B, T, _ = c_kv.shape
    kv = (c_kv @ w_ukv).reshape(B, T, MLA_HEADS, MLA_NOPE + MLA_V)
    k_r = jnp.broadcast_to(k_rope[:, :, None, :], (B, T, MLA_HEADS, MLA_ROPE)).astype(kv.dtype)
    return jnp.concatenate([kv[..., :MLA_NOPE], k_r], axis=-1), kv[..., MLA_NOPE:]


def _mla_mixer(u, P, pos, ctx):
    B, T, _ = u['mla_cq'].shape
    q = (_rmsnorm(u['mla_cq'], P['mla_q_norm']) @ P['mla_w_uq']).reshape(B, T, MLA_HEADS, MLA_NOPE + MLA_ROPE)
    q_nope, q_rope = q[..., :MLA_NOPE], q[..., MLA_NOPE:]
    c_kv = _rmsnorm(u['mla_ckv'], P['mla_kv_norm'])
    k_rope = u['mla_kr']
    if pos is not None:
        q_rope = _axial_rope(q_rope, *pos)
        k_rope_used = _axial_rope(k_rope[:, :, None, :], *pos)[:, :, 0]
    else:
        k_rope_used = k_rope
    q = jnp.concatenate([q_nope, q_rope], axis=-1)[:, :, :, None, :]
    k, v = _mla_keys(c_kv, k_rope_used, P['mla_w_ukv'])
    segments = [(k, v, None)]
    if ctx is not None:
        ck, cv = _mla_keys(ctx['mla_ckv'], ctx['mla_kr'], P['mla_w_ukv'])
        segments = [(ck, cv, None)] + segments
    o = _dense_blocked(q, segments, (MLA_NOPE + MLA_ROPE) ** -0.5)
    return o.reshape(B, T, MLA_HEADS * MLA_V), (c_kv, k_rope)


def _gqa_mixer(u, P, pos, ctx):
    B, T, _ = u['gqa_q'].shape
    grp = GQA_HEADS // GQA_KV_HEADS
    q = _rmsnorm(u['gqa_q'].reshape(B, T, GQA_HEADS, GQA_HEAD_DIM), P['gqa_q_norm'])
    k = _rmsnorm(u['gqa_k'].reshape(B, T, GQA_KV_HEADS, GQA_HEAD_DIM), P['gqa_k_norm'])
    v = u['gqa_v'].reshape(B, T, GQA_KV_HEADS, GQA_HEAD_DIM)
    scale = GQA_HEAD_DIM ** -0.5
    if ctx is None:
        segments = [(k, v, None)]
    else:
        q = _axial_rope(q, *pos)
        segments = [(ctx['gqa_k'], ctx['gqa_v'], None), (_axial_rope(k, *pos), v, None)]
    o = _dense_blocked(q.reshape(B, T, GQA_KV_HEADS, grp, GQA_HEAD_DIM), segments, scale)
    return o.reshape(B, T, GQA_HEADS * GQA_HEAD_DIM), (k, v)


def _merge(outs, gates, P):
    B, T, _ = gates.shape
    o = jnp.stack(outs, axis=2)
    br = jnp.einsum('btnw,nwd->btnd', o, P['w_branch'])
    g = jax.nn.sigmoid(gates.reshape(B, T, N_BRANCH, D_MODEL).astype(jnp.float32)).astype(br.dtype)
    return jnp.sum(br * g, axis=2) @ P['w_out']


def _layer(x, mod, P, lb_f, lb_b, pos, ctx):
    sh1, sc1, g1, sh2, sc2, g2 = jnp.split(mod.astype(x.dtype), N_MOD, axis=-1)
    h = _rmsnorm(x, P['norm_mix_pre']) * (1 + sc1) + sh1
    u = _split_cols(h @ P['w_in'])
    o_a, st_a = _hgrn_mixer(u, P, lb_f, lb_b, ctx)
    o_b, st_b = _swa_mixer(u, P, pos, ctx)
    o_c, st_c = _mla_mixer(u, P, pos, ctx)
    o_d, st_d = _gqa_mixer(u, P, pos, ctx)
    y = _merge((o_a, o_b, o_c, o_d), u['gates'], P)
    x = x + g1 * _rmsnorm(y, P['norm_mix_post'])
    h = _rmsnorm(x, P['norm_mlp_pre']) * (1 + sc2) + sh2
    y = jnp.square(jax.nn.relu(h @ P['w_mlp_in'])) @ P['w_mlp_out']
    x = x + g2 * _rmsnorm(y, P['norm_mlp_post'])
    return x, (st_a, st_b[0], st_b[1], st_c[0], st_c[1], st_d[0], st_d[1])


def setup_inputs(seed: int = 0) -> dict:
    key = jax.random.key(seed)
    ks = jax.random.split(key, 32)

    def nrm(i, shape, s):
        return s * jax.random.normal(ks[i], shape, jnp.float32)

    def gain(i, shape):
        return 1.0 + 0.05 * jax.random.normal(ks[i], shape, jnp.float32)

    kv_swa = (DEC_BATCH, DEPTH, PAST_LEN, SWA_KV_HEADS, SWA_HEAD_DIM)
    kv_gqa = (DEC_BATCH, DEPTH, PAST_LEN, GQA_KV_HEADS, GQA_HEAD_DIM)
    return {
        'x_prompt': nrm(0, (BATCH, SEQ, D_MODEL), 1.0),
        'x_sample': nrm(1, (DEC_BATCH, DEC_SEQ, D_MODEL), 1.0),
        'state_hgrn': nrm(2, (DEC_BATCH, DEPTH, 2, HGRN_HEADS, HGRN_DK, HGRN_DV), 0.5),
        'cache_swa_k': nrm(3, kv_swa, 1.0),
        'cache_swa_v': nrm(4, kv_swa, 1.0),
        'cache_mla_ckv': nrm(5, (DEC_BATCH, DEPTH, PAST_LEN, MLA_KV_RANK), 1.0),
        'cache_mla_kr': nrm(6, (DEC_BATCH, DEPTH, PAST_LEN, MLA_ROPE), 1.0),
        'cache_gqa_k': nrm(7, kv_gqa, 1.0),
        'cache_gqa_v': nrm(8, kv_gqa, 1.0),
        'c': nrm(9, (DEC_BATCH, D_MODEL), 1.0),
        'c_ctx': nrm(10, (D_MODEL,), 1.0),
        'w_ada': nrm(11, (DEPTH, D_MODEL, N_MOD * D_MODEL), 0.5 * D_MODEL ** -0.5),
        'b_ada': nrm(12, (DEPTH, N_MOD * D_MODEL), 0.01),
        'norm_mix_pre': gain(13, (DEPTH, D_MODEL)),
        'norm_mix_post': gain(14, (DEPTH, D_MODEL)),
        'norm_mlp_pre': gain(15, (DEPTH, D_MODEL)),
        'norm_mlp_post': gain(16, (DEPTH, D_MODEL)),
        'w_in': nrm(17, (DEPTH, D_MODEL, D_IN), D_MODEL ** -0.5),
        'hgrn_lb_fwd': nrm(18, (DEPTH, HGRN_HEADS * HGRN_DK), 0.5),
        'hgrn_lb_bwd': nrm(19, (DEPTH, HGRN_HEADS * HGRN_DK), 0.5),
        'hgrn_norm': gain(20, (DEPTH, HGRN_HEADS * HGRN_DV)),
        'swa_sink': nrm(21, (DEPTH, SWA_HEADS), 0.5),
        'mla_q_norm': gain(22, (DEPTH, MLA_Q_RANK)),
        'mla_kv_norm': gain(23, (DEPTH, MLA_KV_RANK)),
        'mla_w_uq': nrm(24, (DEPTH, MLA_Q_RANK, MLA_HEADS * (MLA_NOPE + MLA_ROPE)), MLA_Q_RANK ** -0.5),
        'mla_w_ukv': nrm(25, (DEPTH, MLA_KV_RANK, MLA_HEADS * (MLA_NOPE + MLA_V)), MLA_KV_RANK ** -0.5),
        'gqa_q_norm': gain(26, (DEPTH, GQA_HEAD_DIM)),
        'gqa_k_norm': gain(27, (DEPTH, GQA_HEAD_DIM)),
        'w_branch': nrm(28, (DEPTH, N_BRANCH, BRANCH_W, D_MODEL), BRANCH_W ** -0.5),
        'w_out': nrm(29, (DEPTH, D_MODEL, D_MODEL), D_MODEL ** -0.5),
        'w_mlp_in': nrm(30, (DEPTH, D_MODEL, D_FF), D_MODEL ** -0.5),
        'w_mlp_out': nrm(31, (DEPTH, D_FF, D_MODEL), D_FF ** -0.5),
    }


def reference(x_prompt, x_sample, state_hgrn, cache_swa_k, cache_swa_v, cache_mla_ckv, cache_mla_kr,
              cache_gqa_k, cache_gqa_v, c, c_ctx, w_ada, b_ada, norm_mix_pre, norm_mix_post, norm_mlp_pre,
              norm_mlp_post, w_in, hgrn_lb_fwd, hgrn_lb_bwd, hgrn_norm, swa_sink, mla_q_norm, mla_kv_norm,
              mla_w_uq, mla_w_ukv, gqa_q_norm, gqa_k_norm, w_branch, w_out, w_mlp_in, w_mlp_out):
    lb_fwd = _lower_bounds(hgrn_lb_fwd)
    lb_bwd = _lower_bounds(hgrn_lb_bwd)

    def layer_params(l):
        return dict(
            w_in=w_in[l], norm_mix_pre=norm_mix_pre[l], norm_mix_post=norm_mix_post[l],
            norm_mlp_pre=norm_mlp_pre[l], norm_mlp_post=norm_mlp_post[l], hgrn_norm=hgrn_norm[l],
            swa_sink=swa_sink[l], mla_q_norm=mla_q_norm[l], mla_kv_norm=mla_kv_norm[l],
            mla_w_uq=mla_w_uq[l], mla_w_ukv=mla_w_ukv[l], gqa_q_norm=gqa_q_norm[l],
            gqa_k_norm=gqa_k_norm[l], w_branch=w_branch[l], w_out=w_out[l],
            w_mlp_in=w_mlp_in[l], w_mlp_out=w_mlp_out[l])

    y_prompt = x_prompt
    ctx_per_layer = []
    for l in range(DEPTH):
        mod = jax.nn.silu(c_ctx) @ w_ada[l] + b_ada[l]
        y_prompt, st = _layer(y_prompt, mod, layer_params(l), lb_fwd[l], lb_bwd[l], None, None)
        ctx_per_layer.append(st)
    stacked = [jnp.stack(t, axis=1) for t in zip(*ctx_per_layer)]
    (new_state_hgrn, new_cache_swa_k, new_cache_swa_v, new_cache_mla_ckv, new_cache_mla_kr,
     new_cache_gqa_k, new_cache_gqa_v) = stacked

    pos = _grid_positions(x_sample.shape[1])
    y_sample = x_sample
    for l in range(DEPTH):
        mod = (jax.nn.silu(c) @ w_ada[l] + b_ada[l])[:, None, :]
        ctx = dict(hgrn=state_hgrn[:, l], swa_k=cache_swa_k[:, l], swa_v=cache_swa_v[:, l],
                   mla_ckv=cache_mla_ckv[:, l], mla_kr=cache_mla_kr[:, l],
                   gqa_k=cache_gqa_k[:, l], gqa_v=cache_gqa_v[:, l])
        y_sample, _ = _layer(y_sample, mod, layer_params(l), lb_fwd[l], lb_bwd[l], pos, ctx)

    return (y_prompt, y_sample, new_state_hgrn, new_cache_swa_k, new_cache_swa_v, new_cache_mla_ckv,
            new_cache_mla_kr, new_cache_gqa_k, new_cache_gqa_v)
```

```python
import functools

import jax
import jax.numpy as jnp
import numpy as np
from jax import lax
from jax.experimental import pallas as pl
from jax.experimental.pallas import tpu as pltpu

D_MODEL = 1024
BATCH, SEQ = 32, 256
DEC_BATCH, DEC_SEQ = 2, 2048
DEPTH = 2
PAST_LEN = 512
GRID_W = 64
ROPE_BASE = 10000.0
NORM_EPS = 1e-6
MASK_VALUE = -1e30
LOG_FLOOR = 1e-30
N_MOD = 6
N_BRANCH = 4
BRANCH_W = 512
D_FF = 4 * D_MODEL
HGRN_HEADS, HGRN_DK = 4, 128
SWA_WINDOW = 128
MLA_HEADS, MLA_Q_RANK, MLA_KV_RANK, MLA_NOPE, MLA_ROPE, MLA_V = 8, 256, 128, 64, 32, 64
HEAD_DIM = 64
N_HEADS, N_KV = 8, 2

N_CTX = BATCH * SEQ
N_LAT = DEC_BATCH * DEC_SEQ
N_TOK = N_CTX + N_LAT

LANES = 128
SUBLANES = 8
VMEM_LIMIT_BYTES = 56 * 1024 * 1024

TM = 256
N_TILES = N_TOK // TM
CTX_TILES = N_CTX // TM
LAT_TILES_PER_SEQ = DEC_SEQ // TM
HGRN_CHUNK = 64
TQ_LAT = 128

BF16 = jnp.bfloat16
F32 = jnp.float32

_C_HGRN = 0
_C_SWA_Q = 2560
_C_SWA_KV = 3072
_C_SWA_KDUP = 3328
_C_SWA_VDUP = 3584
_C_MLA_CQ = 3840
_C_MLA_CKV = 4096
_C_MLA_KR = 4224
_C_GQA_Q = 4352
_C_GQA_KV = 4864
_C_GQA_KDUP = 5120
_C_GQA_VDUP = 5376
_C_GATES = 5632
W_IN_COLS = _C_GATES + N_BRANCH * D_MODEL


def _w_in_perm():
    off, o = {}, 0
    for name, w in (("hgrn", 2560), ("swa_q", 512), ("swa_k", 128), ("swa_v", 128), ("mla_cq", 256),
                    ("mla_ckv", 128), ("mla_kr", 32), ("gqa_q", 512), ("gqa_k", 128), ("gqa_v", 128),
                    ("gates", 4096)):
        off[name] = o
        o += w
    r = np.arange

    def dup(base):
        return np.concatenate([base + r(64), base + r(64), base + 64 + r(64), base + 64 + r(64)])

    cols = np.concatenate([
        off["hgrn"] + r(2560), off["swa_q"] + r(512), off["swa_k"] + r(128), off["swa_v"] + r(128),
        dup(off["swa_k"]), dup(off["swa_v"]), off["mla_cq"] + r(256), off["mla_ckv"] + r(128),
        off["mla_kr"] + r(32), np.full(96, -1), off["gqa_q"] + r(512), off["gqa_k"] + r(128),
        off["gqa_v"] + r(128), dup(off["gqa_k"]), dup(off["gqa_v"]), off["gates"] + r(4096)])
    assert cols.shape[0] == W_IN_COLS
    return cols


_W_IN_PERM = _w_in_perm()


def _sigmoid_pair(x):
    a = jnp.exp(-jnp.abs(x))
    r = 1.0 / (1.0 + a)
    ar = a * r
    pos = x >= 0
    return jnp.where(pos, r, ar), jnp.where(pos, ar, r)


def _sigmoid(x):
    return _sigmoid_pair(x)[0]


def _silu(x):
    return x * _sigmoid(x)


def _rms(x, gain):
    return x * lax.rsqrt(jnp.mean(x * x, axis=-1, keepdims=True) + NORM_EPS) * gain


def _dot(a, b):
    return jnp.dot(a, b, preferred_element_type=F32)


def _dot_nt(a, b):
    return lax.dot_general(a, b, (((1,), (1,)), ((), ())), preferred_element_type=F32)


def _dot_tn(a, b):
    return lax.dot_general(a, b, (((0,), (0,)), ((), ())), preferred_element_type=F32)


def _tile_lanes(t, width):
    reps = width // LANES
    return t if reps == 1 else jnp.concatenate([t] * reps, axis=1)


def _rope(x, tab_ref, base, shift):
    w = x.shape[1]
    c = _tile_lanes(tab_ref[base], w)
    s1 = _tile_lanes(tab_ref[base + 1], w)
    s2 = _tile_lanes(tab_ref[base + 2], w)
    return x * c + pltpu.roll(x, shift, 1) * s1 + pltpu.roll(x, w - shift, 1) * s2


def _head_rms(x, bmat_ref, gain):
    ms = _dot((x * x).astype(BF16), bmat_ref[...])
    return x * lax.rsqrt(ms + NORM_EPS) * gain


_ADA_TN = 2048


def _ada_kernel(c_ref, w_ref, b_ref, o_ref):
    s = _silu(c_ref[...]).astype(BF16)
    o_ref[0] = _dot(s, w_ref[0].astype(BF16)) + b_ref[0]


def _ada(cond8, w_ada, b_ada):
    n = N_MOD * D_MODEL
    return pl.pallas_call(
        _ada_kernel,
        out_shape=jax.ShapeDtypeStruct((DEPTH, SUBLANES, n), F32),
        grid=(DEPTH, n // _ADA_TN),
        in_specs=[
            pl.BlockSpec((SUBLANES, D_MODEL), lambda l, j: (0, 0)),
            pl.BlockSpec((1, D_MODEL, _ADA_TN), lambda l, j: (l, 0, j)),
            pl.BlockSpec((1, 1, _ADA_TN), lambda l, j: (l, 0, j)),
        ],
        out_specs=pl.BlockSpec((1, SUBLANES, _ADA_TN), lambda l, j: (l, 0, j)),
        compiler_params=pltpu.CompilerParams(dimension_semantics=("arbitrary", "arbitrary"),
                                             vmem_limit_bytes=VMEM_LIMIT_BYTES),
        name="ada",
    )(cond8, w_ada, b_ada.reshape(DEPTH, 1, n))


def _mod_row(i):
    return jnp.where(i < CTX_TILES, 0, 1 + (i - CTX_TILES) // LAT_TILES_PER_SEQ)


def _rope_block(i):
    return jnp.where(i < CTX_TILES, 0, 1 + (i - CTX_TILES) % LAT_TILES_PER_SEQ)


_SCALE_64 = HEAD_DIM ** -0.5
_SCALE_MLA = (MLA_NOPE + MLA_ROPE) ** -0.5
_TAB_64, _TAB_MLAQ, _TAB_KR = 0, 3, 6


def _proj_kernel(x_ref, mod_ref, npre_ref, w_ref, tab_ref, mlaq_g_ref, mlakv_g_ref, wuq_ref, wk_ref, wuv_ref,
                 gq_g_ref, gk_g_ref, b512_ref, b256_ref, b128_ref,
                 hq_ref, hff_ref, hfb_ref, hi_ref, hg_ref,
                 sq_ref, sk_ref, sv_ref, skc_ref, svc_ref,
                 mq_ref, mk_ref, mv_ref, mckv_ref, mkr_ref,
                 gq_ref, gk_ref, gv_ref, gkc_ref, gvc_ref, gates_ref):
    x = x_ref[...]
    sh1 = mod_ref[0, 0]
    sc1 = mod_ref[0, 1]
    h = (_rms(x, npre_ref[...]) * (1.0 + sc1) + sh1).astype(BF16)

    def seg(a, w):
        return _dot(h, w_ref[:, a:a + w])

    for j, ref in enumerate((hq_ref, hff_ref, hfb_ref, hi_ref, hg_ref)):
        ref[...] = seg(_C_HGRN + 512 * j, 512)

    sq_ref[...] = (_rope(seg(_C_SWA_Q, 512), tab_ref, _TAB_64, 16) * _SCALE_64).astype(BF16)
    kv = seg(_C_SWA_KV, 256)
    skc_ref[...] = kv[:, :128]
    svc_ref[...] = kv[:, 128:]
    sk_ref[...] = _rope(seg(_C_SWA_KDUP, 256), tab_ref, _TAB_64, 16).astype(BF16)
    sv_ref[...] = seg(_C_SWA_VDUP, 256).astype(BF16)

    cq = _rms(seg(_C_MLA_CQ, 256), mlaq_g_ref[...]).astype(BF16)
    q = _dot(cq, wuq_ref[...])
    mq_ref[...] = (_rope(q, tab_ref, _TAB_MLAQ, 8) * _SCALE_MLA).astype(BF16)
    ckv = _rms(seg(_C_MLA_CKV, 128), mlakv_g_ref[...])
    kr = seg(_C_MLA_KR, 128)
    mckv_ref[...] = ckv
    mkr_ref[...] = kr
    kr_used = _rope(kr, tab_ref, _TAB_KR, 8)
    ckv16 = ckv.astype(BF16)
    mk_ref[...] = _dot(jnp.concatenate([ckv16, kr_used.astype(BF16)], axis=1), wk_ref[...]).astype(BF16)
    mv_ref[...] = _dot(ckv16, wuv_ref[...]).astype(BF16)

    gq = _head_rms(seg(_C_GQA_Q, 512), b512_ref, gq_g_ref[...])
    gq_ref[...] = (_rope(gq, tab_ref, _TAB_64, 16) * _SCALE_64).astype(BF16)
    kv = seg(_C_GQA_KV, 256)
    gkc_ref[...] = _head_rms(kv[:, :128], b128_ref, gk_g_ref[:, :128])
    gvc_ref[...] = kv[:, 128:]
    gk = _head_rms(seg(_C_GQA_KDUP, 256), b256_ref, gk_g_ref[...])
    gk_ref[...] = _rope(gk, tab_ref, _TAB_64, 16).astype(BF16)
    gv_ref[...] = seg(_C_GQA_VDUP, 256).astype(BF16)

    for j in range(N_BRANCH):
        gates_ref[:, D_MODEL * j:D_MODEL * (j + 1)] = _sigmoid(seg(_C_GATES + D_MODEL * j, D_MODEL)).astype(BF16)


_PROJ_OUTS = (
    (512, F32), (512, F32), (512, F32), (512, F32), (512, F32),
    (512, BF16), (256, BF16), (256, BF16), (128, F32), (128, F32),
    (1024, BF16), (1024, BF16), (512, BF16), (128, F32), (128, F32),
    (512, BF16), (256, BF16), (256, BF16), (128, F32), (128, F32), (4096, BF16))


def _const_spec(shape):
    nd = len(shape)
    return pl.BlockSpec(shape, lambda *_: (0,) * nd, pipeline_mode=pl.Buffered(1))


def _proj(x, mod4, lp, tabs):
    in_specs = [
        pl.BlockSpec((TM, D_MODEL), lambda i: (i, 0)),
        pl.BlockSpec((1, N_MOD, 1, D_MODEL), lambda i: (_mod_row(i), 0, 0, 0)),
        _const_spec((1, D_MODEL)),
        _const_spec((D_MODEL, W_IN_COLS)),
        pl.BlockSpec((9, TM, LANES), lambda i: (0, _rope_block(i), 0)),
        _const_spec((1, MLA_Q_RANK)), _const_spec((1, MLA_KV_RANK)),
        _const_spec((MLA_Q_RANK, 1024)), _const_spec((256, 1024)), _const_spec((MLA_KV_RANK, 512)),
        _const_spec((1, 512)), _const_spec((1, 256)),
        _const_spec((512, 512)), _const_spec((256, 256)), _const_spec((128, 128)),
    ]
    out_specs = [pl.BlockSpec((TM, w), lambda i: (i, 0)) for w, _ in _PROJ_OUTS]
    out_shape = [jax.ShapeDtypeStruct((N_TOK, w), dt) for w, dt in _PROJ_OUTS]
    return pl.pallas_call(
        _proj_kernel, out_shape=out_shape, grid=(N_TILES,), in_specs=in_specs, out_specs=out_specs,
        compiler_params=pltpu.CompilerParams(dimension_semantics=("arbitrary",), vmem_limit_bytes=VMEM_LIMIT_BYTES),
        name="proj",
    )(x, mod4, lp["norm_mix_pre"], lp["w_in"], tabs, lp["mla_q_norm"], lp["mla_kv_norm"], lp["w_uq"], lp["w_k"],
      lp["w_uv"], lp["gqa_q_gain"], lp["gqa_k_gain"], lp["b512"], lp["b256"], lp["b128"])


def _hgrn_chunk(q, k, lf, v, st_ref, d, fwd):
    C = HGRN_CHUNK
    nv = C // SUBLANES
    q3, k3, v3 = (a.reshape(nv, SUBLANES, LANES) for a in (q, k, v))
    sub = lax.broadcasted_iota(jnp.int32, (nv, SUBLANES, LANES), 1)

    p = lf.reshape(nv, SUBLANES, LANES)
    for s in (1, 2, 4):
        if fwd:
            p = p + jnp.where(sub >= s, pltpu.roll(p, s, 1), 0.0)
        else:
            p = p + jnp.where(sub < SUBLANES - s, pltpu.roll(p, SUBLANES - s, 1), 0.0)

    o = jnp.zeros((nv, SUBLANES, LANES), F32)
    for off in range(SUBLANES):
        if off == 0:
            w = q3 * k3
            vd = v3
        else:
            sh = off if fwd else SUBLANES - off
            valid = (sub >= off) if fwd else (sub < SUBLANES - off)
            e = jnp.exp(jnp.where(valid, p - pltpu.roll(p, sh, 1), MASK_VALUE))
            w = q3 * pltpu.roll(k3, sh, 1) * e
            vd = pltpu.roll(v3, sh, 1)
        o = o + jnp.sum(w, axis=-1, keepdims=True) * vd
    o = o.reshape(C, LANES)

    p = p.reshape(C, LANES)
    ti = lax.broadcasted_iota(jnp.int32, (C, C), 0)
    si = lax.broadcasted_iota(jnp.int32, (C, C), 1)
    xor = ti ^ si
    causal = (ti > si) if fwd else (ti < si)
    a_mat = jnp.zeros((C, C), F32)
    m = SUBLANES
    while m < C:
        nb = C // (2 * m)
        p4 = p.reshape(nb, 2, m, LANES)
        lo, hi = p4[:, 0], p4[:, 1]
        if fwd:
            tot = lo[:, m - 1:m]
            e_lo, e_hi = tot - lo, hi
            p_lo, p_hi = lo, hi + tot
        else:
            tot = hi[:, 0:1]
            e_lo, e_hi = lo, tot - hi
            p_lo, p_hi = lo + tot, hi
        e = jnp.exp(jnp.concatenate([e_lo[:, None], e_hi[:, None]], axis=1).reshape(C, LANES))
        p = jnp.concatenate([p_lo[:, None], p_hi[:, None]], axis=1).reshape(C, LANES)
        a_m = _dot_nt((q * e).astype(BF16), (k * e).astype(BF16))
        a_mat = jnp.where(((xor >> int(np.log2(m))) == 1) & causal, a_m, a_mat)
        m *= 2
    v16 = v.astype(BF16)
    o = o + _dot(a_mat.astype(BF16), v16)

    st = st_ref[d]
    o = o + _dot_nt((q * jnp.exp(p)).astype(BF16), st.astype(BF16))
    tot = p[C - 1:C] if fwd else p[0:1]
    k_end = (k * jnp.exp(tot - p)).astype(BF16)
    st_ref[d] = st * jnp.exp(tot) + _dot_tn(v16, k_end)
    return o


def _hgrn_kernel(*refs, n_chunks, has_init, emit_state):
    lbf_ref, lbb_ref, gn_ref, q_ref, ff_ref, fb_ref, v_ref, g_ref = refs[:8]
    pos = 8
    s0_ref = None
    if has_init:
        s0_ref = refs[pos]
        pos += 1
    o_ref = refs[pos]
    pos += 1
    sfin_ref = None
    if emit_state:
        sfin_ref = refs[pos]
        pos += 1
    of_ref, ob_ref, st_ref = refs[pos:pos + 3]
    C = HGRN_CHUNK

    for d in range(2):
        if has_init:
            st_ref[d] = s0_ref[0, 0, d, 0].T
        else:
            st_ref[d] = jnp.zeros((LANES, LANES), F32)
    lbf = lbf_ref[...]
    lbb = lbb_ref[...]

    def forget(f_pre, lb):
        s_pos, s_neg = _sigmoid_pair(f_pre)
        f = lb + (1.0 - lb) * s_pos
        return (1.0 - lb) * s_neg, jnp.log(jnp.maximum(f, LOG_FLOOR))

    def body(c, carry):
        rf = pl.multiple_of(c * C, C)
        rb = pl.multiple_of((n_chunks - 1 - c) * C, C)
        for d, r0, f_ref, lb, out_ref in ((0, rf, ff_ref, lbf, of_ref), (1, rb, fb_ref, lbb, ob_ref)):
            sl = pl.ds(r0, C)
            q = _silu(q_ref[sl, :])
            k, lf = forget(f_ref[sl, :], lb)
            out_ref[sl, :] = _hgrn_chunk(q, k, lf, v_ref[sl, :], st_ref, d, fwd=(d == 0))
        return carry

    lax.fori_loop(0, n_chunks, body, 0)
    o = of_ref[...] + ob_ref[...]
    o = _rms(o, gn_ref[...]) * _silu(g_ref[...])
    o_ref[...] = o.astype(BF16)
    if emit_state:
        for d in range(2):
            sfin_ref[0, d, 0] = st_ref[d].T


def _hgrn(pre, lp, row0, n_seq, seq_len, state_in=None, layer=None):
    hq, hff, hfb, hi, hg = pre
    blk0 = row0 // seq_len
    tok_spec = pl.BlockSpec((seq_len, LANES), lambda b, h: (blk0 + b, h))
    vec_spec = pl.BlockSpec((1, LANES), lambda b, h: (0, h))
    in_specs = [vec_spec, vec_spec, vec_spec, tok_spec, tok_spec, tok_spec, tok_spec, tok_spec]
    args = [lp["lb_f"], lp["lb_b"], lp["hgrn_norm"], hq, hff, hfb, hi, hg]
    has_init = state_in is not None
    if has_init:
        in_specs.append(pl.BlockSpec((1, 1, 2, 1, LANES, LANES), lambda b, h: (b, layer, 0, h, 0, 0)))
        args.append(state_in)
    out_shape = [jax.ShapeDtypeStruct((n_seq * seq_len, 512), BF16)]
    out_specs = [pl.BlockSpec((seq_len, LANES), lambda b, h: (b, h))]
    emit_state = not has_init
    if emit_state:
        out_shape.append(jax.ShapeDtypeStruct((n_seq, 2, HGRN_HEADS, LANES, LANES), F32))
        out_specs.append(pl.BlockSpec((1, 2, 1, LANES, LANES), lambda b, h: (b, 0, h, 0, 0)))
    kern = functools.partial(_hgrn_kernel, n_chunks=seq_len // HGRN_CHUNK, has_init=has_init, emit_state=emit_state)
    return pl.pallas_call(
        kern, out_shape=out_shape, grid=(n_seq, HGRN_HEADS), in_specs=in_specs, out_specs=out_specs,
        scratch_shapes=[pltpu.VMEM((seq_len, LANES), F32), pltpu.VMEM((seq_len, LANES), F32),
                        pltpu.VMEM((2, LANES, LANES), F32)],
        compiler_params=pltpu.CompilerParams(dimension_semantics=("arbitrary", "arbitrary"),
                                             vmem_limit_bytes=VMEM_LIMIT_BYTES),
        name="hgrn_lat" if has_init else "hgrn_ctx",
    )(*args)


def _lane_lo(shape):
    return lax.broadcasted_iota(jnp.int32, shape, len(shape) - 1) < HEAD_DIM


def _softmax_pv(segs, sink_col=None):
    mx = None
    for s, _ in segs:
        m = jnp.max(s, axis=-1, keepdims=True)
        mx = m if mx is None else jnp.maximum(mx, m)
    if sink_col is not None:
        mx = jnp.maximum(mx, sink_col)
    den = None
    acc = None
    for s, v in segs:
        e = jnp.exp(s - mx)
        dsum = jnp.sum(e, axis=-1, keepdims=True)
        den = dsum if den is None else den + dsum
        pv = _dot(e.astype(BF16), v)
        acc = pv if acc is None else acc + pv
    if sink_col is not None:
        den = den + jnp.exp(sink_col - mx)
    return acc / den


def _gqa_group(q_ref, g, tq, segs_fn, sink_ref=None):
    lo = _lane_lo((tq, LANES))
    parts = []
    for j in range(2):
        qp = q_ref[:, 256 * g + 128 * j:256 * g + 128 * (j + 1)]
        zero = jnp.zeros_like(qp)
        parts += [jnp.where(lo, qp, zero), jnp.where(lo, zero, qp)]
    q4 = jnp.concatenate(parts, axis=0)
    sink_col = None
    if sink_ref is not None:
        sink_col = jnp.concatenate(
            [jnp.broadcast_to(sink_ref[0:1, 4 * g + i:4 * g + i + 1], (tq, 1)) for i in range(4)], axis=0)
    pv = _softmax_pv(segs_fn(q4), sink_col)
    return (jnp.where(lo, pv[0:tq], pv[tq:2 * tq]), jnp.where(lo, pv[2 * tq:3 * tq], pv[3 * tq:4 * tq]))


def _mla_heads(q_ref, tq, segs_fn, o_ref):
    lo = _lane_lo((tq, LANES))
    for hp in range(MLA_HEADS // 2):
        outs = []
        for h in (2 * hp, 2 * hp + 1):
            q_h = q_ref[:, 128 * h:128 * (h + 1)]
            outs.append(_softmax_pv(segs_fn(h, q_h)))
        o_ref[:, 128 * hp:128 * (hp + 1)] = jnp.where(lo, outs[0], outs[1]).astype(BF16)


def _attn_ctx_kernel(sink_ref, sq_ref, sk_ref, sv_ref, mq_ref, mk_ref, mv_ref, gq_ref, gk_ref, gv_ref,
                     ob_ref, oc_ref, od_ref):
    tq = SEQ
    for g in range(N_KV):
        def swa_segs(q4, g=g):
            return [(_dot_nt(q4, sk_ref[:, 128 * g:128 * (g + 1)]), sv_ref[:, 128 * g:128 * (g + 1)])]

        a, b = _gqa_group(sq_ref, g, tq, swa_segs, sink_ref)
        ob_ref[:, 256 * g:256 * g + 128] = a.astype(BF16)
        ob_ref[:, 256 * g + 128:256 * (g + 1)] = b.astype(BF16)

        def gqa_segs(q4, g=g):
            return [(_dot_nt(q4, gk_ref[:, 128 * g:128 * (g + 1)]), gv_ref[:, 128 * g:128 * (g + 1)])]

        a, b = _gqa_group(gq_ref, g, tq, gqa_segs)
        od_ref[:, 256 * g:256 * g + 128] = a.astype(BF16)
        od_ref[:, 256 * g + 128:256 * (g + 1)] = b.astype(BF16)

    def mla_segs(h, q_h):
        return [(_dot_nt(q_h, mk_ref[:, 128 * h:128 * (h + 1)]), mv_ref[:, 128 * (h // 2):128 * (h // 2 + 1)])]

    _mla_heads(mq_ref, tq, mla_segs, oc_ref)


def _attn_ctx(sink, p):
    def spec(w):
        return pl.BlockSpec((SEQ, w), lambda b: (b, 0))

    names = ("sq", "sk", "sv", "mq", "mk", "mv", "gq", "gk", "gv")
    widths = (512, 256, 256, 1024, 1024, 512, 512, 256, 256)
    return pl.pallas_call(
        _attn_ctx_kernel,
        out_shape=[jax.ShapeDtypeStruct((N_CTX, 512), BF16)] * 3,
        grid=(BATCH,),
        in_specs=[_const_spec((1, N_HEADS))] + [spec(w) for w in widths],
        out_specs=[spec(512)] * 3,
        compiler_params=pltpu.CompilerParams(dimension_semantics=("arbitrary",), vmem_limit_bytes=VMEM_LIMIT_BYTES),
        name="attn_ctx",
    )(sink, *[p[n] for n in names])


def _dup_heads(x):
    lo = _lane_lo(x.shape)
    r = pltpu.roll(x, HEAD_DIM, 1)
    return jnp.where(lo, x, r), jnp.where(lo, r, x)


def _attn_lat_kernel(sink_ref, wk_ref, wuv_ref,
                     sq_ref, sk_ref, sv_ref, mq_ref, mk_ref, mv_ref, gq_ref, gk_ref, gv_ref,
                     csk_ref, csv_ref, cckv_ref, ckr_ref, cgk_ref, cgv_ref,
                     ob_ref, oc_ref, od_ref,
                     cs_k, cs_v, cm_k, cm_v, cg_k, cg_v):
    tq = TQ_LAT
    qi = pl.program_id(1)

    @pl.when(qi == 0)
    def _():
        for src, dst in ((csk_ref, cs_k), (csv_ref, cs_v), (cgk_ref, cg_k), (cgv_ref, cg_v)):
            d0, d1 = _dup_heads(src[0, 0])
            dst[0] = d0.astype(BF16)
            dst[1] = d1.astype(BF16)
        ckv16 = cckv_ref[0, 0].astype(BF16)
        k = _dot(ckv16, wk_ref[0:MLA_KV_RANK, :]) + _dot(ckr_ref[0, 0].astype(BF16),
                                                           wk_ref[MLA_KV_RANK:MLA_KV_RANK + MLA_ROPE, :])
        cm_k[...] = k.astype(BF16)
        cm_v[...] = _dot(ckv16, wuv_ref[...]).astype(BF16)

    q0 = qi * tq
    win = tq + 2 * SWA_WINDOW
    start = pl.multiple_of(jnp.clip(q0 - SWA_WINDOW, 0, DEC_SEQ - win), SWA_WINDOW)
    rows = lax.broadcasted_iota(jnp.int32, (4 * tq, win), 0)
    qpos = q0 + (rows & (tq - 1))
    kpos = start + lax.broadcasted_iota(jnp.int32, (4 * tq, win), 1)
    band = jnp.abs(qpos - kpos) <= SWA_WINDOW
    for g in range(N_KV):
        def swa_segs(q4, g=g):
            ks = sk_ref[pl.ds(start, win), 128 * g:128 * (g + 1)]
            vs = sv_ref[pl.ds(start, win), 128 * g:128 * (g + 1)]
            s_own = jnp.where(band, _dot_nt(q4, ks), MASK_VALUE)
            return [(_dot_nt(q4, cs_k[g]), cs_v[g]), (s_own, vs)]

        a, b = _gqa_group(sq_ref, g, tq, swa_segs, sink_ref)
        ob_ref[:, 256 * g:256 * g + 128] = a.astype(BF16)
        ob_ref[:, 256 * g + 128:256 * (g + 1)] = b.astype(BF16)

        def gqa_segs(q4, g=g):
            return [(_dot_nt(q4, cg_k[g]), cg_v[g]),
                    (_dot_nt(q4, gk_ref[:, 128 * g:128 * (g + 1)]), gv_ref[:, 128 * g:128 * (g + 1)])]

        a, b = _gqa_group(gq_ref, g, tq, gqa_segs)
        od_ref[:, 256 * g:256 * g + 128] = a.astype(BF16)
        od_ref[:, 256 * g + 128:256 * (g + 1)] = b.astype(BF16)

    def mla_segs(h, q_h):
        hp = h // 2
        return [(_dot_nt(q_h, cm_k[:, 128 * h:128 * (h + 1)]), cm_v[:, 128 * hp:128 * (hp + 1)]),
                (_dot_nt(q_h, mk_ref[:, 128 * h:128 * (h + 1)]), mv_ref[:, 128 * hp:128 * (hp + 1)])]

    _mla_heads(mq_ref, tq, mla_segs, oc_ref)


def _attn_lat(sink, lp, p, caches, layer):
    nq = DEC_SEQ // TQ_LAT
    qblk0 = N_CTX // TQ_LAT
    kblk0 = N_CTX // DEC_SEQ

    def qspec(w):
        return pl.BlockSpec((TQ_LAT, w), lambda b, i: (qblk0 + b * nq + i, 0))

    def kspec(w):
        return pl.BlockSpec((DEC_SEQ, w), lambda b, i: (kblk0 + b, 0))

    def cspec(w):
        return pl.BlockSpec((1, 1, PAST_LEN, w), lambda b, i: (b, layer, 0, 0))

    in_specs = [_const_spec((1, N_HEADS)), _const_spec((256, 1024)), _const_spec((MLA_KV_RANK, 512)),
                qspec(512), kspec(256), kspec(256), qspec(1024), kspec(1024), kspec(512),
                qspec(512), kspec(256), kspec(256),
                cspec(128), cspec(128), cspec(128), cspec(MLA_ROPE), cspec(128), cspec(128)]
    ospec = pl.BlockSpec((TQ_LAT, 512), lambda b, i: (b * nq + i, 0))
    names = ("sq", "sk", "sv", "mq", "mk", "mv", "gq", "gk", "gv")
    return pl.pallas_call(
        _attn_lat_kernel,
        out_shape=[jax.ShapeDtypeStruct((N_LAT, 512), BF16)] * 3,
        grid=(DEC_BATCH, nq),
        in_specs=in_specs,
        out_specs=[ospec] * 3,
        scratch_shapes=[pltpu.VMEM((N_KV, PAST_LEN, LANES), BF16), pltpu.VMEM((N_KV, PAST_LEN, LANES), BF16),
                        pltpu.VMEM((PAST_LEN, 1024), BF16), pltpu.VMEM((PAST_LEN, 512), BF16),
                        pltpu.VMEM((N_KV, PAST_LEN, LANES), BF16), pltpu.VMEM((N_KV, PAST_LEN, LANES), BF16)],
        compiler_params=pltpu.CompilerParams(dimension_semantics=("arbitrary", "arbitrary"),
                                             vmem_limit_bytes=VMEM_LIMIT_BYTES),
        name="attn_lat",
    )(sink, lp["w_k"], lp["w_uv"], *[p[n] for n in names], *caches)


def _merge_kernel(x_ref, mod_ref, oa_ref, ob_ref, oc_ref, od_ref, gates_ref, wb_ref, wo_ref, npost_ref,
                  nmpre_ref, nmpost_ref, w1_ref, w2_ref, y_ref):
    x = x_ref[...]
    g1 = mod_ref[0, 2]
    sh2 = mod_ref[0, 3]
    sc2 = mod_ref[0, 4]
    g2 = mod_ref[0, 5]
    acc = None
    for n, o_ref in enumerate((oa_ref, ob_ref, oc_ref, od_ref)):
        br = _dot(o_ref[...], wb_ref[n]) * gates_ref[:, D_MODEL * n:D_MODEL * (n + 1)].astype(F32)
        acc = br if acc is None else acc + br
    y = _dot(acc.astype(BF16), wo_ref[...])
    x = x + g1 * _rms(y, npost_ref[...])
    h = (_rms(x, nmpre_ref[...]) * (1.0 + sc2) + sh2).astype(BF16)
    a = jnp.maximum(_dot(h, w1_ref[...]), 0.0)
    y = _dot((a * a).astype(BF16), w2_ref[...])
    y_ref[...] = x + g2 * _rms(y, nmpost_ref[...])


def _merge(x, mod4, o_a, o_b, o_c, o_d, gates, lp):
    def tok(w):
        return pl.BlockSpec((TM, w), lambda i: (i, 0))

    in_specs = [tok(D_MODEL), pl.BlockSpec((1, N_MOD, 1, D_MODEL), lambda i: (_mod_row(i), 0, 0, 0)),
                tok(512), tok(512), tok(512), tok(512), tok(4096),
                _const_spec((N_BRANCH, BRANCH_W, D_MODEL)), _const_spec((D_MODEL, D_MODEL)),
                _const_spec((1, D_MODEL)), _const_spec((1, D_MODEL)), _const_spec((1, D_MODEL)),
                _const_spec((D_MODEL, D_FF)), _const_spec((D_FF, D_MODEL))]
    return pl.pallas_call(
        _merge_kernel, out_shape=jax.ShapeDtypeStruct((N_TOK, D_MODEL), F32), grid=(N_TILES,),
        in_specs=in_specs, out_specs=tok(D_MODEL),
        compiler_params=pltpu.CompilerParams(dimension_semantics=("arbitrary",), vmem_limit_bytes=VMEM_LIMIT_BYTES),
        name="merge",
    )(x, mod4, o_a, o_b, o_c, o_d, gates, lp["w_branch"], lp["w_out"], lp["norm_mix_post"], lp["norm_mlp_pre"],
      lp["norm_mlp_post"], lp["w_mlp_in"], lp["w_mlp_out"])


def _rope_tables():
    t = np.arange(DEC_SEQ)
    row, col = (t // GRID_W).astype(np.float32), (t % GRID_W).astype(np.float32)

    def pattern(d):
        nf = d // 4
        inv = (ROPE_BASE ** (-np.arange(nf, dtype=np.float32) / nf)).astype(np.float32)
        ang = np.concatenate([row[:, None] * inv, row[:, None] * inv, col[:, None] * inv, col[:, None] * inv], axis=1)
        cos, sin = np.cos(ang), np.sin(ang)
        second = (np.arange(d) % (2 * nf)) >= nf
        return cos, np.where(second, sin, 0.0), np.where(second, 0.0, -sin)

    out = np.zeros((9, DEC_SEQ, LANES), np.float32)
    out[[0, 3, 6]] = 1.0
    c, s1, s2 = pattern(HEAD_DIM)
    for a, tab in enumerate((c, s1, s2)):
        out[_TAB_64 + a] = np.tile(tab, (1, LANES // HEAD_DIM))
    c, s1, s2 = pattern(MLA_ROPE)
    for a, tab in enumerate((c, s1, s2)):
        out[_TAB_MLAQ + a][:, MLA_NOPE:MLA_NOPE + MLA_ROPE] = tab
        out[_TAB_KR + a][:, :MLA_ROPE] = tab
    ident = np.zeros((9, TM, LANES), np.float32)
    ident[[0, 3, 6]] = 1.0
    return np.concatenate([ident, out], axis=1)


_ROPE_TABLES = _rope_tables()


def _block_mean_matrix(n):
    i = np.arange(n)
    return ((i[:, None] // HEAD_DIM) == (i[None, :] // HEAD_DIM)).astype(np.float32) / HEAD_DIM


def _mla_weight_layouts(w_uq, w_ukv):
    dq = MLA_NOPE + MLA_ROPE
    wq = w_uq.reshape(MLA_Q_RANK, MLA_HEADS, dq)
    wq = jnp.pad(wq, ((0, 0), (0, 0), (0, LANES - dq))).reshape(MLA_Q_RANK, MLA_HEADS * LANES)
    wkv = w_ukv.reshape(MLA_KV_RANK, MLA_HEADS, MLA_NOPE + MLA_V)
    wk = jnp.pad(wkv[:, :, :MLA_NOPE], ((0, 0), (0, 0), (0, LANES - MLA_NOPE))).reshape(MLA_KV_RANK, MLA_HEADS * LANES)
    place = np.zeros((LANES, MLA_HEADS, LANES), np.float32)
    for j in range(MLA_ROPE):
        place[j, :, MLA_NOPE + j] = 1.0
    w_k = jnp.concatenate([wk, jnp.asarray(place.reshape(LANES, MLA_HEADS * LANES))], axis=0)
    w_uv = wkv[:, :, MLA_NOPE:].reshape(MLA_KV_RANK, MLA_HEADS * MLA_V)
    return wq.astype(BF16), w_k.astype(BF16), w_uv.astype(BF16)


def _lower_bounds(p):
    s = jax.nn.softmax(p.astype(F32), axis=0)
    return jnp.cumsum(s, axis=0) - s[0]


def kernel(x_prompt, x_sample, state_hgrn, cache_swa_k, cache_swa_v, cache_mla_ckv, cache_mla_kr, cache_gqa_k, cache_gqa_v, c, c_ctx, w_ada, b_ada, norm_mix_pre, norm_mix_post, norm_mlp_pre, norm_mlp_post, w_in, hgrn_lb_fwd, hgrn_lb_bwd, hgrn_norm, swa_sink, mla_q_norm, mla_kv_norm, mla_w_uq, mla_w_ukv, gqa_q_norm, gqa_k_norm, w_branch, w_out, w_mlp_in, w_mlp_out):
    lb_fwd = _lower_bounds(hgrn_lb_fwd)
    lb_bwd = _lower_bounds(hgrn_lb_bwd)
    perm = jnp.asarray(np.maximum(_W_IN_PERM, 0))
    keep = jnp.asarray((_W_IN_PERM >= 0).astype(np.float32))
    tabs = jnp.asarray(_ROPE_TABLES)
    b512 = jnp.asarray(_block_mean_matrix(512)).astype(BF16)

    cond8 = jnp.zeros((SUBLANES, D_MODEL), F32).at[0].set(c_ctx).at[1:1 + DEC_BATCH].set(c)
    mod = _ada(cond8, w_ada, b_ada)

    caches = (cache_swa_k.reshape(DEC_BATCH, DEPTH, PAST_LEN, 128), cache_swa_v.reshape(DEC_BATCH, DEPTH, PAST_LEN, 128),
              cache_mla_ckv, cache_mla_kr,
              cache_gqa_k.reshape(DEC_BATCH, DEPTH, PAST_LEN, 128), cache_gqa_v.reshape(DEC_BATCH, DEPTH, PAST_LEN, 128))

    x = jnp.concatenate([x_prompt.reshape(N_CTX, D_MODEL), x_sample.reshape(N_LAT, D_MODEL)], axis=0)
    per_layer = []
    for l in range(DEPTH):
        w_uq, w_k, w_uv = _mla_weight_layouts(mla_w_uq[l], mla_w_ukv[l])
        lp = dict(
            w_in=(w_in[l][:, perm] * keep).astype(BF16),
            norm_mix_pre=norm_mix_pre[l][None], norm_mix_post=norm_mix_post[l][None],
            norm_mlp_pre=norm_mlp_pre[l][None], norm_mlp_post=norm_mlp_post[l][None],
            mla_q_norm=mla_q_norm[l][None], mla_kv_norm=mla_kv_norm[l][None], w_uq=w_uq, w_k=w_k, w_uv=w_uv,
            gqa_q_gain=jnp.tile(gqa_q_norm[l], N_HEADS)[None], gqa_k_gain=jnp.tile(gqa_k_norm[l], 4)[None],
            b512=b512, b256=b512[:256, :256], b128=b512[:128, :128],
            lb_f=lb_fwd[l][None], lb_b=lb_bwd[l][None], hgrn_norm=hgrn_norm[l][None],
            w_branch=w_branch[l].astype(BF16), w_out=w_out[l].astype(BF16),
            w_mlp_in=w_mlp_in[l].astype(BF16), w_mlp_out=w_mlp_out[l].astype(BF16))
        sink = swa_sink[l][None]
        mod4 = mod[l].reshape(SUBLANES, N_MOD, 1, D_MODEL)

        (hq, hff, hfb, hi, hg, sq, sk, sv, skc, svc, mq, mk, mv, mckv, mkr, gq, gk, gv, gkc, gvc, gates) = _proj(
            x, mod4, lp, tabs)
        pre = (hq, hff, hfb, hi, hg)
        oa_ctx, st_ctx = _hgrn(pre, lp, 0, BATCH, SEQ)
        (oa_lat,) = _hgrn(pre, lp, N_CTX, DEC_BATCH, DEC_SEQ, state_in=state_hgrn, layer=l)
        p = dict(sq=sq, sk=sk, sv=sv, mq=mq, mk=mk, mv=mv, gq=gq, gk=gk, gv=gv)
        ob_c, oc_c, od_c = _attn_ctx(sink, p)
        ob_l, oc_l, od_l = _attn_lat(sink, lp, p, caches, l)
        o_a = jnp.concatenate([oa_ctx, oa_lat], axis=0)
        o_b = jnp.concatenate([ob_c, ob_l], axis=0)
        o_c = jnp.concatenate([oc_c, oc_l], axis=0)
        o_d = jnp.concatenate([od_c, od_l], axis=0)
        x = _merge(x, mod4, o_a, o_b, o_c, o_d, gates, lp)
        per_layer.append((st_ctx, skc[:N_CTX], svc[:N_CTX], mckv[:N_CTX], mkr[:N_CTX, :MLA_ROPE], gkc[:N_CTX],
                          gvc[:N_CTX]))

    def stack(idx, shape):
        return jnp.stack([per_layer[l][idx].reshape(shape) for l in range(DEPTH)], axis=1)

    y_prompt = x[:N_CTX].reshape(BATCH, SEQ, D_MODEL)
    y_sample = x[N_CTX:].reshape(DEC_BATCH, DEC_SEQ, D_MODEL)
    kv_shape = (BATCH, SEQ, N_KV, HEAD_DIM)
    return (y_prompt, y_sample,
            stack(0, (BATCH, 2, HGRN_HEADS, HGRN_DK, LANES)),
            stack(1, kv_shape), stack(2, kv_shape),
            stack(3, (BATCH, SEQ, MLA_KV_RANK)), stack(4, (BATCH, SEQ, MLA_ROPE)),
            stack(5, kv_shape), stack(6, kv_shape))
```

```python
import functools

import jax
import jax.numpy as jnp
import numpy as np
from jax import lax
from jax.experimental import pallas as pl
from jax.experimental.pallas import tpu as pltpu

D_MODEL = 1024
BATCH, SEQ = 32, 256
DEC_BATCH, DEC_SEQ = 2, 2048
DEPTH = 2
PAST_LEN = 512
GRID_W = 64
ROPE_BASE = 10000.0
NORM_EPS = 1e-6
MASK_VALUE = -1e30
LOG_FLOOR = 1e-30
N_MOD = 6
N_BRANCH = 4
BRANCH_W = 512
D_FF = 4 * D_MODEL
HGRN_HEADS, HGRN_DK = 4, 128
SWA_WINDOW = 128
MLA_HEADS, MLA_Q_RANK, MLA_KV_RANK, MLA_NOPE, MLA_ROPE, MLA_V = 8, 256, 128, 64, 32, 64
HEAD_DIM = 64
N_HEADS, N_KV = 8, 2

N_CTX = BATCH * SEQ
N_LAT = DEC_BATCH * DEC_SEQ

LANES = 128
SUBLANES = 8
VMEM_LIMIT_BYTES = 56 * 1024 * 1024

TM = 256
LAT_TILES_PER_SEQ = DEC_SEQ // TM
HGRN_CHUNK = 64
TQ_LAT = 128

BF16 = jnp.bfloat16
F32 = jnp.float32

_C_HGRN = 0
_C_SWA_Q = 2560
_C_SWA_KV = 3072
_C_SWA_KDUP = 3328
_C_SWA_VDUP = 3584
_C_MLA_CQ = 3840
_C_MLA_CKV = 4096
_C_MLA_KR = 4224
_C_GQA_Q = 4352
_C_GQA_KV = 4864
_C_GQA_KDUP = 5120
_C_GQA_VDUP = 5376
_C_GATES = 5632
W_IN_COLS = _C_GATES + N_BRANCH * D_MODEL


def _w_in_segments():
    off, o = {}, 0
    for name, w in (("hgrn", 2560), ("swa_q", 512), ("swa_k", 128), ("swa_v", 128), ("mla_cq", 256),
                    ("mla_ckv", 128), ("mla_kr", 32), ("gqa_q", 512), ("gqa_k", 128), ("gqa_v", 128),
                    ("gates", 4096)):
        off[name] = o
        o += w

    def dup(base):
        return [(base, 64), (base, 64), (base + 64, 64), (base + 64, 64)]

    segs = ([(off["hgrn"], 2560), (off["swa_q"], 512), (off["swa_k"], 256)] + dup(off["swa_k"]) + dup(off["swa_v"])
            + [(off["mla_cq"], 256), (off["mla_ckv"], 128), (off["mla_kr"], 32), (-1, 96), (off["gqa_q"], 512),
               (off["gqa_k"], 256)] + dup(off["gqa_k"]) + dup(off["gqa_v"]) + [(off["gates"], 4096)])
    assert sum(w for _, w in segs) == W_IN_COLS
    return segs


_W_IN_SEGMENTS = _w_in_segments()


def _permute_w_in(w):
    parts = [jnp.zeros((D_MODEL, n), BF16) if a < 0 else w[:, a:a + n].astype(BF16) for a, n in _W_IN_SEGMENTS]
    return jnp.concatenate(parts, axis=1)


def _sigmoid_pair(x):
    a = jnp.exp(-jnp.abs(x))
    r = 1.0 / (1.0 + a)
    ar = a * r
    pos = x >= 0
    return jnp.where(pos, r, ar), jnp.where(pos, ar, r)


def _sigmoid(x):
    return _sigmoid_pair(x)[0]


def _silu(x):
    return x * _sigmoid(x)


def _rms(x, gain):
    return x * lax.rsqrt(jnp.mean(x * x, axis=-1, keepdims=True) + NORM_EPS) * gain


def _dot(a, b):
    return jnp.dot(a, b, preferred_element_type=F32)


def _dot_nt(a, b):
    return lax.dot_general(a, b, (((1,), (1,)), ((), ())), preferred_element_type=F32)


def _dot_tn(a, b):
    return lax.dot_general(a, b, (((0,), (0,)), ((), ())), preferred_element_type=F32)


def _tile_lanes(t, width):
    reps = width // LANES
    return t if reps == 1 else jnp.concatenate([t] * reps, axis=1)


def _rope(x, tab_ref, base, shift):
    w = x.shape[1]
    c = _tile_lanes(tab_ref[base], w)
    s1 = _tile_lanes(tab_ref[base + 1], w)
    s2 = _tile_lanes(tab_ref[base + 2], w)
    return x * c + pltpu.roll(x, shift, 1) * s1 + pltpu.roll(x, w - shift, 1) * s2


def _head_rms(x, bmat_ref, gain):
    ms = _dot((x * x).astype(BF16), bmat_ref[...])
    return x * lax.rsqrt(ms + NORM_EPS) * gain


def _const_spec(shape):
    nd = len(shape)
    return pl.BlockSpec(shape, lambda *_: (0,) * nd, pipeline_mode=pl.Buffered(1))


_ANY_SPEC = pl.BlockSpec(memory_space=pl.ANY)


def _params(n_grid_dims):
    return pltpu.CompilerParams(dimension_semantics=("arbitrary",) * n_grid_dims, vmem_limit_bytes=VMEM_LIMIT_BYTES)


_ADA_TN = 2048


def _ada_kernel(c_ref, w_ref, b_ref, o_ref):
    s = _silu(c_ref[...]).astype(BF16)
    o_ref[0] = _dot(s, w_ref[0].astype(BF16)) + b_ref[0]


def _ada(cond8, w_ada, b_ada):
    n = N_MOD * D_MODEL
    return pl.pallas_call(
        _ada_kernel,
        out_shape=jax.ShapeDtypeStruct((DEPTH, SUBLANES, n), F32),
        grid=(DEPTH, n // _ADA_TN),
        in_specs=[
            pl.BlockSpec((SUBLANES, D_MODEL), lambda l, j: (0, 0)),
            pl.BlockSpec((1, D_MODEL, _ADA_TN), lambda l, j: (l, 0, j)),
            pl.BlockSpec((1, 1, _ADA_TN), lambda l, j: (l, 0, j)),
        ],
        out_specs=pl.BlockSpec((1, SUBLANES, _ADA_TN), lambda l, j: (l, 0, j)),
        compiler_params=_params(2),
        name="ada",
    )(cond8, w_ada, b_ada.reshape(DEPTH, 1, n))


def _mod_spec(is_ctx):
    if is_ctx:
        return pl.BlockSpec((1, N_MOD, 1, D_MODEL), lambda i: (0, 0, 0, 0))
    return pl.BlockSpec((1, N_MOD, 1, D_MODEL), lambda i: (1 + i // LAT_TILES_PER_SEQ, 0, 0, 0))


_SCALE_64 = HEAD_DIM ** -0.5
_SCALE_MLA = (MLA_NOPE + MLA_ROPE) ** -0.5
_TAB_64, _TAB_MLAQ, _TAB_KR = 0, 3, 6

_PROJ_OUTS = (
    ("hq", 512, F32), ("hff", 512, F32), ("hfb", 512, F32), ("hi", 512, F32), ("hg", 512, F32),
    ("sq", 512, BF16), ("sk", 256, BF16), ("sv", 256, BF16),
    ("mq", 1024, BF16), ("mk", 1024, BF16), ("mv", 512, BF16),
    ("gq", 512, BF16), ("gk", 256, BF16), ("gv", 256, BF16), ("gates", 4096, BF16))
_CACHE_OUTS = (("skc", 128), ("svc", 128), ("mckv", 128), ("mkr", MLA_ROPE), ("gkc", 128), ("gvc", 128))
_PROJ_CONSTS = ("npre", "w", "mlaq_g", "mlakv_g", "wuq", "wk", "wuv", "gq_g", "gk_g", "b512", "b256", "b128")


def _proj_kernel(*refs, is_ctx, n_alias):
    names = ["x", "mod"] + list(_PROJ_CONSTS) + ([] if is_ctx else ["tab"])
    r = dict(zip(names, refs))
    pos = len(names) + n_alias
    for name, _, _ in _PROJ_OUTS:
        r[name] = refs[pos]
        pos += 1
    if is_ctx:
        for name, _ in _CACHE_OUTS:
            r[name] = refs[pos]
            pos += 1

    x = r["x"][...]
    sh1 = r["mod"][0, 0]
    sc1 = r["mod"][0, 1]
    h = (_rms(x, r["npre"][...]) * (1.0 + sc1) + sh1).astype(BF16)
    w_ref = r["w"]

    def seg(a, w):
        return _dot(h, w_ref[:, a:a + w])

    def rope(v, base, shift):
        return v if is_ctx else _rope(v, r["tab"], base, shift)

    for j, name in enumerate(("hq", "hff", "hfb", "hi", "hg")):
        r[name][...] = seg(_C_HGRN + 512 * j, 512)

    r["sq"][...] = (rope(seg(_C_SWA_Q, 512), _TAB_64, 16) * _SCALE_64).astype(BF16)
    r["sk"][...] = rope(seg(_C_SWA_KDUP, 256), _TAB_64, 16).astype(BF16)
    r["sv"][...] = seg(_C_SWA_VDUP, 256).astype(BF16)
    if is_ctx:
        kv = seg(_C_SWA_KV, 256)
        r["skc"][0, 0] = kv[:, :128]
        r["svc"][0, 0] = kv[:, 128:]

    cq = _rms(seg(_C_MLA_CQ, 256), r["mlaq_g"][...]).astype(BF16)
    q = _dot(cq, r["wuq"][...])
    r["mq"][...] = (rope(q, _TAB_MLAQ, 8) * _SCALE_MLA).astype(BF16)
    ckv = _rms(seg(_C_MLA_CKV, 128), r["mlakv_g"][...])
    kr = seg(_C_MLA_KR, 128)
    if is_ctx:
        r["mckv"][0, 0] = ckv
        r["mkr"][0, 0] = kr[:, :MLA_ROPE]
    kr_used = rope(kr, _TAB_KR, 8)
    ckv16 = ckv.astype(BF16)
    r["mk"][...] = _dot(jnp.concatenate([ckv16, kr_used.astype(BF16)], axis=1), r["wk"][...]).astype(BF16)
    r["mv"][...] = _dot(ckv16, r["wuv"][...]).astype(BF16)

    gq = _head_rms(seg(_C_GQA_Q, 512), r["b512"], r["gq_g"][...])
    r["gq"][...] = (rope(gq, _TAB_64, 16) * _SCALE_64).astype(BF16)
    gk = _head_rms(seg(_C_GQA_KDUP, 256), r["b256"], r["gk_g"][...])
    r["gk"][...] = rope(gk, _TAB_64, 16).astype(BF16)
    r["gv"][...] = seg(_C_GQA_VDUP, 256).astype(BF16)
    if is_ctx:
        kv = seg(_C_GQA_KV, 256)
        r["gkc"][0, 0] = _head_rms(kv[:, :128], r["b128"], r["gk_g"][:, :128])
        r["gvc"][0, 0] = kv[:, 128:]

    for j in range(N_BRANCH):
        r["gates"][:, D_MODEL * j:D_MODEL * (j + 1)] = _sigmoid(seg(_C_GATES + D_MODEL * j, D_MODEL)).astype(BF16)


def _proj(x, mod4, lp, is_ctx, layer=0, tabs=None, caches_prev=None):
    n_tok = x.shape[0]
    const_shapes = dict(npre=(1, D_MODEL), w=(D_MODEL, W_IN_COLS), mlaq_g=(1, MLA_Q_RANK), mlakv_g=(1, MLA_KV_RANK),
                        wuq=(MLA_Q_RANK, 1024), wk=(256, 1024), wuv=(MLA_KV_RANK, 512), gq_g=(1, 512), gk_g=(1, 256),
                        b512=(512, 512), b256=(256, 256), b128=(128, 128))
    const_args = dict(npre=lp["norm_mix_pre"], w=lp["w_in"], mlaq_g=lp["mla_q_norm"], mlakv_g=lp["mla_kv_norm"],
                      wuq=lp["w_uq"], wk=lp["w_k"], wuv=lp["w_uv"], gq_g=lp["gqa_q_gain"], gk_g=lp["gqa_k_gain"],
                      b512=lp["b512"], b256=lp["b256"], b128=lp["b128"])
    in_specs = [pl.BlockSpec((TM, D_MODEL), lambda i: (i, 0)), _mod_spec(is_ctx)]
    in_specs += [_const_spec(const_shapes[n]) for n in _PROJ_CONSTS]
    args = [x, mod4] + [const_args[n] for n in _PROJ_CONSTS]
    if not is_ctx:
        in_specs.append(pl.BlockSpec((9, TM, LANES), lambda i: (0, i % LAT_TILES_PER_SEQ, 0)))
        args.append(tabs)
    aliases = {}
    n_alias = 0
    if caches_prev is not None:
        n_alias = len(caches_prev)
        for j, c in enumerate(caches_prev):
            aliases[len(args)] = len(_PROJ_OUTS) + j
            in_specs.append(_ANY_SPEC)
            args.append(c)
    out_specs = [pl.BlockSpec((TM, w), lambda i: (i, 0)) for _, w, _ in _PROJ_OUTS]
    out_shape = [jax.ShapeDtypeStruct((n_tok, w), dt) for _, w, dt in _PROJ_OUTS]
    if is_ctx:
        out_specs += [pl.BlockSpec((1, 1, SEQ, w), lambda i: (i, layer, 0, 0)) for _, w in _CACHE_OUTS]
        out_shape += [jax.ShapeDtypeStruct((BATCH, DEPTH, SEQ, w), F32) for _, w in _CACHE_OUTS]
    outs = pl.pallas_call(
        functools.partial(_proj_kernel, is_ctx=is_ctx, n_alias=n_alias),
        out_shape=out_shape, grid=(n_tok // TM,), in_specs=in_specs, out_specs=out_specs,
        input_output_aliases=aliases, compiler_params=_params(1),
        name="proj_ctx" if is_ctx else "proj_lat",
    )(*args)
    p = {name: o for (name, _, _), o in zip(_PROJ_OUTS, outs)}
    return p, tuple(outs[len(_PROJ_OUTS):])


def _hgrn_chunk(q, k, lf, v, st_ref, d, fwd):
    C = HGRN_CHUNK
    nv = C // SUBLANES
    q3, k3, v3 = (a.reshape(nv, SUBLANES, LANES) for a in (q, k, v))
    sub = lax.broadcasted_iota(jnp.int32, (nv, SUBLANES, LANES), 1)

    p = lf.reshape(nv, SUBLANES, LANES)
    for s in (1, 2, 4):
        if fwd:
            p = p + jnp.where(sub >= s, pltpu.roll(p, s, 1), 0.0)
        else:
            p = p + jnp.where(sub < SUBLANES - s, pltpu.roll(p, SUBLANES - s, 1), 0.0)

    o = jnp.zeros((nv, SUBLANES, LANES), F32)
    for off in range(SUBLANES):
        if off == 0:
            w = q3 * k3
            vd = v3
        else:
            sh = off if fwd else SUBLANES - off
            valid = (sub >= off) if fwd else (sub < SUBLANES - off)
            e = jnp.exp(jnp.where(valid, p - pltpu.roll(p, sh, 1), MASK_VALUE))
            w = q3 * pltpu.roll(k3, sh, 1) * e
            vd = pltpu.roll(v3, sh, 1)
        o = o + jnp.sum(w, axis=-1, keepdims=True) * vd
    o = o.reshape(C, LANES)

    p = p.reshape(C, LANES)
    ti = lax.broadcasted_iota(jnp.int32, (C, C), 0)
    si = lax.broadcasted_iota(jnp.int32, (C, C), 1)
    xor = ti ^ si
    causal = (ti > si) if fwd else (ti < si)
    a_mat = jnp.zeros((C, C), F32)
    m = SUBLANES
    while m < C:
        nb = C // (2 * m)
        p4 = p.reshape(nb, 2, m, LANES)
        lo, hi = p4[:, 0], p4[:, 1]
        if fwd:
            tot = lo[:, m - 1:m]
            e_lo, e_hi = tot - lo, hi
            p_lo, p_hi = lo, hi + tot
        else:
            tot = hi[:, 0:1]
            e_lo, e_hi = lo, tot - hi
            p_lo, p_hi = lo + tot, hi
        e = jnp.exp(jnp.concatenate([e_lo[:, None], e_hi[:, None]], axis=1).reshape(C, LANES))
        p = jnp.concatenate([p_lo[:, None], p_hi[:, None]], axis=1).reshape(C, LANES)
        a_m = _dot_nt((q * e).astype(BF16), (k * e).astype(BF16))
        a_mat = jnp.where(((xor >> int(np.log2(m))) == 1) & causal, a_m, a_mat)
        m *= 2
    v16 = v.astype(BF16)
    o = o + _dot(a_mat.astype(BF16), v16)

    st = st_ref[d]
    o = o + _dot_nt((q * jnp.exp(p)).astype(BF16), st.astype(BF16))
    tot = p[C - 1:C] if fwd else p[0:1]
    k_end = (k * jnp.exp(tot - p)).astype(BF16)
    st_ref[d] = st * jnp.exp(tot) + _dot_tn(v16, k_end)
    return o


def _hgrn_kernel(*refs, n_chunks, has_init, n_alias):
    lbf_ref, lbb_ref, gn_ref, q_ref, ff_ref, fb_ref, v_ref, g_ref = refs[:8]
    pos = 8
    s0_ref = None
    if has_init:
        s0_ref = refs[pos]
        pos += 1
    pos += n_alias
    o_ref = refs[pos]
    pos += 1
    sfin_ref = None
    if not has_init:
        sfin_ref = refs[pos]
        pos += 1
    of_ref, ob_ref, st_ref = refs[pos:pos + 3]
    C = HGRN_CHUNK

    for d in range(2):
        if has_init:
            st_ref[d] = s0_ref[0, 0, d, 0].T
        else:
            st_ref[d] = jnp.zeros((LANES, LANES), F32)
    lbf = lbf_ref[...]
    lbb = lbb_ref[...]

    def forget(f_pre, lb):
        s_pos, s_neg = _sigmoid_pair(f_pre)
        f = lb + (1.0 - lb) * s_pos
        return (1.0 - lb) * s_neg, jnp.log(jnp.maximum(f, LOG_FLOOR))

    def body(c, carry):
        rf = pl.multiple_of(c * C, C)
        rb = pl.multiple_of((n_chunks - 1 - c) * C, C)
        for d, r0, f_ref, lb, out_ref in ((0, rf, ff_ref, lbf, of_ref), (1, rb, fb_ref, lbb, ob_ref)):
            sl = pl.ds(r0, C)
            q = _silu(q_ref[sl, :])
            k, lf = forget(f_ref[sl, :], lb)
            out_ref[sl, :] = _hgrn_chunk(q, k, lf, v_ref[sl, :], st_ref, d, fwd=(d == 0))
        return carry

    lax.fori_loop(0, n_chunks, body, 0)
    o = of_ref[...] + ob_ref[...]
    o = _rms(o, gn_ref[...]) * _silu(g_ref[...])
    o_ref[...] = o.astype(BF16)
    if sfin_ref is not None:
        for d in range(2):
            sfin_ref[0, 0, d, 0] = st_ref[d].T


def _hgrn(p, lp, n_seq, seq_len, layer, state_in=None, state_prev=None):
    tok_spec = pl.BlockSpec((seq_len, LANES), lambda b, h: (b, h))
    vec_spec = pl.BlockSpec((1, LANES), lambda b, h: (0, h))
    in_specs = [vec_spec, vec_spec, vec_spec, tok_spec, tok_spec, tok_spec, tok_spec, tok_spec]
    args = [lp["lb_f"], lp["lb_b"], lp["hgrn_norm"], p["hq"], p["hff"], p["hfb"], p["hi"], p["hg"]]
    has_init = state_in is not None
    state_spec = pl.BlockSpec((1, 1, 2, 1, LANES, LANES), lambda b, h: (b, layer, 0, h, 0, 0))
    aliases = {}
    n_alias = 0
    if has_init:
        in_specs.append(state_spec)
        args.append(state_in)
    elif state_prev is not None:
        aliases[len(args)] = 1
        n_alias = 1
        in_specs.append(_ANY_SPEC)
        args.append(state_prev)
    out_shape = [jax.ShapeDtypeStruct((n_seq * seq_len, 512), BF16)]
    out_specs = [tok_spec]
    if not has_init:
        out_shape.append(jax.ShapeDtypeStruct((n_seq, DEPTH, 2, HGRN_HEADS, LANES, LANES), F32))
        out_specs.append(state_spec)
    kern = functools.partial(_hgrn_kernel, n_chunks=seq_len // HGRN_CHUNK, has_init=has_init, n_alias=n_alias)
    return pl.pallas_call(
        kern, out_shape=out_shape, grid=(n_seq, HGRN_HEADS), in_specs=in_specs, out_specs=out_specs,
        scratch_shapes=[pltpu.VMEM((seq_len, LANES), F32), pltpu.VMEM((seq_len, LANES), F32),
                        pltpu.VMEM((2, LANES, LANES), F32)],
        input_output_aliases=aliases, compiler_params=_params(2),
        name="hgrn_lat" if has_init else "hgrn_ctx",
    )(*args)


def _lane_lo(shape):
    return lax.broadcasted_iota(jnp.int32, shape, len(shape) - 1) < HEAD_DIM


def _softmax_pv(segs, sink_col=None):
    mx = None
    for s, _ in segs:
        m = jnp.max(s, axis=-1, keepdims=True)
        mx = m if mx is None else jnp.maximum(mx, m)
    if sink_col is not None:
        mx = jnp.maximum(mx, sink_col)
    den = None
    acc = None
    for s, v in segs:
        e = jnp.exp(s - mx)
        dsum = jnp.sum(e, axis=-1, keepdims=True)
        den = dsum if den is None else den + dsum
        pv = _dot(e.astype(BF16), v)
        acc = pv if acc is None else acc + pv
    if sink_col is not None:
        den = den + jnp.exp(sink_col - mx)
    return acc / den


def _gqa_group(q_ref, g, tq, segs_fn, sink_ref=None):
    lo = _lane_lo((tq, LANES))
    parts = []
    for j in range(2):
        qp = q_ref[:, 256 * g + 128 * j:256 * g + 128 * (j + 1)]
        zero = jnp.zeros_like(qp)
        parts += [jnp.where(lo, qp, zero), jnp.where(lo, zero, qp)]
    q4 = jnp.concatenate(parts, axis=0)
    sink_col = None
    if sink_ref is not None:
        sink_col = jnp.concatenate(
            [jnp.broadcast_to(sink_ref[0:1, 4 * g + i:4 * g + i + 1], (tq, 1)) for i in range(4)], axis=0)
    pv = _softmax_pv(segs_fn(q4), sink_col)
    return (jnp.where(lo, pv[0:tq], pv[tq:2 * tq]), jnp.where(lo, pv[2 * tq:3 * tq], pv[3 * tq:4 * tq]))


def _mla_heads(q_ref, tq, segs_fn, o_ref):
    lo = _lane_lo((tq, LANES))
    for hp in range(MLA_HEADS // 2):
        outs = []
        for h in (2 * hp, 2 * hp + 1):
            q_h = q_ref[:, 128 * h:128 * (h + 1)]
            outs.append(_softmax_pv(segs_fn(h, q_h)))
        o_ref[:, 128 * hp:128 * (hp + 1)] = jnp.where(lo, outs[0], outs[1]).astype(BF16)


_ATTN_IN = ("sq", "sk", "sv", "mq", "mk", "mv", "gq", "gk", "gv")
_ATTN_WIDTH = dict(sq=512, sk=256, sv=256, mq=1024, mk=1024, mv=512, gq=512, gk=256, gv=256)


def _attn_ctx_kernel(sink_ref, sq_ref, sk_ref, sv_ref, mq_ref, mk_ref, mv_ref, gq_ref, gk_ref, gv_ref,
                     ob_ref, oc_ref, od_ref):
    tq = SEQ
    for g in range(N_KV):
        def swa_segs(q4, g=g):
            return [(_dot_nt(q4, sk_ref[:, 128 * g:128 * (g + 1)]), sv_ref[:, 128 * g:128 * (g + 1)])]

        a, b = _gqa_group(sq_ref, g, tq, swa_segs, sink_ref)
        ob_ref[:, 256 * g:256 * g + 128] = a.astype(BF16)
        ob_ref[:, 256 * g + 128:256 * (g + 1)] = b.astype(BF16)

        def gqa_segs(q4, g=g):
            return [(_dot_nt(q4, gk_ref[:, 128 * g:128 * (g + 1)]), gv_ref[:, 128 * g:128 * (g + 1)])]

        a, b = _gqa_group(gq_ref, g, tq, gqa_segs)
        od_ref[:, 256 * g:256 * g + 128] = a.astype(BF16)
        od_ref[:, 256 * g + 128:256 * (g + 1)] = b.astype(BF16)

    def mla_segs(h, q_h):
        return [(_dot_nt(q_h, mk_ref[:, 128 * h:128 * (h + 1)]), mv_ref[:, 128 * (h // 2):128 * (h // 2 + 1)])]

    _mla_heads(mq_ref, tq, mla_segs, oc_ref)


def _attn_ctx(sink, p):
    def spec(w):
        return pl.BlockSpec((SEQ, w), lambda b: (b, 0))

    return pl.pallas_call(
        _attn_ctx_kernel,
        out_shape=[jax.ShapeDtypeStruct((N_CTX, 512), BF16)] * 3,
        grid=(BATCH,),
        in_specs=[_const_spec((1, N_HEADS))] + [spec(_ATTN_WIDTH[n]) for n in _ATTN_IN],
        out_specs=[spec(512)] * 3,
        compiler_params=_params(1),
        name="attn_ctx",
    )(sink, *[p[n] for n in _ATTN_IN])


def _dup_heads(x):
    lo = _lane_lo(x.shape)
    r = pltpu.roll(x, HEAD_DIM, 1)
    return jnp.where(lo, x, r), jnp.where(lo, r, x)


def _attn_lat_kernel(sink_ref, wk_ref, wuv_ref,
                     sq_ref, sk_ref, sv_ref, mq_ref, mk_ref, mv_ref, gq_ref, gk_ref, gv_ref,
                     csk_ref, csv_ref, cckv_ref, ckr_ref, cgk_ref, cgv_ref,
                     ob_ref, oc_ref, od_ref,
                     cs_k, cs_v, cm_k, cm_v, cg_k, cg_v):
    tq = TQ_LAT
    qi = pl.program_id(1)

    @pl.when(qi == 0)
    def _():
        for src, dst in ((csk_ref, cs_k), (csv_ref, cs_v), (cgk_ref, cg_k), (cgv_ref, cg_v)):
            d0, d1 = _dup_heads(src[0, 0])
            dst[0] = d0.astype(BF16)
            dst[1] = d1.astype(BF16)
        ckv16 = cckv_ref[0, 0].astype(BF16)
        k = _dot(ckv16, wk_ref[0:MLA_KV_RANK, :]) + _dot(ckr_ref[0, 0].astype(BF16),
                                                           wk_ref[MLA_KV_RANK:MLA_KV_RANK + MLA_ROPE, :])
        cm_k[...] = k.astype(BF16)
        cm_v[...] = _dot(ckv16, wuv_ref[...]).astype(BF16)

    q0 = qi * tq
    win = tq + 2 * SWA_WINDOW
    start = pl.multiple_of(jnp.clip(q0 - SWA_WINDOW, 0, DEC_SEQ - win), SWA_WINDOW)
    rows = lax.broadcasted_iota(jnp.int32, (4 * tq, win), 0)
    qpos = q0 + (rows & (tq - 1))
    kpos = start + lax.broadcasted_iota(jnp.int32, (4 * tq, win), 1)
    band = jnp.abs(qpos - kpos) <= SWA_WINDOW
    for g in range(N_KV):
        def swa_segs(q4, g=g):
            ks = sk_ref[pl.ds(start, win), 128 * g:128 * (g + 1)]
            vs = sv_ref[pl.ds(start, win), 128 * g:128 * (g + 1)]
            s_own = jnp.where(band, _dot_nt(q4, ks), MASK_VALUE)
            return [(_dot_nt(q4, cs_k[g]), cs_v[g]), (s_own, vs)]

        a, b = _gqa_group(sq_ref, g, tq, swa_segs, sink_ref)
        ob_ref[:, 256 * g:256 * g + 128] = a.astype(BF16)
        ob_ref[:, 256 * g + 128:256 * (g + 1)] = b.astype(BF16)

        def gqa_segs(q4, g=g):
            return [(_dot_nt(q4, cg_k[g]), cg_v[g]),
                    (_dot_nt(q4, gk_ref[:, 128 * g:128 * (g + 1)]), gv_ref[:, 128 * g:128 * (g + 1)])]

        a, b = _gqa_group(gq_ref, g, tq, gqa_segs)
        od_ref[:, 256 * g:256 * g + 128] = a.astype(BF16)
        od_ref[:, 256 * g + 128:256 * (g + 1)] = b.astype(BF16)

    def mla_segs(h, q_h):
        hp = h // 2
        return [(_dot_nt(q_h, cm_k[:, 128 * h:128 * (h + 1)]), cm_v[:, 128 * hp:128 * (hp + 1)]),
                (_dot_nt(q_h, mk_ref[:, 128 * h:128 * (h + 1)]), mv_ref[:, 128 * hp:128 * (hp + 1)])]

    _mla_heads(mq_ref, tq, mla_segs, oc_ref)


def _attn_lat(sink, lp, p, caches, layer):
    nq = DEC_SEQ // TQ_LAT

    def qspec(w):
        return pl.BlockSpec((TQ_LAT, w), lambda b, i: (b * nq + i, 0))

    def kspec(w):
        return pl.BlockSpec((DEC_SEQ, w), lambda b, i: (b, 0))

    def cspec(w):
        return pl.BlockSpec((1, 1, PAST_LEN, w), lambda b, i: (b, layer, 0, 0))

    is_q = dict(sq=True, mq=True, gq=True)
    in_specs = [_const_spec((1, N_HEADS)), _const_spec((256, 1024)), _const_spec((MLA_KV_RANK, 512))]
    in_specs += [(qspec if is_q.get(n) else kspec)(_ATTN_WIDTH[n]) for n in _ATTN_IN]
    in_specs += [cspec(128), cspec(128), cspec(128), cspec(MLA_ROPE), cspec(128), cspec(128)]
    return pl.pallas_call(
        _attn_lat_kernel,
        out_shape=[jax.ShapeDtypeStruct((N_LAT, 512), BF16)] * 3,
        grid=(DEC_BATCH, nq),
        in_specs=in_specs,
        out_specs=[qspec(512)] * 3,
        scratch_shapes=[pltpu.VMEM((N_KV, PAST_LEN, LANES), BF16), pltpu.VMEM((N_KV, PAST_LEN, LANES), BF16),
                        pltpu.VMEM((PAST_LEN, 1024), BF16), pltpu.VMEM((PAST_LEN, 512), BF16),
                        pltpu.VMEM((N_KV, PAST_LEN, LANES), BF16), pltpu.VMEM((N_KV, PAST_LEN, LANES), BF16)],
        compiler_params=_params(2),
        name="attn_lat",
    )(sink, lp["w_k"], lp["w_uv"], *[p[n] for n in _ATTN_IN], *caches)


def _merge_kernel(x_ref, mod_ref, oa_ref, ob_ref, oc_ref, od_ref, gates_ref, wb_ref, wo_ref, npost_ref,
                  nmpre_ref, nmpost_ref, w1_ref, w2_ref, y_ref):
    x = x_ref[...]
    g1 = mod_ref[0, 2]
    sh2 = mod_ref[0, 3]
    sc2 = mod_ref[0, 4]
    g2 = mod_ref[0, 5]
    acc = None
    for n, o_ref in enumerate((oa_ref, ob_ref, oc_ref, od_ref)):
        br = _dot(o_ref[...], wb_ref[n]) * gates_ref[:, D_MODEL * n:D_MODEL * (n + 1)].astype(F32)
        acc = br if acc is None else acc + br
    y = _dot(acc.astype(BF16), wo_ref[...])
    x = x + g1 * _rms(y, npost_ref[...])
    h = (_rms(x, nmpre_ref[...]) * (1.0 + sc2) + sh2).astype(BF16)
    a = jnp.maximum(_dot(h, w1_ref[...]), 0.0)
    y = _dot((a * a).astype(BF16), w2_ref[...])
    y_ref[...] = x + g2 * _rms(y, nmpost_ref[...])


def _merge(x, mod4, o_a, o_b, o_c, o_d, gates, lp, is_ctx):
    n_tok = x.shape[0]

    def tok(w):
        return pl.BlockSpec((TM, w), lambda i: (i, 0))

    in_specs = [tok(D_MODEL), _mod_spec(is_ctx), tok(512), tok(512), tok(512), tok(512), tok(4096),
                _const_spec((N_BRANCH, BRANCH_W, D_MODEL)), _const_spec((D_MODEL, D_MODEL)),
                _const_spec((1, D_MODEL)), _const_spec((1, D_MODEL)), _const_spec((1, D_MODEL)),
                _const_spec((D_MODEL, D_FF)), _const_spec((D_FF, D_MODEL))]
    return pl.pallas_call(
        _merge_kernel, out_shape=jax.ShapeDtypeStruct((n_tok, D_MODEL), F32), grid=(n_tok // TM,),
        in_specs=in_specs, out_specs=tok(D_MODEL), compiler_params=_params(1),
        name="merge_ctx" if is_ctx else "merge_lat",
    )(x, mod4, o_a, o_b, o_c, o_d, gates, lp["w_branch"], lp["w_out"], lp["norm_mix_post"], lp["norm_mlp_pre"],
      lp["norm_mlp_post"], lp["w_mlp_in"], lp["w_mlp_out"])


def _rope_tables():
    t = np.arange(DEC_SEQ)
    row, col = (t // GRID_W).astype(np.float32), (t % GRID_W).astype(np.float32)

    def pattern(d):
        nf = d // 4
        inv = (ROPE_BASE ** (-np.arange(nf, dtype=np.float32) / nf)).astype(np.float32)
        ang = np.concatenate([row[:, None] * inv, row[:, None] * inv, col[:, None] * inv, col[:, None] * inv], axis=1)
        cos, sin = np.cos(ang), np.sin(ang)
        second = (np.arange(d) % (2 * nf)) >= nf
        return cos, np.where(second, sin, 0.0), np.where(second, 0.0, -sin)

    out = np.zeros((9, DEC_SEQ, LANES), np.float32)
    out[[0, 3, 6]] = 1.0
    c, s1, s2 = pattern(HEAD_DIM)
    for a, tab in enumerate((c, s1, s2)):
        out[_TAB_64 + a] = np.tile(tab, (1, LANES // HEAD_DIM))
    c, s1, s2 = pattern(MLA_ROPE)
    for a, tab in enumerate((c, s1, s2)):
        out[_TAB_MLAQ + a][:, MLA_NOPE:MLA_NOPE + MLA_ROPE] = tab
        out[_TAB_KR + a][:, :MLA_ROPE] = tab
    return out


_ROPE_TABLES = _rope_tables()


def _block_mean_matrix(n):
    i = np.arange(n)
    return ((i[:, None] // HEAD_DIM) == (i[None, :] // HEAD_DIM)).astype(np.float32) / HEAD_DIM


def _mla_weight_layouts(w_uq, w_ukv):
    dq = MLA_NOPE + MLA_ROPE
    wq = w_uq.reshape(MLA_Q_RANK, MLA_HEADS, dq)
    wq = jnp.pad(wq, ((0, 0), (0, 0), (0, LANES - dq))).reshape(MLA_Q_RANK, MLA_HEADS * LANES)
    wkv = w_ukv.reshape(MLA_KV_RANK, MLA_HEADS, MLA_NOPE + MLA_V)
    wk = jnp.pad(wkv[:, :, :MLA_NOPE], ((0, 0), (0, 0), (0, LANES - MLA_NOPE))).reshape(MLA_KV_RANK, MLA_HEADS * LANES)
    place = np.zeros((LANES, MLA_HEADS, LANES), np.float32)
    for j in range(MLA_ROPE):
        place[j, :, MLA_NOPE + j] = 1.0
    w_k = jnp.concatenate([wk, jnp.asarray(place.reshape(LANES, MLA_HEADS * LANES))], axis=0)
    w_uv = wkv[:, :, MLA_NOPE:].reshape(MLA_KV_RANK, MLA_HEADS * MLA_V)
    return wq.astype(BF16), w_k.astype(BF16), w_uv.astype(BF16)


def _lower_bounds(p):
    s = jax.nn.softmax(p.astype(F32), axis=0)
    return jnp.cumsum(s, axis=0) - s[0]


def kernel(x_prompt, x_sample, state_hgrn, cache_swa_k, cache_swa_v, cache_mla_ckv, cache_mla_kr, cache_gqa_k, cache_gqa_v, c, c_ctx, w_ada, b_ada, norm_mix_pre, norm_mix_post, norm_mlp_pre, norm_mlp_post, w_in, hgrn_lb_fwd, hgrn_lb_bwd, hgrn_norm, swa_sink, mla_q_norm, mla_kv_norm, mla_w_uq, mla_w_ukv, gqa_q_norm, gqa_k_norm, w_branch, w_out, w_mlp_in, w_mlp_out):
    lb_fwd = _lower_bounds(hgrn_lb_fwd)
    lb_bwd = _lower_bounds(hgrn_lb_bwd)
    tabs = jnp.asarray(_ROPE_TABLES)
    b512 = jnp.asarray(_block_mean_matrix(512)).astype(BF16)

    cond8 = jnp.zeros((SUBLANES, D_MODEL), F32).at[0].set(c_ctx).at[1:1 + DEC_BATCH].set(c)
    mod = _ada(cond8, w_ada, b_ada)

    caches_in = (cache_swa_k.reshape(DEC_BATCH, DEPTH, PAST_LEN, 128),
                 cache_swa_v.reshape(DEC_BATCH, DEPTH, PAST_LEN, 128), cache_mla_ckv, cache_mla_kr,
                 cache_gqa_k.reshape(DEC_BATCH, DEPTH, PAST_LEN, 128),
                 cache_gqa_v.reshape(DEC_BATCH, DEPTH, PAST_LEN, 128))

    x_ctx = x_prompt.reshape(N_CTX, D_MODEL)
    x_lat = x_sample.reshape(N_LAT, D_MODEL)
    caches_out = None
    states_out = None
    for l in range(DEPTH):
        w_uq, w_k, w_uv = _mla_weight_layouts(mla_w_uq[l], mla_w_ukv[l])
        lp = dict(
            w_in=_permute_w_in(w_in[l]),
            norm_mix_pre=norm_mix_pre[l][None], norm_mix_post=norm_mix_post[l][None],
            norm_mlp_pre=norm_mlp_pre[l][None], norm_mlp_post=norm_mlp_post[l][None],
            mla_q_norm=mla_q_norm[l][None], mla_kv_norm=mla_kv_norm[l][None], w_uq=w_uq, w_k=w_k, w_uv=w_uv,
            gqa_q_gain=jnp.tile(gqa_q_norm[l], N_HEADS)[None], gqa_k_gain=jnp.tile(gqa_k_norm[l], 4)[None],
            b512=b512, b256=b512[:256, :256], b128=b512[:128, :128],
            lb_f=lb_fwd[l][None], lb_b=lb_bwd[l][None], hgrn_norm=hgrn_norm[l][None],
            w_branch=w_branch[l].astype(BF16), w_out=w_out[l].astype(BF16),
            w_mlp_in=w_mlp_in[l].astype(BF16), w_mlp_out=w_mlp_out[l].astype(BF16))
        sink = swa_sink[l][None]
        mod4 = mod[l].reshape(SUBLANES, N_MOD, 1, D_MODEL)

        pc, caches_out = _proj(x_ctx, mod4, lp, True, layer=l, caches_prev=caches_out)
        oa, states_out = _hgrn(pc, lp, BATCH, SEQ, l, state_prev=states_out)
        ob, oc, od = _attn_ctx(sink, pc)
        x_ctx = _merge(x_ctx, mod4, oa, ob, oc, od, pc["gates"], lp, True)

        pq, _ = _proj(x_lat, mod4, lp, False, tabs=tabs)
        (oa,) = _hgrn(pq, lp, DEC_BATCH, DEC_SEQ, l, state_in=state_hgrn)
        ob, oc, od = _attn_lat(sink, lp, pq, caches_in, l)
        x_lat = _merge(x_lat, mod4, oa, ob, oc, od, pq["gates"], lp, False)

    skc, svc, mckv, mkr, gkc, gvc = caches_out
    kv_shape = (BATCH, DEPTH, SEQ, N_KV, HEAD_DIM)
    return (x_ctx.reshape(BATCH, SEQ, D_MODEL), x_lat.reshape(DEC_BATCH, DEC_SEQ, D_MODEL), states_out,
            skc.reshape(kv_shape), svc.reshape(kv_shape), mckv, mkr, gkc.reshape(kv_shape), gvc.reshape(kv_shape))
```

```python
import functools

import jax
import jax.numpy as jnp
import numpy as np
from jax import lax
from jax.experimental import pallas as pl
from jax.experimental.pallas import tpu as pltpu

D_MODEL = 1024
BATCH, SEQ = 32, 256
DEC_BATCH, DEC_SEQ = 2, 2048
DEPTH = 2
PAST_LEN = 512
GRID_W = 64
ROPE_BASE = 10000.0
NORM_EPS = 1e-6
MASK_VALUE = -1e30
LOG_FLOOR = 1e-30
N_MOD = 6
N_BRANCH = 4
BRANCH_W = 512
D_FF = 4 * D_MODEL
HGRN_HEADS, HGRN_DK = 4, 128
SWA_WINDOW = 128
MLA_HEADS, MLA_Q_RANK, MLA_KV_RANK, MLA_NOPE, MLA_ROPE, MLA_V = 8, 256, 128, 64, 32, 64
HEAD_DIM = 64
N_HEADS, N_KV = 8, 2

N_CTX = BATCH * SEQ
N_LAT = DEC_BATCH * DEC_SEQ

LANES = 128
SUBLANES = 8
VMEM_LIMIT_BYTES = 56 * 1024 * 1024

TM = 256
LAT_TILES_PER_SEQ = DEC_SEQ // TM
HGRN_CHUNK = 64
HGRN_UNROLL = 4
TQ_LAT = 128

BF16 = jnp.bfloat16
F32 = jnp.float32

_C_HGRN = 0
_C_SWA_Q = 2560
_C_SWA_KV = 3072
_C_SWA_KDUP = 3328
_C_SWA_VDUP = 3584
_C_MLA_CQ = 3840
_C_MLA_CKV = 4096
_C_MLA_KR = 4224
_C_GQA_Q = 4352
_C_GQA_KV = 4864
_C_GQA_KDUP = 5120
_C_GQA_VDUP = 5376
_C_GATES = 5632
W_IN_COLS = _C_GATES + N_BRANCH * D_MODEL


def _w_in_segments():
    off, o = {}, 0
    for name, w in (("hgrn", 2560), ("swa_q", 512), ("swa_k", 128), ("swa_v", 128), ("mla_cq", 256),
                    ("mla_ckv", 128), ("mla_kr", 32), ("gqa_q", 512), ("gqa_k", 128), ("gqa_v", 128),
                    ("gates", 4096)):
        off[name] = o
        o += w

    def dup(base):
        return [(base, 64), (base, 64), (base + 64, 64), (base + 64, 64)]

    segs = ([(off["hgrn"], 2560), (off["swa_q"], 512), (off["swa_k"], 256)] + dup(off["swa_k"]) + dup(off["swa_v"])
            + [(off["mla_cq"], 256), (off["mla_ckv"], 128), (off["mla_kr"], 32), (-1, 96), (off["gqa_q"], 512),
               (off["gqa_k"], 256)] + dup(off["gqa_k"]) + dup(off["gqa_v"]) + [(off["gates"], 4096)])
    assert sum(w for _, w in segs) == W_IN_COLS
    return segs


_W_IN_SEGMENTS = _w_in_segments()


def _permute_w_in(w):
    parts = [jnp.zeros((D_MODEL, n), BF16) if a < 0 else w[:, a:a + n].astype(BF16) for a, n in _W_IN_SEGMENTS]
    return jnp.concatenate(parts, axis=1)


def _sigmoid_pair(x):
    a = jnp.exp(-jnp.abs(x))
    r = 1.0 / (1.0 + a)
    ar = a * r
    pos = x >= 0
    return jnp.where(pos, r, ar), jnp.where(pos, ar, r)


def _sigmoid(x):
    return _sigmoid_pair(x)[0]


def _silu(x):
    return x * _sigmoid(x)


def _rms(x, gain):
    return x * lax.rsqrt(jnp.mean(x * x, axis=-1, keepdims=True) + NORM_EPS) * gain


def _dot(a, b):
    return jnp.dot(a, b, preferred_element_type=F32)


def _dot_nt(a, b):
    return lax.dot_general(a, b, (((1,), (1,)), ((), ())), preferred_element_type=F32)


def _dot_tn(a, b):
    return lax.dot_general(a, b, (((0,), (0,)), ((), ())), preferred_element_type=F32)


def _tile_lanes(t, width):
    reps = width // LANES
    return t if reps == 1 else jnp.concatenate([t] * reps, axis=1)


def _rope(x, tab_ref, base, shift):
    w = x.shape[1]
    c = _tile_lanes(tab_ref[base], w)
    s1 = _tile_lanes(tab_ref[base + 1], w)
    s2 = _tile_lanes(tab_ref[base + 2], w)
    return x * c + pltpu.roll(x, shift, 1) * s1 + pltpu.roll(x, w - shift, 1) * s2


def _head_rms(x, bmat_ref, gain):
    ms = _dot((x * x).astype(BF16), bmat_ref[...])
    return x * lax.rsqrt(ms + NORM_EPS) * gain


def _const_spec(shape):
    nd = len(shape)
    return pl.BlockSpec(shape, lambda *_: (0,) * nd, pipeline_mode=pl.Buffered(1))


_ANY_SPEC = pl.BlockSpec(memory_space=pl.ANY)


def _params(n_grid_dims):
    return pltpu.CompilerParams(dimension_semantics=("arbitrary",) * n_grid_dims, vmem_limit_bytes=VMEM_LIMIT_BYTES)


_ADA_TN = 2048


def _ada_kernel(c_ref, w_ref, b_ref, o_ref):
    s = _silu(c_ref[...]).astype(BF16)
    o_ref[0] = _dot(s, w_ref[0].astype(BF16)) + b_ref[0]


def _ada(cond8, w_ada, b_ada):
    n = N_MOD * D_MODEL
    return pl.pallas_call(
        _ada_kernel,
        out_shape=jax.ShapeDtypeStruct((DEPTH, SUBLANES, n), F32),
        grid=(DEPTH, n // _ADA_TN),
        in_specs=[
            pl.BlockSpec((SUBLANES, D_MODEL), lambda l, j: (0, 0)),
            pl.BlockSpec((1, D_MODEL, _ADA_TN), lambda l, j: (l, 0, j)),
            pl.BlockSpec((1, 1, _ADA_TN), lambda l, j: (l, 0, j)),
        ],
        out_specs=pl.BlockSpec((1, SUBLANES, _ADA_TN), lambda l, j: (l, 0, j)),
        compiler_params=_params(2),
        name="ada",
    )(cond8, w_ada, b_ada.reshape(DEPTH, 1, n))


def _mod_spec(is_ctx):
    if is_ctx:
        return pl.BlockSpec((1, N_MOD, 1, D_MODEL), lambda i: (0, 0, 0, 0))
    return pl.BlockSpec((1, N_MOD, 1, D_MODEL), lambda i: (1 + i // LAT_TILES_PER_SEQ, 0, 0, 0))


_SCALE_64 = HEAD_DIM ** -0.5
_SCALE_MLA = (MLA_NOPE + MLA_ROPE) ** -0.5
_TAB_64, _TAB_MLAQ, _TAB_KR = 0, 3, 6

_PROJ_OUTS = (
    ("hq", 512, F32), ("hff", 512, F32), ("hfb", 512, F32), ("hi", 512, F32), ("hg", 512, F32),
    ("sq", 512, BF16), ("sk", 256, BF16), ("sv", 256, BF16),
    ("mq", 1024, BF16), ("mk", 1024, BF16), ("mv", 512, BF16),
    ("gq", 512, BF16), ("gk", 256, BF16), ("gv", 256, BF16), ("gates", 4096, BF16))
_CACHE_OUTS = (("skc", 128), ("svc", 128), ("mckv", 128), ("mkr", MLA_ROPE), ("gkc", 128), ("gvc", 128))
_PROJ_CONSTS = ("npre", "w", "mlaq_g", "mlakv_g", "wuq", "wk", "wuv", "gq_g", "gk_g", "b512", "b256", "b128")


def _proj_kernel(*refs, is_ctx, n_alias):
    names = ["x", "mod"] + list(_PROJ_CONSTS) + ([] if is_ctx else ["tab"])
    r = dict(zip(names, refs))
    pos = len(names) + n_alias
    for name, _, _ in _PROJ_OUTS:
        r[name] = refs[pos]
        pos += 1
    if is_ctx:
        for name, _ in _CACHE_OUTS:
            r[name] = refs[pos]
            pos += 1

    x = r["x"][...]
    sh1 = r["mod"][0, 0]
    sc1 = r["mod"][0, 1]
    h = (_rms(x, r["npre"][...]) * (1.0 + sc1) + sh1).astype(BF16)
    w_ref = r["w"]

    def seg(a, w):
        return _dot(h, w_ref[:, a:a + w])

    def rope(v, base, shift):
        return v if is_ctx else _rope(v, r["tab"], base, shift)

    for j, name in enumerate(("hq", "hff", "hfb", "hi", "hg")):
        r[name][...] = seg(_C_HGRN + 512 * j, 512)

    r["sq"][...] = (rope(seg(_C_SWA_Q, 512), _TAB_64, 16) * _SCALE_64).astype(BF16)
    r["sk"][...] = rope(seg(_C_SWA_KDUP, 256), _TAB_64, 16).astype(BF16)
    r["sv"][...] = seg(_C_SWA_VDUP, 256).astype(BF16)
    if is_ctx:
        kv = seg(_C_SWA_KV, 256)
        r["skc"][0, 0] = kv[:, :128]
        r["svc"][0, 0] = kv[:, 128:]

    cq = _rms(seg(_C_MLA_CQ, 256), r["mlaq_g"][...]).astype(BF16)
    q = _dot(cq, r["wuq"][...])
    r["mq"][...] = (rope(q, _TAB_MLAQ, 8) * _SCALE_MLA).astype(BF16)
    ckv = _rms(seg(_C_MLA_CKV, 128), r["mlakv_g"][...])
    kr = seg(_C_MLA_KR, 128)
    if is_ctx:
        r["mckv"][0, 0] = ckv
        r["mkr"][0, 0] = kr[:, :MLA_ROPE]
    kr_used = rope(kr, _TAB_KR, 8)
    ckv16 = ckv.astype(BF16)
    r["mk"][...] = _dot(jnp.concatenate([ckv16, kr_used.astype(BF16)], axis=1), r["wk"][...]).astype(BF16)
    r["mv"][...] = _dot(ckv16, r["wuv"][...]).astype(BF16)

    gq = _head_rms(seg(_C_GQA_Q, 512), r["b512"], r["gq_g"][...])
    r["gq"][...] = (rope(gq, _TAB_64, 16) * _SCALE_64).astype(BF16)
    gk = _head_rms(seg(_C_GQA_KDUP, 256), r["b256"], r["gk_g"][...])
    r["gk"][...] = rope(gk, _TAB_64, 16).astype(BF16)
    r["gv"][...] = seg(_C_GQA_VDUP, 256).astype(BF16)
    if is_ctx:
        kv = seg(_C_GQA_KV, 256)
        r["gkc"][0, 0] = _head_rms(kv[:, :128], r["b128"], r["gk_g"][:, :128])
        r["gvc"][0, 0] = kv[:, 128:]

    for j in range(N_BRANCH):
        r["gates"][:, D_MODEL * j:D_MODEL * (j + 1)] = _sigmoid(seg(_C_GATES + D_MODEL * j, D_MODEL)).astype(BF16)


def _proj(x, mod4, lp, is_ctx, layer=0, tabs=None, caches_prev=None):
    n_tok = x.shape[0]
    const_shapes = dict(npre=(1, D_MODEL), w=(D_MODEL, W_IN_COLS), mlaq_g=(1, MLA_Q_RANK), mlakv_g=(1, MLA_KV_RANK),
                        wuq=(MLA_Q_RANK, 1024), wk=(256, 1024), wuv=(MLA_KV_RANK, 512), gq_g=(1, 512), gk_g=(1, 256),
                        b512=(512, 512), b256=(256, 256), b128=(128, 128))
    const_args = dict(npre=lp["norm_mix_pre"], w=lp["w_in"], mlaq_g=lp["mla_q_norm"], mlakv_g=lp["mla_kv_norm"],
                      wuq=lp["w_uq"], wk=lp["w_k"], wuv=lp["w_uv"], gq_g=lp["gqa_q_gain"], gk_g=lp["gqa_k_gain"],
                      b512=lp["b512"], b256=lp["b256"], b128=lp["b128"])
    in_specs = [pl.BlockSpec((TM, D_MODEL), lambda i: (i, 0)), _mod_spec(is_ctx)]
    in_specs += [_const_spec(const_shapes[n]) for n in _PROJ_CONSTS]
    args = [x, mod4] + [const_args[n] for n in _PROJ_CONSTS]
    if not is_ctx:
        in_specs.append(pl.BlockSpec((9, TM, LANES), lambda i: (0, i % LAT_TILES_PER_SEQ, 0)))
        args.append(tabs)
    aliases = {}
    n_alias = 0
    if caches_prev is not None:
        n_alias = len(caches_prev)
        for j, c in enumerate(caches_prev):
            aliases[len(args)] = len(_PROJ_OUTS) + j
            in_specs.append(_ANY_SPEC)
            args.append(c)
    out_specs = [pl.BlockSpec((TM, w), lambda i: (i, 0)) for _, w, _ in _PROJ_OUTS]
    out_shape = [jax.ShapeDtypeStruct((n_tok, w), dt) for _, w, dt in _PROJ_OUTS]
    if is_ctx:
        out_specs += [pl.BlockSpec((1, 1, SEQ, w), lambda i: (i, layer, 0, 0)) for _, w in _CACHE_OUTS]
        out_shape += [jax.ShapeDtypeStruct((BATCH, DEPTH, SEQ, w), F32) for _, w in _CACHE_OUTS]
    outs = pl.pallas_call(
        functools.partial(_proj_kernel, is_ctx=is_ctx, n_alias=n_alias),
        out_shape=out_shape, grid=(n_tok // TM,), in_specs=in_specs, out_specs=out_specs,
        input_output_aliases=aliases, compiler_params=_params(1),
        name="proj_ctx" if is_ctx else "proj_lat",
    )(*args)
    p = {name: o for (name, _, _), o in zip(_PROJ_OUTS, outs)}
    return p, tuple(outs[len(_PROJ_OUTS):])


def _pair_levels():
    C = HGRN_CHUNK
    ti = lax.broadcasted_iota(jnp.int32, (C, C), 0)
    si = lax.broadcasted_iota(jnp.int32, (C, C), 1)
    x = ti ^ si
    lvl = jnp.where(x == 0, -1, 0)
    b = 2
    while b < C:
        lvl = lvl + (x >= b).astype(jnp.int32)
        b *= 2
    return ti, si, lvl


def _hgrn_chunk(q, k, lf2, v, st_ref, d, fwd, lvl):
    C = HGRN_CHUNK
    nv = C // SUBLANES
    sub = lax.broadcasted_iota(jnp.int32, (nv, SUBLANES, LANES), 1)

    p = lf2.reshape(nv, SUBLANES, LANES)
    for s in (1, 2, 4):
        if fwd:
            p = p + jnp.where(sub >= s, pltpu.roll(p, s, 1), 0.0)
        else:
            p = p + jnp.where(sub < SUBLANES - s, pltpu.roll(p, SUBLANES - s, 1), 0.0)

    a_mat = jnp.zeros((C, C), F32)

    def add_level(a_mat, e, level):
        a_m = _dot_nt((q * e).astype(BF16), (k * e).astype(BF16))
        return jnp.where(lvl == level, a_m, a_mat)

    def row(r):
        return jnp.broadcast_to(p[:, r:r + 1, :], p.shape)

    odd = (sub & 1) == 1
    if fwd:
        bnd = (jnp.where(odd, pltpu.roll(p, 1, 1), p), jnp.where(sub < 4, row(1), row(5)), row(3))
    else:
        bnd = (jnp.where(odd, p, pltpu.roll(p, SUBLANES - 1, 1)), jnp.where(sub < 4, row(2), row(6)), row(4))
    for level, b in enumerate(bnd):
        a_mat = add_level(a_mat, jnp.exp2(-jnp.abs(p - b)).reshape(C, LANES), level)

    p = p.reshape(C, LANES)
    m = SUBLANES
    while m < C:
        nb = C // (2 * m)
        p4 = p.reshape(nb, 2, m, LANES)
        lo, hi = p4[:, 0], p4[:, 1]
        if fwd:
            tot = lo[:, m - 1:m]
            e_lo, e_hi = tot - lo, hi
            p_lo, p_hi = lo, hi + tot
        else:
            tot = hi[:, 0:1]
            e_lo, e_hi = lo, tot - hi
            p_lo, p_hi = lo + tot, hi
        e = jnp.exp2(jnp.concatenate([e_lo[:, None], e_hi[:, None]], axis=1).reshape(C, LANES))
        p = jnp.concatenate([p_lo[:, None], p_hi[:, None]], axis=1).reshape(C, LANES)
        a_mat = add_level(a_mat, e, int(np.log2(m)))
        m *= 2
    v16 = v.astype(BF16)
    o = jnp.sum(q * k, axis=-1, keepdims=True) * v + _dot(a_mat.astype(BF16), v16)

    st = st_ref[d]
    o = o + _dot_nt((q * jnp.exp2(p)).astype(BF16), st.astype(BF16))
    tot = p[C - 1:C] if fwd else p[0:1]
    k_end = (k * jnp.exp2(tot - p)).astype(BF16)
    st_ref[d] = st * jnp.exp2(tot) + _dot_tn(v16, k_end)
    return o


def _hgrn_kernel(*refs, n_chunks, has_init, n_alias):
    lbf_ref, lbb_ref, gn_ref, q_ref, ff_ref, fb_ref, v_ref, g_ref = refs[:8]
    pos = 8
    s0_ref = None
    if has_init:
        s0_ref = refs[pos]
        pos += 1
    pos += n_alias
    o_ref = refs[pos]
    pos += 1
    sfin_ref = None
    if not has_init:
        sfin_ref = refs[pos]
        pos += 1
    of_ref, ob_ref, st_ref = refs[pos:pos + 3]
    C = HGRN_CHUNK

    for d in range(2):
        if has_init:
            st_ref[d] = s0_ref[0, 0, d, 0].T
        else:
            st_ref[d] = jnp.zeros((LANES, LANES), F32)
    lbf = lbf_ref[...]
    lbb = lbb_ref[...]

    def forget(f_pre, lb):
        s_pos, s_neg = _sigmoid_pair(f_pre)
        f = lb + (1.0 - lb) * s_pos
        return (1.0 - lb) * s_neg, jnp.log2(jnp.maximum(f, LOG_FLOOR))

    ti, si, lvl = _pair_levels()
    lvls = (jnp.where(ti > si, lvl, -1), jnp.where(ti < si, lvl, -1))

    def body(c, carry):
        rf = pl.multiple_of(c * C, C)
        rb = pl.multiple_of((n_chunks - 1 - c) * C, C)
        for d, r0, f_ref, lb, out_ref in ((0, rf, ff_ref, lbf, of_ref), (1, rb, fb_ref, lbb, ob_ref)):
            sl = pl.ds(r0, C)
            q = _silu(q_ref[sl, :])
            k, lf2 = forget(f_ref[sl, :], lb)
            out_ref[sl, :] = _hgrn_chunk(q, k, lf2, v_ref[sl, :], st_ref, d, d == 0, lvls[d])
        return carry

    lax.fori_loop(0, n_chunks, body, 0, unroll=HGRN_UNROLL)
    o = of_ref[...] + ob_ref[...]
    o = _rms(o, gn_ref[...]) * _silu(g_ref[...])
    o_ref[...] = o.astype(BF16)
    if sfin_ref is not None:
        for d in range(2):
            sfin_ref[0, 0, d, 0] = st_ref[d].T


def _hgrn(p, lp, n_seq, seq_len, layer, state_in=None, state_prev=None):
    tok_spec = pl.BlockSpec((seq_len, LANES), lambda b, h: (b, h))
    vec_spec = pl.BlockSpec((1, LANES), lambda b, h: (0, h))
    in_specs = [vec_spec, vec_spec, vec_spec, tok_spec, tok_spec, tok_spec, tok_spec, tok_spec]
    args = [lp["lb_f"], lp["lb_b"], lp["hgrn_norm"], p["hq"], p["hff"], p["hfb"], p["hi"], p["hg"]]
    has_init = state_in is not None
    state_spec = pl.BlockSpec((1, 1, 2, 1, LANES, LANES), lambda b, h: (b, layer, 0, h, 0, 0))
    aliases = {}
    n_alias = 0
    if has_init:
        in_specs.append(state_spec)
        args.append(state_in)
    elif state_prev is not None:
        aliases[len(args)] = 1
        n_alias = 1
        in_specs.append(_ANY_SPEC)
        args.append(state_prev)
    out_shape = [jax.ShapeDtypeStruct((n_seq * seq_len, 512), BF16)]
    out_specs = [tok_spec]
    if not has_init:
        out_shape.append(jax.ShapeDtypeStruct((n_seq, DEPTH, 2, HGRN_HEADS, LANES, LANES), F32))
        out_specs.append(state_spec)
    kern = functools.partial(_hgrn_kernel, n_chunks=seq_len // HGRN_CHUNK, has_init=has_init, n_alias=n_alias)
    return pl.pallas_call(
        kern, out_shape=out_shape, grid=(n_seq, HGRN_HEADS), in_specs=in_specs, out_specs=out_specs,
        scratch_shapes=[pltpu.VMEM((seq_len, LANES), F32), pltpu.VMEM((seq_len, LANES), F32),
                        pltpu.VMEM((2, LANES, LANES), F32)],
        input_output_aliases=aliases, compiler_params=_params(2),
        name="hgrn_lat" if has_init else "hgrn_ctx",
    )(*args)


def _lane_lo(shape):
    return lax.broadcasted_iota(jnp.int32, shape, len(shape) - 1) < HEAD_DIM


def _softmax_pv(segs, sink_col=None):
    mx = None
    for s, _ in segs:
        m = jnp.max(s, axis=-1, keepdims=True)
        mx = m if mx is None else jnp.maximum(mx, m)
    if sink_col is not None:
        mx = jnp.maximum(mx, sink_col)
    den = None
    acc = None
    for s, v in segs:
        e = jnp.exp(s - mx)
        dsum = jnp.sum(e, axis=-1, keepdims=True)
        den = dsum if den is None else den + dsum
        pv = _dot(e.astype(BF16), v)
        acc = pv if acc is None else acc + pv
    if sink_col is not None:
        den = den + jnp.exp(sink_col - mx)
    return acc / den


def _gqa_group(q_ref, g, tq, segs_fn, sink_ref=None):
    lo = _lane_lo((tq, LANES))
    parts = []
    for j in range(2):
        qp = q_ref[:, 256 * g + 128 * j:256 * g + 128 * (j + 1)]
        zero = jnp.zeros_like(qp)
        parts += [jnp.where(lo, qp, zero), jnp.where(lo, zero, qp)]
    q4 = jnp.concatenate(parts, axis=0)
    sink_col = None
    if sink_ref is not None:
        sink_col = jnp.concatenate(
            [jnp.broadcast_to(sink_ref[0:1, 4 * g + i:4 * g + i + 1], (tq, 1)) for i in range(4)], axis=0)
    pv = _softmax_pv(segs_fn(q4), sink_col)
    return (jnp.where(lo, pv[0:tq], pv[tq:2 * tq]), jnp.where(lo, pv[2 * tq:3 * tq], pv[3 * tq:4 * tq]))


def _mla_heads(q_ref, tq, segs_fn, o_ref):
    lo = _lane_lo((tq, LANES))
    for hp in range(MLA_HEADS // 2):
        outs = []
        for h in (2 * hp, 2 * hp + 1):
            q_h = q_ref[:, 128 * h:128 * (h + 1)]
            outs.append(_softmax_pv(segs_fn(h, q_h)))
        o_ref[:, 128 * hp:128 * (hp + 1)] = jnp.where(lo, outs[0], outs[1]).astype(BF16)


_ATTN_IN = ("sq", "sk", "sv", "mq", "mk", "mv", "gq", "gk", "gv")
_ATTN_WIDTH = dict(sq=512, sk=256, sv=256, mq=1024, mk=1024, mv=512, gq=512, gk=256, gv=256)


def _attn_ctx_kernel(sink_ref, sq_ref, sk_ref, sv_ref, mq_ref, mk_ref, mv_ref, gq_ref, gk_ref, gv_ref,
                     ob_ref, oc_ref, od_ref):
    tq = SEQ
    for g in range(N_KV):
        def swa_segs(q4, g=g):
            return [(_dot_nt(q4, sk_ref[:, 128 * g:128 * (g + 1)]), sv_ref[:, 128 * g:128 * (g + 1)])]

        a, b = _gqa_group(sq_ref, g, tq, swa_segs, sink_ref)
        ob_ref[:, 256 * g:256 * g + 128] = a.astype(BF16)
        ob_ref[:, 256 * g + 128:256 * (g + 1)] = b.astype(BF16)

        def gqa_segs(q4, g=g):
            return [(_dot_nt(q4, gk_ref[:, 128 * g:128 * (g + 1)]), gv_ref[:, 128 * g:128 * (g + 1)])]

        a, b = _gqa_group(gq_ref, g, tq, gqa_segs)
        od_ref[:, 256 * g:256 * g + 128] = a.astype(BF16)
        od_ref[:, 256 * g + 128:256 * (g + 1)] = b.astype(BF16)

    def mla_segs(h, q_h):
        return [(_dot_nt(q_h, mk_ref[:, 128 * h:128 * (h + 1)]), mv_ref[:, 128 * (h // 2):128 * (h // 2 + 1)])]

    _mla_heads(mq_ref, tq, mla_segs, oc_ref)


def _attn_ctx(sink, p):
    def spec(w):
        return pl.BlockSpec((SEQ, w), lambda b: (b, 0))

    return pl.pallas_call(
        _attn_ctx_kernel,
        out_shape=[jax.ShapeDtypeStruct((N_CTX, 512), BF16)] * 3,
        grid=(BATCH,),
        in_specs=[_const_spec((1, N_HEADS))] + [spec(_ATTN_WIDTH[n]) for n in _ATTN_IN],
        out_specs=[spec(512)] * 3,
        compiler_params=_params(1),
        name="attn_ctx",
    )(sink, *[p[n] for n in _ATTN_IN])


def _dup_heads(x):
    lo = _lane_lo(x.shape)
    r = pltpu.roll(x, HEAD_DIM, 1)
    return jnp.where(lo, x, r), jnp.where(lo, r, x)


def _attn_lat_kernel(sink_ref, wk_ref, wuv_ref,
                     sq_ref, sk_ref, sv_ref, mq_ref, mk_ref, mv_ref, gq_ref, gk_ref, gv_ref,
                     csk_ref, csv_ref, cckv_ref, ckr_ref, cgk_ref, cgv_ref,
                     ob_ref, oc_ref, od_ref,
                     cs_k, cs_v, cm_k, cm_v, cg_k, cg_v):
    tq = TQ_LAT
    qi = pl.program_id(1)

    @pl.when(qi == 0)
    def _():
        for src, dst in ((csk_ref, cs_k), (csv_ref, cs_v), (cgk_ref, cg_k), (cgv_ref, cg_v)):
            d0, d1 = _dup_heads(src[0, 0])
            dst[0] = d0.astype(BF16)
            dst[1] = d1.astype(BF16)
        ckv16 = cckv_ref[0, 0].astype(BF16)
        k = _dot(ckv16, wk_ref[0:MLA_KV_RANK, :]) + _dot(ckr_ref[0, 0].astype(BF16),
                                                           wk_ref[MLA_KV_RANK:MLA_KV_RANK + MLA_ROPE, :])
        cm_k[...] = k.astype(BF16)
        cm_v[...] = _dot(ckv16, wuv_ref[...]).astype(BF16)

    q0 = qi * tq
    win = tq + 2 * SWA_WINDOW
    start = pl.multiple_of(jnp.clip(q0 - SWA_WINDOW, 0, DEC_SEQ - win), SWA_WINDOW)
    rows = lax.broadcasted_iota(jnp.int32, (4 * tq, win), 0)
    qpos = q0 + (rows & (tq - 1))
    kpos = start + lax.broadcasted_iota(jnp.int32, (4 * tq, win), 1)
    band = jnp.abs(qpos - kpos) <= SWA_WINDOW
    for g in range(N_KV):
        def swa_segs(q4, g=g):
            ks = sk_ref[pl.ds(start, win), 128 * g:128 * (g + 1)]
            vs = sv_ref[pl.ds(start, win), 128 * g:128 * (g + 1)]
            s_own = jnp.where(band, _dot_nt(q4, ks), MASK_VALUE)
            return [(_dot_nt(q4, cs_k[g]), cs_v[g]), (s_own, vs)]

        a, b = _gqa_group(sq_ref, g, tq, swa_segs, sink_ref)
        ob_ref[:, 256 * g:256 * g + 128] = a.astype(BF16)
        ob_ref[:, 256 * g + 128:256 * (g + 1)] = b.astype(BF16)

        def gqa_segs(q4, g=g):
            return [(_dot_nt(q4, cg_k[g]), cg_v[g]),
                    (_dot_nt(q4, gk_ref[:, 128 * g:128 * (g + 1)]), gv_ref[:, 128 * g:128 * (g + 1)])]

        a, b = _gqa_group(gq_ref, g, tq, gqa_segs)
        od_ref[:, 256 * g:256 * g + 128] = a.astype(BF16)
        od_ref[:, 256 * g + 128:256 * (g + 1)] = b.astype(BF16)

    def mla_segs(h, q_h):
        hp = h // 2
        return [(_dot_nt(q_h, cm_k[:, 128 * h:128 * (h + 1)]), cm_v[:, 128 * hp:128 * (hp + 1)]),
                (_dot_nt(q_h, mk_ref[:, 128 * h:128 * (h + 1)]), mv_ref[:, 128 * hp:128 * (hp + 1)])]

    _mla_heads(mq_ref, tq, mla_segs, oc_ref)


def _attn_lat(sink, lp, p, caches, layer):
    nq = DEC_SEQ // TQ_LAT

    def qspec(w):
        return pl.BlockSpec((TQ_LAT, w), lambda b, i: (b * nq + i, 0))

    def kspec(w):
        return pl.BlockSpec((DEC_SEQ, w), lambda b, i: (b, 0))

    def cspec(w):
        return pl.BlockSpec((1, 1, PAST_LEN, w), lambda b, i: (b, layer, 0, 0))

    is_q = dict(sq=True, mq=True, gq=True)
    in_specs = [_const_spec((1, N_HEADS)), _const_spec((256, 1024)), _const_spec((MLA_KV_RANK, 512))]
    in_specs += [(qspec if is_q.get(n) else kspec)(_ATTN_WIDTH[n]) for n in _ATTN_IN]
    in_specs += [cspec(128), cspec(128), cspec(128), cspec(MLA_ROPE), cspec(128), cspec(128)]
    return pl.pallas_call(
        _attn_lat_kernel,
        out_shape=[jax.ShapeDtypeStruct((N_LAT, 512), BF16)] * 3,
        grid=(DEC_BATCH, nq),
        in_specs=in_specs,
        out_specs=[qspec(512)] * 3,
        scratch_shapes=[pltpu.VMEM((N_KV, PAST_LEN, LANES), BF16), pltpu.VMEM((N_KV, PAST_LEN, LANES), BF16),
                        pltpu.VMEM((PAST_LEN, 1024), BF16), pltpu.VMEM((PAST_LEN, 512), BF16),
                        pltpu.VMEM((N_KV, PAST_LEN, LANES), BF16), pltpu.VMEM((N_KV, PAST_LEN, LANES), BF16)],
        compiler_params=_params(2),
        name="attn_lat",
    )(sink, lp["w_k"], lp["w_uv"], *[p[n] for n in _ATTN_IN], *caches)


def _merge_kernel(x_ref, mod_ref, oa_ref, ob_ref, oc_ref, od_ref, gates_ref, wb_ref, wo_ref, npost_ref,
                  nmpre_ref, nmpost_ref, w1_ref, w2_ref, y_ref):
    x = x_ref[...]
    g1 = mod_ref[0, 2]
    sh2 = mod_ref[0, 3]
    sc2 = mod_ref[0, 4]
    g2 = mod_ref[0, 5]
    acc = None
    for n, o_ref in enumerate((oa_ref, ob_ref, oc_ref, od_ref)):
        br = _dot(o_ref[...], wb_ref[n]) * gates_ref[:, D_MODEL * n:D_MODEL * (n + 1)].astype(F32)
        acc = br if acc is None else acc + br
    y = _dot(acc.astype(BF16), wo_ref[...])
    x = x + g1 * _rms(y, npost_ref[...])
    h = (_rms(x, nmpre_ref[...]) * (1.0 + sc2) + sh2).astype(BF16)
    a = jnp.maximum(_dot(h, w1_ref[...]), 0.0)
    y = _dot((a * a).astype(BF16), w2_ref[...])
    y_ref[...] = x + g2 * _rms(y, nmpost_ref[...])


def _merge(x, mod4, o_a, o_b, o_c, o_d, gates, lp, is_ctx):
    n_tok = x.shape[0]

    def tok(w):
        return pl.BlockSpec((TM, w), lambda i: (i, 0))

    in_specs = [tok(D_MODEL), _mod_spec(is_ctx), tok(512), tok(512), tok(512), tok(512), tok(4096),
                _const_spec((N_BRANCH, BRANCH_W, D_MODEL)), _const_spec((D_MODEL, D_MODEL)),
                _const_spec((1, D_MODEL)), _const_spec((1, D_MODEL)), _const_spec((1, D_MODEL)),
                _const_spec((D_MODEL, D_FF)), _const_spec((D_FF, D_MODEL))]
    return pl.pallas_call(
        _merge_kernel, out_shape=jax.ShapeDtypeStruct((n_tok, D_MODEL), F32), grid=(n_tok // TM,),
        in_specs=in_specs, out_specs=tok(D_MODEL), compiler_params=_params(1),
        name="merge_ctx" if is_ctx else "merge_lat",
    )(x, mod4, o_a, o_b, o_c, o_d, gates, lp["w_branch"], lp["w_out"], lp["norm_mix_post"], lp["norm_mlp_pre"],
      lp["norm_mlp_post"], lp["w_mlp_in"], lp["w_mlp_out"])


def _rope_tables():
    t = np.arange(DEC_SEQ)
    row, col = (t // GRID_W).astype(np.float32), (t % GRID_W).astype(np.float32)

    def pattern(d):
        nf = d // 4
        inv = (ROPE_BASE ** (-np.arange(nf, dtype=np.float32) / nf)).astype(np.float32)
        ang = np.concatenate([row[:, None] * inv, row[:, None] * inv, col[:, None] * inv, col[:, None] * inv], axis=1)
        cos, sin = np.cos(ang), np.sin(ang)
        second = (np.arange(d) % (2 * nf)) >= nf
        return cos, np.where(second, sin, 0.0), np.where(second, 0.0, -sin)

    out = np.zeros((9, DEC_SEQ, LANES), np.float32)
    out[[0, 3, 6]] = 1.0
    c, s1, s2 = pattern(HEAD_DIM)
    for a, tab in enumerate((c, s1, s2)):
        out[_TAB_64 + a] = np.tile(tab, (1, LANES // HEAD_DIM))
    c, s1, s2 = pattern(MLA_ROPE)
    for a, tab in enumerate((c, s1, s2)):
        out[_TAB_MLAQ + a][:, MLA_NOPE:MLA_NOPE + MLA_ROPE] = tab
        out[_TAB_KR + a][:, :MLA_ROPE] = tab
    return out


_ROPE_TABLES = _rope_tables()


def _block_mean_matrix(n):
    i = np.arange(n)
    return ((i[:, None] // HEAD_DIM) == (i[None, :] // HEAD_DIM)).astype(np.float32) / HEAD_DIM


def _mla_weight_layouts(w_uq, w_ukv):
    dq = MLA_NOPE + MLA_ROPE
    wq = w_uq.reshape(MLA_Q_RANK, MLA_HEADS, dq)
    wq = jnp.pad(wq, ((0, 0), (0, 0), (0, LANES - dq))).reshape(MLA_Q_RANK, MLA_HEADS * LANES)
    wkv = w_ukv.reshape(MLA_KV_RANK, MLA_HEADS, MLA_NOPE + MLA_V)
    wk = jnp.pad(wkv[:, :, :MLA_NOPE], ((0, 0), (0, 0), (0, LANES - MLA_NOPE))).reshape(MLA_KV_RANK, MLA_HEADS * LANES)
    place = np.zeros((LANES, MLA_HEADS, LANES), np.float32)
    for j in range(MLA_ROPE):
        place[j, :, MLA_NOPE + j] = 1.0
    w_k = jnp.concatenate([wk, jnp.asarray(place.reshape(LANES, MLA_HEADS * LANES))], axis=0)
    w_uv = wkv[:, :, MLA_NOPE:].reshape(MLA_KV_RANK, MLA_HEADS * MLA_V)
    return wq.astype(BF16), w_k.astype(BF16), w_uv.astype(BF16)


def _lower_bounds(p):
    s = jax.nn.softmax(p.astype(F32), axis=0)
    return jnp.cumsum(s, axis=0) - s[0]


def kernel(x_prompt, x_sample, state_hgrn, cache_swa_k, cache_swa_v, cache_mla_ckv, cache_mla_kr, cache_gqa_k, cache_gqa_v, c, c_ctx, w_ada, b_ada, norm_mix_pre, norm_mix_post, norm_mlp_pre, norm_mlp_post, w_in, hgrn_lb_fwd, hgrn_lb_bwd, hgrn_norm, swa_sink, mla_q_norm, mla_kv_norm, mla_w_uq, mla_w_ukv, gqa_q_norm, gqa_k_norm, w_branch, w_out, w_mlp_in, w_mlp_out):
    lb_fwd = _lower_bounds(hgrn_lb_fwd)
    lb_bwd = _lower_bounds(hgrn_lb_bwd)
    tabs = jnp.asarray(_ROPE_TABLES)
    b512 = jnp.asarray(_block_mean_matrix(512)).astype(BF16)

    cond8 = jnp.zeros((SUBLANES, D_MODEL), F32).at[0].set(c_ctx).at[1:1 + DEC_BATCH].set(c)
    mod = _ada(cond8, w_ada, b_ada)

    caches_in = (cache_swa_k.reshape(DEC_BATCH, DEPTH, PAST_LEN, 128),
                 cache_swa_v.reshape(DEC_BATCH, DEPTH, PAST_LEN, 128), cache_mla_ckv, cache_mla_kr,
                 cache_gqa_k.reshape(DEC_BATCH, DEPTH, PAST_LEN, 128),
                 cache_gqa_v.reshape(DEC_BATCH, DEPTH, PAST_LEN, 128))

    x_ctx = x_prompt.reshape(N_CTX, D_MODEL)
    x_lat = x_sample.reshape(N_LAT, D_MODEL)
    caches_out = None
    states_out = None
    for l in range(DEPTH):
        w_uq, w_k, w_uv = _mla_weight_layouts(mla_w_uq[l], mla_w_ukv[l])
        lp = dict(
            w_in=_permute_w_in(w_in[l]),
            norm_mix_pre=norm_mix_pre[l][None], norm_mix_post=norm_mix_post[l][None],
            norm_mlp_pre=norm_mlp_pre[l][None], norm_mlp_post=norm_mlp_post[l][None],
            mla_q_norm=mla_q_norm[l][None], mla_kv_norm=mla_kv_norm[l][None], w_uq=w_uq, w_k=w_k, w_uv=w_uv,
            gqa_q_gain=jnp.tile(gqa_q_norm[l], N_HEADS)[None], gqa_k_gain=jnp.tile(gqa_k_norm[l], 4)[None],
            b512=b512, b256=b512[:256, :256], b128=b512[:128, :128],
            lb_f=lb_fwd[l][None], lb_b=lb_bwd[l][None], hgrn_norm=hgrn_norm[l][None],
            w_branch=w_branch[l].astype(BF16), w_out=w_out[l].astype(BF16),
            w_mlp_in=w_mlp_in[l].astype(BF16), w_mlp_out=w_mlp_out[l].astype(BF16))
        sink = swa_sink[l][None]
        mod4 = mod[l].reshape(SUBLANES, N_MOD, 1, D_MODEL)

        pc, caches_out = _proj(x_ctx, mod4, lp, True, layer=l, caches_prev=caches_out)
        oa, states_out = _hgrn(pc, lp, BATCH, SEQ, l, state_prev=states_out)
        ob, oc, od = _attn_ctx(sink, pc)
        x_ctx = _merge(x_ctx, mod4, oa, ob, oc, od, pc["gates"], lp, True)

        pq, _ = _proj(x_lat, mod4, lp, False, tabs=tabs)
        (oa,) = _hgrn(pq, lp, DEC_BATCH, DEC_SEQ, l, state_in=state_hgrn)
        ob, oc, od = _attn_lat(sink, lp, pq, caches_in, l)
        x_lat = _merge(x_lat, mod4, oa, ob, oc, od, pq["gates"], lp, False)

    skc, svc, mckv, mkr, gkc, gvc = caches_out
    kv_shape = (BATCH, DEPTH, SEQ, N_KV, HEAD_DIM)
    return (x_ctx.reshape(BATCH, SEQ, D_MODEL), x_lat.reshape(DEC_BATCH, DEC_SEQ, D_MODEL), states_out,
            skc.reshape(kv_shape), svc.reshape(kv_shape), mckv, mkr, gkc.reshape(kv_shape), gvc.reshape(kv_shape))
```

```python
import functools

import jax
import jax.numpy as jnp
import numpy as np
from jax import lax
from jax.experimental import pallas as pl
from jax.experimental.pallas import tpu as pltpu

D_MODEL = 1024
BATCH, SEQ = 32, 256
DEC_BATCH, DEC_SEQ = 2, 2048
DEPTH = 2
PAST_LEN = 512
GRID_W = 64
ROPE_BASE = 10000.0
NORM_EPS = 1e-6
MASK_VALUE = -1e30
LOG_FLOOR = 1e-30
N_MOD = 6
N_BRANCH = 4
BRANCH_W = 512
D_FF = 4 * D_MODEL
HGRN_HEADS, HGRN_DK = 4, 128
SWA_WINDOW = 128
MLA_HEADS, MLA_Q_RANK, MLA_KV_RANK, MLA_NOPE, MLA_ROPE, MLA_V = 8, 256, 128, 64, 32, 64
HEAD_DIM = 64
N_HEADS, N_KV = 8, 2

N_CTX = BATCH * SEQ
N_LAT = DEC_BATCH * DEC_SEQ

LANES = 128
SUBLANES = 8
VMEM_LIMIT_BYTES = 56 * 1024 * 1024

TM = 256
LAT_TILES_PER_SEQ = DEC_SEQ // TM
HGRN_CHUNK = 64
HGRN_UNROLL = 4
TQ_LAT = 128

BF16 = jnp.bfloat16
F32 = jnp.float32

_C_HGRN = 0
_C_SWA_Q = 2560
_C_SWA_KV = 3072
_C_SWA_KDUP = 3328
_C_SWA_VDUP = 3584
_C_MLA_CQ = 3840
_C_MLA_CKV = 4096
_C_MLA_KR = 4224
_C_GQA_Q = 4352
_C_GQA_KV = 4864
_C_GQA_KDUP = 5120
_C_GQA_VDUP = 5376
_C_GATES = 5632
W_IN_COLS = _C_GATES + N_BRANCH * D_MODEL


def _w_in_segments():
    off, o = {}, 0
    for name, w in (("hgrn", 2560), ("swa_q", 512), ("swa_k", 128), ("swa_v", 128), ("mla_cq", 256),
                    ("mla_ckv", 128), ("mla_kr", 32), ("gqa_q", 512), ("gqa_k", 128), ("gqa_v", 128),
                    ("gates", 4096)):
        off[name] = o
        o += w

    def dup(base):
        return [(base, 64), (base, 64), (base + 64, 64), (base + 64, 64)]

    segs = ([(off["hgrn"], 2560), (off["swa_q"], 512), (off["swa_k"], 256)] + dup(off["swa_k"]) + dup(off["swa_v"])
            + [(off["mla_cq"], 256), (off["mla_ckv"], 128), (off["mla_kr"], 32), (-1, 96), (off["gqa_q"], 512),
               (off["gqa_k"], 256)] + dup(off["gqa_k"]) + dup(off["gqa_v"]) + [(off["gates"], 4096)])
    assert sum(w for _, w in segs) == W_IN_COLS
    return segs


_W_IN_SEGMENTS = _w_in_segments()


def _permute_w_in(w):
    parts = [jnp.zeros((D_MODEL, n), BF16) if a < 0 else w[:, a:a + n].astype(BF16) for a, n in _W_IN_SEGMENTS]
    return jnp.concatenate(parts, axis=1)


def _sigmoid_pair(x):
    a = jnp.exp(-jnp.abs(x))
    r = 1.0 / (1.0 + a)
    ar = a * r
    pos = x >= 0
    return jnp.where(pos, r, ar), jnp.where(pos, ar, r)


def _sigmoid(x):
    return _sigmoid_pair(x)[0]


def _silu(x):
    return x * _sigmoid(x)


def _rms(x, gain):
    return x * lax.rsqrt(jnp.mean(x * x, axis=-1, keepdims=True) + NORM_EPS) * gain


def _dot(a, b):
    return jnp.dot(a, b, preferred_element_type=F32)


def _dot_nt(a, b):
    return lax.dot_general(a, b, (((1,), (1,)), ((), ())), preferred_element_type=F32)


def _dot_tn(a, b):
    return lax.dot_general(a, b, (((0,), (0,)), ((), ())), preferred_element_type=F32)


def _tile_lanes(t, width):
    reps = width // LANES
    return t if reps == 1 else jnp.concatenate([t] * reps, axis=1)


def _rope(x, tab_ref, base, shift):
    w = x.shape[1]
    c = _tile_lanes(tab_ref[base], w)
    s1 = _tile_lanes(tab_ref[base + 1], w)
    s2 = _tile_lanes(tab_ref[base + 2], w)
    return x * c + pltpu.roll(x, shift, 1) * s1 + pltpu.roll(x, w - shift, 1) * s2


def _head_rms(x, bmat_ref, gain):
    ms = _dot((x * x).astype(BF16), bmat_ref[...])
    return x * lax.rsqrt(ms + NORM_EPS) * gain


def _const_spec(shape):
    nd = len(shape)
    return pl.BlockSpec(shape, lambda *_: (0,) * nd, pipeline_mode=pl.Buffered(1))


_ANY_SPEC = pl.BlockSpec(memory_space=pl.ANY)


def _params(n_grid_dims):
    return pltpu.CompilerParams(dimension_semantics=("arbitrary",) * n_grid_dims, vmem_limit_bytes=VMEM_LIMIT_BYTES)


_ADA_TN = 2048


def _ada_kernel(c_ref, w_ref, b_ref, o_ref):
    s = _silu(c_ref[...]).astype(BF16)
    o_ref[0] = _dot(s, w_ref[0].astype(BF16)) + b_ref[0]


def _ada(cond8, w_ada, b_ada):
    n = N_MOD * D_MODEL
    return pl.pallas_call(
        _ada_kernel,
        out_shape=jax.ShapeDtypeStruct((DEPTH, SUBLANES, n), F32),
        grid=(DEPTH, n // _ADA_TN),
        in_specs=[
            pl.BlockSpec((SUBLANES, D_MODEL), lambda l, j: (0, 0)),
            pl.BlockSpec((1, D_MODEL, _ADA_TN), lambda l, j: (l, 0, j)),
            pl.BlockSpec((1, 1, _ADA_TN), lambda l, j: (l, 0, j)),
        ],
        out_specs=pl.BlockSpec((1, SUBLANES, _ADA_TN), lambda l, j: (l, 0, j)),
        compiler_params=_params(2),
        name="ada",
    )(cond8, w_ada, b_ada.reshape(DEPTH, 1, n))


def _mod_spec(is_ctx):
    if is_ctx:
        return pl.BlockSpec((1, N_MOD, 1, D_MODEL), lambda i: (0, 0, 0, 0))
    return pl.BlockSpec((1, N_MOD, 1, D_MODEL), lambda i: (1 + i // LAT_TILES_PER_SEQ, 0, 0, 0))


LOG2E = float(np.log2(np.e))
_SCALE_64 = HEAD_DIM ** -0.5 * LOG2E
_SCALE_MLA = (MLA_NOPE + MLA_ROPE) ** -0.5 * LOG2E
_TAB_64, _TAB_MLAQ, _TAB_KR = 0, 3, 6

_HGRN_PRE = ("hq", "hff", "hfb", "hi", "hg")
_ATTN_OUTS = (
    ("sq", 512, BF16), ("sk", 256, BF16), ("sv", 256, BF16),
    ("mq", 1024, BF16), ("mk", 1024, BF16), ("mv", 512, BF16),
    ("gq", 512, BF16), ("gk", 256, BF16), ("gv", 256, BF16), ("gates", 4096, BF16))
_CACHE_OUTS = (("skc", 128), ("svc", 128), ("mckv", 128), ("mkr", MLA_ROPE), ("gkc", 128), ("gvc", 128))
_PROJ_CONSTS = ("npre", "w", "mlaq_g", "mlakv_g", "wuq", "wk", "wuv", "gq_g", "gk_g", "b512", "b256", "b128")
_HGRN_CONSTS = ("lbf", "lbb", "gn")


def _proj_out_names(is_ctx):
    attn = [n for n, _, _ in _ATTN_OUTS]
    if is_ctx:
        return ["oa"] + attn + [n for n, _ in _CACHE_OUTS] + ["st"]
    return list(_HGRN_PRE) + attn


def _proj_kernel(*refs, is_ctx, n_alias):
    names = ["x", "mod"] + list(_PROJ_CONSTS) + (list(_HGRN_CONSTS) if is_ctx else ["tab"])
    r = dict(zip(names, refs))
    pos = len(names) + n_alias
    for name in _proj_out_names(is_ctx) + (["hbuf"] if is_ctx else []):
        r[name] = refs[pos]
        pos += 1

    x = r["x"][...]
    sh1 = r["mod"][0, 0]
    sc1 = r["mod"][0, 1]
    h = (_rms(x, r["npre"][...]) * (1.0 + sc1) + sh1).astype(BF16)
    w_ref = r["w"]

    def seg(a, w):
        return _dot(h, w_ref[:, a:a + w])

    def rope(v, base, shift):
        return v if is_ctx else _rope(v, r["tab"], base, shift)

    for j, name in enumerate(_HGRN_PRE):
        u = seg(_C_HGRN + 512 * j, 512)
        if is_ctx:
            r["hbuf"][j] = u
        else:
            r[name][...] = u

    def hgrn_all(between):
        C = HGRN_CHUNK
        n = TM // C
        lvls = _hgrn_levels()
        heads = range(HGRN_HEADS)
        cols = [slice(LANES * hd, LANES * (hd + 1)) for hd in heads]
        st = [[None, None] for _ in heads]
        outs = [[[None] * n, [None] * n] for _ in heads]
        for c in range(n):
            for hd in heads:
                for d, cc, lb in ((0, c, r["lbf"]), (1, n - 1 - c, r["lbb"])):
                    rows = slice(C * cc, C * (cc + 1))
                    k, lf2 = _hgrn_forget(r["hbuf"][1 + d, rows, cols[hd]], lb[:, cols[hd]])
                    outs[hd][d][cc], st[hd][d] = _hgrn_chunk(
                        _silu(r["hbuf"][0, rows, cols[hd]]), k, lf2, r["hbuf"][3, rows, cols[hd]], st[hd][d],
                        d == 0, lvls[d])
            between(c)
        for hd in heads:
            for c in range(n):
                rows = slice(C * c, C * (c + 1))
                r["oa"][rows, cols[hd]] = _hgrn_finish(outs[hd][0][c] + outs[hd][1][c], r["gn"][:, cols[hd]],
                                                       r["hbuf"][4, rows, cols[hd]])
            for d in range(2):
                r["st"][0, 0, d, hd] = st[hd][d].T

    def swa():
        r["sq"][...] = (rope(seg(_C_SWA_Q, 512), _TAB_64, 16) * _SCALE_64).astype(BF16)
        r["sk"][...] = rope(seg(_C_SWA_KDUP, 256), _TAB_64, 16).astype(BF16)
        r["sv"][...] = seg(_C_SWA_VDUP, 256).astype(BF16)
        if is_ctx:
            kv = seg(_C_SWA_KV, 256)
            r["skc"][0, 0] = kv[:, :128]
            r["svc"][0, 0] = kv[:, 128:]

    def mla():
        cq = _rms(seg(_C_MLA_CQ, 256), r["mlaq_g"][...]).astype(BF16)
        q = _dot(cq, r["wuq"][...])
        r["mq"][...] = (rope(q, _TAB_MLAQ, 8) * _SCALE_MLA).astype(BF16)
        ckv = _rms(seg(_C_MLA_CKV, 128), r["mlakv_g"][...])
        kr = seg(_C_MLA_KR, 128)
        if is_ctx:
            r["mckv"][0, 0] = ckv
            r["mkr"][0, 0] = kr[:, :MLA_ROPE]
        kr_used = rope(kr, _TAB_KR, 8)
        ckv16 = ckv.astype(BF16)
        r["mk"][...] = _dot(jnp.concatenate([ckv16, kr_used.astype(BF16)], axis=1), r["wk"][...]).astype(BF16)
        r["mv"][...] = _dot(ckv16, r["wuv"][...]).astype(BF16)

    def gqa():
        gq = _head_rms(seg(_C_GQA_Q, 512), r["b512"], r["gq_g"][...])
        r["gq"][...] = (rope(gq, _TAB_64, 16) * _SCALE_64).astype(BF16)
        gk = _head_rms(seg(_C_GQA_KDUP, 256), r["b256"], r["gk_g"][...])
        r["gk"][...] = rope(gk, _TAB_64, 16).astype(BF16)
        r["gv"][...] = seg(_C_GQA_VDUP, 256).astype(BF16)
        if is_ctx:
            kv = seg(_C_GQA_KV, 256)
            r["gkc"][0, 0] = _head_rms(kv[:, :128], r["b128"], r["gk_g"][:, :128])
            r["gvc"][0, 0] = kv[:, 128:]

    def gates(j):
        r["gates"][:, D_MODEL * j:D_MODEL * (j + 1)] = _sigmoid(seg(_C_GATES + D_MODEL * j, D_MODEL)).astype(BF16)

    tasks = [swa, mla, gqa] + [functools.partial(gates, j) for j in range(N_BRANCH)]
    if is_ctx:
        per_step = -(-len(tasks) // (TM // HGRN_CHUNK))

        def between(c):
            for t in tasks[per_step * c:per_step * (c + 1)]:
                t()

        hgrn_all(between)
    else:
        for t in tasks:
            t()


def _proj(x, mod4, lp, is_ctx, layer=0, tabs=None, prev=None):
    n_tok = x.shape[0]
    const_shapes = dict(npre=(1, D_MODEL), w=(D_MODEL, W_IN_COLS), mlaq_g=(1, MLA_Q_RANK), mlakv_g=(1, MLA_KV_RANK),
                        wuq=(MLA_Q_RANK, 1024), wk=(256, 1024), wuv=(MLA_KV_RANK, 512), gq_g=(1, 512), gk_g=(1, 256),
                        b512=(512, 512), b256=(256, 256), b128=(128, 128), lbf=(1, 512), lbb=(1, 512), gn=(1, 512))
    const_args = dict(npre=lp["norm_mix_pre"], w=lp["w_in"], mlaq_g=lp["mla_q_norm"], mlakv_g=lp["mla_kv_norm"],
                      wuq=lp["w_uq"], wk=lp["w_k"], wuv=lp["w_uv"], gq_g=lp["gqa_q_gain"], gk_g=lp["gqa_k_gain"],
                      b512=lp["b512"], b256=lp["b256"], b128=lp["b128"], lbf=lp["lb_f"], lbb=lp["lb_b"],
                      gn=lp["hgrn_norm"])
    consts = list(_PROJ_CONSTS) + (list(_HGRN_CONSTS) if is_ctx else [])
    in_specs = [pl.BlockSpec((TM, D_MODEL), lambda i: (i, 0)), _mod_spec(is_ctx)]
    in_specs += [_const_spec(const_shapes[n]) for n in consts]
    args = [x, mod4] + [const_args[n] for n in consts]
    if not is_ctx:
        in_specs.append(pl.BlockSpec((9, TM, LANES), lambda i: (0, i % LAT_TILES_PER_SEQ, 0)))
        args.append(tabs)

    def tok(w):
        return pl.BlockSpec((TM, w), lambda i: (i, 0))

    specs = {n: (tok(w), jax.ShapeDtypeStruct((n_tok, w), dt)) for n, w, dt in _ATTN_OUTS}
    if is_ctx:
        specs["oa"] = (tok(512), jax.ShapeDtypeStruct((n_tok, 512), BF16))
        for n, w in _CACHE_OUTS:
            specs[n] = (pl.BlockSpec((1, 1, SEQ, w), lambda i: (i, layer, 0, 0)),
                        jax.ShapeDtypeStruct((BATCH, DEPTH, SEQ, w), F32))
        specs["st"] = (pl.BlockSpec((1, 1, 2, HGRN_HEADS, LANES, LANES), lambda i: (i, layer, 0, 0, 0, 0)),
                       jax.ShapeDtypeStruct((BATCH, DEPTH, 2, HGRN_HEADS, LANES, LANES), F32))
    else:
        for n in _HGRN_PRE:
            specs[n] = (tok(512), jax.ShapeDtypeStruct((n_tok, 512), F32))
    out_names = _proj_out_names(is_ctx)
    aliases = {}
    n_alias = 0
    if prev is not None:
        for n in [n for n, _ in _CACHE_OUTS] + ["st"]:
            aliases[len(args)] = out_names.index(n)
            in_specs.append(_ANY_SPEC)
            args.append(prev[n])
            n_alias += 1
    outs = pl.pallas_call(
        functools.partial(_proj_kernel, is_ctx=is_ctx, n_alias=n_alias),
        out_shape=[specs[n][1] for n in out_names], grid=(n_tok // TM,), in_specs=in_specs,
        out_specs=[specs[n][0] for n in out_names],
        scratch_shapes=[pltpu.VMEM((len(_HGRN_PRE), TM, 512), F32)] if is_ctx else [],
        input_output_aliases=aliases, compiler_params=_params(1),
        name="proj_ctx" if is_ctx else "proj_lat",
    )(*args)
    return dict(zip(out_names, outs))


def _pair_levels():
    C = HGRN_CHUNK
    ti = lax.broadcasted_iota(jnp.int32, (C, C), 0)
    si = lax.broadcasted_iota(jnp.int32, (C, C), 1)
    x = ti ^ si
    lvl = jnp.where(x == 0, -1, 0)
    b = 2
    while b < C:
        lvl = lvl + (x >= b).astype(jnp.int32)
        b *= 2
    return ti, si, lvl


def _hgrn_forget(f_pre, lb):
    s_pos, s_neg = _sigmoid_pair(f_pre)
    f = lb + (1.0 - lb) * s_pos
    return (1.0 - lb) * s_neg, jnp.log2(jnp.maximum(f, LOG_FLOOR))


def _hgrn_levels():
    ti, si, lvl = _pair_levels()
    return jnp.where(ti > si, lvl, -1), jnp.where(ti < si, lvl, -1)


def _hgrn_chunk(q, k, lf2, v, st, fwd, lvl):
    C = HGRN_CHUNK
    nv = C // SUBLANES
    sub = lax.broadcasted_iota(jnp.int32, (nv, SUBLANES, LANES), 1)

    p = lf2.reshape(nv, SUBLANES, LANES)
    for s in (1, 2, 4):
        if fwd:
            p = p + jnp.where(sub >= s, pltpu.roll(p, s, 1), 0.0)
        else:
            p = p + jnp.where(sub < SUBLANES - s, pltpu.roll(p, SUBLANES - s, 1), 0.0)

    a_mat = jnp.zeros((C, C), F32)

    def add_level(a_mat, e, level):
        a_m = _dot_nt((q * e).astype(BF16), (k * e).astype(BF16))
        return jnp.where(lvl == level, a_m, a_mat)

    def row(r):
        return jnp.broadcast_to(p[:, r:r + 1, :], p.shape)

    odd = (sub & 1) == 1
    if fwd:
        bnd = (jnp.where(odd, pltpu.roll(p, 1, 1), p), jnp.where(sub < 4, row(1), row(5)), row(3))
    else:
        bnd = (jnp.where(odd, p, pltpu.roll(p, SUBLANES - 1, 1)), jnp.where(sub < 4, row(2), row(6)), row(4))
    for level, b in enumerate(bnd):
        a_mat = add_level(a_mat, jnp.exp2(-jnp.abs(p - b)).reshape(C, LANES), level)

    p = p.reshape(C, LANES)
    m = SUBLANES
    while m < C:
        nb = C // (2 * m)
        p4 = p.reshape(nb, 2, m, LANES)
        lo, hi = p4[:, 0], p4[:, 1]
        if fwd:
            tot = lo[:, m - 1:m]
            e_lo, e_hi = tot - lo, hi
            p_lo, p_hi = lo, hi + tot
        else:
            tot = hi[:, 0:1]
            e_lo, e_hi = lo, tot - hi
            p_lo, p_hi = lo + tot, hi
        e = jnp.exp2(jnp.concatenate([e_lo[:, None], e_hi[:, None]], axis=1).reshape(C, LANES))
        p = jnp.concatenate([p_lo[:, None], p_hi[:, None]], axis=1).reshape(C, LANES)
        a_mat = add_level(a_mat, e, int(np.log2(m)))
        m *= 2
    v16 = v.astype(BF16)
    o = jnp.sum(q * k, axis=-1, keepdims=True) * v + _dot(a_mat.astype(BF16), v16)

    if st is not None:
        o = o + _dot_nt((q * jnp.exp2(p)).astype(BF16), st.astype(BF16))
    tot = p[C - 1:C] if fwd else p[0:1]
    k_end = (k * jnp.exp2(tot - p)).astype(BF16)
    st_new = _dot_tn(v16, k_end)
    if st is not None:
        st_new = st_new + st * jnp.exp2(tot)
    return o, st_new


def _hgrn_finish(o, gn, g_pre):
    return (_rms(o, gn) * _silu(g_pre)).astype(BF16)


def _hgrn_head_static(read, lbf, lbb, gn, n_chunks):
    lvls = _hgrn_levels()
    st = [None, None]
    outs = [[None] * n_chunks, [None] * n_chunks]
    for c in range(n_chunks):
        for d, cc, lb in ((0, c, lbf), (1, n_chunks - 1 - c, lbb)):
            k, lf2 = _hgrn_forget(read(1 + d, cc), lb)
            outs[d][cc], st[d] = _hgrn_chunk(_silu(read(0, cc)), k, lf2, read(3, cc), st[d], d == 0, lvls[d])
    return [_hgrn_finish(outs[0][c] + outs[1][c], gn, read(4, c)) for c in range(n_chunks)], st


def _hgrn_lat_kernel(lbf_ref, lbb_ref, gn_ref, q_ref, ff_ref, fb_ref, v_ref, g_ref, s0_ref, o_ref,
                     of_ref, ob_ref, st_ref, *, n_chunks):
    C = HGRN_CHUNK
    for d in range(2):
        st_ref[d] = s0_ref[0, 0, d, 0].T
    lbf = lbf_ref[...]
    lbb = lbb_ref[...]
    lvls = _hgrn_levels()

    def body(c, carry):
        rf = pl.multiple_of(c * C, C)
        rb = pl.multiple_of((n_chunks - 1 - c) * C, C)
        for d, r0, f_ref, lb, out_ref in ((0, rf, ff_ref, lbf, of_ref), (1, rb, fb_ref, lbb, ob_ref)):
            sl = pl.ds(r0, C)
            k, lf2 = _hgrn_forget(f_ref[sl, :], lb)
            out_ref[sl, :], st_ref[d] = _hgrn_chunk(_silu(q_ref[sl, :]), k, lf2, v_ref[sl, :], st_ref[d], d == 0,
                                                    lvls[d])
        return carry

    lax.fori_loop(0, n_chunks, body, 0, unroll=HGRN_UNROLL)
    o_ref[...] = _hgrn_finish(of_ref[...] + ob_ref[...], gn_ref[...], g_ref[...])


def _hgrn_lat(p, lp, layer, state_in):
    tok_spec = pl.BlockSpec((DEC_SEQ, LANES), lambda b, h: (b, h))
    vec_spec = pl.BlockSpec((1, LANES), lambda b, h: (0, h))
    state_spec = pl.BlockSpec((1, 1, 2, 1, LANES, LANES), lambda b, h: (b, layer, 0, h, 0, 0))
    return pl.pallas_call(
        functools.partial(_hgrn_lat_kernel, n_chunks=DEC_SEQ // HGRN_CHUNK),
        out_shape=jax.ShapeDtypeStruct((N_LAT, 512), BF16), grid=(DEC_BATCH, HGRN_HEADS),
        in_specs=[vec_spec] * 3 + [tok_spec] * 5 + [state_spec], out_specs=tok_spec,
        scratch_shapes=[pltpu.VMEM((DEC_SEQ, LANES), F32), pltpu.VMEM((DEC_SEQ, LANES), F32),
                        pltpu.VMEM((2, LANES, LANES), F32)],
        compiler_params=_params(2), name="hgrn_lat",
    )(lp["lb_f"], lp["lb_b"], lp["hgrn_norm"], p["hq"], p["hff"], p["hfb"], p["hi"], p["hg"], state_in)


def _lane_lo(shape):
    return lax.broadcasted_iota(jnp.int32, shape, len(shape) - 1) < HEAD_DIM


def _softmax_pv(segs, sink_col=None):
    mx = None
    for s, _ in segs:
        m = jnp.max(s, axis=-1, keepdims=True)
        mx = m if mx is None else jnp.maximum(mx, m)
    if sink_col is not None:
        mx = jnp.maximum(mx, sink_col)
    den = None
    acc = None
    for s, v in segs:
        e = jnp.exp2(s - mx)
        dsum = jnp.sum(e, axis=-1, keepdims=True)
        den = dsum if den is None else den + dsum
        pv = _dot(e.astype(BF16), v)
        acc = pv if acc is None else acc + pv
    if sink_col is not None:
        den = den + jnp.exp2(sink_col - mx)
    return acc / den


def _gqa_group(q_ref, g, tq, segs_fn, sink_ref=None):
    lo = _lane_lo((tq, LANES))
    parts = []
    for j in range(2):
        qp = q_ref[:, 256 * g + 128 * j:256 * g + 128 * (j + 1)]
        zero = jnp.zeros_like(qp)
        parts += [jnp.where(lo, qp, zero), jnp.where(lo, zero, qp)]
    q4 = jnp.concatenate(parts, axis=0)
    sink_col = None
    if sink_ref is not None:
        sink_col = jnp.concatenate(
            [jnp.broadcast_to(sink_ref[0:1, 4 * g + i:4 * g + i + 1] * LOG2E, (tq, 1)) for i in range(4)], axis=0)
    pv = _softmax_pv(segs_fn(q4), sink_col)
    return (jnp.where(lo, pv[0:tq], pv[tq:2 * tq]), jnp.where(lo, pv[2 * tq:3 * tq], pv[3 * tq:4 * tq]))


def _mla_heads(q_ref, tq, segs_fn, o_ref):
    lo = _lane_lo((tq, LANES))
    for hp in range(MLA_HEADS // 2):
        outs = []
        for h in (2 * hp, 2 * hp + 1):
            q_h = q_ref[:, 128 * h:128 * (h + 1)]
            outs.append(_softmax_pv(segs_fn(h, q_h)))
        o_ref[:, 128 * hp:128 * (hp + 1)] = jnp.where(lo, outs[0], outs[1]).astype(BF16)


_ATTN_IN = ("sq", "sk", "sv", "mq", "mk", "mv", "gq", "gk", "gv")
_ATTN_WIDTH = dict(sq=512, sk=256, sv=256, mq=1024, mk=1024, mv=512, gq=512, gk=256, gv=256)


def _attn_ctx_kernel(sink_ref, sq_ref, sk_ref, sv_ref, mq_ref, mk_ref, mv_ref, gq_ref, gk_ref, gv_ref,
                     ob_ref, oc_ref, od_ref):
    tq = SEQ
    for g in range(N_KV):
        def swa_segs(q4, g=g):
            return [(_dot_nt(q4, sk_ref[:, 128 * g:128 * (g + 1)]), sv_ref[:, 128 * g:128 * (g + 1)])]

        a, b = _gqa_group(sq_ref, g, tq, swa_segs, sink_ref)
        ob_ref[:, 256 * g:256 * g + 128] = a.astype(BF16)
        ob_ref[:, 256 * g + 128:256 * (g + 1)] = b.astype(BF16)

        def gqa_segs(q4, g=g):
            return [(_dot_nt(q4, gk_ref[:, 128 * g:128 * (g + 1)]), gv_ref[:, 128 * g:128 * (g + 1)])]

        a, b = _gqa_group(gq_ref, g, tq, gqa_segs)
        od_ref[:, 256 * g:256 * g + 128] = a.astype(BF16)
        od_ref[:, 256 * g + 128:256 * (g + 1)] = b.astype(BF16)

    def mla_segs(h, q_h):
        return [(_dot_nt(q_h, mk_ref[:, 128 * h:128 * (h + 1)]), mv_ref[:, 128 * (h // 2):128 * (h // 2 + 1)])]

    _mla_heads(mq_ref, tq, mla_segs, oc_ref)


def _attn_ctx(sink, p):
    def spec(w):
        return pl.BlockSpec((SEQ, w), lambda b: (b, 0))

    return pl.pallas_call(
        _attn_ctx_kernel,
        out_shape=[jax.ShapeDtypeStruct((N_CTX, 512), BF16)] * 3,
        grid=(BATCH,),
        in_specs=[_const_spec((1, N_HEADS))] + [spec(_ATTN_WIDTH[n]) for n in _ATTN_IN],
        out_specs=[spec(512)] * 3,
        compiler_params=_params(1),
        name="attn_ctx",
    )(sink, *[p[n] for n in _ATTN_IN])


def _dup_heads(x):
    lo = _lane_lo(x.shape)
    r = pltpu.roll(x, HEAD_DIM, 1)
    return jnp.where(lo, x, r), jnp.where(lo, r, x)


def _attn_lat_kernel(sink_ref, wk_ref, wuv_ref,
                     sq_ref, sk_ref, sv_ref, mq_ref, mk_ref, mv_ref, gq_ref, gk_ref, gv_ref,
                     csk_ref, csv_ref, cckv_ref, ckr_ref, cgk_ref, cgv_ref,
                     ob_ref, oc_ref, od_ref,
                     cs_k, cs_v, cm_k, cm_v, cg_k, cg_v):
    tq = TQ_LAT
    qi = pl.program_id(1)

    @pl.when(qi == 0)
    def _():
        for src, dst in ((csk_ref, cs_k), (csv_ref, cs_v), (cgk_ref, cg_k), (cgv_ref, cg_v)):
            d0, d1 = _dup_heads(src[0, 0])
            dst[0] = d0.astype(BF16)
            dst[1] = d1.astype(BF16)
        ckv16 = cckv_ref[0, 0].astype(BF16)
        k = _dot(ckv16, wk_ref[0:MLA_KV_RANK, :]) + _dot(ckr_ref[0, 0].astype(BF16),
                                                           wk_ref[MLA_KV_RANK:MLA_KV_RANK + MLA_ROPE, :])
        cm_k[...] = k.astype(BF16)
        cm_v[...] = _dot(ckv16, wuv_ref[...]).astype(BF16)

    q0 = qi * tq
    win = tq + 2 * SWA_WINDOW
    start = pl.multiple_of(jnp.clip(q0 - SWA_WINDOW, 0, DEC_SEQ - win), SWA_WINDOW)
    rows = lax.broadcasted_iota(jnp.int32, (4 * tq, win), 0)
    qpos = q0 + (rows & (tq - 1))
    kpos = start + lax.broadcasted_iota(jnp.int32, (4 * tq, win), 1)
    band = jnp.abs(qpos - kpos) <= SWA_WINDOW
    for g in range(N_KV):
        def swa_segs(q4, g=g):
            ks = sk_ref[pl.ds(start, win), 128 * g:128 * (g + 1)]
            vs = sv_ref[pl.ds(start, win), 128 * g:128 * (g + 1)]
            s_own = jnp.where(band, _dot_nt(q4, ks), MASK_VALUE)
            return [(_dot_nt(q4, cs_k[g]), cs_v[g]), (s_own, vs)]

        a, b = _gqa_group(sq_ref, g, tq, swa_segs, sink_ref)
        ob_ref[:, 256 * g:256 * g + 128] = a.astype(BF16)
        ob_ref[:, 256 * g + 128:256 * (g + 1)] = b.astype(BF16)

        def gqa_segs(q4, g=g):
            return [(_dot_nt(q4, cg_k[g]), cg_v[g]),
                    (_dot_nt(q4, gk_ref[:, 128 * g:128 * (g + 1)]), gv_ref[:, 128 * g:128 * (g + 1)])]

        a, b = _gqa_group(gq_ref, g, tq, gqa_segs)
        od_ref[:, 256 * g:256 * g + 128] = a.astype(BF16)
        od_ref[:, 256 * g + 128:256 * (g + 1)] = b.astype(BF16)

    def mla_segs(h, q_h):
        hp = h // 2
        return [(_dot_nt(q_h, cm_k[:, 128 * h:128 * (h + 1)]), cm_v[:, 128 * hp:128 * (hp + 1)]),
                (_dot_nt(q_h, mk_ref[:, 128 * h:128 * (h + 1)]), mv_ref[:, 128 * hp:128 * (hp + 1)])]

    _mla_heads(mq_ref, tq, mla_segs, oc_ref)


def _attn_lat(sink, lp, p, caches, layer):
    nq = DEC_SEQ // TQ_LAT

    def qspec(w):
        return pl.BlockSpec((TQ_LAT, w), lambda b, i: (b * nq + i, 0))

    def kspec(w):
        return pl.BlockSpec((DEC_SEQ, w), lambda b, i: (b, 0))

    def cspec(w):
        return pl.BlockSpec((1, 1, PAST_LEN, w), lambda b, i: (b, layer, 0, 0))

    is_q = dict(sq=True, mq=True, gq=True)
    in_specs = [_const_spec((1, N_HEADS)), _const_spec((256, 1024)), _const_spec((MLA_KV_RANK, 512))]
    in_specs += [(qspec if is_q.get(n) else kspec)(_ATTN_WIDTH[n]) for n in _ATTN_IN]
    in_specs += [cspec(128), cspec(128), cspec(128), cspec(MLA_ROPE), cspec(128), cspec(128)]
    return pl.pallas_call(
        _attn_lat_kernel,
        out_shape=[jax.ShapeDtypeStruct((N_LAT, 512), BF16)] * 3,
        grid=(DEC_BATCH, nq),
        in_specs=in_specs,
        out_specs=[qspec(512)] * 3,
        scratch_shapes=[pltpu.VMEM((N_KV, PAST_LEN, LANES), BF16), pltpu.VMEM((N_KV, PAST_LEN, LANES), BF16),
                        pltpu.VMEM((PAST_LEN, 1024), BF16), pltpu.VMEM((PAST_LEN, 512), BF16),
                        pltpu.VMEM((N_KV, PAST_LEN, LANES), BF16), pltpu.VMEM((N_KV, PAST_LEN, LANES), BF16)],
        compiler_params=_params(2),
        name="attn_lat",
    )(sink, lp["w_k"], lp["w_uv"], *[p[n] for n in _ATTN_IN], *caches)


def _merge_kernel(x_ref, mod_ref, oa_ref, ob_ref, oc_ref, od_ref, gates_ref, wb_ref, wo_ref, npost_ref,
                  nmpre_ref, nmpost_ref, w1_ref, w2_ref, y_ref):
    x = x_ref[...]
    g1 = mod_ref[0, 2]
    sh2 = mod_ref[0, 3]
    sc2 = mod_ref[0, 4]
    g2 = mod_ref[0, 5]
    acc = None
    for n, o_ref in enumerate((oa_ref, ob_ref, oc_ref, od_ref)):
        br = _dot(o_ref[...], wb_ref[n]) * gates_ref[:, D_MODEL * n:D_MODEL * (n + 1)].astype(F32)
        acc = br if acc is None else acc + br
    y = _dot(acc.astype(BF16), wo_ref[...])
    x = x + g1 * _rms(y, npost_ref[...])
    h = (_rms(x, nmpre_ref[...]) * (1.0 + sc2) + sh2).astype(BF16)
    a = jnp.maximum(_dot(h, w1_ref[...]), 0.0)
    y = _dot((a * a).astype(BF16), w2_ref[...])
    y_ref[...] = x + g2 * _rms(y, nmpost_ref[...])


def _merge(x, mod4, o_a, o_b, o_c, o_d, gates, lp, is_ctx):
    n_tok = x.shape[0]

    def tok(w):
        return pl.BlockSpec((TM, w), lambda i: (i, 0))

    in_specs = [tok(D_MODEL), _mod_spec(is_ctx), tok(512), tok(512), tok(512), tok(512), tok(4096),
                _const_spec((N_BRANCH, BRANCH_W, D_MODEL)), _const_spec((D_MODEL, D_MODEL)),
                _const_spec((1, D_MODEL)), _const_spec((1, D_MODEL)), _const_spec((1, D_MODEL)),
                _const_spec((D_MODEL, D_FF)), _const_spec((D_FF, D_MODEL))]
    return pl.pallas_call(
        _merge_kernel, out_shape=jax.ShapeDtypeStruct((n_tok, D_MODEL), F32), grid=(n_tok // TM,),
        in_specs=in_specs, out_specs=tok(D_MODEL), compiler_params=_params(1),
        name="merge_ctx" if is_ctx else "merge_lat",
    )(x, mod4, o_a, o_b, o_c, o_d, gates, lp["w_branch"], lp["w_out"], lp["norm_mix_post"], lp["norm_mlp_pre"],
      lp["norm_mlp_post"], lp["w_mlp_in"], lp["w_mlp_out"])


def _rope_tables():
    t = np.arange(DEC_SEQ)
    row, col = (t // GRID_W).astype(np.float32), (t % GRID_W).astype(np.float32)

    def pattern(d):
        nf = d // 4
        inv = (ROPE_BASE ** (-np.arange(nf, dtype=np.float32) / nf)).astype(np.float32)
        ang = np.concatenate([row[:, None] * inv, row[:, None] * inv, col[:, None] * inv, col[:, None] * inv], axis=1)
        cos, sin = np.cos(ang), np.sin(ang)
        second = (np.arange(d) % (2 * nf)) >= nf
        return cos, np.where(second, sin, 0.0), np.where(second, 0.0, -sin)

    out = np.zeros((9, DEC_SEQ, LANES), np.float32)
    out[[0, 3, 6]] = 1.0
    c, s1, s2 = pattern(HEAD_DIM)
    for a, tab in enumerate((c, s1, s2)):
        out[_TAB_64 + a] = np.tile(tab, (1, LANES // HEAD_DIM))
    c, s1, s2 = pattern(MLA_ROPE)
    for a, tab in enumerate((c, s1, s2)):
        out[_TAB_MLAQ + a][:, MLA_NOPE:MLA_NOPE + MLA_ROPE] = tab
        out[_TAB_KR + a][:, :MLA_ROPE] = tab
    return out


_ROPE_TABLES = _rope_tables()


def _block_mean_matrix(n):
    i = np.arange(n)
    return ((i[:, None] // HEAD_DIM) == (i[None, :] // HEAD_DIM)).astype(np.float32) / HEAD_DIM


def _mla_weight_layouts(w_uq, w_ukv):
    dq = MLA_NOPE + MLA_ROPE
    wq = w_uq.reshape(MLA_Q_RANK, MLA_HEADS, dq)
    wq = jnp.pad(wq, ((0, 0), (0, 0), (0, LANES - dq))).reshape(MLA_Q_RANK, MLA_HEADS * LANES)
    wkv = w_ukv.reshape(MLA_KV_RANK, MLA_HEADS, MLA_NOPE + MLA_V)
    wk = jnp.pad(wkv[:, :, :MLA_NOPE], ((0, 0), (0, 0), (0, LANES - MLA_NOPE))).reshape(MLA_KV_RANK, MLA_HEADS * LANES)
    place = np.zeros((LANES, MLA_HEADS, LANES), np.float32)
    for j in range(MLA_ROPE):
        place[j, :, MLA_NOPE + j] = 1.0
    w_k = jnp.concatenate([wk, jnp.asarray(place.reshape(LANES, MLA_HEADS * LANES))], axis=0)
    w_uv = wkv[:, :, MLA_NOPE:].reshape(MLA_KV_RANK, MLA_HEADS * MLA_V)
    return wq.astype(BF16), w_k.astype(BF16), w_uv.astype(BF16)


def _lower_bounds(p):
    s = jax.nn.softmax(p.astype(F32), axis=0)
    return jnp.cumsum(s, axis=0) - s[0]


def kernel(x_prompt, x_sample, state_hgrn, cache_swa_k, cache_swa_v, cache_mla_ckv, cache_mla_kr, cache_gqa_k, cache_gqa_v, c, c_ctx, w_ada, b_ada, norm_mix_pre, norm_mix_post, norm_mlp_pre, norm_mlp_post, w_in, hgrn_lb_fwd, hgrn_lb_bwd, hgrn_norm, swa_sink, mla_q_norm, mla_kv_norm, mla_w_uq, mla_w_ukv, gqa_q_norm, gqa_k_norm, w_branch, w_out, w_mlp_in, w_mlp_out):
    lb_fwd = _lower_bounds(hgrn_lb_fwd)
    lb_bwd = _lower_bounds(hgrn_lb_bwd)
    tabs = jnp.asarray(_ROPE_TABLES)
    b512 = jnp.asarray(_block_mean_matrix(512)).astype(BF16)

    cond8 = jnp.zeros((SUBLANES, D_MODEL), F32).at[0].set(c_ctx).at[1:1 + DEC_BATCH].set(c)
    mod = _ada(cond8, w_ada, b_ada)

    caches_in = (cache_swa_k.reshape(DEC_BATCH, DEPTH, PAST_LEN, 128),
                 cache_swa_v.reshape(DEC_BATCH, DEPTH, PAST_LEN, 128), cache_mla_ckv, cache_mla_kr,
                 cache_gqa_k.reshape(DEC_BATCH, DEPTH, PAST_LEN, 128),
                 cache_gqa_v.reshape(DEC_BATCH, DEPTH, PAST_LEN, 128))

    x_ctx = x_prompt.reshape(N_CTX, D_MODEL)
    x_lat = x_sample.reshape(N_LAT, D_MODEL)
    pc = None
    for l in range(DEPTH):
        w_uq, w_k, w_uv = _mla_weight_layouts(mla_w_uq[l], mla_w_ukv[l])
        lp = dict(
            w_in=_permute_w_in(w_in[l]),
            norm_mix_pre=norm_mix_pre[l][None], norm_mix_post=norm_mix_post[l][None],
            norm_mlp_pre=norm_mlp_pre[l][None], norm_mlp_post=norm_mlp_post[l][None],
            mla_q_norm=mla_q_norm[l][None], mla_kv_norm=mla_kv_norm[l][None], w_uq=w_uq, w_k=w_k, w_uv=w_uv,
            gqa_q_gain=jnp.tile(gqa_q_norm[l], N_HEADS)[None], gqa_k_gain=jnp.tile(gqa_k_norm[l], 4)[None],
            b512=b512, b256=b512[:256, :256], b128=b512[:128, :128],
            lb_f=lb_fwd[l][None], lb_b=lb_bwd[l][None], hgrn_norm=hgrn_norm[l][None],
            w_branch=w_branch[l].astype(BF16), w_out=w_out[l].astype(BF16),
            w_mlp_in=w_mlp_in[l].astype(BF16), w_mlp_out=w_mlp_out[l].astype(BF16))
        sink = swa_sink[l][None]
        mod4 = mod[l].reshape(SUBLANES, N_MOD, 1, D_MODEL)

        pc = _proj(x_ctx, mod4, lp, True, layer=l, prev=pc)
        ob, oc, od = _attn_ctx(sink, pc)
        x_ctx = _merge(x_ctx, mod4, pc["oa"], ob, oc, od, pc["gates"], lp, True)

        pq = _proj(x_lat, mod4, lp, False, tabs=tabs)
        oa = _hgrn_lat(pq, lp, l, state_hgrn)
        ob, oc, od = _attn_lat(sink, lp, pq, caches_in, l)
        x_lat = _merge(x_lat, mod4, oa, ob, oc, od, pq["gates"], lp, False)

    kv_shape = (BATCH, DEPTH, SEQ, N_KV, HEAD_DIM)
    return (x_ctx.reshape(BATCH, SEQ, D_MODEL), x_lat.reshape(DEC_BATCH, DEC_SEQ, D_MODEL), pc["st"],
            pc["skc"].reshape(kv_shape), pc["svc"].reshape(kv_shape), pc["mckv"], pc["mkr"],
            pc["gkc"].reshape(kv_shape), pc["gvc"].reshape(kv_shape))
```

```python
import functools

import jax
import jax.numpy as jnp
import numpy as np
from jax import lax
from jax.experimental import pallas as pl
from jax.experimental.pallas import tpu as pltpu

D_MODEL = 1024
BATCH, SEQ = 32, 256
DEC_BATCH, DEC_SEQ = 2, 2048
DEPTH = 2
PAST_LEN = 512
GRID_W = 64
ROPE_BASE = 10000.0
NORM_EPS = 1e-6
MASK_VALUE = -1e30
LOG_FLOOR = 1e-30
N_MOD = 6
N_BRANCH = 4
BRANCH_W = 512
D_FF = 4 * D_MODEL
HGRN_HEADS, HGRN_DK = 4, 128
SWA_WINDOW = 128
MLA_HEADS, MLA_Q_RANK, MLA_KV_RANK, MLA_NOPE, MLA_ROPE, MLA_V = 8, 256, 128, 64, 32, 64
HEAD_DIM = 64
N_HEADS, N_KV = 8, 2

N_CTX = BATCH * SEQ
N_LAT = DEC_BATCH * DEC_SEQ

LANES = 128
SUBLANES = 8
VMEM_LIMIT_BYTES = 56 * 1024 * 1024

TM = 256
LAT_TILES_PER_SEQ = DEC_SEQ // TM
HGRN_CHUNK = 64
HGRN_UNROLL = 4
TQ_LAT = 128

BF16 = jnp.bfloat16
F32 = jnp.float32

_C_HGRN = 0
_C_SWA_Q = 2560
_C_SWA_KV = 3072
_C_MLA_CQ = 3328
_C_MLA_CKV = 3584
_C_MLA_KR = 3712
_KR_PAD = LANES - MLA_ROPE
_C_GQA_Q = 3840
_C_GQA_KV = 4352
_C_GATES = 4608
W_IN_COLS = _C_GATES + N_BRANCH * D_MODEL


def _pad_w_in(w):
    cut = _C_MLA_KR + MLA_ROPE
    zeros = jnp.zeros(w.shape[:-1] + (_KR_PAD,), BF16)
    return jnp.concatenate([w[..., :cut].astype(BF16), zeros, w[..., cut:].astype(BF16)], axis=-1)


def _sigmoid_pair(x):
    a = jnp.exp(-jnp.abs(x))
    r = 1.0 / (1.0 + a)
    ar = a * r
    pos = x >= 0
    return jnp.where(pos, r, ar), jnp.where(pos, ar, r)


def _sigmoid(x):
    return _sigmoid_pair(x)[0]


def _silu(x):
    return x * _sigmoid(x)


def _rms(x, gain):
    return x * lax.rsqrt(jnp.mean(x * x, axis=-1, keepdims=True) + NORM_EPS) * gain


def _dot(a, b):
    return jnp.dot(a, b, preferred_element_type=F32)


def _dot_nt(a, b):
    return lax.dot_general(a, b, (((1,), (1,)), ((), ())), preferred_element_type=F32)


def _dot_tn(a, b):
    return lax.dot_general(a, b, (((0,), (0,)), ((), ())), preferred_element_type=F32)


def _tile_lanes(t, width):
    reps = width // LANES
    return t if reps == 1 else jnp.concatenate([t] * reps, axis=1)


def _rope(x, tab_ref, base, shift):
    w = x.shape[1]
    c = _tile_lanes(tab_ref[base], w)
    s1 = _tile_lanes(tab_ref[base + 1], w)
    s2 = _tile_lanes(tab_ref[base + 2], w)
    return x * c + pltpu.roll(x, shift, 1) * s1 + pltpu.roll(x, w - shift, 1) * s2


def _head_rms(x, bmat_ref, gain):
    ms = _dot((x * x).astype(BF16), bmat_ref[...])
    return x * lax.rsqrt(ms + NORM_EPS) * gain


def _const_spec(shape):
    nd = len(shape)
    return pl.BlockSpec(shape, lambda *_: (0,) * nd, pipeline_mode=pl.Buffered(1))


def _layer_spec(shape, layer):
    nd = len(shape)
    return pl.BlockSpec((None,) + tuple(shape), lambda *_: (layer,) + (0,) * nd, pipeline_mode=pl.Buffered(1))


_ANY_SPEC = pl.BlockSpec(memory_space=pl.ANY)


def _params(n_grid_dims):
    return pltpu.CompilerParams(dimension_semantics=("arbitrary",) * n_grid_dims, vmem_limit_bytes=VMEM_LIMIT_BYTES)


_ADA_TN = 2048


def _ada_kernel(c_ref, w_ref, b_ref, o_ref):
    s = _silu(c_ref[...]).astype(BF16)
    o_ref[0] = _dot(s, w_ref[0].astype(BF16)) + b_ref[0]


def _ada(cond8, w_ada, b_ada):
    n = N_MOD * D_MODEL
    return pl.pallas_call(
        _ada_kernel,
        out_shape=jax.ShapeDtypeStruct((DEPTH, SUBLANES, n), F32),
        grid=(DEPTH, n // _ADA_TN),
        in_specs=[
            pl.BlockSpec((SUBLANES, D_MODEL), lambda l, j: (0, 0)),
            pl.BlockSpec((1, D_MODEL, _ADA_TN), lambda l, j: (l, 0, j)),
            pl.BlockSpec((1, 1, _ADA_TN), lambda l, j: (l, 0, j)),
        ],
        out_specs=pl.BlockSpec((1, SUBLANES, _ADA_TN), lambda l, j: (l, 0, j)),
        compiler_params=_params(2),
        name="ada",
    )(cond8, w_ada, b_ada.reshape(DEPTH, 1, n))


def _mod_spec(is_ctx, layer):
    if is_ctx:
        return pl.BlockSpec((None, 1, N_MOD, 1, D_MODEL), lambda i: (layer, 0, 0, 0, 0))
    return pl.BlockSpec((None, 1, N_MOD, 1, D_MODEL), lambda i: (layer, 1 + i // LAT_TILES_PER_SEQ, 0, 0, 0))


LOG2E = float(np.log2(np.e))
_SCALE_64 = HEAD_DIM ** -0.5 * LOG2E
_SCALE_MLA = (MLA_NOPE + MLA_ROPE) ** -0.5 * LOG2E
_TAB_64, _TAB_MLAQ, _TAB_KR = 0, 3, 6

_HGRN_PRE = ("hq", "hff", "hfb", "hi", "hg")
_ATTN_OUTS = (
    ("sq", 512, BF16), ("sk", 256, BF16), ("sv", 256, BF16),
    ("mq", 1024, BF16), ("mk", 1024, BF16), ("mv", 512, BF16),
    ("gq", 512, BF16), ("gk", 256, BF16), ("gv", 256, BF16), ("gates", 4096, BF16))
_CACHE_OUTS = (("skc", 128), ("svc", 128), ("mckv", 128), ("mkr", MLA_ROPE), ("gkc", 128), ("gvc", 128))
_PROJ_PARAMS = dict(npre=(1, D_MODEL), w=(D_MODEL, W_IN_COLS), mlaq_g=(1, MLA_Q_RANK), mlakv_g=(1, MLA_KV_RANK),
                    wuq=(MLA_Q_RANK, 1024), wk=(256, 1024), wuv=(MLA_KV_RANK, 512), gq_g=(1, 512), gk_g=(1, 128))
_HGRN_PARAMS = dict(lbf=(1, 512), lbb=(1, 512), gn=(1, 512))
_PROJ_CONSTS = tuple(_PROJ_PARAMS) + ("b512", "b128")
_HGRN_CONSTS = tuple(_HGRN_PARAMS)


def _proj_out_names(is_ctx):
    attn = [n for n, _, _ in _ATTN_OUTS]
    if is_ctx:
        return ["oa"] + attn + [n for n, _ in _CACHE_OUTS] + ["st"]
    return list(_HGRN_PRE) + attn


def _proj_kernel(*refs, is_ctx, n_alias):
    names = ["x", "mod"] + list(_PROJ_CONSTS) + (list(_HGRN_CONSTS) if is_ctx else ["tab"])
    r = dict(zip(names, refs))
    pos = len(names) + n_alias
    for name in _proj_out_names(is_ctx) + (["hbuf"] if is_ctx else []):
        r[name] = refs[pos]
        pos += 1

    x = r["x"][...]
    sh1 = r["mod"][0, 0]
    sc1 = r["mod"][0, 1]
    h = (_rms(x, r["npre"][...]) * (1.0 + sc1) + sh1).astype(BF16)
    w_ref = r["w"]

    def seg(a, w):
        return _dot(h, w_ref[:, a:a + w])

    def rope(v, base, shift):
        return v if is_ctx else _rope(v, r["tab"], base, shift)

    for j, name in enumerate(_HGRN_PRE):
        u = seg(_C_HGRN + 512 * j, 512)
        if is_ctx:
            r["hbuf"][j] = u
        else:
            r[name][...] = u

    def hgrn_all(between):
        C = HGRN_CHUNK
        n = TM // C
        lvls = _hgrn_levels()
        heads = range(HGRN_HEADS)
        cols = [slice(LANES * hd, LANES * (hd + 1)) for hd in heads]
        st = [[None, None] for _ in heads]
        outs = [[[None] * n, [None] * n] for _ in heads]
        for c in range(n):
            for hd in heads:
                for d, cc, lb in ((0, c, r["lbf"]), (1, n - 1 - c, r["lbb"])):
                    rows = slice(C * cc, C * (cc + 1))
                    k, lf2 = _hgrn_forget(r["hbuf"][1 + d, rows, cols[hd]], lb[:, cols[hd]])
                    outs[hd][d][cc], st[hd][d] = _hgrn_chunk(
                        _silu(r["hbuf"][0, rows, cols[hd]]), k, lf2, r["hbuf"][3, rows, cols[hd]], st[hd][d],
                        d == 0, lvls[d])
            between(c)
        for hd in heads:
            for c in range(n):
                rows = slice(C * c, C * (c + 1))
                r["oa"][rows, cols[hd]] = _hgrn_finish(outs[hd][0][c] + outs[hd][1][c], r["gn"][:, cols[hd]],
                                                       r["hbuf"][4, rows, cols[hd]])
            for d in range(2):
                r["st"][0, 0, d, hd] = st[hd][d].T

    def dup16(v):
        return jnp.concatenate([d.astype(BF16) for d in _dup_heads(v)], axis=1)

    def swa():
        r["sq"][...] = (rope(seg(_C_SWA_Q, 512), _TAB_64, 16) * _SCALE_64).astype(BF16)
        kv = seg(_C_SWA_KV, 256)
        k, v = kv[:, :128], kv[:, 128:]
        r["sk"][...] = dup16(rope(k, _TAB_64, 16))
        r["sv"][...] = dup16(v)
        if is_ctx:
            r["skc"][0, 0] = k
            r["svc"][0, 0] = v

    def mla():
        cq = _rms(seg(_C_MLA_CQ, 256), r["mlaq_g"][...]).astype(BF16)
        q = _dot(cq, r["wuq"][...])
        r["mq"][...] = (rope(q, _TAB_MLAQ, 8) * _SCALE_MLA).astype(BF16)
        ckv = _rms(seg(_C_MLA_CKV, 128), r["mlakv_g"][...])
        kr = seg(_C_MLA_KR, 128)
        if is_ctx:
            r["mckv"][0, 0] = ckv
            r["mkr"][0, 0] = kr[:, :MLA_ROPE]
        kr_used = rope(kr, _TAB_KR, 8)
        ckv16 = ckv.astype(BF16)
        r["mk"][...] = _dot(jnp.concatenate([ckv16, kr_used.astype(BF16)], axis=1), r["wk"][...]).astype(BF16)
        r["mv"][...] = _dot(ckv16, r["wuv"][...]).astype(BF16)

    def gqa():
        gq = _head_rms(seg(_C_GQA_Q, 512), r["b512"], r["gq_g"][...])
        r["gq"][...] = (rope(gq, _TAB_64, 16) * _SCALE_64).astype(BF16)
        kv = seg(_C_GQA_KV, 256)
        k, v = _head_rms(kv[:, :128], r["b128"], r["gk_g"][...]), kv[:, 128:]
        r["gk"][...] = dup16(rope(k, _TAB_64, 16))
        r["gv"][...] = dup16(v)
        if is_ctx:
            r["gkc"][0, 0] = k
            r["gvc"][0, 0] = v

    def gates(j):
        r["gates"][:, D_MODEL * j:D_MODEL * (j + 1)] = _sigmoid(seg(_C_GATES + D_MODEL * j, D_MODEL)).astype(BF16)

    tasks = [swa, mla, gqa] + [functools.partial(gates, j) for j in range(N_BRANCH)]
    if is_ctx:
        per_step = -(-len(tasks) // (TM // HGRN_CHUNK))

        def between(c):
            for t in tasks[per_step * c:per_step * (c + 1)]:
                t()

        hgrn_all(between)
    else:
        for t in tasks:
            t()


def _proj(x, mod4, lp, is_ctx, layer=0, tabs=None, prev=None):
    n_tok = x.shape[0]
    params = dict(_PROJ_PARAMS, **(_HGRN_PARAMS if is_ctx else {}))
    in_specs = [pl.BlockSpec((TM, D_MODEL), lambda i: (i, 0)), _mod_spec(is_ctx, layer)]
    in_specs += [_layer_spec(params[n], layer) for n in _PROJ_PARAMS] + [_const_spec((512, 512)), _const_spec((128, 128))]
    args = [x, mod4] + [lp[n] for n in _PROJ_PARAMS] + [lp["b512"], lp["b128"]]
    if is_ctx:
        in_specs += [_layer_spec(params[n], layer) for n in _HGRN_PARAMS]
        args += [lp[n] for n in _HGRN_PARAMS]
    if not is_ctx:
        in_specs.append(pl.BlockSpec((9, TM, LANES), lambda i: (0, i % LAT_TILES_PER_SEQ, 0)))
        args.append(tabs)

    def tok(w):
        return pl.BlockSpec((TM, w), lambda i: (i, 0))

    specs = {n: (tok(w), jax.ShapeDtypeStruct((n_tok, w), dt)) for n, w, dt in _ATTN_OUTS}
    if is_ctx:
        specs["oa"] = (tok(512), jax.ShapeDtypeStruct((n_tok, 512), BF16))
        for n, w in _CACHE_OUTS:
            specs[n] = (pl.BlockSpec((1, 1, SEQ, w), lambda i: (i, layer, 0, 0)),
                        jax.ShapeDtypeStruct((BATCH, DEPTH, SEQ, w), F32))
        specs["st"] = (pl.BlockSpec((1, 1, 2, HGRN_HEADS, LANES, LANES), lambda i: (i, layer, 0, 0, 0, 0)),
                       jax.ShapeDtypeStruct((BATCH, DEPTH, 2, HGRN_HEADS, LANES, LANES), F32))
    else:
        for n in _HGRN_PRE:
            specs[n] = (tok(512), jax.ShapeDtypeStruct((n_tok, 512), F32))
    out_names = _proj_out_names(is_ctx)
    aliases = {}
    n_alias = 0
    if prev is not None:
        for n in [n for n, _ in _CACHE_OUTS] + ["st"]:
            aliases[len(args)] = out_names.index(n)
            in_specs.append(_ANY_SPEC)
            args.append(prev[n])
            n_alias += 1
    outs = pl.pallas_call(
        functools.partial(_proj_kernel, is_ctx=is_ctx, n_alias=n_alias),
        out_shape=[specs[n][1] for n in out_names], grid=(n_tok // TM,), in_specs=in_specs,
        out_specs=[specs[n][0] for n in out_names],
        scratch_shapes=[pltpu.VMEM((len(_HGRN_PRE), TM, 512), F32)] if is_ctx else [],
        input_output_aliases=aliases, compiler_params=_params(1),
        name="proj_ctx" if is_ctx else "proj_lat",
    )(*args)
    return dict(zip(out_names, outs))


def _pair_levels():
    C = HGRN_CHUNK
    ti = lax.broadcasted_iota(jnp.int32, (C, C), 0)
    si = lax.broadcasted_iota(jnp.int32, (C, C), 1)
    x = ti ^ si
    lvl = jnp.where(x == 0, -1, 0)
    b = 2
    while b < C:
        lvl = lvl + (x >= b).astype(jnp.int32)
        b *= 2
    return ti, si, lvl


def _hgrn_forget(f_pre, lb):
    s_pos, s_neg = _sigmoid_pair(f_pre)
    f = lb + (1.0 - lb) * s_pos
    return (1.0 - lb) * s_neg, jnp.log2(jnp.maximum(f, LOG_FLOOR))


def _hgrn_levels():
    ti, si, lvl = _pair_levels()
    return jnp.where(ti > si, lvl, -1), jnp.where(ti < si, lvl, -1)


def _hgrn_chunk(q, k, lf2, v, st, fwd, lvl):
    C = HGRN_CHUNK
    nv = C // SUBLANES
    sub = lax.broadcasted_iota(jnp.int32, (nv, SUBLANES, LANES), 1)

    p = lf2.reshape(nv, SUBLANES, LANES)
    for s in (1, 2, 4):
        if fwd:
            p = p + jnp.where(sub >= s, pltpu.roll(p, s, 1), 0.0)
        else:
            p = p + jnp.where(sub < SUBLANES - s, pltpu.roll(p, SUBLANES - s, 1), 0.0)

    a_mat = jnp.zeros((C, C), F32)

    def add_level(a_mat, e, level):
        a_m = _dot_nt((q * e).astype(BF16), (k * e).astype(BF16))
        return jnp.where(lvl == level, a_m, a_mat)

    def row(r):
        return jnp.broadcast_to(p[:, r:r + 1, :], p.shape)

    odd = (sub & 1) == 1
    if fwd:
        bnd = (jnp.where(odd, pltpu.roll(p, 1, 1), p), jnp.where(sub < 4, row(1), row(5)), row(3))
    else:
        bnd = (jnp.where(odd, p, pltpu.roll(p, SUBLANES - 1, 1)), jnp.where(sub < 4, row(2), row(6)), row(4))
    for level, b in enumerate(bnd):
        a_mat = add_level(a_mat, jnp.exp2(-jnp.abs(p - b)).reshape(C, LANES), level)

    p = p.reshape(C, LANES)
    m = SUBLANES
    while m < C:
        nb = C // (2 * m)
        p4 = p.reshape(nb, 2, m, LANES)
        lo, hi = p4[:, 0], p4[:, 1]
        if fwd:
            tot = lo[:, m - 1:m]
            e_lo, e_hi = tot - lo, hi
            p_lo, p_hi = lo, hi + tot
        else:
            tot = hi[:, 0:1]
            e_lo, e_hi = lo, tot - hi
            p_lo, p_hi = lo + tot, hi
        e = jnp.exp2(jnp.concatenate([e_lo[:, None], e_hi[:, None]], axis=1).reshape(C, LANES))
        p = jnp.concatenate([p_lo[:, None], p_hi[:, None]], axis=1).reshape(C, LANES)
        a_mat = add_level(a_mat, e, int(np.log2(m)))
        m *= 2
    v16 = v.astype(BF16)
    o = jnp.sum(q * k, axis=-1, keepdims=True) * v + _dot(a_mat.astype(BF16), v16)

    if st is not None:
        o = o + _dot_nt((q * jnp.exp2(p)).astype(BF16), st.astype(BF16))
    tot = p[C - 1:C] if fwd else p[0:1]
    k_end = (k * jnp.exp2(tot - p)).astype(BF16)
    st_new = _dot_tn(v16, k_end)
    if st is not None:
        st_new = st_new + st * jnp.exp2(tot)
    return o, st_new


def _hgrn_finish(o, gn, g_pre):
    return (_rms(o, gn) * _silu(g_pre)).astype(BF16)


def _hgrn_head_static(read, lbf, lbb, gn, n_chunks):
    lvls = _hgrn_levels()
    st = [None, None]
    outs = [[None] * n_chunks, [None] * n_chunks]
    for c in range(n_chunks):
        for d, cc, lb in ((0, c, lbf), (1, n_chunks - 1 - c, lbb)):
            k, lf2 = _hgrn_forget(read(1 + d, cc), lb)
            outs[d][cc], st[d] = _hgrn_chunk(_silu(read(0, cc)), k, lf2, read(3, cc), st[d], d == 0, lvls[d])
    return [_hgrn_finish(outs[0][c] + outs[1][c], gn, read(4, c)) for c in range(n_chunks)], st


def _hgrn_lat_kernel(lbf_ref, lbb_ref, gn_ref, q_ref, ff_ref, fb_ref, v_ref, g_ref, s0_ref, o_ref,
                     of_ref, ob_ref, st_ref, *, n_chunks):
    C = HGRN_CHUNK
    for d in range(2):
        st_ref[d] = s0_ref[0, 0, d, 0].T
    lbf = lbf_ref[...]
    lbb = lbb_ref[...]
    lvls = _hgrn_levels()

    def body(c, carry):
        rf = pl.multiple_of(c * C, C)
        rb = pl.multiple_of((n_chunks - 1 - c) * C, C)
        for d, r0, f_ref, lb, out_ref in ((0, rf, ff_ref, lbf, of_ref), (1, rb, fb_ref, lbb, ob_ref)):
            sl = pl.ds(r0, C)
            k, lf2 = _hgrn_forget(f_ref[sl, :], lb)
            out_ref[sl, :], st_ref[d] = _hgrn_chunk(_silu(q_ref[sl, :]), k, lf2, v_ref[sl, :], st_ref[d], d == 0,
                                                    lvls[d])
        return carry

    lax.fori_loop(0, n_chunks, body, 0, unroll=HGRN_UNROLL)
    o_ref[...] = _hgrn_finish(of_ref[...] + ob_ref[...], gn_ref[...], g_ref[...])


def _hgrn_lat(p, lp, layer, state_in):
    tok_spec = pl.BlockSpec((DEC_SEQ, LANES), lambda b, h: (b, h))
    vec_spec = pl.BlockSpec((None, 1, LANES), lambda b, h: (layer, 0, h))
    state_spec = pl.BlockSpec((1, 1, 2, 1, LANES, LANES), lambda b, h: (b, layer, 0, h, 0, 0))
    return pl.pallas_call(
        functools.partial(_hgrn_lat_kernel, n_chunks=DEC_SEQ // HGRN_CHUNK),
        out_shape=jax.ShapeDtypeStruct((N_LAT, 512), BF16), grid=(DEC_BATCH, HGRN_HEADS),
        in_specs=[vec_spec] * 3 + [tok_spec] * 5 + [state_spec], out_specs=tok_spec,
        scratch_shapes=[pltpu.VMEM((DEC_SEQ, LANES), F32), pltpu.VMEM((DEC_SEQ, LANES), F32),
                        pltpu.VMEM((2, LANES, LANES), F32)],
        compiler_params=_params(2), name="hgrn_lat",
    )(lp["lbf"], lp["lbb"], lp["gn"], p["hq"], p["hff"], p["hfb"], p["hi"], p["hg"], state_in)


def _lane_lo(shape):
    return lax.broadcasted_iota(jnp.int32, shape, len(shape) - 1) < HEAD_DIM


def _softmax_pv(segs, sink_col=None):
    mx = None
    for s, _ in segs:
        m = jnp.max(s, axis=-1, keepdims=True)
        mx = m if mx is None else jnp.maximum(mx, m)
    if sink_col is not None:
        mx = jnp.maximum(mx, sink_col)
    den = None
    acc = None
    for s, v in segs:
        e = jnp.exp2(s - mx)
        dsum = jnp.sum(e, axis=-1, keepdims=True)
        den = dsum if den is None else den + dsum
        pv = _dot(e.astype(BF16), v)
        acc = pv if acc is None else acc + pv
    if sink_col is not None:
        den = den + jnp.exp2(sink_col - mx)
    return acc / den


def _gqa_group(q_ref, g, tq, segs_fn, sink_ref=None):
    lo = _lane_lo((tq, LANES))
    parts = []
    for j in range(2):
        qp = q_ref[:, 256 * g + 128 * j:256 * g + 128 * (j + 1)]
        zero = jnp.zeros_like(qp)
        parts += [jnp.where(lo, qp, zero), jnp.where(lo, zero, qp)]
    q4 = jnp.concatenate(parts, axis=0)
    sink_col = None
    if sink_ref is not None:
        sink_col = jnp.concatenate(
            [jnp.broadcast_to(sink_ref[0:1, 4 * g + i:4 * g + i + 1] * LOG2E, (tq, 1)) for i in range(4)], axis=0)
    pv = _softmax_pv(segs_fn(q4), sink_col)
    return (jnp.where(lo, pv[0:tq], pv[tq:2 * tq]), jnp.where(lo, pv[2 * tq:3 * tq], pv[3 * tq:4 * tq]))


def _mla_heads(q_ref, tq, segs_fn, o_ref):
    lo = _lane_lo((tq, LANES))
    for hp in range(MLA_HEADS // 2):
        outs = []
        for h in (2 * hp, 2 * hp + 1):
            q_h = q_ref[:, 128 * h:128 * (h + 1)]
            outs.append(_softmax_pv(segs_fn(h, q_h)))
        o_ref[:, 128 * hp:128 * (hp + 1)] = jnp.where(lo, outs[0], outs[1]).astype(BF16)


_ATTN_IN = ("sq", "sk", "sv", "mq", "mk", "mv", "gq", "gk", "gv")
_ATTN_WIDTH = dict(sq=512, sk=256, sv=256, mq=1024, mk=1024, mv=512, gq=512, gk=256, gv=256)


def _attn_ctx_kernel(sink_ref, sq_ref, sk_ref, sv_ref, mq_ref, mk_ref, mv_ref, gq_ref, gk_ref, gv_ref,
                     ob_ref, oc_ref, od_ref):
    tq = SEQ
    for g in range(N_KV):
        def swa_segs(q4, g=g):
            return [(_dot_nt(q4, sk_ref[:, 128 * g:128 * (g + 1)]), sv_ref[:, 128 * g:128 * (g + 1)])]

        a, b = _gqa_group(sq_ref, g, tq, swa_segs, sink_ref)
        ob_ref[:, 256 * g:256 * g + 128] = a.astype(BF16)
        ob_ref[:, 256 * g + 128:256 * (g + 1)] = b.astype(BF16)

        def gqa_segs(q4, g=g):
            return [(_dot_nt(q4, gk_ref[:, 128 * g:128 * (g + 1)]), gv_ref[:, 128 * g:128 * (g + 1)])]

        a, b = _gqa_group(gq_ref, g, tq, gqa_segs)
        od_ref[:, 256 * g:256 * g + 128] = a.astype(BF16)
        od_ref[:, 256 * g + 128:256 * (g + 1)] = b.astype(BF16)

    def mla_segs(h, q_h):
        return [(_dot_nt(q_h, mk_ref[:, 128 * h:128 * (h + 1)]), mv_ref[:, 128 * (h // 2):128 * (h // 2 + 1)])]

    _mla_heads(mq_ref, tq, mla_segs, oc_ref)


def _attn_ctx(lp, p, layer):
    def spec(w):
        return pl.BlockSpec((SEQ, w), lambda b: (b, 0))

    return pl.pallas_call(
        _attn_ctx_kernel,
        out_shape=[jax.ShapeDtypeStruct((N_CTX, 512), BF16)] * 3,
        grid=(BATCH,),
        in_specs=[_layer_spec((1, N_HEADS), layer)] + [spec(_ATTN_WIDTH[n]) for n in _ATTN_IN],
        out_specs=[spec(512)] * 3,
        compiler_params=_params(1),
        name="attn_ctx",
    )(lp["sink"], *[p[n] for n in _ATTN_IN])


def _dup_heads(x):
    lo = _lane_lo(x.shape)
    r = pltpu.roll(x, HEAD_DIM, 1)
    return jnp.where(lo, x, r), jnp.where(lo, r, x)


def _attn_lat_kernel(sink_ref, wk_ref, wuv_ref,
                     sq_ref, sk_ref, sv_ref, mq_ref, mk_ref, mv_ref, gq_ref, gk_ref, gv_ref,
                     csk_ref, csv_ref, cckv_ref, ckr_ref, cgk_ref, cgv_ref,
                     ob_ref, oc_ref, od_ref,
                     cs_k, cs_v, cm_k, cm_v, cg_k, cg_v):
    tq = TQ_LAT
    qi = pl.program_id(1)

    @pl.when(qi == 0)
    def _():
        for src, dst in ((csk_ref, cs_k), (csv_ref, cs_v), (cgk_ref, cg_k), (cgv_ref, cg_v)):
            d0, d1 = _dup_heads(src[0, 0])
            dst[0] = d0.astype(BF16)
            dst[1] = d1.astype(BF16)
        ckv16 = cckv_ref[0, 0].astype(BF16)
        k = _dot(ckv16, wk_ref[0:MLA_KV_RANK, :]) + _dot(ckr_ref[0, 0].astype(BF16),
                                                           wk_ref[MLA_KV_RANK:MLA_KV_RANK + MLA_ROPE, :])
        cm_k[...] = k.astype(BF16)
        cm_v[...] = _dot(ckv16, wuv_ref[...]).astype(BF16)

    q0 = qi * tq
    win = tq + 2 * SWA_WINDOW
    start = pl.multiple_of(jnp.clip(q0 - SWA_WINDOW, 0, DEC_SEQ - win), SWA_WINDOW)
    rows = lax.broadcasted_iota(jnp.int32, (4 * tq, win), 0)
    qpos = q0 + (rows & (tq - 1))
    kpos = start + lax.broadcasted_iota(jnp.int32, (4 * tq, win), 1)
    band = jnp.abs(qpos - kpos) <= SWA_WINDOW
    for g in range(N_KV):
        def swa_segs(q4, g=g):
            ks = sk_ref[pl.ds(start, win), 128 * g:128 * (g + 1)]
            vs = sv_ref[pl.ds(start, win), 128 * g:128 * (g + 1)]
            s_own = jnp.where(band, _dot_nt(q4, ks), MASK_VALUE)
            return [(_dot_nt(q4, cs_k[g]), cs_v[g]), (s_own, vs)]

        a, b = _gqa_group(sq_ref, g, tq, swa_segs, sink_ref)
        ob_ref[:, 256 * g:256 * g + 128] = a.astype(BF16)
        ob_ref[:, 256 * g + 128:256 * (g + 1)] = b.astype(BF16)

        def gqa_segs(q4, g=g):
            return [(_dot_nt(q4, cg_k[g]), cg_v[g]),
                    (_dot_nt(q4, gk_ref[:, 128 * g:128 * (g + 1)]), gv_ref[:, 128 * g:128 * (g + 1)])]

        a, b = _gqa_group(gq_ref, g, tq, gqa_segs)
        od_ref[:, 256 * g:256 * g + 128] = a.astype(BF16)
        od_ref[:, 256 * g + 128:256 * (g + 1)] = b.astype(BF16)

    def mla_segs(h, q_h):
        hp = h // 2
        return [(_dot_nt(q_h, cm_k[:, 128 * h:128 * (h + 1)]), cm_v[:, 128 * hp:128 * (hp + 1)]),
                (_dot_nt(q_h, mk_ref[:, 128 * h:128 * (h + 1)]), mv_ref[:, 128 * hp:128 * (hp + 1)])]

    _mla_heads(mq_ref, tq, mla_segs, oc_ref)


def _attn_lat(lp, p, caches, layer):
    nq = DEC_SEQ // TQ_LAT

    def qspec(w):
        return pl.BlockSpec((TQ_LAT, w), lambda b, i: (b * nq + i, 0))

    def kspec(w):
        return pl.BlockSpec((DEC_SEQ, w), lambda b, i: (b, 0))

    def cspec(w):
        return pl.BlockSpec((1, 1, PAST_LEN, w), lambda b, i: (b, layer, 0, 0))

    is_q = dict(sq=True, mq=True, gq=True)
    in_specs = [_layer_spec((1, N_HEADS), layer), _layer_spec((256, 1024), layer),
                _layer_spec((MLA_KV_RANK, 512), layer)]
    in_specs += [(qspec if is_q.get(n) else kspec)(_ATTN_WIDTH[n]) for n in _ATTN_IN]
    in_specs += [cspec(128), cspec(128), cspec(128), cspec(MLA_ROPE), cspec(128), cspec(128)]
    return pl.pallas_call(
        _attn_lat_kernel,
        out_shape=[jax.ShapeDtypeStruct((N_LAT, 512), BF16)] * 3,
        grid=(DEC_BATCH, nq),
        in_specs=in_specs,
        out_specs=[qspec(512)] * 3,
        scratch_shapes=[pltpu.VMEM((N_KV, PAST_LEN, LANES), BF16), pltpu.VMEM((N_KV, PAST_LEN, LANES), BF16),
                        pltpu.VMEM((PAST_LEN, 1024), BF16), pltpu.VMEM((PAST_LEN, 512), BF16),
                        pltpu.VMEM((N_KV, PAST_LEN, LANES), BF16), pltpu.VMEM((N_KV, PAST_LEN, LANES), BF16)],
        compiler_params=_params(2),
        name="attn_lat",
    )(lp["sink"], lp["wk"], lp["wuv"], *[p[n] for n in _ATTN_IN], *caches)


def _merge_kernel(x_ref, mod_ref, oa_ref, ob_ref, oc_ref, od_ref, gates_ref, wb_ref, wo_ref, npost_ref,
                  nmpre_ref, nmpost_ref, w1_ref, w2_ref, y_ref):
    x = x_ref[...]
    g1 = mod_ref[0, 2]
    sh2 = mod_ref[0, 3]
    sc2 = mod_ref[0, 4]
    g2 = mod_ref[0, 5]
    acc = None
    for n, o_ref in enumerate((oa_ref, ob_ref, oc_ref, od_ref)):
        br = _dot(o_ref[...], wb_ref[n]) * gates_ref[:, D_MODEL * n:D_MODEL * (n + 1)].astype(F32)
        acc = br if acc is None else acc + br
    y = _dot(acc.astype(BF16), wo_ref[...])
    x = x + g1 * _rms(y, npost_ref[...])
    h = (_rms(x, nmpre_ref[...]) * (1.0 + sc2) + sh2).astype(BF16)
    a = jnp.maximum(_dot(h, w1_ref[...]), 0.0)
    y = _dot((a * a).astype(BF16), w2_ref[...])
    y_ref[...] = x + g2 * _rms(y, nmpost_ref[...])


_MERGE_PARAMS = dict(w_branch=(N_BRANCH, BRANCH_W, D_MODEL), w_out=(D_MODEL, D_MODEL), npost=(1, D_MODEL),
                     nmpre=(1, D_MODEL), nmpost=(1, D_MODEL), w_mlp_in=(D_MODEL, D_FF), w_mlp_out=(D_FF, D_MODEL))


def _merge(x, mod4, o_a, o_b, o_c, o_d, gates, lp, is_ctx, layer):
    n_tok = x.shape[0]

    def tok(w):
        return pl.BlockSpec((TM, w), lambda i: (i, 0))

    in_specs = [tok(D_MODEL), _mod_spec(is_ctx, layer), tok(512), tok(512), tok(512), tok(512), tok(4096)]
    in_specs += [_layer_spec(shape, layer) for shape in _MERGE_PARAMS.values()]
    return pl.pallas_call(
        _merge_kernel, out_shape=jax.ShapeDtypeStruct((n_tok, D_MODEL), F32), grid=(n_tok // TM,),
        in_specs=in_specs, out_specs=tok(D_MODEL), compiler_params=_params(1),
        name="merge_ctx" if is_ctx else "merge_lat",
    )(x, mod4, o_a, o_b, o_c, o_d, gates, *[lp[n] for n in _MERGE_PARAMS])


def _rope_tables():
    t = np.arange(DEC_SEQ)
    row, col = (t // GRID_W).astype(np.float32), (t % GRID_W).astype(np.float32)

    def pattern(d):
        nf = d // 4
        inv = (ROPE_BASE ** (-np.arange(nf, dtype=np.float32) / nf)).astype(np.float32)
        ang = np.concatenate([row[:, None] * inv, row[:, None] * inv, col[:, None] * inv, col[:, None] * inv], axis=1)
        cos, sin = np.cos(ang), np.sin(ang)
        second = (np.arange(d) % (2 * nf)) >= nf
        return cos, np.where(second, sin, 0.0), np.where(second, 0.0, -sin)

    out = np.zeros((9, DEC_SEQ, LANES), np.float32)
    out[[0, 3, 6]] = 1.0
    c, s1, s2 = pattern(HEAD_DIM)
    for a, tab in enumerate((c, s1, s2)):
        out[_TAB_64 + a] = np.tile(tab, (1, LANES // HEAD_DIM))
    c, s1, s2 = pattern(MLA_ROPE)
    for a, tab in enumerate((c, s1, s2)):
        out[_TAB_MLAQ + a][:, MLA_NOPE:MLA_NOPE + MLA_ROPE] = tab
        out[_TAB_KR + a][:, :MLA_ROPE] = tab
    return out


_ROPE_TABLES = _rope_tables()


def _block_mean_matrix(n):
    i = np.arange(n)
    return ((i[:, None] // HEAD_DIM) == (i[None, :] // HEAD_DIM)).astype(np.float32) / HEAD_DIM


def _mla_weight_layouts(w_uq, w_ukv):
    dq = MLA_NOPE + MLA_ROPE
    wq = w_uq.reshape(DEPTH, MLA_Q_RANK, MLA_HEADS, dq)
    wq = jnp.pad(wq, ((0, 0), (0, 0), (0, 0), (0, LANES - dq))).reshape(DEPTH, MLA_Q_RANK, MLA_HEADS * LANES)
    wkv = w_ukv.reshape(DEPTH, MLA_KV_RANK, MLA_HEADS, MLA_NOPE + MLA_V)
    wk = jnp.pad(wkv[..., :MLA_NOPE], ((0, 0), (0, 0), (0, 0), (0, LANES - MLA_NOPE)))
    wk = wk.reshape(DEPTH, MLA_KV_RANK, MLA_HEADS * LANES)
    place = np.zeros((LANES, MLA_HEADS, LANES), np.float32)
    for j in range(MLA_ROPE):
        place[j, :, MLA_NOPE + j] = 1.0
    place = jnp.broadcast_to(jnp.asarray(place.reshape(1, LANES, MLA_HEADS * LANES)), (DEPTH, LANES, MLA_HEADS * LANES))
    w_k = jnp.concatenate([wk, place], axis=1)
    w_uv = wkv[..., MLA_NOPE:].reshape(DEPTH, MLA_KV_RANK, MLA_HEADS * MLA_V)
    return wq.astype(BF16), w_k.astype(BF16), w_uv.astype(BF16)


def _lower_bounds(p):
    s = jax.nn.softmax(p.astype(F32), axis=0)
    return jnp.cumsum(s, axis=0) - s[0]


def kernel(x_prompt, x_sample, state_hgrn, cache_swa_k, cache_swa_v, cache_mla_ckv, cache_mla_kr, cache_gqa_k, cache_gqa_v, c, c_ctx, w_ada, b_ada, norm_mix_pre, norm_mix_post, norm_mlp_pre, norm_mlp_post, w_in, hgrn_lb_fwd, hgrn_lb_bwd, hgrn_norm, swa_sink, mla_q_norm, mla_kv_norm, mla_w_uq, mla_w_ukv, gqa_q_norm, gqa_k_norm, w_branch, w_out, w_mlp_in, w_mlp_out):
    tabs = jnp.asarray(_ROPE_TABLES)
    b512 = jnp.asarray(_block_mean_matrix(512)).astype(BF16)

    def rows(a):
        return a.reshape(DEPTH, 1, -1)

    w_uq, w_k, w_uv = _mla_weight_layouts(mla_w_uq, mla_w_ukv)
    lp = dict(
        npre=rows(norm_mix_pre), w=_pad_w_in(w_in), mlaq_g=rows(mla_q_norm), mlakv_g=rows(mla_kv_norm),
        wuq=w_uq, wk=w_k, wuv=w_uv, gq_g=rows(jnp.tile(gqa_q_norm, (1, N_HEADS))),
        gk_g=rows(jnp.tile(gqa_k_norm, (1, N_KV))), b512=b512, b128=b512[:128, :128],
        lbf=rows(_lower_bounds(hgrn_lb_fwd)), lbb=rows(_lower_bounds(hgrn_lb_bwd)), gn=rows(hgrn_norm),
        sink=rows(swa_sink),
        w_branch=w_branch.astype(BF16), w_out=w_out.astype(BF16), npost=rows(norm_mix_post),
        nmpre=rows(norm_mlp_pre), nmpost=rows(norm_mlp_post),
        w_mlp_in=w_mlp_in.astype(BF16), w_mlp_out=w_mlp_out.astype(BF16))

    cond8 = jnp.zeros((SUBLANES, D_MODEL), F32).at[0].set(c_ctx).at[1:1 + DEC_BATCH].set(c)
    mod4 = _ada(cond8, w_ada, b_ada).reshape(DEPTH, SUBLANES, N_MOD, 1, D_MODEL)

    caches_in = (cache_swa_k.reshape(DEC_BATCH, DEPTH, PAST_LEN, 128),
                 cache_swa_v.reshape(DEC_BATCH, DEPTH, PAST_LEN, 128), cache_mla_ckv, cache_mla_kr,
                 cache_gqa_k.reshape(DEC_BATCH, DEPTH, PAST_LEN, 128),
                 cache_gqa_v.reshape(DEC_BATCH, DEPTH, PAST_LEN, 128))

    x_ctx = x_prompt.reshape(N_CTX, D_MODEL)
    x_lat = x_sample.reshape(N_LAT, D_MODEL)
    pc = None
    for l in range(DEPTH):
        pc = _proj(x_ctx, mod4, lp, True, layer=l, prev=pc)
        ob, oc, od = _attn_ctx(lp, pc, l)
        x_ctx = _merge(x_ctx, mod4, pc["oa"], ob, oc, od, pc["gates"], lp, True, l)

        pq = _proj(x_lat, mod4, lp, False, layer=l, tabs=tabs)
        oa = _hgrn_lat(pq, lp, l, state_hgrn)
        ob, oc, od = _attn_lat(lp, pq, caches_in, l)
        x_lat = _merge(x_lat, mod4, oa, ob, oc, od, pq["gates"], lp, False, l)

    kv_shape = (BATCH, DEPTH, SEQ, N_KV, HEAD_DIM)
    return (x_ctx.reshape(BATCH, SEQ, D_MODEL), x_lat.reshape(DEC_BATCH, DEC_SEQ, D_MODEL), pc["st"],
            pc["skc"].reshape(kv_shape), pc["svc"].reshape(kv_shape), pc["mckv"], pc["mkr"],
            pc["gkc"].reshape(kv_shape), pc["gvc"].reshape(kv_shape))
```

```python
import functools

import jax
import jax.numpy as jnp
import numpy as np
from jax import lax
from jax.experimental import pallas as pl
from jax.experimental.pallas import tpu as pltpu

D_MODEL = 1024
BATCH, SEQ = 32, 256
DEC_BATCH, DEC_SEQ = 2, 2048
DEPTH = 2
PAST_LEN = 512
GRID_W = 64
ROPE_BASE = 10000.0
NORM_EPS = 1e-6
MASK_VALUE = -1e30
LOG_FLOOR = 1e-30
N_MOD = 6
N_BRANCH = 4
BRANCH_W = 512
D_FF = 4 * D_MODEL
HGRN_HEADS, HGRN_DK = 4, 128
SWA_WINDOW = 128
MLA_HEADS, MLA_Q_RANK, MLA_KV_RANK, MLA_NOPE, MLA_ROPE, MLA_V = 8, 256, 128, 64, 32, 64
HEAD_DIM = 64
N_HEADS, N_KV = 8, 2

N_CTX = BATCH * SEQ
N_LAT = DEC_BATCH * DEC_SEQ

LANES = 128
SUBLANES = 8
VMEM_LIMIT_BYTES = 56 * 1024 * 1024

TM = 256
LAT_TILES_PER_SEQ = DEC_SEQ // TM
HGRN_CHUNK = 256
HGRN_UNROLL = 4
TQ_LAT = 128

BF16 = jnp.bfloat16
F32 = jnp.float32

_C_HGRN = 0
_C_SWA_Q = 2560
_C_SWA_KV = 3072
_C_MLA_CQ = 3328
_C_MLA_CKV = 3584
_C_MLA_KR = 3712
_KR_PAD = LANES - MLA_ROPE
_C_GQA_Q = 3840
_C_GQA_KV = 4352
_C_GATES = 4608
W_IN_COLS = _C_GATES + N_BRANCH * D_MODEL


def _pad_w_in(w):
    cut = _C_MLA_KR + MLA_ROPE
    zeros = jnp.zeros(w.shape[:-1] + (_KR_PAD,), BF16)
    return jnp.concatenate([w[..., :cut].astype(BF16), zeros, w[..., cut:].astype(BF16)], axis=-1)


def _sigmoid_pair(x):
    a = jnp.exp(-jnp.abs(x))
    r = 1.0 / (1.0 + a)
    ar = a * r
    pos = x >= 0
    return jnp.where(pos, r, ar), jnp.where(pos, ar, r)


def _sigmoid(x):
    return _sigmoid_pair(x)[0]


def _silu(x):
    return x * _sigmoid(x)


def _rms(x, gain):
    return x * lax.rsqrt(jnp.mean(x * x, axis=-1, keepdims=True) + NORM_EPS) * gain


def _dot(a, b):
    return jnp.dot(a, b, preferred_element_type=F32)


def _dot_nt(a, b):
    return lax.dot_general(a, b, (((1,), (1,)), ((), ())), preferred_element_type=F32)


def _dot_tn(a, b):
    return lax.dot_general(a, b, (((0,), (0,)), ((), ())), preferred_element_type=F32)


def _tile_lanes(t, width):
    reps = width // LANES
    return t if reps == 1 else jnp.concatenate([t] * reps, axis=1)


def _rope(x, tab_ref, base, shift):
    w = x.shape[1]
    c = _tile_lanes(tab_ref[base], w)
    s1 = _tile_lanes(tab_ref[base + 1], w)
    s2 = _tile_lanes(tab_ref[base + 2], w)
    return x * c + pltpu.roll(x, shift, 1) * s1 + pltpu.roll(x, w - shift, 1) * s2


def _head_rms(x, bmat_ref, gain):
    ms = _dot((x * x).astype(BF16), bmat_ref[...])
    return x * lax.rsqrt(ms + NORM_EPS) * gain


def _const_spec(shape):
    nd = len(shape)
    return pl.BlockSpec(shape, lambda *_: (0,) * nd, pipeline_mode=pl.Buffered(1))


def _layer_spec(shape, layer):
    nd = len(shape)
    return pl.BlockSpec((None,) + tuple(shape), lambda *_: (layer,) + (0,) * nd, pipeline_mode=pl.Buffered(1))


_ANY_SPEC = pl.BlockSpec(memory_space=pl.ANY)


def _params(n_grid_dims):
    return pltpu.CompilerParams(dimension_semantics=("arbitrary",) * n_grid_dims, vmem_limit_bytes=VMEM_LIMIT_BYTES)


_ADA_TN = 2048


def _ada_kernel(c_ref, w_ref, b_ref, o_ref):
    s = _silu(c_ref[...]).astype(BF16)
    o_ref[0] = _dot(s, w_ref[0].astype(BF16)) + b_ref[0]


def _ada(cond8, w_ada, b_ada):
    n = N_MOD * D_MODEL
    return pl.pallas_call(
        _ada_kernel,
        out_shape=jax.ShapeDtypeStruct((DEPTH, SUBLANES, n), F32),
        grid=(DEPTH, n // _ADA_TN),
        in_specs=[
            pl.BlockSpec((SUBLANES, D_MODEL), lambda l, j: (0, 0)),
            pl.BlockSpec((1, D_MODEL, _ADA_TN), lambda l, j: (l, 0, j)),
            pl.BlockSpec((1, 1, _ADA_TN), lambda l, j: (l, 0, j)),
        ],
        out_specs=pl.BlockSpec((1, SUBLANES, _ADA_TN), lambda l, j: (l, 0, j)),
        compiler_params=_params(2),
        name="ada",
    )(cond8, w_ada, b_ada.reshape(DEPTH, 1, n))


def _mod_spec(is_ctx, layer):
    if is_ctx:
        return pl.BlockSpec((None, 1, N_MOD, 1, D_MODEL), lambda i: (layer, 0, 0, 0, 0))
    return pl.BlockSpec((None, 1, N_MOD, 1, D_MODEL), lambda i: (layer, 1 + i // LAT_TILES_PER_SEQ, 0, 0, 0))


LOG2E = float(np.log2(np.e))
_SCALE_64 = HEAD_DIM ** -0.5 * LOG2E
_SCALE_MLA = (MLA_NOPE + MLA_ROPE) ** -0.5 * LOG2E
_TAB_64, _TAB_MLAQ, _TAB_KR = 0, 3, 6

_HGRN_PRE = ("hq", "hff", "hfb", "hi", "hg")
_ATTN_OUTS = (
    ("sq", 512, BF16), ("sk", 256, BF16), ("sv", 256, BF16),
    ("mq", 1024, BF16), ("mk", 1024, BF16), ("mv", 512, BF16),
    ("gq", 512, BF16), ("gk", 256, BF16), ("gv", 256, BF16), ("gates", 4096, BF16))
_CACHE_OUTS = (("skc", 128), ("svc", 128), ("mckv", 128), ("mkr", MLA_ROPE), ("gkc", 128), ("gvc", 128))
_PROJ_PARAMS = dict(npre=(1, D_MODEL), w=(D_MODEL, W_IN_COLS), mlaq_g=(1, MLA_Q_RANK), mlakv_g=(1, MLA_KV_RANK),
                    wuq=(MLA_Q_RANK, 1024), wk=(256, 1024), wuv=(MLA_KV_RANK, 512), gq_g=(1, 512), gk_g=(1, 128))
_HGRN_PARAMS = dict(lbf=(1, 512), lbb=(1, 512), gn=(1, 512))
_PROJ_CONSTS = tuple(_PROJ_PARAMS) + ("b512", "b128")
_HGRN_CONSTS = tuple(_HGRN_PARAMS)


def _proj_out_names(is_ctx):
    attn = [n for n, _, _ in _ATTN_OUTS]
    if is_ctx:
        return ["oa"] + attn + [n for n, _ in _CACHE_OUTS] + ["st"]
    return list(_HGRN_PRE) + attn


def _proj_kernel(*refs, is_ctx, n_alias):
    names = ["x", "mod"] + list(_PROJ_CONSTS) + (list(_HGRN_CONSTS) if is_ctx else ["tab"])
    r = dict(zip(names, refs))
    pos = len(names) + n_alias
    for name in _proj_out_names(is_ctx) + (["hbuf"] if is_ctx else []):
        r[name] = refs[pos]
        pos += 1

    x = r["x"][...]
    sh1 = r["mod"][0, 0]
    sc1 = r["mod"][0, 1]
    h = (_rms(x, r["npre"][...]) * (1.0 + sc1) + sh1).astype(BF16)
    w_ref = r["w"]

    def seg(a, w):
        return _dot(h, w_ref[:, a:a + w])

    def rope(v, base, shift):
        return v if is_ctx else _rope(v, r["tab"], base, shift)

    for j, name in enumerate(_HGRN_PRE):
        u = seg(_C_HGRN + 512 * j, 512)
        if is_ctx:
            r["hbuf"][j] = u
        else:
            r[name][...] = u

    def hgrn_all(between):
        C = HGRN_CHUNK
        n = TM // C
        lvls = _hgrn_levels()
        heads = range(HGRN_HEADS)
        cols = [slice(LANES * hd, LANES * (hd + 1)) for hd in heads]
        st = [[None, None] for _ in heads]
        outs = [[[None] * n, [None] * n] for _ in heads]
        for c in range(n):
            for hd in heads:
                for d, cc, lb in ((0, c, r["lbf"]), (1, n - 1 - c, r["lbb"])):
                    rows = slice(C * cc, C * (cc + 1))
                    k, lf2 = _hgrn_forget(r["hbuf"][1 + d, rows, cols[hd]], lb[:, cols[hd]])
                    outs[hd][d][cc], st[hd][d] = _hgrn_chunk(
                        _silu(r["hbuf"][0, rows, cols[hd]]), k, lf2, r["hbuf"][3, rows, cols[hd]], st[hd][d],
                        d == 0, lvls[d])
            between(c)
        for hd in heads:
            for c in range(n):
                rows = slice(C * c, C * (c + 1))
                r["oa"][rows, cols[hd]] = _hgrn_finish(outs[hd][0][c] + outs[hd][1][c], r["gn"][:, cols[hd]],
                                                       r["hbuf"][4, rows, cols[hd]])
            for d in range(2):
                r["st"][0, 0, d, hd] = st[hd][d].T

    def dup16(v):
        return jnp.concatenate([d.astype(BF16) for d in _dup_heads(v)], axis=1)

    def swa():
        r["sq"][...] = (rope(seg(_C_SWA_Q, 512), _TAB_64, 16) * _SCALE_64).astype(BF16)
        kv = seg(_C_SWA_KV, 256)
        k, v = kv[:, :128], kv[:, 128:]
        r["sk"][...] = dup16(rope(k, _TAB_64, 16))
        r["sv"][...] = dup16(v)
        if is_ctx:
            r["skc"][0, 0] = k
            r["svc"][0, 0] = v

    def mla():
        cq = _rms(seg(_C_MLA_CQ, 256), r["mlaq_g"][...]).astype(BF16)
        q = _dot(cq, r["wuq"][...])
        r["mq"][...] = (rope(q, _TAB_MLAQ, 8) * _SCALE_MLA).astype(BF16)
        ckv = _rms(seg(_C_MLA_CKV, 128), r["mlakv_g"][...])
        kr = seg(_C_MLA_KR, 128)
        if is_ctx:
            r["mckv"][0, 0] = ckv
            r["mkr"][0, 0] = kr[:, :MLA_ROPE]
        kr_used = rope(kr, _TAB_KR, 8)
        ckv16 = ckv.astype(BF16)
        r["mk"][...] = _dot(jnp.concatenate([ckv16, kr_used.astype(BF16)], axis=1), r["wk"][...]).astype(BF16)
        r["mv"][...] = _dot(ckv16, r["wuv"][...]).astype(BF16)

    def gqa():
        gq = _head_rms(seg(_C_GQA_Q, 512), r["b512"], r["gq_g"][...])
        r["gq"][...] = (rope(gq, _TAB_64, 16) * _SCALE_64).astype(BF16)
        kv = seg(_C_GQA_KV, 256)
        k, v = _head_rms(kv[:, :128], r["b128"], r["gk_g"][...]), kv[:, 128:]
        r["gk"][...] = dup16(rope(k, _TAB_64, 16))
        r["gv"][...] = dup16(v)
        if is_ctx:
            r["gkc"][0, 0] = k
            r["gvc"][0, 0] = v

    def gates(j):
        r["gates"][:, D_MODEL * j:D_MODEL * (j + 1)] = _sigmoid(seg(_C_GATES + D_MODEL * j, D_MODEL)).astype(BF16)

    tasks = [swa, mla, gqa] + [functools.partial(gates, j) for j in range(N_BRANCH)]
    if is_ctx:
        per_step = -(-len(tasks) // (TM // HGRN_CHUNK))

        def between(c):
            for t in tasks[per_step * c:per_step * (c + 1)]:
                t()

        hgrn_all(between)
    else:
        for t in tasks:
            t()


def _proj(x, mod4, lp, is_ctx, layer=0, tabs=None, prev=None):
    n_tok = x.shape[0]
    params = dict(_PROJ_PARAMS, **(_HGRN_PARAMS if is_ctx else {}))
    in_specs = [pl.BlockSpec((TM, D_MODEL), lambda i: (i, 0)), _mod_spec(is_ctx, layer)]
    in_specs += [_layer_spec(params[n], layer) for n in _PROJ_PARAMS] + [_const_spec((512, 512)), _const_spec((128, 128))]
    args = [x, mod4] + [lp[n] for n in _PROJ_PARAMS] + [lp["b512"], lp["b128"]]
    if is_ctx:
        in_specs += [_layer_spec(params[n], layer) for n in _HGRN_PARAMS]
        args += [lp[n] for n in _HGRN_PARAMS]
    if not is_ctx:
        in_specs.append(pl.BlockSpec((9, TM, LANES), lambda i: (0, i % LAT_TILES_PER_SEQ, 0)))
        args.append(tabs)

    def tok(w):
        return pl.BlockSpec((TM, w), lambda i: (i, 0))

    specs = {n: (tok(w), jax.ShapeDtypeStruct((n_tok, w), dt)) for n, w, dt in _ATTN_OUTS}
    if is_ctx:
        specs["oa"] = (tok(512), jax.ShapeDtypeStruct((n_tok, 512), BF16))
        for n, w in _CACHE_OUTS:
            specs[n] = (pl.BlockSpec((1, 1, SEQ, w), lambda i: (i, layer, 0, 0)),
                        jax.ShapeDtypeStruct((BATCH, DEPTH, SEQ, w), F32))
        specs["st"] = (pl.BlockSpec((1, 1, 2, HGRN_HEADS, LANES, LANES), lambda i: (i, layer, 0, 0, 0, 0)),
                       jax.ShapeDtypeStruct((BATCH, DEPTH, 2, HGRN_HEADS, LANES, LANES), F32))
    else:
        for n in _HGRN_PRE:
            specs[n] = (tok(512), jax.ShapeDtypeStruct((n_tok, 512), F32))
    out_names = _proj_out_names(is_ctx)
    aliases = {}
    n_alias = 0
    if prev is not None:
        for n in [n for n, _ in _CACHE_OUTS] + ["st"]:
            aliases[len(args)] = out_names.index(n)
            in_specs.append(_ANY_SPEC)
            args.append(prev[n])
            n_alias += 1
    outs = pl.pallas_call(
        functools.partial(_proj_kernel, is_ctx=is_ctx, n_alias=n_alias),
        out_shape=[specs[n][1] for n in out_names], grid=(n_tok // TM,), in_specs=in_specs,
        out_specs=[specs[n][0] for n in out_names],
        scratch_shapes=[pltpu.VMEM((len(_HGRN_PRE), TM, 512), F32)] if is_ctx else [],
        input_output_aliases=aliases, compiler_params=_params(1),
        name="proj_ctx" if is_ctx else "proj_lat",
    )(*args)
    return dict(zip(out_names, outs))


def _pair_levels():
    C = HGRN_CHUNK
    ti = lax.broadcasted_iota(jnp.int32, (C, C), 0)
    si = lax.broadcasted_iota(jnp.int32, (C, C), 1)
    x = ti ^ si
    lvl = jnp.where(x == 0, -1, 0)
    b = 2
    while b < C:
        lvl = lvl + (x >= b).astype(jnp.int32)
        b *= 2
    return ti, si, lvl


def _hgrn_forget(f_pre, lb):
    s_pos, s_neg = _sigmoid_pair(f_pre)
    f = lb + (1.0 - lb) * s_pos
    return (1.0 - lb) * s_neg, jnp.log2(jnp.maximum(f, LOG_FLOOR))


def _hgrn_levels():
    ti, si, lvl = _pair_levels()
    return jnp.where(ti > si, lvl, -1), jnp.where(ti < si, lvl, -1)


def _hgrn_chunk(q, k, lf2, v, st, fwd, lvl):
    C = HGRN_CHUNK
    nv = C // SUBLANES
    sub = lax.broadcasted_iota(jnp.int32, (nv, SUBLANES, LANES), 1)

    p = lf2.reshape(nv, SUBLANES, LANES)
    for s in (1, 2, 4):
        if fwd:
            p = p + jnp.where(sub >= s, pltpu.roll(p, s, 1), 0.0)
        else:
            p = p + jnp.where(sub < SUBLANES - s, pltpu.roll(p, SUBLANES - s, 1), 0.0)

    a_mat = jnp.zeros((C, C), F32)

    def add_level(a_mat, e, level):
        a_m = _dot_nt((q * e).astype(BF16), (k * e).astype(BF16))
        return jnp.where(lvl == level, a_m, a_mat)

    def row(r):
        return jnp.broadcast_to(p[:, r:r + 1, :], p.shape)

    odd = (sub & 1) == 1
    if fwd:
        bnd = (jnp.where(odd, pltpu.roll(p, 1, 1), p), jnp.where(sub < 4, row(1), row(5)), row(3))
    else:
        bnd = (jnp.where(odd, p, pltpu.roll(p, SUBLANES - 1, 1)), jnp.where(sub < 4, row(2), row(6)), row(4))
    for level, b in enumerate(bnd):
        a_mat = add_level(a_mat, jnp.exp2(-jnp.abs(p - b)).reshape(C, LANES), level)

    p = p.reshape(C, LANES)
    m = SUBLANES
    while m < C:
        nb = C // (2 * m)
        p4 = p.reshape(nb, 2, m, LANES)
        lo, hi = p4[:, 0], p4[:, 1]
        if fwd:
            tot = lo[:, m - 1:m]
            e_lo, e_hi = tot - lo, hi
            p_lo, p_hi = lo, hi + tot
        else:
            tot = hi[:, 0:1]
            e_lo, e_hi = lo, tot - hi
            p_lo, p_hi = lo + tot, hi
        e = jnp.exp2(jnp.concatenate([e_lo[:, None], e_hi[:, None]], axis=1).reshape(C, LANES))
        p = jnp.concatenate([p_lo[:, None], p_hi[:, None]], axis=1).reshape(C, LANES)
        a_mat = add_level(a_mat, e, int(np.log2(m)))
        m *= 2
    v16 = v.astype(BF16)
    o = jnp.sum(q * k, axis=-1, keepdims=True) * v + _dot(a_mat.astype(BF16), v16)

    if st is not None:
        o = o + _dot_nt((q * jnp.exp2(p)).astype(BF16), st.astype(BF16))
    tot = p[C - 1:C] if fwd else p[0:1]
    k_end = (k * jnp.exp2(tot - p)).astype(BF16)
    st_new = _dot_tn(v16, k_end)
    if st is not None:
        st_new = st_new + st * jnp.exp2(tot)
    return o, st_new


def _hgrn_finish(o, gn, g_pre):
    return (_rms(o, gn) * _silu(g_pre)).astype(BF16)


def _hgrn_head_static(read, lbf, lbb, gn, n_chunks):
    lvls = _hgrn_levels()
    st = [None, None]
    outs = [[None] * n_chunks, [None] * n_chunks]
    for c in range(n_chunks):
        for d, cc, lb in ((0, c, lbf), (1, n_chunks - 1 - c, lbb)):
            k, lf2 = _hgrn_forget(read(1 + d, cc), lb)
            outs[d][cc], st[d] = _hgrn_chunk(_silu(read(0, cc)), k, lf2, read(3, cc), st[d], d == 0, lvls[d])
    return [_hgrn_finish(outs[0][c] + outs[1][c], gn, read(4, c)) for c in range(n_chunks)], st


def _hgrn_lat_kernel(lbf_ref, lbb_ref, gn_ref, q_ref, ff_ref, fb_ref, v_ref, g_ref, s0_ref, o_ref,
                     of_ref, ob_ref, st_ref, *, n_chunks):
    C = HGRN_CHUNK
    for d in range(2):
        st_ref[d] = s0_ref[0, 0, d, 0].T
    lbf = lbf_ref[...]
    lbb = lbb_ref[...]
    lvls = _hgrn_levels()

    def body(c, carry):
        rf = pl.multiple_of(c * C, C)
        rb = pl.multiple_of((n_chunks - 1 - c) * C, C)
        for d, r0, f_ref, lb, out_ref in ((0, rf, ff_ref, lbf, of_ref), (1, rb, fb_ref, lbb, ob_ref)):
            sl = pl.ds(r0, C)
            k, lf2 = _hgrn_forget(f_ref[sl, :], lb)
            out_ref[sl, :], st_ref[d] = _hgrn_chunk(_silu(q_ref[sl, :]), k, lf2, v_ref[sl, :], st_ref[d], d == 0,
                                                    lvls[d])
        return carry

    lax.fori_loop(0, n_chunks, body, 0, unroll=HGRN_UNROLL)
    o_ref[...] = _hgrn_finish(of_ref[...] + ob_ref[...], gn_ref[...], g_ref[...])


def _hgrn_lat(p, lp, layer, state_in):
    tok_spec = pl.BlockSpec((DEC_SEQ, LANES), lambda b, h: (b, h))
    vec_spec = pl.BlockSpec((None, 1, LANES), lambda b, h: (layer, 0, h))
    state_spec = pl.BlockSpec((1, 1, 2, 1, LANES, LANES), lambda b, h: (b, layer, 0, h, 0, 0))
    return pl.pallas_call(
        functools.partial(_hgrn_lat_kernel, n_chunks=DEC_SEQ // HGRN_CHUNK),
        out_shape=jax.ShapeDtypeStruct((N_LAT, 512), BF16), grid=(DEC_BATCH, HGRN_HEADS),
        in_specs=[vec_spec] * 3 + [tok_spec] * 5 + [state_spec], out_specs=tok_spec,
        scratch_shapes=[pltpu.VMEM((DEC_SEQ, LANES), F32), pltpu.VMEM((DEC_SEQ, LANES), F32),
                        pltpu.VMEM((2, LANES, LANES), F32)],
        compiler_params=_params(2), name="hgrn_lat",
    )(lp["lbf"], lp["lbb"], lp["gn"], p["hq"], p["hff"], p["hfb"], p["hi"], p["hg"], state_in)


def _lane_lo(shape):
    return lax.broadcasted_iota(jnp.int32, shape, len(shape) - 1) < HEAD_DIM


def _softmax_pv(segs, sink_col=None):
    mx = None
    for s, _ in segs:
        m = jnp.max(s, axis=-1, keepdims=True)
        mx = m if mx is None else jnp.maximum(mx, m)
    if sink_col is not None:
        mx = jnp.maximum(mx, sink_col)
    den = None
    acc = None
    for s, v in segs:
        e = jnp.exp2(s - mx)
        dsum = jnp.sum(e, axis=-1, keepdims=True)
        den = dsum if den is None else den + dsum
        pv = _dot(e.astype(BF16), v)
        acc = pv if acc is None else acc + pv
    if sink_col is not None:
        den = den + jnp.exp2(sink_col - mx)
    return acc / den


def _gqa_group(q_ref, g, tq, segs_fn, sink_ref=None):
    lo = _lane_lo((tq, LANES))
    parts = []
    for j in range(2):
        qp = q_ref[:, 256 * g + 128 * j:256 * g + 128 * (j + 1)]
        zero = jnp.zeros_like(qp)
        parts += [jnp.where(lo, qp, zero), jnp.where(lo, zero, qp)]
    q4 = jnp.concatenate(parts, axis=0)
    sink_col = None
    if sink_ref is not None:
        sink_col = jnp.concatenate(
            [jnp.broadcast_to(sink_ref[0:1, 4 * g + i:4 * g + i + 1] * LOG2E, (tq, 1)) for i in range(4)], axis=0)
    pv = _softmax_pv(segs_fn(q4), sink_col)
    return (jnp.where(lo, pv[0:tq], pv[tq:2 * tq]), jnp.where(lo, pv[2 * tq:3 * tq], pv[3 * tq:4 * tq]))


def _mla_heads(q_ref, tq, segs_fn, o_ref):
    lo = _lane_lo((tq, LANES))
    for hp in range(MLA_HEADS // 2):
        outs = []
        for h in (2 * hp, 2 * hp + 1):
            q_h = q_ref[:, 128 * h:128 * (h + 1)]
            outs.append(_softmax_pv(segs_fn(h, q_h)))
        o_ref[:, 128 * hp:128 * (hp + 1)] = jnp.where(lo, outs[0], outs[1]).astype(BF16)


_ATTN_IN = ("sq", "sk", "sv", "mq", "mk", "mv", "gq", "gk", "gv")
_ATTN_WIDTH = dict(sq=512, sk=256, sv=256, mq=1024, mk=1024, mv=512, gq=512, gk=256, gv=256)


def _attn_ctx_kernel(sink_ref, sq_ref, sk_ref, sv_ref, mq_ref, mk_ref, mv_ref, gq_ref, gk_ref, gv_ref,
                     ob_ref, oc_ref, od_ref):
    tq = SEQ
    for g in range(N_KV):
        def swa_segs(q4, g=g):
            return [(_dot_nt(q4, sk_ref[:, 128 * g:128 * (g + 1)]), sv_ref[:, 128 * g:128 * (g + 1)])]

        a, b = _gqa_group(sq_ref, g, tq, swa_segs, sink_ref)
        ob_ref[:, 256 * g:256 * g + 128] = a.astype(BF16)
        ob_ref[:, 256 * g + 128:256 * (g + 1)] = b.astype(BF16)

        def gqa_segs(q4, g=g):
            return [(_dot_nt(q4, gk_ref[:, 128 * g:128 * (g + 1)]), gv_ref[:, 128 * g:128 * (g + 1)])]

        a, b = _gqa_group(gq_ref, g, tq, gqa_segs)
        od_ref[:, 256 * g:256 * g + 128] = a.astype(BF16)
        od_ref[:, 256 * g + 128:256 * (g + 1)] = b.astype(BF16)

    def mla_segs(h, q_h):
        return [(_dot_nt(q_h, mk_ref[:, 128 * h:128 * (h + 1)]), mv_ref[:, 128 * (h // 2):128 * (h // 2 + 1)])]

    _mla_heads(mq_ref, tq, mla_segs, oc_ref)


def _attn_ctx(lp, p, layer):
    def spec(w):
        return pl.BlockSpec((SEQ, w), lambda b: (b, 0))

    return pl.pallas_call(
        _attn_ctx_kernel,
        out_shape=[jax.ShapeDtypeStruct((N_CTX, 512), BF16)] * 3,
        grid=(BATCH,),
        in_specs=[_layer_spec((1, N_HEADS), layer)] + [spec(_ATTN_WIDTH[n]) for n in _ATTN_IN],
        out_specs=[spec(512)] * 3,
        compiler_params=_params(1),
        name="attn_ctx",
    )(lp["sink"], *[p[n] for n in _ATTN_IN])


def _dup_heads(x):
    lo = _lane_lo(x.shape)
    r = pltpu.roll(x, HEAD_DIM, 1)
    return jnp.where(lo, x, r), jnp.where(lo, r, x)


def _attn_lat_kernel(sink_ref, wk_ref, wuv_ref,
                     sq_ref, sk_ref, sv_ref, mq_ref, mk_ref, mv_ref, gq_ref, gk_ref, gv_ref,
                     csk_ref, csv_ref, cckv_ref, ckr_ref, cgk_ref, cgv_ref,
                     ob_ref, oc_ref, od_ref,
                     cs_k, cs_v, cm_k, cm_v, cg_k, cg_v):
    tq = TQ_LAT
    qi = pl.program_id(1)

    @pl.when(qi == 0)
    def _():
        for src, dst in ((csk_ref, cs_k), (csv_ref, cs_v), (cgk_ref, cg_k), (cgv_ref, cg_v)):
            d0, d1 = _dup_heads(src[0, 0])
            dst[0] = d0.astype(BF16)
            dst[1] = d1.astype(BF16)
        ckv16 = cckv_ref[0, 0].astype(BF16)
        k = _dot(ckv16, wk_ref[0:MLA_KV_RANK, :]) + _dot(ckr_ref[0, 0].astype(BF16),
                                                           wk_ref[MLA_KV_RANK:MLA_KV_RANK + MLA_ROPE, :])
        cm_k[...] = k.astype(BF16)
        cm_v[...] = _dot(ckv16, wuv_ref[...]).astype(BF16)

    q0 = qi * tq
    win = tq + 2 * SWA_WINDOW
    start = pl.multiple_of(jnp.clip(q0 - SWA_WINDOW, 0, DEC_SEQ - win), SWA_WINDOW)
    rows = lax.broadcasted_iota(jnp.int32, (4 * tq, win), 0)
    qpos = q0 + (rows & (tq - 1))
    kpos = start + lax.broadcasted_iota(jnp.int32, (4 * tq, win), 1)
    band = jnp.abs(qpos - kpos) <= SWA_WINDOW
    for g in range(N_KV):
        def swa_segs(q4, g=g):
            ks = sk_ref[pl.ds(start, win), 128 * g:128 * (g + 1)]
            vs = sv_ref[pl.ds(start, win), 128 * g:128 * (g + 1)]
            s_own = jnp.where(band, _dot_nt(q4, ks), MASK_VALUE)
            return [(_dot_nt(q4, cs_k[g]), cs_v[g]), (s_own, vs)]

        a, b = _gqa_group(sq_ref, g, tq, swa_segs, sink_ref)
        ob_ref[:, 256 * g:256 * g + 128] = a.astype(BF16)
        ob_ref[:, 256 * g + 128:256 * (g + 1)] = b.astype(BF16)

        def gqa_segs(q4, g=g):
            return [(_dot_nt(q4, cg_k[g]), cg_v[g]),
                    (_dot_nt(q4, gk_ref[:, 128 * g:128 * (g + 1)]), gv_ref[:, 128 * g:128 * (g + 1)])]

        a, b = _gqa_group(gq_ref, g, tq, gqa_segs)
        od_ref[:, 256 * g:256 * g + 128] = a.astype(BF16)
        od_ref[:, 256 * g + 128:256 * (g + 1)] = b.astype(BF16)

    def mla_segs(h, q_h):
        hp = h // 2
        return [(_dot_nt(q_h, cm_k[:, 128 * h:128 * (h + 1)]), cm_v[:, 128 * hp:128 * (hp + 1)]),
                (_dot_nt(q_h, mk_ref[:, 128 * h:128 * (h + 1)]), mv_ref[:, 128 * hp:128 * (hp + 1)])]

    _mla_heads(mq_ref, tq, mla_segs, oc_ref)


def _attn_lat(lp, p, caches, layer):
    nq = DEC_SEQ // TQ_LAT

    def qspec(w):
        return pl.BlockSpec((TQ_LAT, w), lambda b, i: (b * nq + i, 0))

    def kspec(w):
        return pl.BlockSpec((DEC_SEQ, w), lambda b, i: (b, 0))

    def cspec(w):
        return pl.BlockSpec((1, 1, PAST_LEN, w), lambda b, i: (b, layer, 0, 0))

    is_q = dict(sq=True, mq=True, gq=True)
    in_specs = [_layer_spec((1, N_HEADS), layer), _layer_spec((256, 1024), layer),
                _layer_spec((MLA_KV_RANK, 512), layer)]
    in_specs += [(qspec if is_q.get(n) else kspec)(_ATTN_WIDTH[n]) for n in _ATTN_IN]
    in_specs += [cspec(128), cspec(128), cspec(128), cspec(MLA_ROPE), cspec(128), cspec(128)]
    return pl.pallas_call(
        _attn_lat_kernel,
        out_shape=[jax.ShapeDtypeStruct((N_LAT, 512), BF16)] * 3,
        grid=(DEC_BATCH, nq),
        in_specs=in_specs,
        out_specs=[qspec(512)] * 3,
        scratch_shapes=[pltpu.VMEM((N_KV, PAST_LEN, LANES), BF16), pltpu.VMEM((N_KV, PAST_LEN, LANES), BF16),
                        pltpu.VMEM((PAST_LEN, 1024), BF16), pltpu.VMEM((PAST_LEN, 512), BF16),
                        pltpu.VMEM((N_KV, PAST_LEN, LANES), BF16), pltpu.VMEM((N_KV, PAST_LEN, LANES), BF16)],
        compiler_params=_params(2),
        name="attn_lat",
    )(lp["sink"], lp["wk"], lp["wuv"], *[p[n] for n in _ATTN_IN], *caches)


def _merge_kernel(x_ref, mod_ref, oa_ref, ob_ref, oc_ref, od_ref, gates_ref, wb_ref, wo_ref, npost_ref,
                  nmpre_ref, nmpost_ref, w1_ref, w2_ref, y_ref):
    x = x_ref[...]
    g1 = mod_ref[0, 2]
    sh2 = mod_ref[0, 3]
    sc2 = mod_ref[0, 4]
    g2 = mod_ref[0, 5]
    acc = None
    for n, o_ref in enumerate((oa_ref, ob_ref, oc_ref, od_ref)):
        br = _dot(o_ref[...], wb_ref[n]) * gates_ref[:, D_MODEL * n:D_MODEL * (n + 1)].astype(F32)
        acc = br if acc is None else acc + br
    y = _dot(acc.astype(BF16), wo_ref[...])
    x = x + g1 * _rms(y, npost_ref[...])
    h = (_rms(x, nmpre_ref[...]) * (1.0 + sc2) + sh2).astype(BF16)
    a = jnp.maximum(_dot(h, w1_ref[...]), 0.0)
    y = _dot((a * a).astype(BF16), w2_ref[...])
    y_ref[...] = x + g2 * _rms(y, nmpost_ref[...])


_MERGE_PARAMS = dict(w_branch=(N_BRANCH, BRANCH_W, D_MODEL), w_out=(D_MODEL, D_MODEL), npost=(1, D_MODEL),
                     nmpre=(1, D_MODEL), nmpost=(1, D_MODEL), w_mlp_in=(D_MODEL, D_FF), w_mlp_out=(D_FF, D_MODEL))


def _merge(x, mod4, o_a, o_b, o_c, o_d, gates, lp, is_ctx, layer):
    n_tok = x.shape[0]

    def tok(w):
        return pl.BlockSpec((TM, w), lambda i: (i, 0))

    in_specs = [tok(D_MODEL), _mod_spec(is_ctx, layer), tok(512), tok(512), tok(512), tok(512), tok(4096)]
    in_specs += [_layer_spec(shape, layer) for shape in _MERGE_PARAMS.values()]
    return pl.pallas_call(
        _merge_kernel, out_shape=jax.ShapeDtypeStruct((n_tok, D_MODEL), F32), grid=(n_tok // TM,),
        in_specs=in_specs, out_specs=tok(D_MODEL), compiler_params=_params(1),
        name="merge_ctx" if is_ctx else "merge_lat",
    )(x, mod4, o_a, o_b, o_c, o_d, gates, *[lp[n] for n in _MERGE_PARAMS])


def _rope_tables():
    t = np.arange(DEC_SEQ)
    row, col = (t // GRID_W).astype(np.float32), (t % GRID_W).astype(np.float32)

    def pattern(d):
        nf = d // 4
        inv = (ROPE_BASE ** (-np.arange(nf, dtype=np.float32) / nf)).astype(np.float32)
        ang = np.concatenate([row[:, None] * inv, row[:, None] * inv, col[:, None] * inv, col[:, None] * inv], axis=1)
        cos, sin = np.cos(ang), np.sin(ang)
        second = (np.arange(d) % (2 * nf)) >= nf
        return cos, np.where(second, sin, 0.0), np.where(second, 0.0, -sin)

    out = np.zeros((9, DEC_SEQ, LANES), np.float32)
    out[[0, 3, 6]] = 1.0
    c, s1, s2 = pattern(HEAD_DIM)
    for a, tab in enumerate((c, s1, s2)):
        out[_TAB_64 + a] = np.tile(tab, (1, LANES // HEAD_DIM))
    c, s1, s2 = pattern(MLA_ROPE)
    for a, tab in enumerate((c, s1, s2)):
        out[_TAB_MLAQ + a][:, MLA_NOPE:MLA_NOPE + MLA_ROPE] = tab
        out[_TAB_KR + a][:, :MLA_ROPE] = tab
    return out


_ROPE_TABLES = _rope_tables()


def _block_mean_matrix(n):
    i = np.arange(n)
    return ((i[:, None] // HEAD_DIM) == (i[None, :] // HEAD_DIM)).astype(np.float32) / HEAD_DIM


def _mla_weight_layouts(w_uq, w_ukv):
    dq = MLA_NOPE + MLA_ROPE
    wq = w_uq.reshape(DEPTH, MLA_Q_RANK, MLA_HEADS, dq)
    wq = jnp.pad(wq, ((0, 0), (0, 0), (0, 0), (0, LANES - dq))).reshape(DEPTH, MLA_Q_RANK, MLA_HEADS * LANES)
    wkv = w_ukv.reshape(DEPTH, MLA_KV_RANK, MLA_HEADS, MLA_NOPE + MLA_V)
    wk = jnp.pad(wkv[..., :MLA_NOPE], ((0, 0), (0, 0), (0, 0), (0, LANES - MLA_NOPE)))
    wk = wk.reshape(DEPTH, MLA_KV_RANK, MLA_HEADS * LANES)
    place = np.zeros((LANES, MLA_HEADS, LANES), np.float32)
    for j in range(MLA_ROPE):
        place[j, :, MLA_NOPE + j] = 1.0
    place = jnp.broadcast_to(jnp.asarray(place.reshape(1, LANES, MLA_HEADS * LANES)), (DEPTH, LANES, MLA_HEADS * LANES))
    w_k = jnp.concatenate([wk, place], axis=1)
    w_uv = wkv[..., MLA_NOPE:].reshape(DEPTH, MLA_KV_RANK, MLA_HEADS * MLA_V)
    return wq.astype(BF16), w_k.astype(BF16), w_uv.astype(BF16)


def _lower_bounds(p):
    s = jax.nn.softmax(p.astype(F32), axis=0)
    return jnp.cumsum(s, axis=0) - s[0]


def kernel(x_prompt, x_sample, state_hgrn, cache_swa_k, cache_swa_v, cache_mla_ckv, cache_mla_kr, cache_gqa_k, cache_gqa_v, c, c_ctx, w_ada, b_ada, norm_mix_pre, norm_mix_post, norm_mlp_pre, norm_mlp_post, w_in, hgrn_lb_fwd, hgrn_lb_bwd, hgrn_norm, swa_sink, mla_q_norm, mla_kv_norm, mla_w_uq, mla_w_ukv, gqa_q_norm, gqa_k_norm, w_branch, w_out, w_mlp_in, w_mlp_out):
    tabs = jnp.asarray(_ROPE_TABLES)
    b512 = jnp.asarray(_block_mean_matrix(512)).astype(BF16)

    def rows(a):
        return a.reshape(DEPTH, 1, -1)

    w_uq, w_k, w_uv = _mla_weight_layouts(mla_w_uq, mla_w_ukv)
    lp = dict(
        npre=rows(norm_mix_pre), w=_pad_w_in(w_in), mlaq_g=rows(mla_q_norm), mlakv_g=rows(mla_kv_norm),
        wuq=w_uq, wk=w_k, wuv=w_uv, gq_g=rows(jnp.tile(gqa_q_norm, (1, N_HEADS))),
        gk_g=rows(jnp.tile(gqa_k_norm, (1, N_KV))), b512=b512, b128=b512[:128, :128],
        lbf=rows(_lower_bounds(hgrn_lb_fwd)), lbb=rows(_lower_bounds(hgrn_lb_bwd)), gn=rows(hgrn_norm),
        sink=rows(swa_sink),
        w_branch=w_branch.astype(BF16), w_out=w_out.astype(BF16), npost=rows(norm_mix_post),
        nmpre=rows(norm_mlp_pre), nmpost=rows(norm_mlp_post),
        w_mlp_in=w_mlp_in.astype(BF16), w_mlp_out=w_mlp_out.astype(BF16))

    cond8 = jnp.zeros((SUBLANES, D_MODEL), F32).at[0].set(c_ctx).at[1:1 + DEC_BATCH].set(c)
    mod4 = _ada(cond8, w_ada, b_ada).reshape(DEPTH, SUBLANES, N_MOD, 1, D_MODEL)

    caches_in = (cache_swa_k.reshape(DEC_BATCH, DEPTH, PAST_LEN, 128),
                 cache_swa_v.reshape(DEC_BATCH, DEPTH, PAST_LEN, 128), cache_mla_ckv, cache_mla_kr,
                 cache_gqa_k.reshape(DEC_BATCH, DEPTH, PAST_LEN, 128),
                 cache_gqa_v.reshape(DEC_BATCH, DEPTH, PAST_LEN, 128))

    x_ctx = x_prompt.reshape(N_CTX, D_MODEL)
    x_lat = x_sample.reshape(N_LAT, D_MODEL)
    pc = None
    for l in range(DEPTH):
        pc = _proj(x_ctx, mod4, lp, True, layer=l, prev=pc)
        ob, oc, od = _attn_ctx(lp, pc, l)
        x_ctx = _merge(x_ctx, mod4, pc["oa"], ob, oc, od, pc["gates"], lp, True, l)

        pq = _proj(x_lat, mod4, lp, False, layer=l, tabs=tabs)
        oa = _hgrn_lat(pq, lp, l, state_hgrn)
        ob, oc, od = _attn_lat(lp, pq, caches_in, l)
        x_lat = _merge(x_lat, mod4, oa, ob, oc, od, pq["gates"], lp, False, l)

    kv_shape = (BATCH, DEPTH, SEQ, N_KV, HEAD_DIM)
    return (x_ctx.reshape(BATCH, SEQ, D_MODEL), x_lat.reshape(DEC_BATCH, DEC_SEQ, D_MODEL), pc["st"],
            pc["skc"].reshape(kv_shape), pc["svc"].reshape(kv_shape), pc["mckv"], pc["mkr"],
            pc["gkc"].reshape(kv_shape), pc["gvc"].reshape(kv_shape))
```

```python
import functools

import jax
import jax.numpy as jnp
import numpy as np
from jax import lax
from jax.experimental import pallas as pl
from jax.experimental.pallas import tpu as pltpu

D_MODEL = 1024
BATCH, SEQ = 32, 256
DEC_BATCH, DEC_SEQ = 2, 2048
DEPTH = 2
PAST_LEN = 512
GRID_W = 64
ROPE_BASE = 10000.0
NORM_EPS = 1e-6
MASK_VALUE = -1e30
LOG_FLOOR = 1e-30
N_MOD = 6
N_BRANCH = 4
BRANCH_W = 512
D_FF = 4 * D_MODEL
HGRN_HEADS, HGRN_DK = 4, 128
SWA_WINDOW = 128
MLA_HEADS, MLA_Q_RANK, MLA_KV_RANK, MLA_NOPE, MLA_ROPE, MLA_V = 8, 256, 128, 64, 32, 64
HEAD_DIM = 64
N_HEADS, N_KV = 8, 2

N_CTX = BATCH * SEQ
N_LAT = DEC_BATCH * DEC_SEQ

LANES = 128
SUBLANES = 8
VMEM_LIMIT_BYTES = 56 * 1024 * 1024

TM = 256
LAT_TILES_PER_SEQ = DEC_SEQ // TM
HGRN_CHUNK = 256
HGRN_UNROLL = 4
TQ_LAT = 128

BF16 = jnp.bfloat16
F32 = jnp.float32

_C_HGRN = 0
_C_SWA_Q = 2560
_C_SWA_KV = 3072
_C_MLA_CQ = 3328
_C_MLA_CKV = 3584
_C_MLA_KR = 3712
_KR_PAD = LANES - MLA_ROPE
_C_GQA_Q = 3840
_C_GQA_KV = 4352
_C_GATES = 4608
W_IN_COLS = _C_GATES + N_BRANCH * D_MODEL


def _pad_w_in(w):
    cut = _C_MLA_KR + MLA_ROPE
    zeros = jnp.zeros(w.shape[:-1] + (_KR_PAD,), BF16)
    return jnp.concatenate([w[..., :cut].astype(BF16), zeros, w[..., cut:].astype(BF16)], axis=-1)


def _sigmoid_pair(x):
    a = jnp.exp(-jnp.abs(x))
    r = 1.0 / (1.0 + a)
    ar = a * r
    pos = x >= 0
    return jnp.where(pos, r, ar), jnp.where(pos, ar, r)


def _sigmoid(x):
    return _sigmoid_pair(x)[0]


def _silu(x):
    return x * _sigmoid(x)


def _rms(x, gain):
    return x * lax.rsqrt(jnp.mean(x * x, axis=-1, keepdims=True) + NORM_EPS) * gain


def _dot(a, b):
    return jnp.dot(a, b, preferred_element_type=F32)


def _dot_nt(a, b):
    return lax.dot_general(a, b, (((1,), (1,)), ((), ())), preferred_element_type=F32)


def _dot_tn(a, b):
    return lax.dot_general(a, b, (((0,), (0,)), ((), ())), preferred_element_type=F32)


def _tile_lanes(t, width):
    reps = width // LANES
    return t if reps == 1 else jnp.concatenate([t] * reps, axis=1)


def _rope(x, tab_ref, base, shift):
    w = x.shape[1]
    c = _tile_lanes(tab_ref[base], w)
    s1 = _tile_lanes(tab_ref[base + 1], w)
    s2 = _tile_lanes(tab_ref[base + 2], w)
    return x * c + pltpu.roll(x, shift, 1) * s1 + pltpu.roll(x, w - shift, 1) * s2


def _head_rms(x, bmat_ref, gain):
    ms = _dot((x * x).astype(BF16), bmat_ref[...])
    return x * lax.rsqrt(ms + NORM_EPS) * gain


def _const_spec(shape):
    nd = len(shape)
    return pl.BlockSpec(shape, lambda *_: (0,) * nd, pipeline_mode=pl.Buffered(1))


def _layer_spec(shape, layer):
    nd = len(shape)
    return pl.BlockSpec((None,) + tuple(shape), lambda *_: (layer,) + (0,) * nd, pipeline_mode=pl.Buffered(1))


_ANY_SPEC = pl.BlockSpec(memory_space=pl.ANY)


def _params(n_grid_dims):
    return pltpu.CompilerParams(dimension_semantics=("arbitrary",) * n_grid_dims, vmem_limit_bytes=VMEM_LIMIT_BYTES)


_ADA_TN = 2048


def _ada_kernel(c_ref, w_ref, b_ref, o_ref):
    s = _silu(c_ref[...]).astype(BF16)
    o_ref[0] = _dot(s, w_ref[0].astype(BF16)) + b_ref[0]


def _ada(cond8, w_ada, b_ada):
    n = N_MOD * D_MODEL
    return pl.pallas_call(
        _ada_kernel,
        out_shape=jax.ShapeDtypeStruct((DEPTH, SUBLANES, n), F32),
        grid=(DEPTH, n // _ADA_TN),
        in_specs=[
            pl.BlockSpec((SUBLANES, D_MODEL), lambda l, j: (0, 0)),
            pl.BlockSpec((1, D_MODEL, _ADA_TN), lambda l, j: (l, 0, j)),
            pl.BlockSpec((1, 1, _ADA_TN), lambda l, j: (l, 0, j)),
        ],
        out_specs=pl.BlockSpec((1, SUBLANES, _ADA_TN), lambda l, j: (l, 0, j)),
        compiler_params=_params(2),
        name="ada",
    )(cond8, w_ada, b_ada.reshape(DEPTH, 1, n))


def _mod_spec(is_ctx, layer):
    if is_ctx:
        return pl.BlockSpec((None, 1, N_MOD, 1, D_MODEL), lambda i: (layer, 0, 0, 0, 0))
    return pl.BlockSpec((None, 1, N_MOD, 1, D_MODEL), lambda i: (layer, 1 + i // LAT_TILES_PER_SEQ, 0, 0, 0))


LOG2E = float(np.log2(np.e))
_SCALE_64 = HEAD_DIM ** -0.5 * LOG2E
_SCALE_MLA = (MLA_NOPE + MLA_ROPE) ** -0.5 * LOG2E
_TAB_64, _TAB_MLAQ, _TAB_KR = 0, 3, 6

_HGRN_PRE = ("hq", "hff", "hfb", "hi", "hg")
_ATTN_OUTS = (
    ("sq", 512, BF16), ("sk", 256, BF16), ("sv", 256, BF16),
    ("mq", 1024, BF16), ("mk", 1024, BF16), ("mv", 1024, BF16),
    ("gq", 512, BF16), ("gk", 256, BF16), ("gv", 256, BF16), ("gates", 4096, BF16))
_CACHE_OUTS = (("skc", 128), ("svc", 128), ("mckv", 128), ("mkr", MLA_ROPE), ("gkc", 128), ("gvc", 128))
_PROJ_PARAMS = dict(npre=(1, D_MODEL), w=(D_MODEL, W_IN_COLS), mlaq_g=(1, MLA_Q_RANK), mlakv_g=(1, MLA_KV_RANK),
                    wuq=(MLA_Q_RANK, 1024), wk=(256, 1024), wuv=(MLA_KV_RANK, 1024), gq_g=(1, 512), gk_g=(1, 128))
_HGRN_PARAMS = dict(lbf=(1, 512), lbb=(1, 512), gn=(1, 512))
_PROJ_CONSTS = tuple(_PROJ_PARAMS) + ("b512", "b128")
_HGRN_CONSTS = tuple(_HGRN_PARAMS)


def _proj_out_names(is_ctx):
    attn = [n for n, _, _ in _ATTN_OUTS]
    if is_ctx:
        return ["oa"] + attn + [n for n, _ in _CACHE_OUTS] + ["st"]
    return list(_HGRN_PRE) + attn


def _proj_kernel(*refs, is_ctx, n_alias):
    names = ["x", "mod"] + list(_PROJ_CONSTS) + (list(_HGRN_CONSTS) if is_ctx else ["tab"])
    r = dict(zip(names, refs))
    pos = len(names) + n_alias
    for name in _proj_out_names(is_ctx) + (["hbuf"] if is_ctx else []):
        r[name] = refs[pos]
        pos += 1

    x = r["x"][...]
    sh1 = r["mod"][0, 0]
    sc1 = r["mod"][0, 1]
    h = (_rms(x, r["npre"][...]) * (1.0 + sc1) + sh1).astype(BF16)
    w_ref = r["w"]

    def seg(a, w):
        return _dot(h, w_ref[:, a:a + w])

    def rope(v, base, shift):
        return v if is_ctx else _rope(v, r["tab"], base, shift)

    for j, name in enumerate(_HGRN_PRE):
        u = seg(_C_HGRN + 512 * j, 512)
        if is_ctx:
            r["hbuf"][j] = u
        else:
            r[name][...] = u

    def hgrn_all(between):
        C = HGRN_CHUNK
        n = TM // C
        lvls = _hgrn_levels()
        heads = range(HGRN_HEADS)
        cols = [slice(LANES * hd, LANES * (hd + 1)) for hd in heads]
        st = [[None, None] for _ in heads]
        outs = [[[None] * n, [None] * n] for _ in heads]
        for c in range(n):
            for hd in heads:
                for d, cc, lb in ((0, c, r["lbf"]), (1, n - 1 - c, r["lbb"])):
                    rows = slice(C * cc, C * (cc + 1))
                    k, lf2 = _hgrn_forget(r["hbuf"][1 + d, rows, cols[hd]], lb[:, cols[hd]])
                    outs[hd][d][cc], st[hd][d] = _hgrn_chunk(
                        _silu(r["hbuf"][0, rows, cols[hd]]), k, lf2, r["hbuf"][3, rows, cols[hd]], st[hd][d],
                        d == 0, lvls[d])
            between(c)
        for hd in heads:
            for c in range(n):
                rows = slice(C * c, C * (c + 1))
                r["oa"][rows, cols[hd]] = _hgrn_finish(outs[hd][0][c] + outs[hd][1][c], r["gn"][:, cols[hd]],
                                                       r["hbuf"][4, rows, cols[hd]])
            for d in range(2):
                r["st"][0, 0, d, hd] = st[hd][d].T

    def dup16(k):
        return jnp.concatenate([d.astype(BF16) for d in _dup_heads(k)], axis=1)

    def ones16(v):
        return jnp.concatenate([d.astype(BF16) for d in _heads_with_ones(v)], axis=1)

    def swa():
        r["sq"][...] = (rope(seg(_C_SWA_Q, 512), _TAB_64, 16) * _SCALE_64).astype(BF16)
        kv = seg(_C_SWA_KV, 256)
        k, v = kv[:, :128], kv[:, 128:]
        r["sk"][...] = dup16(rope(k, _TAB_64, 16))
        r["sv"][...] = ones16(v)
        if is_ctx:
            r["skc"][0, 0] = k
            r["svc"][0, 0] = v

    def mla():
        cq = _rms(seg(_C_MLA_CQ, 256), r["mlaq_g"][...]).astype(BF16)
        q = _dot(cq, r["wuq"][...])
        r["mq"][...] = (rope(q, _TAB_MLAQ, 8) * _SCALE_MLA).astype(BF16)
        ckv = _rms(seg(_C_MLA_CKV, 128), r["mlakv_g"][...])
        kr = seg(_C_MLA_KR, 128)
        if is_ctx:
            r["mckv"][0, 0] = ckv
            r["mkr"][0, 0] = kr[:, :MLA_ROPE]
        kr_used = rope(kr, _TAB_KR, 8)
        ckv16 = ckv.astype(BF16)
        r["mk"][...] = _dot(jnp.concatenate([ckv16, kr_used.astype(BF16)], axis=1), r["wk"][...]).astype(BF16)
        r["mv"][...] = _with_ones(_dot(ckv16, r["wuv"][...])).astype(BF16)

    def gqa():
        gq = _head_rms(seg(_C_GQA_Q, 512), r["b512"], r["gq_g"][...])
        r["gq"][...] = (rope(gq, _TAB_64, 16) * _SCALE_64).astype(BF16)
        kv = seg(_C_GQA_KV, 256)
        k, v = _head_rms(kv[:, :128], r["b128"], r["gk_g"][...]), kv[:, 128:]
        r["gk"][...] = dup16(rope(k, _TAB_64, 16))
        r["gv"][...] = ones16(v)
        if is_ctx:
            r["gkc"][0, 0] = k
            r["gvc"][0, 0] = v

    def gates(j):
        r["gates"][:, D_MODEL * j:D_MODEL * (j + 1)] = _sigmoid(seg(_C_GATES + D_MODEL * j, D_MODEL)).astype(BF16)

    tasks = [swa, mla, gqa] + [functools.partial(gates, j) for j in range(N_BRANCH)]
    if is_ctx:
        per_step = -(-len(tasks) // (TM // HGRN_CHUNK))

        def between(c):
            for t in tasks[per_step * c:per_step * (c + 1)]:
                t()

        hgrn_all(between)
    else:
        for t in tasks:
            t()


def _proj(x, mod4, lp, is_ctx, layer=0, tabs=None, prev=None):
    n_tok = x.shape[0]
    params = dict(_PROJ_PARAMS, **(_HGRN_PARAMS if is_ctx else {}))
    in_specs = [pl.BlockSpec((TM, D_MODEL), lambda i: (i, 0)), _mod_spec(is_ctx, layer)]
    in_specs += [_layer_spec(params[n], layer) for n in _PROJ_PARAMS] + [_const_spec((512, 512)), _const_spec((128, 128))]
    args = [x, mod4] + [lp[n] for n in _PROJ_PARAMS] + [lp["b512"], lp["b128"]]
    if is_ctx:
        in_specs += [_layer_spec(params[n], layer) for n in _HGRN_PARAMS]
        args += [lp[n] for n in _HGRN_PARAMS]
    if not is_ctx:
        in_specs.append(pl.BlockSpec((9, TM, LANES), lambda i: (0, i % LAT_TILES_PER_SEQ, 0)))
        args.append(tabs)

    def tok(w):
        return pl.BlockSpec((TM, w), lambda i: (i, 0))

    specs = {n: (tok(w), jax.ShapeDtypeStruct((n_tok, w), dt)) for n, w, dt in _ATTN_OUTS}
    if is_ctx:
        specs["oa"] = (tok(512), jax.ShapeDtypeStruct((n_tok, 512), BF16))
        for n, w in _CACHE_OUTS:
            specs[n] = (pl.BlockSpec((1, 1, SEQ, w), lambda i: (i, layer, 0, 0)),
                        jax.ShapeDtypeStruct((BATCH, DEPTH, SEQ, w), F32))
        specs["st"] = (pl.BlockSpec((1, 1, 2, HGRN_HEADS, LANES, LANES), lambda i: (i, layer, 0, 0, 0, 0)),
                       jax.ShapeDtypeStruct((BATCH, DEPTH, 2, HGRN_HEADS, LANES, LANES), F32))
    else:
        for n in _HGRN_PRE:
            specs[n] = (tok(512), jax.ShapeDtypeStruct((n_tok, 512), F32))
    out_names = _proj_out_names(is_ctx)
    aliases = {}
    n_alias = 0
    if prev is not None:
        for n in [n for n, _ in _CACHE_OUTS] + ["st"]:
            aliases[len(args)] = out_names.index(n)
            in_specs.append(_ANY_SPEC)
            args.append(prev[n])
            n_alias += 1
    outs = pl.pallas_call(
        functools.partial(_proj_kernel, is_ctx=is_ctx, n_alias=n_alias),
        out_shape=[specs[n][1] for n in out_names], grid=(n_tok // TM,), in_specs=in_specs,
        out_specs=[specs[n][0] for n in out_names],
        scratch_shapes=[pltpu.VMEM((len(_HGRN_PRE), TM, 512), F32)] if is_ctx else [],
        input_output_aliases=aliases, compiler_params=_params(1),
        name="proj_ctx" if is_ctx else "proj_lat",
    )(*args)
    return dict(zip(out_names, outs))


def _pair_levels():
    C = HGRN_CHUNK
    ti = lax.broadcasted_iota(jnp.int32, (C, C), 0)
    si = lax.broadcasted_iota(jnp.int32, (C, C), 1)
    x = ti ^ si
    lvl = jnp.where(x == 0, -1, 0)
    b = 2
    while b < C:
        lvl = lvl + (x >= b).astype(jnp.int32)
        b *= 2
    return ti, si, lvl


def _hgrn_forget(f_pre, lb):
    s_pos, s_neg = _sigmoid_pair(f_pre)
    f = lb + (1.0 - lb) * s_pos
    return (1.0 - lb) * s_neg, jnp.log2(jnp.maximum(f, LOG_FLOOR))


def _hgrn_levels():
    ti, si, lvl = _pair_levels()
    return jnp.where(ti > si, lvl, -1), jnp.where(ti < si, lvl, -1)


def _hgrn_chunk(q, k, lf2, v, st, fwd, lvl):
    C = HGRN_CHUNK
    nv = C // SUBLANES
    sub = lax.broadcasted_iota(jnp.int32, (nv, SUBLANES, LANES), 1)

    p = lf2.reshape(nv, SUBLANES, LANES)
    for s in (1, 2, 4):
        if fwd:
            p = p + jnp.where(sub >= s, pltpu.roll(p, s, 1), 0.0)
        else:
            p = p + jnp.where(sub < SUBLANES - s, pltpu.roll(p, SUBLANES - s, 1), 0.0)

    a_mat = jnp.zeros((C, C), F32)

    def add_level(a_mat, e, level):
        a_m = _dot_nt((q * e).astype(BF16), (k * e).astype(BF16))
        return jnp.where(lvl == level, a_m, a_mat)

    def row(r):
        return jnp.broadcast_to(p[:, r:r + 1, :], p.shape)

    odd = (sub & 1) == 1
    if fwd:
        bnd = (jnp.where(odd, pltpu.roll(p, 1, 1), p), jnp.where(sub < 4, row(1), row(5)), row(3))
    else:
        bnd = (jnp.where(odd, p, pltpu.roll(p, SUBLANES - 1, 1)), jnp.where(sub < 4, row(2), row(6)), row(4))
    for level, b in enumerate(bnd):
        a_mat = add_level(a_mat, jnp.exp2(-jnp.abs(p - b)).reshape(C, LANES), level)

    p = p.reshape(C, LANES)
    m = SUBLANES
    while m < C:
        nb = C // (2 * m)
        p4 = p.reshape(nb, 2, m, LANES)
        lo, hi = p4[:, 0], p4[:, 1]
        if fwd:
            tot = lo[:, m - 1:m]
            e_lo, e_hi = tot - lo, hi
            p_lo, p_hi = lo, hi + tot
        else:
            tot = hi[:, 0:1]
            e_lo, e_hi = lo, tot - hi
            p_lo, p_hi = lo + tot, hi
        e = jnp.exp2(jnp.concatenate([e_lo[:, None], e_hi[:, None]], axis=1).reshape(C, LANES))
        p = jnp.concatenate([p_lo[:, None], p_hi[:, None]], axis=1).reshape(C, LANES)
        a_mat = add_level(a_mat, e, int(np.log2(m)))
        m *= 2
    v16 = v.astype(BF16)
    o = jnp.sum(q * k, axis=-1, keepdims=True) * v + _dot(a_mat.astype(BF16), v16)

    if st is not None:
        o = o + _dot_nt((q * jnp.exp2(p)).astype(BF16), st.astype(BF16))
    tot = p[C - 1:C] if fwd else p[0:1]
    k_end = (k * jnp.exp2(tot - p)).astype(BF16)
    st_new = _dot_tn(v16, k_end)
    if st is not None:
        st_new = st_new + st * jnp.exp2(tot)
    return o, st_new


def _hgrn_finish(o, gn, g_pre):
    return (_rms(o, gn) * _silu(g_pre)).astype(BF16)


def _hgrn_head_static(read, lbf, lbb, gn, n_chunks):
    lvls = _hgrn_levels()
    st = [None, None]
    outs = [[None] * n_chunks, [None] * n_chunks]
    for c in range(n_chunks):
        for d, cc, lb in ((0, c, lbf), (1, n_chunks - 1 - c, lbb)):
            k, lf2 = _hgrn_forget(read(1 + d, cc), lb)
            outs[d][cc], st[d] = _hgrn_chunk(_silu(read(0, cc)), k, lf2, read(3, cc), st[d], d == 0, lvls[d])
    return [_hgrn_finish(outs[0][c] + outs[1][c], gn, read(4, c)) for c in range(n_chunks)], st


def _hgrn_lat_kernel(lbf_ref, lbb_ref, gn_ref, q_ref, ff_ref, fb_ref, v_ref, g_ref, s0_ref, o_ref,
                     of_ref, ob_ref, st_ref, *, n_chunks):
    C = HGRN_CHUNK
    for d in range(2):
        st_ref[d] = s0_ref[0, 0, d, 0].T
    lbf = lbf_ref[...]
    lbb = lbb_ref[...]
    lvls = _hgrn_levels()

    def body(c, carry):
        rf = pl.multiple_of(c * C, C)
        rb = pl.multiple_of((n_chunks - 1 - c) * C, C)
        for d, r0, f_ref, lb, out_ref in ((0, rf, ff_ref, lbf, of_ref), (1, rb, fb_ref, lbb, ob_ref)):
            sl = pl.ds(r0, C)
            k, lf2 = _hgrn_forget(f_ref[sl, :], lb)
            out_ref[sl, :], st_ref[d] = _hgrn_chunk(_silu(q_ref[sl, :]), k, lf2, v_ref[sl, :], st_ref[d], d == 0,
                                                    lvls[d])
        return carry

    lax.fori_loop(0, n_chunks, body, 0, unroll=HGRN_UNROLL)
    o_ref[...] = _hgrn_finish(of_ref[...] + ob_ref[...], gn_ref[...], g_ref[...])


def _hgrn_lat(p, lp, layer, state_in):
    tok_spec = pl.BlockSpec((DEC_SEQ, LANES), lambda b, h: (b, h))
    vec_spec = pl.BlockSpec((None, 1, LANES), lambda b, h: (layer, 0, h))
    state_spec = pl.BlockSpec((1, 1, 2, 1, LANES, LANES), lambda b, h: (b, layer, 0, h, 0, 0))
    return pl.pallas_call(
        functools.partial(_hgrn_lat_kernel, n_chunks=DEC_SEQ // HGRN_CHUNK),
        out_shape=jax.ShapeDtypeStruct((N_LAT, 512), BF16), grid=(DEC_BATCH, HGRN_HEADS),
        in_specs=[vec_spec] * 3 + [tok_spec] * 5 + [state_spec], out_specs=tok_spec,
        scratch_shapes=[pltpu.VMEM((DEC_SEQ, LANES), F32), pltpu.VMEM((DEC_SEQ, LANES), F32),
                        pltpu.VMEM((2, LANES, LANES), F32)],
        compiler_params=_params(2), name="hgrn_lat",
    )(lp["lbf"], lp["lbb"], lp["gn"], p["hq"], p["hff"], p["hfb"], p["hi"], p["hg"], state_in)


def _lane_lo(shape):
    return (lax.broadcasted_iota(jnp.int32, shape, len(shape) - 1) & (LANES - 1)) < HEAD_DIM


def _with_ones(v):
    return jnp.where(_lane_lo(v.shape), v, 1.0)


def _softmax_unit(make_segs, sink_col, finish):
    segs = make_segs()
    mx = None
    for s, _ in segs:
        m = jnp.max(s, axis=-1, keepdims=True)
        mx = m if mx is None else jnp.maximum(mx, m)
    if sink_col is not None:
        mx = jnp.maximum(mx, sink_col)
    yield
    acc = None
    for s, v in segs:
        pv = _dot(jnp.exp2(s - mx).astype(BF16), v)
        acc = pv if acc is None else acc + pv
    den = jnp.where(_lane_lo(acc.shape), pltpu.roll(acc, HEAD_DIM, 1), acc)
    if sink_col is not None:
        den = den + jnp.exp2(sink_col - mx)
    finish(acc / den)


_DONE = object()


def _run_units(units):
    pending = list(units)
    active = []
    while pending or active:
        started = []
        if pending:
            u = pending.pop(0)
            next(u)
            started.append(u)
        active = started + [a for a in active if next(a, _DONE) is not _DONE]


def _pair(lo, even, odd):
    return jnp.where(lo, even, pltpu.roll(odd, HEAD_DIM, 1))


def _gqa_unit(q_ref, g, tq, segs_fn, o_ref, sink_ref=None):
    lo = _lane_lo((tq, LANES))
    parts = []
    for j in range(2):
        qp = q_ref[:, 256 * g + 128 * j:256 * g + 128 * (j + 1)]
        zero = jnp.zeros_like(qp)
        parts += [jnp.where(lo, qp, zero), jnp.where(lo, zero, qp)]
    q4 = jnp.concatenate(parts, axis=0)
    sink_col = None
    if sink_ref is not None:
        sink_col = jnp.concatenate(
            [jnp.broadcast_to(sink_ref[0:1, 4 * g + i:4 * g + i + 1] * LOG2E, (tq, 1)) for i in range(4)], axis=0)

    def finish(pv):
        o_ref[:, 256 * g:256 * g + 128] = _pair(lo, pv[0:tq], pv[tq:2 * tq]).astype(BF16)
        o_ref[:, 256 * g + 128:256 * (g + 1)] = _pair(lo, pv[2 * tq:3 * tq], pv[3 * tq:4 * tq]).astype(BF16)

    yield from _softmax_unit(lambda: segs_fn(q4), sink_col, finish)


def _mla_pair_unit(q_ref, hp, tq, segs_fn, o_ref):
    outs = []
    subs = [_softmax_unit(functools.partial(segs_fn, h, q_ref[:, 128 * h:128 * (h + 1)]), None, outs.append)
            for h in (2 * hp, 2 * hp + 1)]
    while True:
        if [next(s, _DONE) for s in subs][0] is _DONE:
            break
        yield
    o_ref[:, 128 * hp:128 * (hp + 1)] = _pair(_lane_lo((tq, LANES)), outs[0], outs[1]).astype(BF16)


_ATTN_IN = ("sq", "sk", "sv", "mq", "mk", "mv", "gq", "gk", "gv")
_ATTN_WIDTH = dict(sq=512, sk=256, sv=256, mq=1024, mk=1024, mv=1024, gq=512, gk=256, gv=256)


def _attn_ctx_kernel(sink_ref, sq_ref, sk_ref, sv_ref, mq_ref, mk_ref, mv_ref, gq_ref, gk_ref, gv_ref,
                     ob_ref, oc_ref, od_ref):
    tq = SEQ
    units = []
    for g in range(N_KV):
        def swa_segs(q4, g=g):
            return [(_dot_nt(q4, sk_ref[:, 128 * g:128 * (g + 1)]), sv_ref[:, 128 * g:128 * (g + 1)])]

        def gqa_segs(q4, g=g):
            return [(_dot_nt(q4, gk_ref[:, 128 * g:128 * (g + 1)]), gv_ref[:, 128 * g:128 * (g + 1)])]

        units += [_gqa_unit(sq_ref, g, tq, swa_segs, ob_ref, sink_ref), _gqa_unit(gq_ref, g, tq, gqa_segs, od_ref)]

    def mla_segs(h, q_h):
        return [(_dot_nt(q_h, mk_ref[:, 128 * h:128 * (h + 1)]), mv_ref[:, 128 * h:128 * (h + 1)])]

    units += [_mla_pair_unit(mq_ref, hp, tq, mla_segs, oc_ref) for hp in range(MLA_HEADS // 2)]
    _run_units(units)


def _attn_ctx(lp, p, layer):
    def spec(w):
        return pl.BlockSpec((SEQ, w), lambda b: (b, 0))

    return pl.pallas_call(
        _attn_ctx_kernel,
        out_shape=[jax.ShapeDtypeStruct((N_CTX, 512), BF16)] * 3,
        grid=(BATCH,),
        in_specs=[_layer_spec((1, N_HEADS), layer)] + [spec(_ATTN_WIDTH[n]) for n in _ATTN_IN],
        out_specs=[spec(512)] * 3,
        compiler_params=_params(1),
        name="attn_ctx",
    )(lp["sink"], *[p[n] for n in _ATTN_IN])


def _dup_heads(x):
    lo = _lane_lo(x.shape)
    r = pltpu.roll(x, HEAD_DIM, 1)
    return jnp.where(lo, x, r), jnp.where(lo, r, x)


def _heads_with_ones(x):
    return _with_ones(x), _with_ones(pltpu.roll(x, HEAD_DIM, 1))


def _attn_lat_kernel(sink_ref, wk_ref, wuv_ref,
                     sq_ref, sk_ref, sv_ref, mq_ref, mk_ref, mv_ref, gq_ref, gk_ref, gv_ref,
                     csk_ref, csv_ref, cckv_ref, ckr_ref, cgk_ref, cgv_ref,
                     ob_ref, oc_ref, od_ref,
                     cs_k, cs_v, cm_k, cm_v, cg_k, cg_v):
    tq = TQ_LAT
    qi = pl.program_id(1)

    @pl.when(qi == 0)
    def _():
        for src, dst in ((csk_ref, cs_k), (cgk_ref, cg_k)):
            d0, d1 = _dup_heads(src[0, 0])
            dst[0] = d0.astype(BF16)
            dst[1] = d1.astype(BF16)
        for src, dst in ((csv_ref, cs_v), (cgv_ref, cg_v)):
            d0, d1 = _heads_with_ones(src[0, 0])
            dst[0] = d0.astype(BF16)
            dst[1] = d1.astype(BF16)
        ckv16 = cckv_ref[0, 0].astype(BF16)
        k = _dot(ckv16, wk_ref[0:MLA_KV_RANK, :]) + _dot(ckr_ref[0, 0].astype(BF16),
                                                           wk_ref[MLA_KV_RANK:MLA_KV_RANK + MLA_ROPE, :])
        cm_k[...] = k.astype(BF16)
        cm_v[...] = _with_ones(_dot(ckv16, wuv_ref[...])).astype(BF16)

    q0 = qi * tq
    win = tq + 2 * SWA_WINDOW
    start = pl.multiple_of(jnp.clip(q0 - SWA_WINDOW, 0, DEC_SEQ - win), SWA_WINDOW)
    rows = lax.broadcasted_iota(jnp.int32, (4 * tq, win), 0)
    qpos = q0 + (rows & (tq - 1))
    kpos = start + lax.broadcasted_iota(jnp.int32, (4 * tq, win), 1)
    band = jnp.abs(qpos - kpos) <= SWA_WINDOW
    units = []
    for g in range(N_KV):
        def swa_segs(q4, g=g):
            ks = sk_ref[pl.ds(start, win), 128 * g:128 * (g + 1)]
            vs = sv_ref[pl.ds(start, win), 128 * g:128 * (g + 1)]
            s_own = jnp.where(band, _dot_nt(q4, ks), MASK_VALUE)
            return [(_dot_nt(q4, cs_k[g]), cs_v[g]), (s_own, vs)]

        def gqa_segs(q4, g=g):
            return [(_dot_nt(q4, cg_k[g]), cg_v[g]),
                    (_dot_nt(q4, gk_ref[:, 128 * g:128 * (g + 1)]), gv_ref[:, 128 * g:128 * (g + 1)])]

        units += [_gqa_unit(sq_ref, g, tq, swa_segs, ob_ref, sink_ref), _gqa_unit(gq_ref, g, tq, gqa_segs, od_ref)]

    def mla_segs(h, q_h):
        cols = slice(128 * h, 128 * (h + 1))
        return [(_dot_nt(q_h, cm_k[:, cols]), cm_v[:, cols]), (_dot_nt(q_h, mk_ref[:, cols]), mv_ref[:, cols])]

    units += [_mla_pair_unit(mq_ref, hp, tq, mla_segs, oc_ref) for hp in range(MLA_HEADS // 2)]
    _run_units(units)


def _attn_lat(lp, p, caches, layer):
    nq = DEC_SEQ // TQ_LAT

    def qspec(w):
        return pl.BlockSpec((TQ_LAT, w), lambda b, i: (b * nq + i, 0))

    def kspec(w):
        return pl.BlockSpec((DEC_SEQ, w), lambda b, i: (b, 0))

    def cspec(w):
        return pl.BlockSpec((1, 1, PAST_LEN, w), lambda b, i: (b, layer, 0, 0))

    is_q = dict(sq=True, mq=True, gq=True)
    in_specs = [_layer_spec((1, N_HEADS), layer), _layer_spec((256, 1024), layer),
                _layer_spec((MLA_KV_RANK, 1024), layer)]
    in_specs += [(qspec if is_q.get(n) else kspec)(_ATTN_WIDTH[n]) for n in _ATTN_IN]
    in_specs += [cspec(128), cspec(128), cspec(128), cspec(MLA_ROPE), cspec(128), cspec(128)]
    return pl.pallas_call(
        _attn_lat_kernel,
        out_shape=[jax.ShapeDtypeStruct((N_LAT, 512), BF16)] * 3,
        grid=(DEC_BATCH, nq),
        in_specs=in_specs,
        out_specs=[qspec(512)] * 3,
        scratch_shapes=[pltpu.VMEM((N_KV, PAST_LEN, LANES), BF16), pltpu.VMEM((N_KV, PAST_LEN, LANES), BF16),
                        pltpu.VMEM((PAST_LEN, 1024), BF16), pltpu.VMEM((PAST_LEN, 1024), BF16),
                        pltpu.VMEM((N_KV, PAST_LEN, LANES), BF16), pltpu.VMEM((N_KV, PAST_LEN, LANES), BF16)],
        compiler_params=_params(2),
        name="attn_lat",
    )(lp["sink"], lp["wk"], lp["wuv"], *[p[n] for n in _ATTN_IN], *caches)


def _merge_kernel(x_ref, mod_ref, oa_ref, ob_ref, oc_ref, od_ref, gates_ref, wb_ref, wo_ref, npost_ref,
                  nmpre_ref, nmpost_ref, w1_ref, w2_ref, y_ref):
    x = x_ref[...]
    g1 = mod_ref[0, 2]
    sh2 = mod_ref[0, 3]
    sc2 = mod_ref[0, 4]
    g2 = mod_ref[0, 5]
    acc = None
    for n, o_ref in enumerate((oa_ref, ob_ref, oc_ref, od_ref)):
        br = _dot(o_ref[...], wb_ref[n]) * gates_ref[:, D_MODEL * n:D_MODEL * (n + 1)].astype(F32)
        acc = br if acc is None else acc + br
    y = _dot(acc.astype(BF16), wo_ref[...])
    x = x + g1 * _rms(y, npost_ref[...])
    h = (_rms(x, nmpre_ref[...]) * (1.0 + sc2) + sh2).astype(BF16)
    a = jnp.maximum(_dot(h, w1_ref[...]), 0.0)
    y = _dot((a * a).astype(BF16), w2_ref[...])
    y_ref[...] = x + g2 * _rms(y, nmpost_ref[...])


_MERGE_PARAMS = dict(w_branch=(N_BRANCH, BRANCH_W, D_MODEL), w_out=(D_MODEL, D_MODEL), npost=(1, D_MODEL),
                     nmpre=(1, D_MODEL), nmpost=(1, D_MODEL), w_mlp_in=(D_MODEL, D_FF), w_mlp_out=(D_FF, D_MODEL))


def _merge(x, mod4, o_a, o_b, o_c, o_d, gates, lp, is_ctx, layer):
    n_tok = x.shape[0]

    def tok(w):
        return pl.BlockSpec((TM, w), lambda i: (i, 0))

    in_specs = [tok(D_MODEL), _mod_spec(is_ctx, layer), tok(512), tok(512), tok(512), tok(512), tok(4096)]
    in_specs += [_layer_spec(shape, layer) for shape in _MERGE_PARAMS.values()]
    return pl.pallas_call(
        _merge_kernel, out_shape=jax.ShapeDtypeStruct((n_tok, D_MODEL), F32), grid=(n_tok // TM,),
        in_specs=in_specs, out_specs=tok(D_MODEL), compiler_params=_params(1),
        name="merge_ctx" if is_ctx else "merge_lat",
    )(x, mod4, o_a, o_b, o_c, o_d, gates, *[lp[n] for n in _MERGE_PARAMS])


def _rope_tables():
    t = np.arange(DEC_SEQ)
    row, col = (t // GRID_W).astype(np.float32), (t % GRID_W).astype(np.float32)

    def pattern(d):
        nf = d // 4
        inv = (ROPE_BASE ** (-np.arange(nf, dtype=np.float32) / nf)).astype(np.float32)
        ang = np.concatenate([row[:, None] * inv, row[:, None] * inv, col[:, None] * inv, col[:, None] * inv], axis=1)
        cos, sin = np.cos(ang), np.sin(ang)
        second = (np.arange(d) % (2 * nf)) >= nf
        return cos, np.where(second, sin, 0.0), np.where(second, 0.0, -sin)

    out = np.zeros((9, DEC_SEQ, LANES), np.float32)
    out[[0, 3, 6]] = 1.0
    c, s1, s2 = pattern(HEAD_DIM)
    for a, tab in enumerate((c, s1, s2)):
        out[_TAB_64 + a] = np.tile(tab, (1, LANES // HEAD_DIM))
    c, s1, s2 = pattern(MLA_ROPE)
    for a, tab in enumerate((c, s1, s2)):
        out[_TAB_MLAQ + a][:, MLA_NOPE:MLA_NOPE + MLA_ROPE] = tab
        out[_TAB_KR + a][:, :MLA_ROPE] = tab
    return out


_ROPE_TABLES = _rope_tables()


def _block_mean_matrix(n):
    i = np.arange(n)
    return ((i[:, None] // HEAD_DIM) == (i[None, :] // HEAD_DIM)).astype(np.float32) / HEAD_DIM


def _mla_weight_layouts(w_uq, w_ukv):
    dq = MLA_NOPE + MLA_ROPE
    wq = w_uq.reshape(DEPTH, MLA_Q_RANK, MLA_HEADS, dq)
    wq = jnp.pad(wq, ((0, 0), (0, 0), (0, 0), (0, LANES - dq))).reshape(DEPTH, MLA_Q_RANK, MLA_HEADS * LANES)
    wkv = w_ukv.reshape(DEPTH, MLA_KV_RANK, MLA_HEADS, MLA_NOPE + MLA_V)
    wk = jnp.pad(wkv[..., :MLA_NOPE], ((0, 0), (0, 0), (0, 0), (0, LANES - MLA_NOPE)))
    wk = wk.reshape(DEPTH, MLA_KV_RANK, MLA_HEADS * LANES)
    place = np.zeros((LANES, MLA_HEADS, LANES), np.float32)
    for j in range(MLA_ROPE):
        place[j, :, MLA_NOPE + j] = 1.0
    place = jnp.broadcast_to(jnp.asarray(place.reshape(1, LANES, MLA_HEADS * LANES)), (DEPTH, LANES, MLA_HEADS * LANES))
    w_k = jnp.concatenate([wk, place], axis=1)
    w_uv = jnp.pad(wkv[..., MLA_NOPE:], ((0, 0), (0, 0), (0, 0), (0, LANES - MLA_V)))
    w_uv = w_uv.reshape(DEPTH, MLA_KV_RANK, MLA_HEADS * LANES)
    return wq.astype(BF16), w_k.astype(BF16), w_uv.astype(BF16)


def _lower_bounds(p):
    s = jax.nn.softmax(p.astype(F32), axis=0)
    return jnp.cumsum(s, axis=0) - s[0]


def kernel(x_prompt, x_sample, state_hgrn, cache_swa_k, cache_swa_v, cache_mla_ckv, cache_mla_kr, cache_gqa_k, cache_gqa_v, c, c_ctx, w_ada, b_ada, norm_mix_pre, norm_mix_post, norm_mlp_pre, norm_mlp_post, w_in, hgrn_lb_fwd, hgrn_lb_bwd, hgrn_norm, swa_sink, mla_q_norm, mla_kv_norm, mla_w_uq, mla_w_ukv, gqa_q_norm, gqa_k_norm, w_branch, w_out, w_mlp_in, w_mlp_out):
    tabs = jnp.asarray(_ROPE_TABLES)
    b512 = jnp.asarray(_block_mean_matrix(512)).astype(BF16)

    def rows(a):
        return a.reshape(DEPTH, 1, -1)

    w_uq, w_k, w_uv = _mla_weight_layouts(mla_w_uq, mla_w_ukv)
    lp = dict(
        npre=rows(norm_mix_pre), w=_pad_w_in(w_in), mlaq_g=rows(mla_q_norm), mlakv_g=rows(mla_kv_norm),
        wuq=w_uq, wk=w_k, wuv=w_uv, gq_g=rows(jnp.tile(gqa_q_norm, (1, N_HEADS))),
        gk_g=rows(jnp.tile(gqa_k_norm, (1, N_KV))), b512=b512, b128=b512[:128, :128],
        lbf=rows(_lower_bounds(hgrn_lb_fwd)), lbb=rows(_lower_bounds(hgrn_lb_bwd)), gn=rows(hgrn_norm),
        sink=rows(swa_sink),
        w_branch=w_branch.astype(BF16), w_out=w_out.astype(BF16), npost=rows(norm_mix_post),
        nmpre=rows(norm_mlp_pre), nmpost=rows(norm_mlp_post),
        w_mlp_in=w_mlp_in.astype(BF16), w_mlp_out=w_mlp_out.astype(BF16))

    cond8 = jnp.zeros((SUBLANES, D_MODEL), F32).at[0].set(c_ctx).at[1:1 + DEC_BATCH].set(c)
    mod4 = _ada(cond8, w_ada, b_ada).reshape(DEPTH, SUBLANES, N_MOD, 1, D_MODEL)

    caches_in = (cache_swa_k.reshape(DEC_BATCH, DEPTH, PAST_LEN, 128),
                 cache_swa_v.reshape(DEC_BATCH, DEPTH, PAST_LEN, 128), cache_mla_ckv, cache_mla_kr,
                 cache_gqa_k.reshape(DEC_BATCH, DEPTH, PAST_LEN, 128),
                 cache_gqa_v.reshape(DEC_BATCH, DEPTH, PAST_LEN, 128))

    x_ctx = x_prompt.reshape(N_CTX, D_MODEL)
    x_lat = x_sample.reshape(N_LAT, D_MODEL)
    pc = None
    for l in range(DEPTH):
        pc = _proj(x_ctx, mod4, lp, True, layer=l, prev=pc)
        ob, oc, od = _attn_ctx(lp, pc, l)
        x_ctx = _merge(x_ctx, mod4, pc["oa"], ob, oc, od, pc["gates"], lp, True, l)

        pq = _proj(x_lat, mod4, lp, False, layer=l, tabs=tabs)
        oa = _hgrn_lat(pq, lp, l, state_hgrn)
        ob, oc, od = _attn_lat(lp, pq, caches_in, l)
        x_lat = _merge(x_lat, mod4, oa, ob, oc, od, pq["gates"], lp, False, l)

    kv_shape = (BATCH, DEPTH, SEQ, N_KV, HEAD_DIM)
    return (x_ctx.reshape(BATCH, SEQ, D_MODEL), x_lat.reshape(DEC_BATCH, DEC_SEQ, D_MODEL), pc["st"],
            pc["skc"].reshape(kv_shape), pc["svc"].reshape(kv_shape), pc["mckv"], pc["mkr"],
            pc["gkc"].reshape(kv_shape), pc["gvc"].reshape(kv_shape))
```

```python
import functools

import jax
import jax.numpy as jnp
import numpy as np
from jax import lax
from jax.experimental import pallas as pl
from jax.experimental.pallas import tpu as pltpu

D_MODEL = 1024
BATCH, SEQ = 32, 256
DEC_BATCH, DEC_SEQ = 2, 2048
DEPTH = 2
PAST_LEN = 512
GRID_W = 64
ROPE_BASE = 10000.0
NORM_EPS = 1e-6
MASK_VALUE = -1e30
LOG_FLOOR = 1e-30
N_MOD = 6
N_BRANCH = 4
BRANCH_W = 512
D_FF = 4 * D_MODEL
HGRN_HEADS, HGRN_DK = 4, 128
SWA_WINDOW = 128
MLA_HEADS, MLA_Q_RANK, MLA_KV_RANK, MLA_NOPE, MLA_ROPE, MLA_V = 8, 256, 128, 64, 32, 64
HEAD_DIM = 64
N_HEADS, N_KV = 8, 2

N_CTX = BATCH * SEQ
N_LAT = DEC_BATCH * DEC_SEQ

LANES = 128
SUBLANES = 8
VMEM_LIMIT_BYTES = 56 * 1024 * 1024

TM = 256
LAT_TILES_PER_SEQ = DEC_SEQ // TM
HGRN_CHUNK = 256
HGRN_UNROLL = 4
TQ_LAT = 128

BF16 = jnp.bfloat16
F32 = jnp.float32

_W_A_COLS = 3712
_W_B_COLS = 8608 - _W_A_COLS - MLA_ROPE
_C_HGRN = ("wa", 0)
_C_SWA_Q = ("wa", 2560)
_C_SWA_KV = ("wa", 3072)
_C_MLA_CQ = ("wa", 3328)
_C_MLA_CKV = ("wa", 3584)
_C_MLA_KR = ("wkr", 0)
_C_GQA_Q = ("wb", 0)
_C_GQA_KV = ("wb", 512)
_C_GATES = ("wb", 768)


def _split_w_in(w):
    cut = _W_A_COLS + MLA_ROPE
    wkr = jnp.pad(w[..., _W_A_COLS:cut], ((0, 0), (0, 0), (0, LANES - MLA_ROPE)))
    return w[..., :_W_A_COLS].astype(BF16), wkr.astype(BF16), w[..., cut:].astype(BF16)


def _sigmoid_pair(x):
    a = jnp.exp(-jnp.abs(x))
    r = 1.0 / (1.0 + a)
    ar = a * r
    pos = x >= 0
    return jnp.where(pos, r, ar), jnp.where(pos, ar, r)


def _sigmoid(x):
    return _sigmoid_pair(x)[0]


def _silu(x):
    return x * _sigmoid(x)


def _rms(x, gain):
    return x * lax.rsqrt(jnp.mean(x * x, axis=-1, keepdims=True) + NORM_EPS) * gain


def _dot(a, b):
    return jnp.dot(a, b, preferred_element_type=F32)


def _dot_nt(a, b):
    return lax.dot_general(a, b, (((1,), (1,)), ((), ())), preferred_element_type=F32)


def _dot_tn(a, b):
    return lax.dot_general(a, b, (((0,), (0,)), ((), ())), preferred_element_type=F32)


def _tile_lanes(t, width):
    reps = width // LANES
    return t if reps == 1 else jnp.concatenate([t] * reps, axis=1)


def _rope(x, tab_ref, base, shift):
    w = x.shape[1]
    c = _tile_lanes(tab_ref[base], w)
    s1 = _tile_lanes(tab_ref[base + 1], w)
    s2 = _tile_lanes(tab_ref[base + 2], w)
    return x * c + pltpu.roll(x, shift, 1) * s1 + pltpu.roll(x, w - shift, 1) * s2


def _head_rms(x, bmat_ref, gain):
    ms = _dot((x * x).astype(BF16), bmat_ref[...])
    return x * lax.rsqrt(ms + NORM_EPS) * gain


def _const_spec(shape):
    nd = len(shape)
    return pl.BlockSpec(shape, lambda *_: (0,) * nd, pipeline_mode=pl.Buffered(1))


def _layer_spec(shape, layer):
    nd = len(shape)
    return pl.BlockSpec((None,) + tuple(shape), lambda *_: (layer,) + (0,) * nd, pipeline_mode=pl.Buffered(1))


_ANY_SPEC = pl.BlockSpec(memory_space=pl.ANY)


def _params(n_grid_dims):
    return pltpu.CompilerParams(dimension_semantics=("arbitrary",) * n_grid_dims, vmem_limit_bytes=VMEM_LIMIT_BYTES)


_ADA_TN = 2048


def _ada_kernel(c_ref, w_ref, b_ref, o_ref):
    s = _silu(c_ref[...]).astype(BF16)
    o_ref[0] = _dot(s, w_ref[0].astype(BF16)) + b_ref[0]


def _ada(cond8, w_ada, b_ada):
    n = N_MOD * D_MODEL
    return pl.pallas_call(
        _ada_kernel,
        out_shape=jax.ShapeDtypeStruct((DEPTH, SUBLANES, n), F32),
        grid=(DEPTH, n // _ADA_TN),
        in_specs=[
            pl.BlockSpec((SUBLANES, D_MODEL), lambda l, j: (0, 0)),
            pl.BlockSpec((1, D_MODEL, _ADA_TN), lambda l, j: (l, 0, j)),
            pl.BlockSpec((1, 1, _ADA_TN), lambda l, j: (l, 0, j)),
        ],
        out_specs=pl.BlockSpec((1, SUBLANES, _ADA_TN), lambda l, j: (l, 0, j)),
        compiler_params=_params(2),
        name="ada",
    )(cond8, w_ada, b_ada.reshape(DEPTH, 1, n))


def _mod_spec(is_ctx, layer, tile=TM):
    if is_ctx:
        return pl.BlockSpec((None, 1, N_MOD, 1, D_MODEL), lambda i: (layer, 0, 0, 0, 0))
    per_seq = DEC_SEQ // tile
    return pl.BlockSpec((None, 1, N_MOD, 1, D_MODEL), lambda i: (layer, 1 + i // per_seq, 0, 0, 0))


LOG2E = float(np.log2(np.e))
_SCALE_64 = HEAD_DIM ** -0.5 * LOG2E
_SCALE_MLA = (MLA_NOPE + MLA_ROPE) ** -0.5 * LOG2E
_TAB_64, _TAB_MLAQ, _TAB_KR = 0, 3, 6

_HGRN_PRE = ("hq", "hff", "hfb", "hi", "hg")
_ATTN_OUTS = (
    ("sq", 512, BF16), ("sk", 256, BF16), ("sv", 256, BF16),
    ("mq", 1024, BF16), ("mk", 1024, BF16), ("mv", 1024, BF16),
    ("gq", 512, BF16), ("gk", 256, BF16), ("gv", 256, BF16), ("gates", 4096, BF16))
_CACHE_OUTS = (("skc", 128), ("svc", 128), ("mckv", 128), ("mkr", MLA_ROPE), ("gkc", 128), ("gvc", 128))
_PROJ_PARAMS = dict(npre=(1, D_MODEL), wa=(D_MODEL, _W_A_COLS), wkr=(D_MODEL, LANES), wb=(D_MODEL, _W_B_COLS),
                    mlaq_g=(1, MLA_Q_RANK), mlakv_g=(1, MLA_KV_RANK),
                    wuq=(MLA_Q_RANK, 1024), wk=(256, 1024), wuv=(MLA_KV_RANK, 1024), gq_g=(1, 512), gk_g=(1, 128))
_HGRN_PARAMS = dict(lbf=(1, 512), lbb=(1, 512), gn=(1, 512))
_PROJ_CONSTS = tuple(_PROJ_PARAMS) + ("b512", "b128")
_HGRN_CONSTS = tuple(_HGRN_PARAMS)


def _proj_out_names(is_ctx):
    attn = [n for n, _, _ in _ATTN_OUTS]
    if is_ctx:
        return ["oa"] + attn + [n for n, _ in _CACHE_OUTS] + ["st"]
    return list(_HGRN_PRE) + attn


def _proj_kernel(*refs, is_ctx, n_alias):
    names = ["x", "mod"] + list(_PROJ_CONSTS) + (list(_HGRN_CONSTS) if is_ctx else ["tab"])
    r = dict(zip(names, refs))
    pos = len(names) + n_alias
    for name in _proj_out_names(is_ctx) + (["hbuf"] if is_ctx else []):
        r[name] = refs[pos]
        pos += 1

    x = r["x"][...]
    sh1 = r["mod"][0, 0]
    sc1 = r["mod"][0, 1]
    h = (_rms(x, r["npre"][...]) * (1.0 + sc1) + sh1).astype(BF16)
    def seg(where, w, skip=0):
        name, a = where
        return _dot(h, r[name][:, a + skip:a + skip + w])

    def rope(v, base, shift):
        return v if is_ctx else _rope(v, r["tab"], base, shift)

    for j, name in enumerate(_HGRN_PRE):
        u = seg(_C_HGRN, 512, skip=512 * j)
        if is_ctx:
            r["hbuf"][j] = u
        else:
            r[name][...] = u

    def hgrn_all(between):
        C = HGRN_CHUNK
        n = TM // C
        lvls = _hgrn_levels()
        heads = range(HGRN_HEADS)
        cols = [slice(LANES * hd, LANES * (hd + 1)) for hd in heads]
        st = [[None, None] for _ in heads]
        outs = [[[None] * n, [None] * n] for _ in heads]
        for c in range(n):
            for hd in heads:
                for d, cc, lb in ((0, c, r["lbf"]), (1, n - 1 - c, r["lbb"])):
                    rows = slice(C * cc, C * (cc + 1))
                    k, lf2 = _hgrn_forget(r["hbuf"][1 + d, rows, cols[hd]], lb[:, cols[hd]])
                    outs[hd][d][cc], st[hd][d] = _hgrn_chunk(
                        _silu(r["hbuf"][0, rows, cols[hd]]), k, lf2, r["hbuf"][3, rows, cols[hd]], st[hd][d],
                        d == 0, lvls[d])
            between(c)
        for hd in heads:
            for c in range(n):
                rows = slice(C * c, C * (c + 1))
                r["oa"][rows, cols[hd]] = _hgrn_finish(outs[hd][0][c] + outs[hd][1][c], r["gn"][:, cols[hd]],
                                                       r["hbuf"][4, rows, cols[hd]])
            for d in range(2):
                r["st"][0, 0, d, hd] = st[hd][d].T

    def dup16(k):
        return jnp.concatenate([d.astype(BF16) for d in _dup_heads(k)], axis=1)

    def ones16(v):
        return jnp.concatenate([d.astype(BF16) for d in _heads_with_ones(v)], axis=1)

    def swa():
        r["sq"][...] = (rope(seg(_C_SWA_Q, 512), _TAB_64, 16) * _SCALE_64).astype(BF16)
        kv = seg(_C_SWA_KV, 256)
        k, v = kv[:, :128], kv[:, 128:]
        r["sk"][...] = dup16(rope(k, _TAB_64, 16))
        r["sv"][...] = ones16(v)
        if is_ctx:
            r["skc"][0, 0] = k
            r["svc"][0, 0] = v

    def mla():
        cq = _rms(seg(_C_MLA_CQ, 256), r["mlaq_g"][...]).astype(BF16)
        q = _dot(cq, r["wuq"][...])
        r["mq"][...] = (rope(q, _TAB_MLAQ, 8) * _SCALE_MLA).astype(BF16)
        ckv = _rms(seg(_C_MLA_CKV, 128), r["mlakv_g"][...])
        kr = seg(_C_MLA_KR, 128)
        if is_ctx:
            r["mckv"][0, 0] = ckv
            r["mkr"][0, 0] = kr[:, :MLA_ROPE]
        kr_used = rope(kr, _TAB_KR, 8)
        ckv16 = ckv.astype(BF16)
        r["mk"][...] = _dot(jnp.concatenate([ckv16, kr_used.astype(BF16)], axis=1), r["wk"][...]).astype(BF16)
        r["mv"][...] = _with_ones(_dot(ckv16, r["wuv"][...])).astype(BF16)

    def gqa():
        gq = _head_rms(seg(_C_GQA_Q, 512), r["b512"], r["gq_g"][...])
        r["gq"][...] = (rope(gq, _TAB_64, 16) * _SCALE_64).astype(BF16)
        kv = seg(_C_GQA_KV, 256)
        k, v = _head_rms(kv[:, :128], r["b128"], r["gk_g"][...]), kv[:, 128:]
        r["gk"][...] = dup16(rope(k, _TAB_64, 16))
        r["gv"][...] = ones16(v)
        if is_ctx:
            r["gkc"][0, 0] = k
            r["gvc"][0, 0] = v

    def gates(j):
        r["gates"][:, D_MODEL * j:D_MODEL * (j + 1)] = _sigmoid(seg(_C_GATES, D_MODEL, skip=D_MODEL * j)).astype(BF16)

    tasks = [swa, mla, gqa] + [functools.partial(gates, j) for j in range(N_BRANCH)]
    if is_ctx:
        per_step = -(-len(tasks) // (TM // HGRN_CHUNK))

        def between(c):
            for t in tasks[per_step * c:per_step * (c + 1)]:
                t()

        hgrn_all(between)
    else:
        for t in tasks:
            t()


def _proj(x, mod4, lp, is_ctx, layer=0, tabs=None, prev=None):
    n_tok = x.shape[0]
    params = dict(_PROJ_PARAMS, **(_HGRN_PARAMS if is_ctx else {}))
    in_specs = [pl.BlockSpec((TM, D_MODEL), lambda i: (i, 0)), _mod_spec(is_ctx, layer)]
    in_specs += [_layer_spec(params[n], layer) for n in _PROJ_PARAMS] + [_const_spec((512, 512)), _const_spec((128, 128))]
    args = [x, mod4] + [lp[n] for n in _PROJ_PARAMS] + [lp["b512"], lp["b128"]]
    if is_ctx:
        in_specs += [_layer_spec(params[n], layer) for n in _HGRN_PARAMS]
        args += [lp[n] for n in _HGRN_PARAMS]
    if not is_ctx:
        in_specs.append(pl.BlockSpec((9, TM, LANES), lambda i: (0, i % LAT_TILES_PER_SEQ, 0)))
        args.append(tabs)

    def tok(w):
        return pl.BlockSpec((TM, w), lambda i: (i, 0))

    specs = {n: (tok(w), jax.ShapeDtypeStruct((n_tok, w), dt)) for n, w, dt in _ATTN_OUTS}
    if is_ctx:
        specs["oa"] = (tok(512), jax.ShapeDtypeStruct((n_tok, 512), BF16))
        for n, w in _CACHE_OUTS:
            specs[n] = (pl.BlockSpec((1, 1, SEQ, w), lambda i: (i, layer, 0, 0)),
                        jax.ShapeDtypeStruct((BATCH, DEPTH, SEQ, w), F32))
        specs["st"] = (pl.BlockSpec((1, 1, 2, HGRN_HEADS, LANES, LANES), lambda i: (i, layer, 0, 0, 0, 0)),
                       jax.ShapeDtypeStruct((BATCH, DEPTH, 2, HGRN_HEADS, LANES, LANES), F32))
    else:
        for n in _HGRN_PRE:
            specs[n] = (tok(512), jax.ShapeDtypeStruct((n_tok, 512), F32))
    out_names = _proj_out_names(is_ctx)
    aliases = {}
    n_alias = 0
    if prev is not None:
        for n in [n for n, _ in _CACHE_OUTS] + ["st"]:
            aliases[len(args)] = out_names.index(n)
            in_specs.append(_ANY_SPEC)
            args.append(prev[n])
            n_alias += 1
    outs = pl.pallas_call(
        functools.partial(_proj_kernel, is_ctx=is_ctx, n_alias=n_alias),
        out_shape=[specs[n][1] for n in out_names], grid=(n_tok // TM,), in_specs=in_specs,
        out_specs=[specs[n][0] for n in out_names],
        scratch_shapes=[pltpu.VMEM((len(_HGRN_PRE), TM, 512), F32)] if is_ctx else [],
        input_output_aliases=aliases, compiler_params=_params(1),
        name="proj_ctx" if is_ctx else "proj_lat",
    )(*args)
    return dict(zip(out_names, outs))


def _pair_levels():
    C = HGRN_CHUNK
    ti = lax.broadcasted_iota(jnp.int32, (C, C), 0)
    si = lax.broadcasted_iota(jnp.int32, (C, C), 1)
    x = ti ^ si
    lvl = jnp.where(x == 0, -1, 0)
    b = 2
    while b < C:
        lvl = lvl + (x >= b).astype(jnp.int32)
        b *= 2
    return ti, si, lvl


def _hgrn_forget(f_pre, lb):
    s_pos, s_neg = _sigmoid_pair(f_pre)
    f = lb + (1.0 - lb) * s_pos
    return (1.0 - lb) * s_neg, jnp.log2(jnp.maximum(f, LOG_FLOOR))


def _hgrn_levels():
    ti, si, lvl = _pair_levels()
    return jnp.where(ti > si, lvl, -1), jnp.where(ti < si, lvl, -1)


def _hgrn_chunk(q, k, lf2, v, st, fwd, lvl):
    C = HGRN_CHUNK
    nv = C // SUBLANES
    sub = lax.broadcasted_iota(jnp.int32, (nv, SUBLANES, LANES), 1)

    p = lf2.reshape(nv, SUBLANES, LANES)
    for s in (1, 2, 4):
        if fwd:
            p = p + jnp.where(sub >= s, pltpu.roll(p, s, 1), 0.0)
        else:
            p = p + jnp.where(sub < SUBLANES - s, pltpu.roll(p, SUBLANES - s, 1), 0.0)

    a_mat = jnp.zeros((C, C), F32)

    def add_level(a_mat, e, level):
        a_m = _dot_nt((q * e).astype(BF16), (k * e).astype(BF16))
        return jnp.where(lvl == level, a_m, a_mat)

    def row(r):
        return jnp.broadcast_to(p[:, r:r + 1, :], p.shape)

    odd = (sub & 1) == 1
    if fwd:
        bnd = (jnp.where(odd, pltpu.roll(p, 1, 1), p), jnp.where(sub < 4, row(1), row(5)), row(3))
    else:
        bnd = (jnp.where(odd, p, pltpu.roll(p, SUBLANES - 1, 1)), jnp.where(sub < 4, row(2), row(6)), row(4))
    for level, b in enumerate(bnd):
        a_mat = add_level(a_mat, jnp.exp2(-jnp.abs(p - b)).reshape(C, LANES), level)

    p = p.reshape(C, LANES)
    m = SUBLANES
    while m < C:
        nb = C // (2 * m)
        p4 = p.reshape(nb, 2, m, LANES)
        lo, hi = p4[:, 0], p4[:, 1]
        if fwd:
            tot = lo[:, m - 1:m]
            e_lo, e_hi = tot - lo, hi
            p_lo, p_hi = lo, hi + tot
        else:
            tot = hi[:, 0:1]
            e_lo, e_hi = lo, tot - hi
            p_lo, p_hi = lo + tot, hi
        e = jnp.exp2(jnp.concatenate([e_lo[:, None], e_hi[:, None]], axis=1).reshape(C, LANES))
        p = jnp.concatenate([p_lo[:, None], p_hi[:, None]], axis=1).reshape(C, LANES)
        a_mat = add_level(a_mat, e, int(np.log2(m)))
        m *= 2
    v16 = v.astype(BF16)
    o = jnp.sum(q * k, axis=-1, keepdims=True) * v + _dot(a_mat.astype(BF16), v16)

    if st is not None:
        o = o + _dot_nt((q * jnp.exp2(p)).astype(BF16), st.astype(BF16))
    tot = p[C - 1:C] if fwd else p[0:1]
    k_end = (k * jnp.exp2(tot - p)).astype(BF16)
    st_new = _dot_tn(v16, k_end)
    if st is not None:
        st_new = st_new + st * jnp.exp2(tot)
    return o, st_new


def _hgrn_finish(o, gn, g_pre):
    return (_rms(o, gn) * _silu(g_pre)).astype(BF16)


def _hgrn_head_static(read, lbf, lbb, gn, n_chunks):
    lvls = _hgrn_levels()
    st = [None, None]
    outs = [[None] * n_chunks, [None] * n_chunks]
    for c in range(n_chunks):
        for d, cc, lb in ((0, c, lbf), (1, n_chunks - 1 - c, lbb)):
            k, lf2 = _hgrn_forget(read(1 + d, cc), lb)
            outs[d][cc], st[d] = _hgrn_chunk(_silu(read(0, cc)), k, lf2, read(3, cc), st[d], d == 0, lvls[d])
    return [_hgrn_finish(outs[0][c] + outs[1][c], gn, read(4, c)) for c in range(n_chunks)], st


def _hgrn_lat_kernel(lbf_ref, lbb_ref, gn_ref, q_ref, ff_ref, fb_ref, v_ref, g_ref, s0_ref, o_ref,
                     of_ref, ob_ref, st_ref, *, n_chunks):
    C = HGRN_CHUNK
    for d in range(2):
        st_ref[d] = s0_ref[0, 0, d, 0].T
    lbf = lbf_ref[...]
    lbb = lbb_ref[...]
    lvls = _hgrn_levels()

    def body(c, carry):
        rf = pl.multiple_of(c * C, C)
        rb = pl.multiple_of((n_chunks - 1 - c) * C, C)
        for d, r0, f_ref, lb, out_ref in ((0, rf, ff_ref, lbf, of_ref), (1, rb, fb_ref, lbb, ob_ref)):
            sl = pl.ds(r0, C)
            k, lf2 = _hgrn_forget(f_ref[sl, :], lb)
            out_ref[sl, :], st_ref[d] = _hgrn_chunk(_silu(q_ref[sl, :]), k, lf2, v_ref[sl, :], st_ref[d], d == 0,
                                                    lvls[d])
        return carry

    lax.fori_loop(0, n_chunks, body, 0, unroll=HGRN_UNROLL)
    o_ref[...] = _hgrn_finish(of_ref[...] + ob_ref[...], gn_ref[...], g_ref[...])


def _hgrn_lat(p, lp, layer, state_in):
    tok_spec = pl.BlockSpec((DEC_SEQ, LANES), lambda b, h: (b, h))
    vec_spec = pl.BlockSpec((None, 1, LANES), lambda b, h: (layer, 0, h))
    state_spec = pl.BlockSpec((1, 1, 2, 1, LANES, LANES), lambda b, h: (b, layer, 0, h, 0, 0))
    return pl.pallas_call(
        functools.partial(_hgrn_lat_kernel, n_chunks=DEC_SEQ // HGRN_CHUNK),
        out_shape=jax.ShapeDtypeStruct((N_LAT, 512), BF16), grid=(DEC_BATCH, HGRN_HEADS),
        in_specs=[vec_spec] * 3 + [tok_spec] * 5 + [state_spec], out_specs=tok_spec,
        scratch_shapes=[pltpu.VMEM((DEC_SEQ, LANES), F32), pltpu.VMEM((DEC_SEQ, LANES), F32),
                        pltpu.VMEM((2, LANES, LANES), F32)],
        compiler_params=_params(2), name="hgrn_lat",
    )(lp["lbf"], lp["lbb"], lp["gn"], p["hq"], p["hff"], p["hfb"], p["hi"], p["hg"], state_in)


def _lane_lo(shape):
    return (lax.broadcasted_iota(jnp.int32, shape, len(shape) - 1) & (LANES - 1)) < HEAD_DIM


def _with_ones(v):
    return jnp.where(_lane_lo(v.shape), v, 1.0)


def _softmax_unit(make_segs, sink_col, finish):
    segs = make_segs()
    mx = None
    for s, _ in segs:
        m = jnp.max(s, axis=-1, keepdims=True)
        mx = m if mx is None else jnp.maximum(mx, m)
    if sink_col is not None:
        mx = jnp.maximum(mx, sink_col)
    yield
    acc = None
    for s, v in segs:
        pv = _dot(jnp.exp2(s - mx).astype(BF16), v)
        acc = pv if acc is None else acc + pv
    den = jnp.where(_lane_lo(acc.shape), pltpu.roll(acc, HEAD_DIM, 1), acc)
    if sink_col is not None:
        den = den + jnp.exp2(sink_col - mx)
    finish(acc / den)


_DONE = object()


def _run_units(units):
    pending = list(units)
    active = []
    while pending or active:
        started = []
        if pending:
            u = pending.pop(0)
            next(u)
            started.append(u)
        active = started + [a for a in active if next(a, _DONE) is not _DONE]


def _pair(lo, even, odd):
    return jnp.where(lo, even, pltpu.roll(odd, HEAD_DIM, 1))


def _gqa_unit(q_ref, g, tq, segs_fn, o_ref, sink_ref=None):
    lo = _lane_lo((tq, LANES))
    parts = []
    for j in range(2):
        qp = q_ref[:, 256 * g + 128 * j:256 * g + 128 * (j + 1)]
        zero = jnp.zeros_like(qp)
        parts += [jnp.where(lo, qp, zero), jnp.where(lo, zero, qp)]
    q4 = jnp.concatenate(parts, axis=0)
    sink_col = None
    if sink_ref is not None:
        sink_col = jnp.concatenate(
            [jnp.broadcast_to(sink_ref[0:1, 4 * g + i:4 * g + i + 1] * LOG2E, (tq, 1)) for i in range(4)], axis=0)

    def finish(pv):
        o_ref[:, 256 * g:256 * g + 128] = _pair(lo, pv[0:tq], pv[tq:2 * tq]).astype(BF16)
        o_ref[:, 256 * g + 128:256 * (g + 1)] = _pair(lo, pv[2 * tq:3 * tq], pv[3 * tq:4 * tq]).astype(BF16)

    yield from _softmax_unit(lambda: segs_fn(q4), sink_col, finish)


def _mla_pair_unit(q_ref, hp, tq, segs_fn, o_ref):
    outs = []
    subs = [_softmax_unit(functools.partial(segs_fn, h, q_ref[:, 128 * h:128 * (h + 1)]), None, outs.append)
            for h in (2 * hp, 2 * hp + 1)]
    while True:
        if [next(s, _DONE) for s in subs][0] is _DONE:
            break
        yield
    o_ref[:, 128 * hp:128 * (hp + 1)] = _pair(_lane_lo((tq, LANES)), outs[0], outs[1]).astype(BF16)


_ATTN_IN = ("sq", "sk", "sv", "mq", "mk", "mv", "gq", "gk", "gv")
_ATTN_WIDTH = dict(sq=512, sk=256, sv=256, mq=1024, mk=1024, mv=1024, gq=512, gk=256, gv=256)


def _attn_ctx_kernel(sink_ref, sq_ref, sk_ref, sv_ref, mq_ref, mk_ref, mv_ref, gq_ref, gk_ref, gv_ref,
                     ob_ref, oc_ref, od_ref):
    tq = SEQ
    units = []
    for g in range(N_KV):
        def swa_segs(q4, g=g):
            return [(_dot_nt(q4, sk_ref[:, 128 * g:128 * (g + 1)]), sv_ref[:, 128 * g:128 * (g + 1)])]

        def gqa_segs(q4, g=g):
            return [(_dot_nt(q4, gk_ref[:, 128 * g:128 * (g + 1)]), gv_ref[:, 128 * g:128 * (g + 1)])]

        units += [_gqa_unit(sq_ref, g, tq, swa_segs, ob_ref, sink_ref), _gqa_unit(gq_ref, g, tq, gqa_segs, od_ref)]

    def mla_segs(h, q_h):
        return [(_dot_nt(q_h, mk_ref[:, 128 * h:128 * (h + 1)]), mv_ref[:, 128 * h:128 * (h + 1)])]

    units += [_mla_pair_unit(mq_ref, hp, tq, mla_segs, oc_ref) for hp in range(MLA_HEADS // 2)]
    _run_units(units)


def _attn_ctx(lp, p, layer):
    def spec(w):
        return pl.BlockSpec((SEQ, w), lambda b: (b, 0))

    return pl.pallas_call(
        _attn_ctx_kernel,
        out_shape=[jax.ShapeDtypeStruct((N_CTX, 512), BF16)] * 3,
        grid=(BATCH,),
        in_specs=[_layer_spec((1, N_HEADS), layer)] + [spec(_ATTN_WIDTH[n]) for n in _ATTN_IN],
        out_specs=[spec(512)] * 3,
        compiler_params=_params(1),
        name="attn_ctx",
    )(lp["sink"], *[p[n] for n in _ATTN_IN])


def _dup_heads(x):
    lo = _lane_lo(x.shape)
    r = pltpu.roll(x, HEAD_DIM, 1)
    return jnp.where(lo, x, r), jnp.where(lo, r, x)


def _heads_with_ones(x):
    return _with_ones(x), _with_ones(pltpu.roll(x, HEAD_DIM, 1))


def _attn_lat_kernel(sink_ref, wk_ref, wuv_ref,
                     sq_ref, sk_ref, sv_ref, mq_ref, mk_ref, mv_ref, gq_ref, gk_ref, gv_ref,
                     csk_ref, csv_ref, cckv_ref, ckr_ref, cgk_ref, cgv_ref,
                     ob_ref, oc_ref, od_ref,
                     cs_k, cs_v, cm_k, cm_v, cg_k, cg_v):
    tq = TQ_LAT
    qi = pl.program_id(1)

    @pl.when(qi == 0)
    def _():
        for src, dst in ((csk_ref, cs_k), (cgk_ref, cg_k)):
            d0, d1 = _dup_heads(src[0, 0])
            dst[0] = d0.astype(BF16)
            dst[1] = d1.astype(BF16)
        for src, dst in ((csv_ref, cs_v), (cgv_ref, cg_v)):
            d0, d1 = _heads_with_ones(src[0, 0])
            dst[0] = d0.astype(BF16)
            dst[1] = d1.astype(BF16)
        ckv16 = cckv_ref[0, 0].astype(BF16)
        k = _dot(ckv16, wk_ref[0:MLA_KV_RANK, :]) + _dot(ckr_ref[0, 0].astype(BF16),
                                                           wk_ref[MLA_KV_RANK:MLA_KV_RANK + MLA_ROPE, :])
        cm_k[...] = k.astype(BF16)
        cm_v[...] = _with_ones(_dot(ckv16, wuv_ref[...])).astype(BF16)

    q0 = qi * tq
    win = tq + 2 * SWA_WINDOW
    start = pl.multiple_of(jnp.clip(q0 - SWA_WINDOW, 0, DEC_SEQ - win), SWA_WINDOW)
    rows = lax.broadcasted_iota(jnp.int32, (4 * tq, win), 0)
    qpos = q0 + (rows & (tq - 1))
    kpos = start + lax.broadcasted_iota(jnp.int32, (4 * tq, win), 1)
    band = jnp.abs(qpos - kpos) <= SWA_WINDOW
    units = []
    for g in range(N_KV):
        def swa_segs(q4, g=g):
            ks = sk_ref[pl.ds(start, win), 128 * g:128 * (g + 1)]
            vs = sv_ref[pl.ds(start, win), 128 * g:128 * (g + 1)]
            s_own = jnp.where(band, _dot_nt(q4, ks), MASK_VALUE)
            return [(_dot_nt(q4, cs_k[g]), cs_v[g]), (s_own, vs)]

        def gqa_segs(q4, g=g):
            return [(_dot_nt(q4, cg_k[g]), cg_v[g]),
                    (_dot_nt(q4, gk_ref[:, 128 * g:128 * (g + 1)]), gv_ref[:, 128 * g:128 * (g + 1)])]

        units += [_gqa_unit(sq_ref, g, tq, swa_segs, ob_ref, sink_ref), _gqa_unit(gq_ref, g, tq, gqa_segs, od_ref)]

    def mla_segs(h, q_h):
        cols = slice(128 * h, 128 * (h + 1))
        return [(_dot_nt(q_h, cm_k[:, cols]), cm_v[:, cols]), (_dot_nt(q_h, mk_ref[:, cols]), mv_ref[:, cols])]

    units += [_mla_pair_unit(mq_ref, hp, tq, mla_segs, oc_ref) for hp in range(MLA_HEADS // 2)]
    _run_units(units)


def _attn_lat(lp, p, caches, layer):
    nq = DEC_SEQ // TQ_LAT

    def qspec(w):
        return pl.BlockSpec((TQ_LAT, w), lambda b, i: (b * nq + i, 0))

    def kspec(w):
        return pl.BlockSpec((DEC_SEQ, w), lambda b, i: (b, 0))

    def cspec(w):
        return pl.BlockSpec((1, 1, PAST_LEN, w), lambda b, i: (b, layer, 0, 0))

    is_q = dict(sq=True, mq=True, gq=True)
    in_specs = [_layer_spec((1, N_HEADS), layer), _layer_spec((256, 1024), layer),
                _layer_spec((MLA_KV_RANK, 1024), layer)]
    in_specs += [(qspec if is_q.get(n) else kspec)(_ATTN_WIDTH[n]) for n in _ATTN_IN]
    in_specs += [cspec(128), cspec(128), cspec(128), cspec(MLA_ROPE), cspec(128), cspec(128)]
    return pl.pallas_call(
        _attn_lat_kernel,
        out_shape=[jax.ShapeDtypeStruct((N_LAT, 512), BF16)] * 3,
        grid=(DEC_BATCH, nq),
        in_specs=in_specs,
        out_specs=[qspec(512)] * 3,
        scratch_shapes=[pltpu.VMEM((N_KV, PAST_LEN, LANES), BF16), pltpu.VMEM((N_KV, PAST_LEN, LANES), BF16),
                        pltpu.VMEM((PAST_LEN, 1024), BF16), pltpu.VMEM((PAST_LEN, 1024), BF16),
                        pltpu.VMEM((N_KV, PAST_LEN, LANES), BF16), pltpu.VMEM((N_KV, PAST_LEN, LANES), BF16)],
        compiler_params=_params(2),
        name="attn_lat",
    )(lp["sink"], lp["wk"], lp["wuv"], *[p[n] for n in _ATTN_IN], *caches)


MERGE_TILE = 512
MERGE_SUB = 256


def _merge_kernel(x_ref, mod_ref, oa_ref, ob_ref, oc_ref, od_ref, gates_ref, wb_ref, wo_ref, npost_ref,
                  nmpre_ref, nmpost_ref, w1_ref, w2_ref, y_ref):
    g1 = mod_ref[0, 2]
    sh2 = mod_ref[0, 3]
    sc2 = mod_ref[0, 4]
    g2 = mod_ref[0, 5]

    def sub_tile(r0):
        rows = slice(r0, r0 + MERGE_SUB)
        acc = None
        for n, o_ref in enumerate((oa_ref, ob_ref, oc_ref, od_ref)):
            br = _dot(o_ref[rows, :], wb_ref[n]) * gates_ref[rows, D_MODEL * n:D_MODEL * (n + 1)].astype(F32)
            acc = br if acc is None else acc + br
        y = _dot(acc.astype(BF16), wo_ref[...])
        yield
        x = x_ref[rows, :] + g1 * _rms(y, npost_ref[...])
        h = (_rms(x, nmpre_ref[...]) * (1.0 + sc2) + sh2).astype(BF16)
        a = jnp.maximum(_dot(h, w1_ref[...]), 0.0)
        yield
        y = _dot((a * a).astype(BF16), w2_ref[...])
        y_ref[rows, :] = x + g2 * _rms(y, nmpost_ref[...])

    _run_units([sub_tile(r0) for r0 in range(0, MERGE_TILE, MERGE_SUB)])


_MERGE_PARAMS = dict(w_branch=(N_BRANCH, BRANCH_W, D_MODEL), w_out=(D_MODEL, D_MODEL), npost=(1, D_MODEL),
                     nmpre=(1, D_MODEL), nmpost=(1, D_MODEL), w_mlp_in=(D_MODEL, D_FF), w_mlp_out=(D_FF, D_MODEL))


def _merge(x, mod4, o_a, o_b, o_c, o_d, gates, lp, is_ctx, layer):
    n_tok = x.shape[0]

    def tok(w):
        return pl.BlockSpec((MERGE_TILE, w), lambda i: (i, 0))

    in_specs = [tok(D_MODEL), _mod_spec(is_ctx, layer, MERGE_TILE), tok(512), tok(512), tok(512), tok(512), tok(4096)]
    in_specs += [_layer_spec(shape, layer) for shape in _MERGE_PARAMS.values()]
    return pl.pallas_call(
        _merge_kernel, out_shape=jax.ShapeDtypeStruct((n_tok, D_MODEL), F32), grid=(n_tok // MERGE_TILE,),
        in_specs=in_specs, out_specs=tok(D_MODEL), compiler_params=_params(1),
        name="merge_ctx" if is_ctx else "merge_lat",
    )(x, mod4, o_a, o_b, o_c, o_d, gates, *[lp[n] for n in _MERGE_PARAMS])


def _rope_tables():
    t = np.arange(DEC_SEQ)
    row, col = (t // GRID_W).astype(np.float32), (t % GRID_W).astype(np.float32)

    def pattern(d):
        nf = d // 4
        inv = (ROPE_BASE ** (-np.arange(nf, dtype=np.float32) / nf)).astype(np.float32)
        ang = np.concatenate([row[:, None] * inv, row[:, None] * inv, col[:, None] * inv, col[:, None] * inv], axis=1)
        cos, sin = np.cos(ang), np.sin(ang)
        second = (np.arange(d) % (2 * nf)) >= nf
        return cos, np.where(second, sin, 0.0), np.where(second, 0.0, -sin)

    out = np.zeros((9, DEC_SEQ, LANES), np.float32)
    out[[0, 3, 6]] = 1.0
    c, s1, s2 = pattern(HEAD_DIM)
    for a, tab in enumerate((c, s1, s2)):
        out[_TAB_64 + a] = np.tile(tab, (1, LANES // HEAD_DIM))
    c, s1, s2 = pattern(MLA_ROPE)
    for a, tab in enumerate((c, s1, s2)):
        out[_TAB_MLAQ + a][:, MLA_NOPE:MLA_NOPE + MLA_ROPE] = tab
        out[_TAB_KR + a][:, :MLA_ROPE] = tab
    return out


_ROPE_TABLES = _rope_tables()


def _block_mean_matrix(n):
    i = np.arange(n)
    return ((i[:, None] // HEAD_DIM) == (i[None, :] // HEAD_DIM)).astype(np.float32) / HEAD_DIM


def _mla_weight_layouts(w_uq, w_ukv):
    dq = MLA_NOPE + MLA_ROPE
    wq = w_uq.reshape(DEPTH, MLA_Q_RANK, MLA_HEADS, dq)
    wq = jnp.pad(wq, ((0, 0), (0, 0), (0, 0), (0, LANES - dq))).reshape(DEPTH, MLA_Q_RANK, MLA_HEADS * LANES)
    wkv = w_ukv.reshape(DEPTH, MLA_KV_RANK, MLA_HEADS, MLA_NOPE + MLA_V)
    wk = jnp.pad(wkv[..., :MLA_NOPE], ((0, 0), (0, 0), (0, 0), (0, LANES - MLA_NOPE)))
    wk = wk.reshape(DEPTH, MLA_KV_RANK, MLA_HEADS * LANES)
    place = np.zeros((LANES, MLA_HEADS, LANES), np.float32)
    for j in range(MLA_ROPE):
        place[j, :, MLA_NOPE + j] = 1.0
    place = jnp.broadcast_to(jnp.asarray(place.reshape(1, LANES, MLA_HEADS * LANES)), (DEPTH, LANES, MLA_HEADS * LANES))
    w_k = jnp.concatenate([wk, place], axis=1)
    w_uv = jnp.pad(wkv[..., MLA_NOPE:], ((0, 0), (0, 0), (0, 0), (0, LANES - MLA_V)))
    w_uv = w_uv.reshape(DEPTH, MLA_KV_RANK, MLA_HEADS * LANES)
    return wq.astype(BF16), w_k.astype(BF16), w_uv.astype(BF16)


def _lower_bounds(p):
    s = jax.nn.softmax(p.astype(F32), axis=0)
    return jnp.cumsum(s, axis=0) - s[0]


def kernel(x_prompt, x_sample, state_hgrn, cache_swa_k, cache_swa_v, cache_mla_ckv, cache_mla_kr, cache_gqa_k, cache_gqa_v, c, c_ctx, w_ada, b_ada, norm_mix_pre, norm_mix_post, norm_mlp_pre, norm_mlp_post, w_in, hgrn_lb_fwd, hgrn_lb_bwd, hgrn_norm, swa_sink, mla_q_norm, mla_kv_norm, mla_w_uq, mla_w_ukv, gqa_q_norm, gqa_k_norm, w_branch, w_out, w_mlp_in, w_mlp_out):
    tabs = jnp.asarray(_ROPE_TABLES)
    b512 = jnp.asarray(_block_mean_matrix(512)).astype(BF16)

    def rows(a):
        return a.reshape(DEPTH, 1, -1)

    w_uq, w_k, w_uv = _mla_weight_layouts(mla_w_uq, mla_w_ukv)
    w_a, w_kr, w_b = _split_w_in(w_in)
    lp = dict(
        npre=rows(norm_mix_pre), wa=w_a, wkr=w_kr, wb=w_b, mlaq_g=rows(mla_q_norm), mlakv_g=rows(mla_kv_norm),
        wuq=w_uq, wk=w_k, wuv=w_uv, gq_g=rows(jnp.tile(gqa_q_norm, (1, N_HEADS))),
        gk_g=rows(jnp.tile(gqa_k_norm, (1, N_KV))), b512=b512, b128=b512[:128, :128],
        lbf=rows(_lower_bounds(hgrn_lb_fwd)), lbb=rows(_lower_bounds(hgrn_lb_bwd)), gn=rows(hgrn_norm),
        sink=rows(swa_sink),
        w_branch=w_branch.astype(BF16), w_out=w_out.astype(BF16), npost=rows(norm_mix_post),
        nmpre=rows(norm_mlp_pre), nmpost=rows(norm_mlp_post),
        w_mlp_in=w_mlp_in.astype(BF16), w_mlp_out=w_mlp_out.astype(BF16))

    cond8 = jnp.zeros((SUBLANES, D_MODEL), F32).at[0].set(c_ctx).at[1:1 + DEC_BATCH].set(c)
    mod4 = _ada(cond8, w_ada, b_ada).reshape(DEPTH, SUBLANES, N_MOD, 1, D_MODEL)

    caches_in = (cache_swa_k.reshape(DEC_BATCH, DEPTH, PAST_LEN, 128),
                 cache_swa_v.reshape(DEC_BATCH, DEPTH, PAST_LEN, 128), cache_mla_ckv, cache_mla_kr,
                 cache_gqa_k.reshape(DEC_BATCH, DEPTH, PAST_LEN, 128),
                 cache_gqa_v.reshape(DEC_BATCH, DEPTH, PAST_LEN, 128))

    x_ctx = x_prompt.reshape(N_CTX, D_MODEL)
    x_lat = x_sample.reshape(N_LAT, D_MODEL)
    pc = None
    for l in range(DEPTH):
        pc = _proj(x_ctx, mod4, lp, True, layer=l, prev=pc)
        ob, oc, od = _attn_ctx(lp, pc, l)
        x_ctx = _merge(x_ctx, mod4, pc["oa"], ob, oc, od, pc["gates"], lp, True, l)

        pq = _proj(x_lat, mod4, lp, False, layer=l, tabs=tabs)
        oa = _hgrn_lat(pq, lp, l, state_hgrn)
        ob, oc, od = _attn_lat(lp, pq, caches_in, l)
        x_lat = _merge(x_lat, mod4, oa, ob, oc, od, pq["gates"], lp, False, l)

    kv_shape = (BATCH, DEPTH, SEQ, N_KV, HEAD_DIM)
    return (x_ctx.reshape(BATCH, SEQ, D_MODEL), x_lat.reshape(DEC_BATCH, DEC_SEQ, D_MODEL), pc["st"],
            pc["skc"].reshape(kv_shape), pc["svc"].reshape(kv_shape), pc["mckv"], pc["mkr"],
            pc["gkc"].reshape(kv_shape), pc["gvc"].reshape(kv_shape))
```

```python
import functools

import jax
import jax.numpy as jnp
import numpy as np
from jax import lax
from jax.experimental import pallas as pl
from jax.experimental.pallas import tpu as pltpu

D_MODEL = 1024
BATCH, SEQ = 32, 256
DEC_BATCH, DEC_SEQ = 2, 2048
DEPTH = 2
PAST_LEN = 512
GRID_W = 64
ROPE_BASE = 10000.0
NORM_EPS = 1e-6
MASK_VALUE = -1e30
LOG_FLOOR = 1e-30
N_MOD = 6
N_BRANCH = 4
BRANCH_W = 512
D_FF = 4 * D_MODEL
HGRN_HEADS, HGRN_DK = 4, 128
SWA_WINDOW = 128
MLA_HEADS, MLA_Q_RANK, MLA_KV_RANK, MLA_NOPE, MLA_ROPE, MLA_V = 8, 256, 128, 64, 32, 64
HEAD_DIM = 64
N_HEADS, N_KV = 8, 2

N_CTX = BATCH * SEQ
N_LAT = DEC_BATCH * DEC_SEQ

LANES = 128
SUBLANES = 8
VMEM_LIMIT_BYTES = 56 * 1024 * 1024

PROJ_TILE = 256
PROJ_SUB = 256
HGRN_CHUNK = 256
HGRN_UNROLL = 4
TQ_LAT = 128

BF16 = jnp.bfloat16
F32 = jnp.float32

_W_A_COLS = 3712
_W_B_COLS = 8608 - _W_A_COLS - MLA_ROPE
_C_HGRN = ("wa", 0)
_C_SWA_Q = ("wa", 2560)
_C_SWA_KV = ("wa", 3072)
_C_MLA_CQ = ("wa", 3328)
_C_MLA_CKV = ("wa", 3584)
_C_MLA_KR = ("wkr", 0)
_C_GQA_Q = ("wb", 0)
_C_GQA_KV = ("wb", 512)
_C_GATES = ("wb", 768)


def _split_w_in(w):
    w16 = w.astype(BF16)
    return w16, w16, w16[..., _W_A_COLS + MLA_ROPE:]


_W_IN_BLOCK_INDEX = dict(wkr=(0, _W_A_COLS // LANES))


def _sigmoid_pair(x):
    a = jnp.exp(-jnp.abs(x))
    r = 1.0 / (1.0 + a)
    ar = a * r
    pos = x >= 0
    return jnp.where(pos, r, ar), jnp.where(pos, ar, r)


def _sigmoid(x):
    return _sigmoid_pair(x)[0]


def _silu(x):
    return x * _sigmoid(x)


def _rms(x, gain):
    return x * lax.rsqrt(jnp.mean(x * x, axis=-1, keepdims=True) + NORM_EPS) * gain


def _dot(a, b):
    return jnp.dot(a, b, preferred_element_type=F32)


def _dot_nt(a, b):
    return lax.dot_general(a, b, (((1,), (1,)), ((), ())), preferred_element_type=F32)


def _dot_tn(a, b):
    return lax.dot_general(a, b, (((0,), (0,)), ((), ())), preferred_element_type=F32)


def _tile_lanes(t, width):
    reps = width // LANES
    return t if reps == 1 else jnp.concatenate([t] * reps, axis=1)


def _rope(x, tab_ref, base, shift, rows):
    w = x.shape[1]
    c = _tile_lanes(tab_ref[base, rows, :], w)
    s1 = _tile_lanes(tab_ref[base + 1, rows, :], w)
    s2 = _tile_lanes(tab_ref[base + 2, rows, :], w)
    return x * c + pltpu.roll(x, shift, 1) * s1 + pltpu.roll(x, w - shift, 1) * s2


def _head_rms(x, bmat_ref, gain):
    ms = _dot((x * x).astype(BF16), bmat_ref[...])
    return x * lax.rsqrt(ms + NORM_EPS) * gain


def _const_spec(shape):
    nd = len(shape)
    return pl.BlockSpec(shape, lambda *_: (0,) * nd, pipeline_mode=pl.Buffered(1))


def _layer_spec(shape, layer, block_index=None):
    idx = (layer,) + tuple(block_index or (0,) * len(shape))
    return pl.BlockSpec((None,) + tuple(shape), lambda *_: idx, pipeline_mode=pl.Buffered(1))


_ANY_SPEC = pl.BlockSpec(memory_space=pl.ANY)


def _params(n_grid_dims):
    return pltpu.CompilerParams(dimension_semantics=("arbitrary",) * n_grid_dims, vmem_limit_bytes=VMEM_LIMIT_BYTES)


_ADA_TN = 2048


def _ada_kernel(c_ref, w_ref, b_ref, o_ref):
    s = _silu(c_ref[...]).astype(BF16)
    o_ref[0] = _dot(s, w_ref[0].astype(BF16)) + b_ref[0]


def _ada(cond8, w_ada, b_ada):
    n = N_MOD * D_MODEL
    return pl.pallas_call(
        _ada_kernel,
        out_shape=jax.ShapeDtypeStruct((DEPTH, SUBLANES, n), F32),
        grid=(DEPTH, n // _ADA_TN),
        in_specs=[
            pl.BlockSpec((SUBLANES, D_MODEL), lambda l, j: (0, 0)),
            pl.BlockSpec((1, D_MODEL, _ADA_TN), lambda l, j: (l, 0, j)),
            pl.BlockSpec((1, 1, _ADA_TN), lambda l, j: (l, 0, j)),
        ],
        out_specs=pl.BlockSpec((1, SUBLANES, _ADA_TN), lambda l, j: (l, 0, j)),
        compiler_params=_params(2),
        name="ada",
    )(cond8, w_ada, b_ada.reshape(DEPTH, 1, n))


def _mod_spec(is_ctx, layer, tile):
    if is_ctx:
        return pl.BlockSpec((None, 1, N_MOD, 1, D_MODEL), lambda i: (layer, 0, 0, 0, 0))
    per_seq = DEC_SEQ // tile
    return pl.BlockSpec((None, 1, N_MOD, 1, D_MODEL), lambda i: (layer, 1 + i // per_seq, 0, 0, 0))


LOG2E = float(np.log2(np.e))
_SCALE_64 = HEAD_DIM ** -0.5 * LOG2E
_SCALE_MLA = (MLA_NOPE + MLA_ROPE) ** -0.5 * LOG2E
_TAB_64, _TAB_MLAQ, _TAB_KR = 0, 3, 6

_HGRN_PRE = ("hq", "hff", "hfb", "hi", "hg")
_ATTN_OUTS = (
    ("sq", 512, BF16), ("sk", 256, BF16), ("sv", 256, BF16),
    ("mq", 1024, BF16), ("mk", 1024, BF16), ("mv", 1024, BF16),
    ("gq", 512, BF16), ("gk", 256, BF16), ("gv", 256, BF16), ("gates", 4096, BF16))
_CACHE_OUTS = (("skc", 128), ("svc", 128), ("mckv", 128), ("mkr", MLA_ROPE), ("gkc", 128), ("gvc", 128))
_PROJ_PARAMS = dict(npre=(1, D_MODEL), wa=(D_MODEL, _W_A_COLS), wkr=(D_MODEL, LANES), wb=(D_MODEL, _W_B_COLS),
                    mlaq_g=(1, MLA_Q_RANK), mlakv_g=(1, MLA_KV_RANK),
                    wuq=(MLA_Q_RANK, 1024), wk=(256, 1024), wuv=(MLA_KV_RANK, 1024), gq_g=(1, 512), gk_g=(1, 128))
_HGRN_PARAMS = dict(lbf=(1, 512), lbb=(1, 512), gn=(1, 512))
_PROJ_CONSTS = tuple(_PROJ_PARAMS) + ("b512", "b128")
_HGRN_CONSTS = tuple(_HGRN_PARAMS)


def _proj_out_names(is_ctx):
    attn = [n for n, _, _ in _ATTN_OUTS]
    if is_ctx:
        return ["oa"] + attn + [n for n, _ in _CACHE_OUTS] + ["st"]
    return list(_HGRN_PRE) + attn


def _proj_kernel(*refs, is_ctx, n_alias):
    names = ["x", "mod"] + list(_PROJ_CONSTS) + (list(_HGRN_CONSTS) if is_ctx else ["tab"])
    r = dict(zip(names, refs))
    pos = len(names) + n_alias
    for name in _proj_out_names(is_ctx) + (["hbuf"] if is_ctx else []):
        r[name] = refs[pos]
        pos += 1

    sh1 = r["mod"][0, 0]
    sc1 = r["mod"][0, 1]
    for s in range(PROJ_TILE // PROJ_SUB):
        for _ in _proj_sub_tile(r, s, sh1, sc1, is_ctx):
            pass


def _proj_sub_tile(r, s, sh1, sc1, is_ctx):
    rows = slice(PROJ_SUB * s, PROJ_SUB * (s + 1))
    h = (_rms(r["x"][rows, :], r["npre"][...]) * (1.0 + sc1) + sh1).astype(BF16)

    def seg(where, w, skip=0):
        name, a = where
        return _dot(h, r[name][:, a + skip:a + skip + w])

    def rope(v, base, shift):
        return v if is_ctx else _rope(v, r["tab"], base, shift, rows)

    for j, name in enumerate(_HGRN_PRE):
        u = seg(_C_HGRN, 512, skip=512 * j)
        if is_ctx:
            r["hbuf"][s, j] = u
        else:
            r[name][rows, :] = u
    yield

    if is_ctx:
        assert HGRN_CHUNK == PROJ_SUB
        lvls = _hgrn_levels()
        for hd in range(HGRN_HEADS):
            cols = slice(LANES * hd, LANES * (hd + 1))
            q = _silu(r["hbuf"][s, 0, :, cols])
            v = r["hbuf"][s, 3, :, cols]
            o = None
            for d, lb in ((0, r["lbf"]), (1, r["lbb"])):
                k, lf2 = _hgrn_forget(r["hbuf"][s, 1 + d, :, cols], lb[:, cols])
                o_d, st = _hgrn_chunk(q, k, lf2, v, None, d == 0, lvls[d])
                r["st"][s, 0, d, hd] = st.T
                o = o_d if o is None else o + o_d
            r["oa"][rows, cols] = _hgrn_finish(o, r["gn"][:, cols], r["hbuf"][s, 4, :, cols])
        yield

    def dup16(k):
        return jnp.concatenate([d.astype(BF16) for d in _dup_heads(k)], axis=1)

    def ones16(v):
        return jnp.concatenate([d.astype(BF16) for d in _heads_with_ones(v)], axis=1)

    r["sq"][rows, :] = (rope(seg(_C_SWA_Q, 512), _TAB_64, 16) * _SCALE_64).astype(BF16)
    kv = seg(_C_SWA_KV, 256)
    k, v = kv[:, :128], kv[:, 128:]
    r["sk"][rows, :] = dup16(rope(k, _TAB_64, 16))
    r["sv"][rows, :] = ones16(v)
    if is_ctx:
        r["skc"][s, 0] = k
        r["svc"][s, 0] = v

    cq = _rms(seg(_C_MLA_CQ, 256), r["mlaq_g"][...]).astype(BF16)
    q = _dot(cq, r["wuq"][...])
    r["mq"][rows, :] = (rope(q, _TAB_MLAQ, 8) * _SCALE_MLA).astype(BF16)
    ckv = _rms(seg(_C_MLA_CKV, 128), r["mlakv_g"][...])
    kr = seg(_C_MLA_KR, 128)
    if is_ctx:
        r["mckv"][s, 0] = ckv
        r["mkr"][s, 0] = kr[:, :MLA_ROPE]
    kr_used = rope(kr, _TAB_KR, 8)
    ckv16 = ckv.astype(BF16)
    r["mk"][rows, :] = _dot(jnp.concatenate([ckv16, kr_used.astype(BF16)], axis=1), r["wk"][...]).astype(BF16)
    r["mv"][rows, :] = _with_ones(_dot(ckv16, r["wuv"][...])).astype(BF16)

    gq = _head_rms(seg(_C_GQA_Q, 512), r["b512"], r["gq_g"][...])
    r["gq"][rows, :] = (rope(gq, _TAB_64, 16) * _SCALE_64).astype(BF16)
    kv = seg(_C_GQA_KV, 256)
    k, v = _head_rms(kv[:, :128], r["b128"], r["gk_g"][...]), kv[:, 128:]
    r["gk"][rows, :] = dup16(rope(k, _TAB_64, 16))
    r["gv"][rows, :] = ones16(v)
    if is_ctx:
        r["gkc"][s, 0] = k
        r["gvc"][s, 0] = v

    for j in range(N_BRANCH):
        g = _sigmoid(seg(_C_GATES, D_MODEL, skip=D_MODEL * j))
        r["gates"][rows, D_MODEL * j:D_MODEL * (j + 1)] = g.astype(BF16)


def _proj(x, mod4, lp, is_ctx, layer=0, tabs=None, prev=None):
    n_tok = x.shape[0]
    params = dict(_PROJ_PARAMS, **(_HGRN_PARAMS if is_ctx else {}))
    n_sub = PROJ_TILE // PROJ_SUB
    in_specs = [pl.BlockSpec((PROJ_TILE, D_MODEL), lambda i: (i, 0)), _mod_spec(is_ctx, layer, PROJ_TILE)]
    in_specs += [_layer_spec(params[n], layer, _W_IN_BLOCK_INDEX.get(n)) for n in _PROJ_PARAMS]
    in_specs += [_const_spec((512, 512)), _const_spec((128, 128))]
    args = [x, mod4] + [lp[n] for n in _PROJ_PARAMS] + [lp["b512"], lp["b128"]]
    if is_ctx:
        in_specs += [_layer_spec(params[n], layer) for n in _HGRN_PARAMS]
        args += [lp[n] for n in _HGRN_PARAMS]
    if not is_ctx:
        in_specs.append(pl.BlockSpec((9, PROJ_TILE, LANES), lambda i: (0, i % (DEC_SEQ // PROJ_TILE), 0)))
        args.append(tabs)

    def tok(w):
        return pl.BlockSpec((PROJ_TILE, w), lambda i: (i, 0))

    specs = {n: (tok(w), jax.ShapeDtypeStruct((n_tok, w), dt)) for n, w, dt in _ATTN_OUTS}
    if is_ctx:
        specs["oa"] = (tok(512), jax.ShapeDtypeStruct((n_tok, 512), BF16))
        for n, w in _CACHE_OUTS:
            specs[n] = (pl.BlockSpec((n_sub, 1, SEQ, w), lambda i: (i, layer, 0, 0)),
                        jax.ShapeDtypeStruct((BATCH, DEPTH, SEQ, w), F32))
        specs["st"] = (pl.BlockSpec((n_sub, 1, 2, HGRN_HEADS, LANES, LANES), lambda i: (i, layer, 0, 0, 0, 0)),
                       jax.ShapeDtypeStruct((BATCH, DEPTH, 2, HGRN_HEADS, LANES, LANES), F32))
    else:
        for n in _HGRN_PRE:
            specs[n] = (tok(512), jax.ShapeDtypeStruct((n_tok, 512), F32))
    out_names = _proj_out_names(is_ctx)
    aliases = {}
    n_alias = 0
    if prev is not None:
        for n in [n for n, _ in _CACHE_OUTS] + ["st"]:
            aliases[len(args)] = out_names.index(n)
            in_specs.append(_ANY_SPEC)
            args.append(prev[n])
            n_alias += 1
    outs = pl.pallas_call(
        functools.partial(_proj_kernel, is_ctx=is_ctx, n_alias=n_alias),
        out_shape=[specs[n][1] for n in out_names], grid=(n_tok // PROJ_TILE,), in_specs=in_specs,
        out_specs=[specs[n][0] for n in out_names],
        scratch_shapes=[pltpu.VMEM((n_sub, len(_HGRN_PRE), PROJ_SUB, 512), F32)] if is_ctx else [],
        input_output_aliases=aliases, compiler_params=_params(1),
        name="proj_ctx" if is_ctx else "proj_lat",
    )(*args)
    return dict(zip(out_names, outs))


def _pair_levels():
    C = HGRN_CHUNK
    ti = lax.broadcasted_iota(jnp.int32, (C, C), 0)
    si = lax.broadcasted_iota(jnp.int32, (C, C), 1)
    x = ti ^ si
    lvl = jnp.where(x == 0, -1, 0)
    b = 2
    while b < C:
        lvl = lvl + (x >= b).astype(jnp.int32)
        b *= 2
    return ti, si, lvl


def _hgrn_forget(f_pre, lb):
    s_pos, s_neg = _sigmoid_pair(f_pre)
    f = lb + (1.0 - lb) * s_pos
    return (1.0 - lb) * s_neg, jnp.log2(jnp.maximum(f, LOG_FLOOR))


def _hgrn_levels():
    ti, si, lvl = _pair_levels()
    return jnp.where(ti > si, lvl, -1), jnp.where(ti < si, lvl, -1)


def _hgrn_chunk(q, k, lf2, v, st, fwd, lvl):
    C = HGRN_CHUNK
    nv = C // SUBLANES
    sub = lax.broadcasted_iota(jnp.int32, (nv, SUBLANES, LANES), 1)

    p = lf2.reshape(nv, SUBLANES, LANES)
    for s in (1, 2, 4):
        if fwd:
            p = p + jnp.where(sub >= s, pltpu.roll(p, s, 1), 0.0)
        else:
            p = p + jnp.where(sub < SUBLANES - s, pltpu.roll(p, SUBLANES - s, 1), 0.0)

    a_mat = jnp.zeros((C, C), F32)

    def add_level(a_mat, e, level):
        a_m = _dot_nt((q * e).astype(BF16), (k * e).astype(BF16))
        return jnp.where(lvl == level, a_m, a_mat)

    def row(r):
        return jnp.broadcast_to(p[:, r:r + 1, :], p.shape)

    odd = (sub & 1) == 1
    if fwd:
        bnd = (jnp.where(odd, pltpu.roll(p, 1, 1), p), jnp.where(sub < 4, row(1), row(5)), row(3))
    else:
        bnd = (jnp.where(odd, p, pltpu.roll(p, SUBLANES - 1, 1)), jnp.where(sub < 4, row(2), row(6)), row(4))
    for level, b in enumerate(bnd):
        a_mat = add_level(a_mat, jnp.exp2(-jnp.abs(p - b)).reshape(C, LANES), level)

    p = p.reshape(C, LANES)
    m = SUBLANES
    while m < C:
        nb = C // (2 * m)
        p4 = p.reshape(nb, 2, m, LANES)
        lo, hi = p4[:, 0], p4[:, 1]
        if fwd:
            tot = lo[:, m - 1:m]
            e_lo, e_hi = tot - lo, hi
            p_lo, p_hi = lo, hi + tot
        else:
            tot = hi[:, 0:1]
            e_lo, e_hi = lo, tot - hi
            p_lo, p_hi = lo + tot, hi
        e = jnp.exp2(jnp.concatenate([e_lo[:, None], e_hi[:, None]], axis=1).reshape(C, LANES))
        p = jnp.concatenate([p_lo[:, None], p_hi[:, None]], axis=1).reshape(C, LANES)
        a_mat = add_level(a_mat, e, int(np.log2(m)))
        m *= 2
    v16 = v.astype(BF16)
    o = jnp.sum(q * k, axis=-1, keepdims=True) * v + _dot(a_mat.astype(BF16), v16)

    if st is not None:
        o = o + _dot_nt((q * jnp.exp2(p)).astype(BF16), st.astype(BF16))
    tot = p[C - 1:C] if fwd else p[0:1]
    k_end = (k * jnp.exp2(tot - p)).astype(BF16)
    st_new = _dot_tn(v16, k_end)
    if st is not None:
        st_new = st_new + st * jnp.exp2(tot)
    return o, st_new


def _hgrn_finish(o, gn, g_pre):
    return (_rms(o, gn) * _silu(g_pre)).astype(BF16)


def _hgrn_head_static(read, lbf, lbb, gn, n_chunks):
    lvls = _hgrn_levels()
    st = [None, None]
    outs = [[None] * n_chunks, [None] * n_chunks]
    for c in range(n_chunks):
        for d, cc, lb in ((0, c, lbf), (1, n_chunks - 1 - c, lbb)):
            k, lf2 = _hgrn_forget(read(1 + d, cc), lb)
            outs[d][cc], st[d] = _hgrn_chunk(_silu(read(0, cc)), k, lf2, read(3, cc), st[d], d == 0, lvls[d])
    return [_hgrn_finish(outs[0][c] + outs[1][c], gn, read(4, c)) for c in range(n_chunks)], st


def _hgrn_lat_kernel(lbf_ref, lbb_ref, gn_ref, q_ref, ff_ref, fb_ref, v_ref, g_ref, s0_ref, o_ref,
                     of_ref, ob_ref, st_ref, *, n_chunks):
    C = HGRN_CHUNK
    for d in range(2):
        st_ref[d] = s0_ref[0, 0, d, 0].T
    lbf = lbf_ref[...]
    lbb = lbb_ref[...]
    lvls = _hgrn_levels()

    def body(c, carry):
        rf = pl.multiple_of(c * C, C)
        rb = pl.multiple_of((n_chunks - 1 - c) * C, C)
        for d, r0, f_ref, lb, out_ref in ((0, rf, ff_ref, lbf, of_ref), (1, rb, fb_ref, lbb, ob_ref)):
            sl = pl.ds(r0, C)
            k, lf2 = _hgrn_forget(f_ref[sl, :], lb)
            out_ref[sl, :], st_ref[d] = _hgrn_chunk(_silu(q_ref[sl, :]), k, lf2, v_ref[sl, :], st_ref[d], d == 0,
                                                    lvls[d])
        return carry

    lax.fori_loop(0, n_chunks, body, 0, unroll=HGRN_UNROLL)
    o_ref[...] = _hgrn_finish(of_ref[...] + ob_ref[...], gn_ref[...], g_ref[...])


def _hgrn_lat(p, lp, layer, state_in):
    tok_spec = pl.BlockSpec((DEC_SEQ, LANES), lambda b, h: (b, h))
    vec_spec = pl.BlockSpec((None, 1, LANES), lambda b, h: (layer, 0, h))
    state_spec = pl.BlockSpec((1, 1, 2, 1, LANES, LANES), lambda b, h: (b, layer, 0, h, 0, 0))
    return pl.pallas_call(
        functools.partial(_hgrn_lat_kernel, n_chunks=DEC_SEQ // HGRN_CHUNK),
        out_shape=jax.ShapeDtypeStruct((N_LAT, 512), BF16), grid=(DEC_BATCH, HGRN_HEADS),
        in_specs=[vec_spec] * 3 + [tok_spec] * 5 + [state_spec], out_specs=tok_spec,
        scratch_shapes=[pltpu.VMEM((DEC_SEQ, LANES), F32), pltpu.VMEM((DEC_SEQ, LANES), F32),
                        pltpu.VMEM((2, LANES, LANES), F32)],
        compiler_params=_params(2), name="hgrn_lat",
    )(lp["lbf"], lp["lbb"], lp["gn"], p["hq"], p["hff"], p["hfb"], p["hi"], p["hg"], state_in)


def _lane_lo(shape):
    return (lax.broadcasted_iota(jnp.int32, shape, len(shape) - 1) & (LANES - 1)) < HEAD_DIM


def _with_ones(v):
    return jnp.where(_lane_lo(v.shape), v, 1.0)


def _softmax_unit(make_segs, sink_col, finish):
    segs = make_segs()
    mx = None
    for s, _ in segs:
        m = jnp.max(s, axis=-1, keepdims=True)
        mx = m if mx is None else jnp.maximum(mx, m)
    if sink_col is not None:
        mx = jnp.maximum(mx, sink_col)
    yield
    acc = None
    for s, v in segs:
        pv = _dot(jnp.exp2(s - mx).astype(BF16), v)
        acc = pv if acc is None else acc + pv
    den = jnp.where(_lane_lo(acc.shape), pltpu.roll(acc, HEAD_DIM, 1), acc)
    if sink_col is not None:
        den = den + jnp.exp2(sink_col - mx)
    finish(acc / den)


_DONE = object()


def _run_units(units, lookahead=1):
    pending = list(units)
    active = []
    for _ in range(lookahead - 1):
        if pending:
            u = pending.pop(0)
            next(u)
            active.insert(0, u)
    while pending or active:
        started = []
        if pending:
            u = pending.pop(0)
            next(u)
            started.append(u)
        n_adv = max(len(active) - (lookahead - 1), 1) if pending else len(active)
        keep = active[:len(active) - n_adv]
        active = started + keep + [a for a in active[len(active) - n_adv:] if next(a, _DONE) is not _DONE]


def _pair(lo, even, odd):
    return jnp.where(lo, even, pltpu.roll(odd, HEAD_DIM, 1))


def _gqa_unit(q_ref, g, tq, segs_fn, o_ref, sink_ref=None):
    lo = _lane_lo((tq, LANES))
    parts = []
    for j in range(2):
        qp = q_ref[:, 256 * g + 128 * j:256 * g + 128 * (j + 1)]
        zero = jnp.zeros_like(qp)
        parts += [jnp.where(lo, qp, zero), jnp.where(lo, zero, qp)]
    q4 = jnp.concatenate(parts, axis=0)
    sink_col = None
    if sink_ref is not None:
        sink_col = jnp.concatenate(
            [jnp.broadcast_to(sink_ref[0:1, 4 * g + i:4 * g + i + 1] * LOG2E, (tq, 1)) for i in range(4)], axis=0)

    def finish(pv):
        o_ref[:, 256 * g:256 * g + 128] = _pair(lo, pv[0:tq], pv[tq:2 * tq]).astype(BF16)
        o_ref[:, 256 * g + 128:256 * (g + 1)] = _pair(lo, pv[2 * tq:3 * tq], pv[3 * tq:4 * tq]).astype(BF16)

    yield from _softmax_unit(lambda: segs_fn(q4), sink_col, finish)


def _mla_pair_unit(q_ref, hp, tq, segs_fn, o_ref):
    outs = []
    subs = [_softmax_unit(functools.partial(segs_fn, h, q_ref[:, 128 * h:128 * (h + 1)]), None, outs.append)
            for h in (2 * hp, 2 * hp + 1)]
    while True:
        if [next(s, _DONE) for s in subs][0] is _DONE:
            break
        yield
    o_ref[:, 128 * hp:128 * (hp + 1)] = _pair(_lane_lo((tq, LANES)), outs[0], outs[1]).astype(BF16)


_ATTN_IN = ("sq", "sk", "sv", "mq", "mk", "mv", "gq", "gk", "gv")
_ATTN_WIDTH = dict(sq=512, sk=256, sv=256, mq=1024, mk=1024, mv=1024, gq=512, gk=256, gv=256)


def _attn_ctx_kernel(sink_ref, sq_ref, sk_ref, sv_ref, mq_ref, mk_ref, mv_ref, gq_ref, gk_ref, gv_ref,
                     ob_ref, oc_ref, od_ref):
    tq = SEQ
    units = []
    for g in range(N_KV):
        def swa_segs(q4, g=g):
            return [(_dot_nt(q4, sk_ref[:, 128 * g:128 * (g + 1)]), sv_ref[:, 128 * g:128 * (g + 1)])]

        def gqa_segs(q4, g=g):
            return [(_dot_nt(q4, gk_ref[:, 128 * g:128 * (g + 1)]), gv_ref[:, 128 * g:128 * (g + 1)])]

        units += [_gqa_unit(sq_ref, g, tq, swa_segs, ob_ref, sink_ref), _gqa_unit(gq_ref, g, tq, gqa_segs, od_ref)]

    def mla_segs(h, q_h):
        return [(_dot_nt(q_h, mk_ref[:, 128 * h:128 * (h + 1)]), mv_ref[:, 128 * h:128 * (h + 1)])]

    units += [_mla_pair_unit(mq_ref, hp, tq, mla_segs, oc_ref) for hp in range(MLA_HEADS // 2)]
    _run_units(units)


def _attn_ctx(lp, p, layer):
    def spec(w):
        return pl.BlockSpec((SEQ, w), lambda b: (b, 0))

    return pl.pallas_call(
        _attn_ctx_kernel,
        out_shape=[jax.ShapeDtypeStruct((N_CTX, 512), BF16)] * 3,
        grid=(BATCH,),
        in_specs=[_layer_spec((1, N_HEADS), layer)] + [spec(_ATTN_WIDTH[n]) for n in _ATTN_IN],
        out_specs=[spec(512)] * 3,
        compiler_params=_params(1),
        name="attn_ctx",
    )(lp["sink"], *[p[n] for n in _ATTN_IN])


def _dup_heads(x):
    lo = _lane_lo(x.shape)
    r = pltpu.roll(x, HEAD_DIM, 1)
    return jnp.where(lo, x, r), jnp.where(lo, r, x)


def _heads_with_ones(x):
    return _with_ones(x), _with_ones(pltpu.roll(x, HEAD_DIM, 1))


def _attn_lat_kernel(sink_ref, wk_ref, wuv_ref,
                     sq_ref, sk_ref, sv_ref, mq_ref, mk_ref, mv_ref, gq_ref, gk_ref, gv_ref,
                     csk_ref, csv_ref, cckv_ref, ckr_ref, cgk_ref, cgv_ref,
                     ob_ref, oc_ref, od_ref,
                     cs_k, cs_v, cm_k, cm_v, cg_k, cg_v):
    tq = TQ_LAT
    qi = pl.program_id(1)

    @pl.when(qi == 0)
    def _():
        for src, dst in ((csk_ref, cs_k), (cgk_ref, cg_k)):
            d0, d1 = _dup_heads(src[0, 0])
            dst[0] = d0.astype(BF16)
            dst[1] = d1.astype(BF16)
        for src, dst in ((csv_ref, cs_v), (cgv_ref, cg_v)):
            d0, d1 = _heads_with_ones(src[0, 0])
            dst[0] = d0.astype(BF16)
            dst[1] = d1.astype(BF16)
        ckv16 = cckv_ref[0, 0].astype(BF16)
        k = _dot(ckv16, wk_ref[0:MLA_KV_RANK, :]) + _dot(ckr_ref[0, 0].astype(BF16),
                                                           wk_ref[MLA_KV_RANK:MLA_KV_RANK + MLA_ROPE, :])
        cm_k[...] = k.astype(BF16)
        cm_v[...] = _with_ones(_dot(ckv16, wuv_ref[...])).astype(BF16)

    q0 = qi * tq
    win = tq + 2 * SWA_WINDOW
    start = pl.multiple_of(jnp.clip(q0 - SWA_WINDOW, 0, DEC_SEQ - win), SWA_WINDOW)
    rows = lax.broadcasted_iota(jnp.int32, (4 * tq, win), 0)
    qpos = q0 + (rows & (tq - 1))
    kpos = start + lax.broadcasted_iota(jnp.int32, (4 * tq, win), 1)
    band = jnp.abs(qpos - kpos) <= SWA_WINDOW
    units = []
    for g in range(N_KV):
        def swa_segs(q4, g=g):
            ks = sk_ref[pl.ds(start, win), 128 * g:128 * (g + 1)]
            vs = sv_ref[pl.ds(start, win), 128 * g:128 * (g + 1)]
            s_own = jnp.where(band, _dot_nt(q4, ks), MASK_VALUE)
            return [(_dot_nt(q4, cs_k[g]), cs_v[g]), (s_own, vs)]

        def gqa_segs(q4, g=g):
            return [(_dot_nt(q4, cg_k[g]), cg_v[g]),
                    (_dot_nt(q4, gk_ref[:, 128 * g:128 * (g + 1)]), gv_ref[:, 128 * g:128 * (g + 1)])]

        units += [_gqa_unit(sq_ref, g, tq, swa_segs, ob_ref, sink_ref), _gqa_unit(gq_ref, g, tq, gqa_segs, od_ref)]

    def mla_segs(h, q_h):
        cols = slice(128 * h, 128 * (h + 1))
        return [(_dot_nt(q_h, cm_k[:, cols]), cm_v[:, cols]), (_dot_nt(q_h, mk_ref[:, cols]), mv_ref[:, cols])]

    units += [_mla_pair_unit(mq_ref, hp, tq, mla_segs, oc_ref) for hp in range(MLA_HEADS // 2)]
    _run_units(units, lookahead=3)


def _attn_lat(lp, p, caches, layer):
    nq = DEC_SEQ // TQ_LAT

    def qspec(w):
        return pl.BlockSpec((TQ_LAT, w), lambda b, i: (b * nq + i, 0))

    def kspec(w):
        return pl.BlockSpec((DEC_SEQ, w), lambda b, i: (b, 0))

    def cspec(w):
        return pl.BlockSpec((1, 1, PAST_LEN, w), lambda b, i: (b, layer, 0, 0))

    is_q = dict(sq=True, mq=True, gq=True)
    in_specs = [_layer_spec((1, N_HEADS), layer), _layer_spec((256, 1024), layer),
                _layer_spec((MLA_KV_RANK, 1024), layer)]
    in_specs += [(qspec if is_q.get(n) else kspec)(_ATTN_WIDTH[n]) for n in _ATTN_IN]
    in_specs += [cspec(128), cspec(128), cspec(128), cspec(MLA_ROPE), cspec(128), cspec(128)]
    return pl.pallas_call(
        _attn_lat_kernel,
        out_shape=[jax.ShapeDtypeStruct((N_LAT, 512), BF16)] * 3,
        grid=(DEC_BATCH, nq),
        in_specs=in_specs,
        out_specs=[qspec(512)] * 3,
        scratch_shapes=[pltpu.VMEM((N_KV, PAST_LEN, LANES), BF16), pltpu.VMEM((N_KV, PAST_LEN, LANES), BF16),
                        pltpu.VMEM((PAST_LEN, 1024), BF16), pltpu.VMEM((PAST_LEN, 1024), BF16),
                        pltpu.VMEM((N_KV, PAST_LEN, LANES), BF16), pltpu.VMEM((N_KV, PAST_LEN, LANES), BF16)],
        compiler_params=_params(2),
        name="attn_lat",
    )(lp["sink"], lp["wk"], lp["wuv"], *[p[n] for n in _ATTN_IN], *caches)


MERGE_TILE = 512
MERGE_SUB = 256


def _merge_kernel(x_ref, mod_ref, oa_ref, ob_ref, oc_ref, od_ref, gates_ref, wb_ref, wo_ref, npost_ref,
                  nmpre_ref, nmpost_ref, w1_ref, w2_ref, y_ref):
    g1 = mod_ref[0, 2]
    sh2 = mod_ref[0, 3]
    sc2 = mod_ref[0, 4]
    g2 = mod_ref[0, 5]

    def sub_tile(r0):
        rows = slice(r0, r0 + MERGE_SUB)
        acc = None
        for n, o_ref in enumerate((oa_ref, ob_ref, oc_ref, od_ref)):
            br = _dot(o_ref[rows, :], wb_ref[n]) * gates_ref[rows, D_MODEL * n:D_MODEL * (n + 1)].astype(F32)
            acc = br if acc is None else acc + br
        y = _dot(acc.astype(BF16), wo_ref[...])
        yield
        x = x_ref[rows, :] + g1 * _rms(y, npost_ref[...])
        h = (_rms(x, nmpre_ref[...]) * (1.0 + sc2) + sh2).astype(BF16)
        a = jnp.maximum(_dot(h, w1_ref[...]), 0.0)
        yield
        y = _dot((a * a).astype(BF16), w2_ref[...])
        y_ref[rows, :] = x + g2 * _rms(y, nmpost_ref[...])

    _run_units([sub_tile(r0) for r0 in range(0, MERGE_TILE, MERGE_SUB)])


_MERGE_PARAMS = dict(w_branch=(N_BRANCH, BRANCH_W, D_MODEL), w_out=(D_MODEL, D_MODEL), npost=(1, D_MODEL),
                     nmpre=(1, D_MODEL), nmpost=(1, D_MODEL), w_mlp_in=(D_MODEL, D_FF), w_mlp_out=(D_FF, D_MODEL))


def _merge(x, mod4, o_a, o_b, o_c, o_d, gates, lp, is_ctx, layer):
    n_tok = x.shape[0]

    def tok(w):
        return pl.BlockSpec((MERGE_TILE, w), lambda i: (i, 0))

    in_specs = [tok(D_MODEL), _mod_spec(is_ctx, layer, MERGE_TILE), tok(512), tok(512), tok(512), tok(512), tok(4096)]
    in_specs += [_layer_spec(shape, layer) for shape in _MERGE_PARAMS.values()]
    return pl.pallas_call(
        _merge_kernel, out_shape=jax.ShapeDtypeStruct((n_tok, D_MODEL), F32), grid=(n_tok // MERGE_TILE,),
        in_specs=in_specs, out_specs=tok(D_MODEL), compiler_params=_params(1),
        name="merge_ctx" if is_ctx else "merge_lat",
    )(x, mod4, o_a, o_b, o_c, o_d, gates, *[lp[n] for n in _MERGE_PARAMS])


def _rope_tables():
    t = np.arange(DEC_SEQ)
    row, col = (t // GRID_W).astype(np.float32), (t % GRID_W).astype(np.float32)

    def pattern(d):
        nf = d // 4
        inv = (ROPE_BASE ** (-np.arange(nf, dtype=np.float32) / nf)).astype(np.float32)
        ang = np.concatenate([row[:, None] * inv, row[:, None] * inv, col[:, None] * inv, col[:, None] * inv], axis=1)
        cos, sin = np.cos(ang), np.sin(ang)
        second = (np.arange(d) % (2 * nf)) >= nf
        return cos, np.where(second, sin, 0.0), np.where(second, 0.0, -sin)

    out = np.zeros((9, DEC_SEQ, LANES), np.float32)
    out[[0, 3, 6]] = 1.0
    c, s1, s2 = pattern(HEAD_DIM)
    for a, tab in enumerate((c, s1, s2)):
        out[_TAB_64 + a] = np.tile(tab, (1, LANES // HEAD_DIM))
    c, s1, s2 = pattern(MLA_ROPE)
    for a, tab in enumerate((c, s1, s2)):
        out[_TAB_MLAQ + a][:, MLA_NOPE:MLA_NOPE + MLA_ROPE] = tab
        out[_TAB_KR + a][:, :MLA_ROPE] = tab
    return out


_ROPE_TABLES = _rope_tables()


def _block_mean_matrix(n):
    i = np.arange(n)
    return ((i[:, None] // HEAD_DIM) == (i[None, :] // HEAD_DIM)).astype(np.float32) / HEAD_DIM


def _mla_weight_layouts(w_uq, w_ukv):
    dq = MLA_NOPE + MLA_ROPE
    wq = w_uq.reshape(DEPTH, MLA_Q_RANK, MLA_HEADS, dq)
    wq = jnp.pad(wq, ((0, 0), (0, 0), (0, 0), (0, LANES - dq))).reshape(DEPTH, MLA_Q_RANK, MLA_HEADS * LANES)
    wkv = w_ukv.reshape(DEPTH, MLA_KV_RANK, MLA_HEADS, MLA_NOPE + MLA_V)
    wk = jnp.pad(wkv[..., :MLA_NOPE], ((0, 0), (0, 0), (0, 0), (0, LANES - MLA_NOPE)))
    wk = wk.reshape(DEPTH, MLA_KV_RANK, MLA_HEADS * LANES)
    place = np.zeros((LANES, MLA_HEADS, LANES), np.float32)
    for j in range(MLA_ROPE):
        place[j, :, MLA_NOPE + j] = 1.0
    place = jnp.broadcast_to(jnp.asarray(place.reshape(1, LANES, MLA_HEADS * LANES)), (DEPTH, LANES, MLA_HEADS * LANES))
    w_k = jnp.concatenate([wk, place], axis=1)
    w_uv = jnp.pad(wkv[..., MLA_NOPE:], ((0, 0), (0, 0), (0, 0), (0, LANES - MLA_V)))
    w_uv = w_uv.reshape(DEPTH, MLA_KV_RANK, MLA_HEADS * LANES)
    return wq.astype(BF16), w_k.astype(BF16), w_uv.astype(BF16)


def _lower_bounds(p):
    s = jax.nn.softmax(p.astype(F32), axis=0)
    return jnp.cumsum(s, axis=0) - s[0]


def kernel(x_prompt, x_sample, state_hgrn, cache_swa_k, cache_swa_v, cache_mla_ckv, cache_mla_kr, cache_gqa_k, cache_gqa_v, c, c_ctx, w_ada, b_ada, norm_mix_pre, norm_mix_post, norm_mlp_pre, norm_mlp_post, w_in, hgrn_lb_fwd, hgrn_lb_bwd, hgrn_norm, swa_sink, mla_q_norm, mla_kv_norm, mla_w_uq, mla_w_ukv, gqa_q_norm, gqa_k_norm, w_branch, w_out, w_mlp_in, w_mlp_out):
    tabs = jnp.asarray(_ROPE_TABLES)
    b512 = jnp.asarray(_block_mean_matrix(512)).astype(BF16)

    def rows(a):
        return a.reshape(DEPTH, 1, -1)

    w_uq, w_k, w_uv = _mla_weight_layouts(mla_w_uq, mla_w_ukv)
    w_a, w_kr, w_b = _split_w_in(w_in)
    lp = dict(
        npre=rows(norm_mix_pre), wa=w_a, wkr=w_kr, wb=w_b, mlaq_g=rows(mla_q_norm), mlakv_g=rows(mla_kv_norm),
        wuq=w_uq, wk=w_k, wuv=w_uv, gq_g=rows(jnp.tile(gqa_q_norm, (1, N_HEADS))),
        gk_g=rows(jnp.tile(gqa_k_norm, (1, N_KV))), b512=b512, b128=b512[:128, :128],
        lbf=rows(_lower_bounds(hgrn_lb_fwd)), lbb=rows(_lower_bounds(hgrn_lb_bwd)), gn=rows(hgrn_norm),
        sink=rows(swa_sink),
        w_branch=w_branch.astype(BF16), w_out=w_out.astype(BF16), npost=rows(norm_mix_post),
        nmpre=rows(norm_mlp_pre), nmpost=rows(norm_mlp_post),
        w_mlp_in=w_mlp_in.astype(BF16), w_mlp_out=w_mlp_out.astype(BF16))

    cond8 = jnp.zeros((SUBLANES, D_MODEL), F32).at[0].set(c_ctx).at[1:1 + DEC_BATCH].set(c)
    mod4 = _ada(cond8, w_ada, b_ada).reshape(DEPTH, SUBLANES, N_MOD, 1, D_MODEL)

    caches_in = (cache_swa_k.reshape(DEC_BATCH, DEPTH, PAST_LEN, 128),
                 cache_swa_v.reshape(DEC_BATCH, DEPTH, PAST_LEN, 128), cache_mla_ckv, cache_mla_kr,
                 cache_gqa_k.reshape(DEC_BATCH, DEPTH, PAST_LEN, 128),
                 cache_gqa_v.reshape(DEC_BATCH, DEPTH, PAST_LEN, 128))

    x_ctx = x_prompt.reshape(N_CTX, D_MODEL)
    x_lat = x_sample.reshape(N_LAT, D_MODEL)
    pc = None
    for l in range(DEPTH):
        pc = _proj(x_ctx, mod4, lp, True, layer=l, prev=pc)
        ob, oc, od = _attn_ctx(lp, pc, l)
        x_ctx = _merge(x_ctx, mod4, pc["oa"], ob, oc, od, pc["gates"], lp, True, l)

        pq = _proj(x_lat, mod4, lp, False, layer=l, tabs=tabs)
        oa = _hgrn_lat(pq, lp, l, state_hgrn)
        ob, oc, od = _attn_lat(lp, pq, caches_in, l)
        x_lat = _merge(x_lat, mod4, oa, ob, oc, od, pq["gates"], lp, False, l)

    kv_shape = (BATCH, DEPTH, SEQ, N_KV, HEAD_DIM)
    return (x_ctx.reshape(BATCH, SEQ, D_MODEL), x_lat.reshape(DEC_BATCH, DEC_SEQ, D_MODEL), pc["st"],
            pc["skc"].reshape(kv_shape), pc["svc"].reshape(kv_shape), pc["mckv"], pc["mkr"],
            pc["gkc"].reshape(kv_shape), pc["gvc"].reshape(kv_shape))
```

```python
import functools

import jax
import jax.numpy as jnp
import numpy as np
from jax import lax
from jax.experimental import pallas as pl
from jax.experimental.pallas import tpu as pltpu

D_MODEL = 1024
BATCH, SEQ = 32, 256
DEC_BATCH, DEC_SEQ = 2, 2048
DEPTH = 2
PAST_LEN = 512
GRID_W = 64
ROPE_BASE = 10000.0
NORM_EPS = 1e-6
MASK_VALUE = -1e30
LOG_FLOOR = 1e-30
N_MOD = 6
N_BRANCH = 4
BRANCH_W = 512
D_FF = 4 * D_MODEL
HGRN_HEADS, HGRN_DK = 4, 128
SWA_WINDOW = 128
MLA_HEADS, MLA_Q_RANK, MLA_KV_RANK, MLA_NOPE, MLA_ROPE, MLA_V = 8, 256, 128, 64, 32, 64
HEAD_DIM = 64
N_HEADS, N_KV = 8, 2

N_CTX = BATCH * SEQ
N_LAT = DEC_BATCH * DEC_SEQ

LANES = 128
SUBLANES = 8
VMEM_LIMIT_BYTES = 56 * 1024 * 1024

PROJ_TILE = 256
PROJ_SUB = 256
HGRN_CHUNK = 256
HGRN_BLOCK = 128
HGRN_UNROLL = 4
TQ_LAT = 128

BF16 = jnp.bfloat16
F32 = jnp.float32

_W_A_COLS = 3712
_W_B_COLS = 8608 - _W_A_COLS - MLA_ROPE
_C_HGRN = ("wa", 0)
_C_SWA_Q = ("wa", 2560)
_C_SWA_KV = ("wa", 3072)
_C_MLA_CQ = ("wa", 3328)
_C_MLA_CKV = ("wa", 3584)
_C_MLA_KR = ("wkr", 0)
_C_GQA_Q = ("wb", 0)
_C_GQA_KV = ("wb", 512)
_C_GATES = ("wb", 768)


def _split_w_in(w):
    w16 = w.astype(BF16)
    return w16, w16, w16[..., _W_A_COLS + MLA_ROPE:]


_W_IN_BLOCK_INDEX = dict(wkr=(0, _W_A_COLS // LANES))


def _sigmoid_pair(x):
    a = jnp.exp(-jnp.abs(x))
    r = 1.0 / (1.0 + a)
    ar = a * r
    pos = x >= 0
    return jnp.where(pos, r, ar), jnp.where(pos, ar, r)


def _sigmoid(x):
    return _sigmoid_pair(x)[0]


def _silu(x):
    return x * _sigmoid(x)


def _rms(x, gain):
    return x * lax.rsqrt(jnp.mean(x * x, axis=-1, keepdims=True) + NORM_EPS) * gain


def _dot(a, b):
    return jnp.dot(a, b, preferred_element_type=F32)


def _dot_nt(a, b):
    return lax.dot_general(a, b, (((1,), (1,)), ((), ())), preferred_element_type=F32)


def _dot_tn(a, b):
    return lax.dot_general(a, b, (((0,), (0,)), ((), ())), preferred_element_type=F32)


def _tile_lanes(t, width):
    reps = width // LANES
    return t if reps == 1 else jnp.concatenate([t] * reps, axis=1)


def _rope(x, tab_ref, base, shift, rows):
    w = x.shape[1]
    c = _tile_lanes(tab_ref[base, rows, :], w)
    s1 = _tile_lanes(tab_ref[base + 1, rows, :], w)
    s2 = _tile_lanes(tab_ref[base + 2, rows, :], w)
    return x * c + pltpu.roll(x, shift, 1) * s1 + pltpu.roll(x, w - shift, 1) * s2


def _head_rms(x, bmat_ref, gain):
    ms = _dot((x * x).astype(BF16), bmat_ref[...])
    return x * lax.rsqrt(ms + NORM_EPS) * gain


def _const_spec(shape):
    nd = len(shape)
    return pl.BlockSpec(shape, lambda *_: (0,) * nd, pipeline_mode=pl.Buffered(1))


def _layer_spec(shape, layer, block_index=None):
    idx = (layer,) + tuple(block_index or (0,) * len(shape))
    return pl.BlockSpec((None,) + tuple(shape), lambda *_: idx, pipeline_mode=pl.Buffered(1))


_ANY_SPEC = pl.BlockSpec(memory_space=pl.ANY)


def _params(n_grid_dims):
    return pltpu.CompilerParams(dimension_semantics=("arbitrary",) * n_grid_dims, vmem_limit_bytes=VMEM_LIMIT_BYTES)


_ADA_TN = 2048


def _ada_kernel(c_ref, w_ref, b_ref, o_ref):
    s = _silu(c_ref[...]).astype(BF16)
    o_ref[0] = _dot(s, w_ref[0].astype(BF16)) + b_ref[0]


def _ada(cond8, w_ada, b_ada):
    n = N_MOD * D_MODEL
    return pl.pallas_call(
        _ada_kernel,
        out_shape=jax.ShapeDtypeStruct((DEPTH, SUBLANES, n), F32),
        grid=(DEPTH, n // _ADA_TN),
        in_specs=[
            pl.BlockSpec((SUBLANES, D_MODEL), lambda l, j: (0, 0)),
            pl.BlockSpec((1, D_MODEL, _ADA_TN), lambda l, j: (l, 0, j)),
            pl.BlockSpec((1, 1, _ADA_TN), lambda l, j: (l, 0, j)),
        ],
        out_specs=pl.BlockSpec((1, SUBLANES, _ADA_TN), lambda l, j: (l, 0, j)),
        compiler_params=_params(2),
        name="ada",
    )(cond8, w_ada, b_ada.reshape(DEPTH, 1, n))


def _mod_spec(is_ctx, layer, tile):
    if is_ctx:
        return pl.BlockSpec((None, 1, N_MOD, 1, D_MODEL), lambda i: (layer, 0, 0, 0, 0))
    per_seq = DEC_SEQ // tile
    return pl.BlockSpec((None, 1, N_MOD, 1, D_MODEL), lambda i: (layer, 1 + i // per_seq, 0, 0, 0))


LOG2E = float(np.log2(np.e))
_SCALE_64 = HEAD_DIM ** -0.5 * LOG2E
_SCALE_MLA = (MLA_NOPE + MLA_ROPE) ** -0.5 * LOG2E
_TAB_64, _TAB_MLAQ, _TAB_KR = 0, 3, 6

_HGRN_PRE = ("hq", "hff", "hfb", "hi", "hg")
_ATTN_OUTS = (
    ("sq", 512, BF16), ("sk", 256, BF16), ("sv", 256, BF16),
    ("mq", 1024, BF16), ("mk", 1024, BF16), ("mv", 1024, BF16),
    ("gq", 512, BF16), ("gk", 256, BF16), ("gv", 256, BF16), ("gates", 4096, BF16))
_CACHE_OUTS = (("skc", 128), ("svc", 128), ("mckv", 128), ("mkr", MLA_ROPE), ("gkc", 128), ("gvc", 128))
_PROJ_PARAMS = dict(npre=(1, D_MODEL), wa=(D_MODEL, _W_A_COLS), wkr=(D_MODEL, LANES), wb=(D_MODEL, _W_B_COLS),
                    mlaq_g=(1, MLA_Q_RANK), mlakv_g=(1, MLA_KV_RANK),
                    wuq=(MLA_Q_RANK, 1024), wk=(256, 1024), wuv=(MLA_KV_RANK, 1024), gq_g=(1, 512), gk_g=(1, 128))
_HGRN_PARAMS = dict(lbf=(1, 512), lbb=(1, 512), gn=(1, 512))
_PROJ_CONSTS = tuple(_PROJ_PARAMS) + ("b512", "b128")
_HGRN_CONSTS = tuple(_HGRN_PARAMS)


def _proj_out_names(is_ctx):
    attn = [n for n, _, _ in _ATTN_OUTS]
    if is_ctx:
        return ["oa"] + attn + [n for n, _ in _CACHE_OUTS] + ["st"]
    return list(_HGRN_PRE) + attn


def _proj_kernel(*refs, is_ctx, n_alias):
    names = ["x", "mod"] + list(_PROJ_CONSTS) + (list(_HGRN_CONSTS) if is_ctx else ["tab"])
    r = dict(zip(names, refs))
    pos = len(names) + n_alias
    for name in _proj_out_names(is_ctx) + (["hbuf"] if is_ctx else []):
        r[name] = refs[pos]
        pos += 1

    sh1 = r["mod"][0, 0]
    sc1 = r["mod"][0, 1]
    for s in range(PROJ_TILE // PROJ_SUB):
        for _ in _proj_sub_tile(r, s, sh1, sc1, is_ctx):
            pass


def _proj_sub_tile(r, s, sh1, sc1, is_ctx):
    rows = slice(PROJ_SUB * s, PROJ_SUB * (s + 1))
    h = (_rms(r["x"][rows, :], r["npre"][...]) * (1.0 + sc1) + sh1).astype(BF16)

    def seg(where, w, skip=0):
        name, a = where
        return _dot(h, r[name][:, a + skip:a + skip + w])

    def rope(v, base, shift):
        return v if is_ctx else _rope(v, r["tab"], base, shift, rows)

    for j, name in enumerate(_HGRN_PRE):
        u = seg(_C_HGRN, 512, skip=512 * j)
        if is_ctx:
            r["hbuf"][s, j] = u
        else:
            r[name][rows, :] = u
    yield

    if is_ctx:
        assert HGRN_CHUNK == PROJ_SUB
        lvls = _hgrn_levels()
        for hd in range(HGRN_HEADS):
            cols = slice(LANES * hd, LANES * (hd + 1))
            q = _silu(r["hbuf"][s, 0, :, cols])
            v = r["hbuf"][s, 3, :, cols]
            o = None
            for d, lb in ((0, r["lbf"]), (1, r["lbb"])):
                k, lf2 = _hgrn_forget(r["hbuf"][s, 1 + d, :, cols], lb[:, cols])
                o_d, st = _hgrn_chunk(q, k, lf2, v, None, d == 0, lvls[d])
                r["st"][s, 0, d, hd] = st.T
                o = o_d if o is None else o + o_d
            r["oa"][rows, cols] = _hgrn_finish(o, r["gn"][:, cols], r["hbuf"][s, 4, :, cols])
        yield

    def dup16(k):
        return jnp.concatenate([d.astype(BF16) for d in _dup_heads(k)], axis=1)

    def ones16(v):
        return jnp.concatenate([d.astype(BF16) for d in _heads_with_ones(v)], axis=1)

    r["sq"][rows, :] = (rope(seg(_C_SWA_Q, 512), _TAB_64, 16) * _SCALE_64).astype(BF16)
    kv = seg(_C_SWA_KV, 256)
    k, v = kv[:, :128], kv[:, 128:]
    r["sk"][rows, :] = dup16(rope(k, _TAB_64, 16))
    r["sv"][rows, :] = ones16(v)
    if is_ctx:
        r["skc"][s, 0] = k
        r["svc"][s, 0] = v

    cq = _rms(seg(_C_MLA_CQ, 256), r["mlaq_g"][...]).astype(BF16)
    q = _dot(cq, r["wuq"][...])
    r["mq"][rows, :] = (rope(q, _TAB_MLAQ, 8) * _SCALE_MLA).astype(BF16)
    ckv = _rms(seg(_C_MLA_CKV, 128), r["mlakv_g"][...])
    kr = seg(_C_MLA_KR, 128)
    if is_ctx:
        r["mckv"][s, 0] = ckv
        r["mkr"][s, 0] = kr[:, :MLA_ROPE]
    kr_used = rope(kr, _TAB_KR, 8)
    ckv16 = ckv.astype(BF16)
    r["mk"][rows, :] = _dot(jnp.concatenate([ckv16, kr_used.astype(BF16)], axis=1), r["wk"][...]).astype(BF16)
    r["mv"][rows, :] = _with_ones(_dot(ckv16, r["wuv"][...])).astype(BF16)

    gq = _head_rms(seg(_C_GQA_Q, 512), r["b512"], r["gq_g"][...])
    r["gq"][rows, :] = (rope(gq, _TAB_64, 16) * _SCALE_64).astype(BF16)
    kv = seg(_C_GQA_KV, 256)
    k, v = _head_rms(kv[:, :128], r["b128"], r["gk_g"][...]), kv[:, 128:]
    r["gk"][rows, :] = dup16(rope(k, _TAB_64, 16))
    r["gv"][rows, :] = ones16(v)
    if is_ctx:
        r["gkc"][s, 0] = k
        r["gvc"][s, 0] = v

    for j in range(N_BRANCH):
        g = _sigmoid(seg(_C_GATES, D_MODEL, skip=D_MODEL * j))
        r["gates"][rows, D_MODEL * j:D_MODEL * (j + 1)] = g.astype(BF16)


def _proj(x, mod4, lp, is_ctx, layer=0, tabs=None, prev=None):
    n_tok = x.shape[0]
    params = dict(_PROJ_PARAMS, **(_HGRN_PARAMS if is_ctx else {}))
    n_sub = PROJ_TILE // PROJ_SUB
    in_specs = [pl.BlockSpec((PROJ_TILE, D_MODEL), lambda i: (i, 0)), _mod_spec(is_ctx, layer, PROJ_TILE)]
    in_specs += [_layer_spec(params[n], layer, _W_IN_BLOCK_INDEX.get(n)) for n in _PROJ_PARAMS]
    in_specs += [_const_spec((512, 512)), _const_spec((128, 128))]
    args = [x, mod4] + [lp[n] for n in _PROJ_PARAMS] + [lp["b512"], lp["b128"]]
    if is_ctx:
        in_specs += [_layer_spec(params[n], layer) for n in _HGRN_PARAMS]
        args += [lp[n] for n in _HGRN_PARAMS]
    if not is_ctx:
        in_specs.append(pl.BlockSpec((9, PROJ_TILE, LANES), lambda i: (0, i % (DEC_SEQ // PROJ_TILE), 0)))
        args.append(tabs)

    def tok(w):
        return pl.BlockSpec((PROJ_TILE, w), lambda i: (i, 0))

    specs = {n: (tok(w), jax.ShapeDtypeStruct((n_tok, w), dt)) for n, w, dt in _ATTN_OUTS}
    if is_ctx:
        specs["oa"] = (tok(512), jax.ShapeDtypeStruct((n_tok, 512), BF16))
        for n, w in _CACHE_OUTS:
            specs[n] = (pl.BlockSpec((n_sub, 1, SEQ, w), lambda i: (i, layer, 0, 0)),
                        jax.ShapeDtypeStruct((BATCH, DEPTH, SEQ, w), F32))
        specs["st"] = (pl.BlockSpec((n_sub, 1, 2, HGRN_HEADS, LANES, LANES), lambda i: (i, layer, 0, 0, 0, 0)),
                       jax.ShapeDtypeStruct((BATCH, DEPTH, 2, HGRN_HEADS, LANES, LANES), F32))
    else:
        for n in _HGRN_PRE:
            specs[n] = (tok(512), jax.ShapeDtypeStruct((n_tok, 512), F32))
    out_names = _proj_out_names(is_ctx)
    aliases = {}
    n_alias = 0
    if prev is not None:
        for n in [n for n, _ in _CACHE_OUTS] + ["st"]:
            aliases[len(args)] = out_names.index(n)
            in_specs.append(_ANY_SPEC)
            args.append(prev[n])
            n_alias += 1
    outs = pl.pallas_call(
        functools.partial(_proj_kernel, is_ctx=is_ctx, n_alias=n_alias),
        out_shape=[specs[n][1] for n in out_names], grid=(n_tok // PROJ_TILE,), in_specs=in_specs,
        out_specs=[specs[n][0] for n in out_names],
        scratch_shapes=[pltpu.VMEM((n_sub, len(_HGRN_PRE), PROJ_SUB, 512), F32)] if is_ctx else [],
        input_output_aliases=aliases, compiler_params=_params(1),
        name="proj_ctx" if is_ctx else "proj_lat",
    )(*args)
    return dict(zip(out_names, outs))


def _pair_levels():
    C = HGRN_BLOCK
    ti = lax.broadcasted_iota(jnp.int32, (C, C), 0)
    si = lax.broadcasted_iota(jnp.int32, (C, C), 1)
    x = ti ^ si
    lvl = jnp.where(x == 0, -1, 0)
    b = 2
    while b < C:
        lvl = lvl + (x >= b).astype(jnp.int32)
        b *= 2
    return ti, si, lvl


def _hgrn_forget(f_pre, lb):
    s_pos, s_neg = _sigmoid_pair(f_pre)
    f = lb + (1.0 - lb) * s_pos
    return (1.0 - lb) * s_neg, jnp.log2(jnp.maximum(f, LOG_FLOOR))


def _hgrn_levels():
    ti, si, lvl = _pair_levels()
    return jnp.where(ti > si, lvl, -1), jnp.where(ti < si, lvl, -1)


def _hgrn_chunk(q, k, lf2, v, st, fwd, lvl):
    C, B = HGRN_CHUNK, HGRN_BLOCK
    assert C == 2 * B
    nv = C // SUBLANES
    sub = lax.broadcasted_iota(jnp.int32, (nv, SUBLANES, LANES), 1)

    p = lf2.reshape(nv, SUBLANES, LANES)
    for s in (1, 2, 4):
        if fwd:
            p = p + jnp.where(sub >= s, pltpu.roll(p, s, 1), 0.0)
        else:
            p = p + jnp.where(sub < SUBLANES - s, pltpu.roll(p, SUBLANES - s, 1), 0.0)

    a_mat = [jnp.zeros((B, B), F32), jnp.zeros((B, B), F32)]

    def add_level(a_mat, e, level):
        qe, ke = (q * e).astype(BF16), (k * e).astype(BF16)
        return [jnp.where(lvl == level, _dot_nt(qe[B * i:B * (i + 1)], ke[B * i:B * (i + 1)]), a_mat[i])
                for i in range(2)]

    def row(r):
        return jnp.broadcast_to(p[:, r:r + 1, :], p.shape)

    odd = (sub & 1) == 1
    if fwd:
        bnd = (jnp.where(odd, pltpu.roll(p, 1, 1), p), jnp.where(sub < 4, row(1), row(5)), row(3))
    else:
        bnd = (jnp.where(odd, p, pltpu.roll(p, SUBLANES - 1, 1)), jnp.where(sub < 4, row(2), row(6)), row(4))
    for level, b in enumerate(bnd):
        a_mat = add_level(a_mat, jnp.exp2(-jnp.abs(p - b)).reshape(C, LANES), level)

    p = p.reshape(C, LANES)
    m = SUBLANES
    while m < C:
        nb = C // (2 * m)
        p4 = p.reshape(nb, 2, m, LANES)
        lo, hi = p4[:, 0], p4[:, 1]
        if fwd:
            tot = lo[:, m - 1:m]
            e_lo, e_hi = tot - lo, hi
            p_lo, p_hi = lo, hi + tot
        else:
            tot = hi[:, 0:1]
            e_lo, e_hi = lo, tot - hi
            p_lo, p_hi = lo + tot, hi
        e = jnp.exp2(jnp.concatenate([e_lo[:, None], e_hi[:, None]], axis=1).reshape(C, LANES))
        p = jnp.concatenate([p_lo[:, None], p_hi[:, None]], axis=1).reshape(C, LANES)
        if m < B:
            a_mat = add_level(a_mat, e, int(np.log2(m)))
        else:
            qe, ke = (q * e).astype(BF16), (k * e).astype(BF16)
            cross = _dot_nt(qe[B:], ke[:B]) if fwd else _dot_nt(qe[:B], ke[B:])
        m *= 2
    zero = jnp.zeros((B, B), F32)
    if fwd:
        a_full = jnp.concatenate([jnp.concatenate([a_mat[0], zero], axis=1),
                                  jnp.concatenate([cross, a_mat[1]], axis=1)], axis=0)
    else:
        a_full = jnp.concatenate([jnp.concatenate([a_mat[0], cross], axis=1),
                                  jnp.concatenate([zero, a_mat[1]], axis=1)], axis=0)
    v16 = v.astype(BF16)
    o = jnp.sum(q * k, axis=-1, keepdims=True) * v + _dot(a_full.astype(BF16), v16)

    if st is not None:
        o = o + _dot_nt((q * jnp.exp2(p)).astype(BF16), st.astype(BF16))
    tot = p[C - 1:C] if fwd else p[0:1]
    k_end = (k * jnp.exp2(tot - p)).astype(BF16)
    st_new = _dot_tn(v16, k_end)
    if st is not None:
        st_new = st_new + st * jnp.exp2(tot)
    return o, st_new


def _hgrn_finish(o, gn, g_pre):
    return (_rms(o, gn) * _silu(g_pre)).astype(BF16)


def _hgrn_lat_kernel(lbf_ref, lbb_ref, gn_ref, q_ref, ff_ref, fb_ref, v_ref, g_ref, s0_ref, o_ref,
                     of_ref, ob_ref, st_ref, *, n_chunks):
    C = HGRN_CHUNK
    for d in range(2):
        st_ref[d] = s0_ref[0, 0, d, 0].T
    lbf = lbf_ref[...]
    lbb = lbb_ref[...]
    lvls = _hgrn_levels()

    def body(c, carry):
        rf = pl.multiple_of(c * C, C)
        rb = pl.multiple_of((n_chunks - 1 - c) * C, C)
        for d, r0, f_ref, lb, out_ref in ((0, rf, ff_ref, lbf, of_ref), (1, rb, fb_ref, lbb, ob_ref)):
            sl = pl.ds(r0, C)
            k, lf2 = _hgrn_forget(f_ref[sl, :], lb)
            out_ref[sl, :], st_ref[d] = _hgrn_chunk(_silu(q_ref[sl, :]), k, lf2, v_ref[sl, :], st_ref[d], d == 0,
                                                    lvls[d])
        return carry

    lax.fori_loop(0, n_chunks, body, 0, unroll=HGRN_UNROLL)
    o_ref[...] = _hgrn_finish(of_ref[...] + ob_ref[...], gn_ref[...], g_ref[...])


def _hgrn_lat(p, lp, layer, state_in):
    tok_spec = pl.BlockSpec((DEC_SEQ, LANES), lambda b, h: (b, h))
    vec_spec = pl.BlockSpec((None, 1, LANES), lambda b, h: (layer, 0, h))
    state_spec = pl.BlockSpec((1, 1, 2, 1, LANES, LANES), lambda b, h: (b, layer, 0, h, 0, 0))
    return pl.pallas_call(
        functools.partial(_hgrn_lat_kernel, n_chunks=DEC_SEQ // HGRN_CHUNK),
        out_shape=jax.ShapeDtypeStruct((N_LAT, 512), BF16), grid=(DEC_BATCH, HGRN_HEADS),
        in_specs=[vec_spec] * 3 + [tok_spec] * 5 + [state_spec], out_specs=tok_spec,
        scratch_shapes=[pltpu.VMEM((DEC_SEQ, LANES), F32), pltpu.VMEM((DEC_SEQ, LANES), F32),
                        pltpu.VMEM((2, LANES, LANES), F32)],
        compiler_params=_params(2), name="hgrn_lat",
    )(lp["lbf"], lp["lbb"], lp["gn"], p["hq"], p["hff"], p["hfb"], p["hi"], p["hg"], state_in)


def _lane_lo(shape):
    return (lax.broadcasted_iota(jnp.int32, shape, len(shape) - 1) & (LANES - 1)) < HEAD_DIM


def _with_ones(v):
    return jnp.where(_lane_lo(v.shape), v, 1.0)


def _softmax_unit(make_segs, sink_col, finish):
    segs = make_segs()
    mx = None
    for s, _ in segs:
        m = jnp.max(s, axis=-1, keepdims=True)
        mx = m if mx is None else jnp.maximum(mx, m)
    if sink_col is not None:
        mx = jnp.maximum(mx, sink_col)
    yield
    acc = None
    for s, v in segs:
        pv = _dot(jnp.exp2(s - mx).astype(BF16), v)
        acc = pv if acc is None else acc + pv
    den = jnp.where(_lane_lo(acc.shape), pltpu.roll(acc, HEAD_DIM, 1), acc)
    if sink_col is not None:
        den = den + jnp.exp2(sink_col - mx)
    finish(acc / den)


_DONE = object()


def _run_units(units, lookahead=1):
    pending = list(units)
    active = []
    for _ in range(lookahead - 1):
        if pending:
            u = pending.pop(0)
            next(u)
            active.insert(0, u)
    while pending or active:
        started = []
        if pending:
            u = pending.pop(0)
            next(u)
            started.append(u)
        n_adv = max(len(active) - (lookahead - 1), 1) if pending else len(active)
        keep = active[:len(active) - n_adv]
        active = started + keep + [a for a in active[len(active) - n_adv:] if next(a, _DONE) is not _DONE]


def _pair(lo, even, odd):
    return jnp.where(lo, even, pltpu.roll(odd, HEAD_DIM, 1))


def _gqa_unit(q_ref, g, tq, segs_fn, o_ref, sink_ref=None):
    lo = _lane_lo((tq, LANES))
    parts = []
    for j in range(2):
        qp = q_ref[:, 256 * g + 128 * j:256 * g + 128 * (j + 1)]
        zero = jnp.zeros_like(qp)
        parts += [jnp.where(lo, qp, zero), jnp.where(lo, zero, qp)]
    q4 = jnp.concatenate(parts, axis=0)
    sink_col = None
    if sink_ref is not None:
        sink_col = jnp.concatenate(
            [jnp.broadcast_to(sink_ref[0:1, 4 * g + i:4 * g + i + 1] * LOG2E, (tq, 1)) for i in range(4)], axis=0)

    def finish(pv):
        o_ref[:, 256 * g:256 * g + 128] = _pair(lo, pv[0:tq], pv[tq:2 * tq]).astype(BF16)
        o_ref[:, 256 * g + 128:256 * (g + 1)] = _pair(lo, pv[2 * tq:3 * tq], pv[3 * tq:4 * tq]).astype(BF16)

    yield from _softmax_unit(lambda: segs_fn(q4), sink_col, finish)


def _mla_pair_unit(q_ref, hp, tq, segs_fn, o_ref):
    outs = []
    subs = [_softmax_unit(functools.partial(segs_fn, h, q_ref[:, 128 * h:128 * (h + 1)]), None, outs.append)
            for h in (2 * hp, 2 * hp + 1)]
    while True:
        if [next(s, _DONE) for s in subs][0] is _DONE:
            break
        yield
    o_ref[:, 128 * hp:128 * (hp + 1)] = _pair(_lane_lo((tq, LANES)), outs[0], outs[1]).astype(BF16)


_ATTN_IN = ("sq", "sk", "sv", "mq", "mk", "mv", "gq", "gk", "gv")
_ATTN_WIDTH = dict(sq=512, sk=256, sv=256, mq=1024, mk=1024, mv=1024, gq=512, gk=256, gv=256)


def _attn_ctx_kernel(sink_ref, sq_ref, sk_ref, sv_ref, mq_ref, mk_ref, mv_ref, gq_ref, gk_ref, gv_ref,
                     ob_ref, oc_ref, od_ref):
    tq = SEQ
    units = []
    for g in range(N_KV):
        def swa_segs(q4, g=g):
            return [(_dot_nt(q4, sk_ref[:, 128 * g:128 * (g + 1)]), sv_ref[:, 128 * g:128 * (g + 1)])]

        def gqa_segs(q4, g=g):
            return [(_dot_nt(q4, gk_ref[:, 128 * g:128 * (g + 1)]), gv_ref[:, 128 * g:128 * (g + 1)])]

        units += [_gqa_unit(sq_ref, g, tq, swa_segs, ob_ref, sink_ref), _gqa_unit(gq_ref, g, tq, gqa_segs, od_ref)]

    def mla_segs(h, q_h):
        return [(_dot_nt(q_h, mk_ref[:, 128 * h:128 * (h + 1)]), mv_ref[:, 128 * h:128 * (h + 1)])]

    units += [_mla_pair_unit(mq_ref, hp, tq, mla_segs, oc_ref) for hp in range(MLA_HEADS // 2)]
    _run_units(units)


def _attn_ctx(lp, p, layer):
    def spec(w):
        return pl.BlockSpec((SEQ, w), lambda b: (b, 0))

    return pl.pallas_call(
        _attn_ctx_kernel,
        out_shape=[jax.ShapeDtypeStruct((N_CTX, 512), BF16)] * 3,
        grid=(BATCH,),
        in_specs=[_layer_spec((1, N_HEADS), layer)] + [spec(_ATTN_WIDTH[n]) for n in _ATTN_IN],
        out_specs=[spec(512)] * 3,
        compiler_params=_params(1),
        name="attn_ctx",
    )(lp["sink"], *[p[n] for n in _ATTN_IN])


def _dup_heads(x):
    lo = _lane_lo(x.shape)
    r = pltpu.roll(x, HEAD_DIM, 1)
    return jnp.where(lo, x, r), jnp.where(lo, r, x)


def _heads_with_ones(x):
    return _with_ones(x), _with_ones(pltpu.roll(x, HEAD_DIM, 1))


def _attn_lat_kernel(sink_ref, wk_ref, wuv_ref,
                     sq_ref, sk_ref, sv_ref, mq_ref, mk_ref, mv_ref, gq_ref, gk_ref, gv_ref,
                     csk_ref, csv_ref, cckv_ref, ckr_ref, cgk_ref, cgv_ref,
                     ob_ref, oc_ref, od_ref,
                     cs_k, cs_v, cm_k, cm_v, cg_k, cg_v):
    tq = TQ_LAT
    qi = pl.program_id(1)

    @pl.when(qi == 0)
    def _():
        for src, dst in ((csk_ref, cs_k), (cgk_ref, cg_k)):
            d0, d1 = _dup_heads(src[0, 0])
            dst[0] = d0.astype(BF16)
            dst[1] = d1.astype(BF16)
        for src, dst in ((csv_ref, cs_v), (cgv_ref, cg_v)):
            d0, d1 = _heads_with_ones(src[0, 0])
            dst[0] = d0.astype(BF16)
            dst[1] = d1.astype(BF16)
        ckv16 = cckv_ref[0, 0].astype(BF16)
        k = _dot(ckv16, wk_ref[0:MLA_KV_RANK, :]) + _dot(ckr_ref[0, 0].astype(BF16),
                                                           wk_ref[MLA_KV_RANK:MLA_KV_RANK + MLA_ROPE, :])
        cm_k[...] = k.astype(BF16)
        cm_v[...] = _with_ones(_dot(ckv16, wuv_ref[...])).astype(BF16)

    q0 = qi * tq
    win = tq + 2 * SWA_WINDOW
    start = pl.multiple_of(jnp.clip(q0 - SWA_WINDOW, 0, DEC_SEQ - win), SWA_WINDOW)
    rows = lax.broadcasted_iota(jnp.int32, (4 * tq, win), 0)
    qpos = q0 + (rows & (tq - 1))
    kpos = start + lax.broadcasted_iota(jnp.int32, (4 * tq, win), 1)
    band = jnp.abs(qpos - kpos) <= SWA_WINDOW
    units = []
    for g in range(N_KV):
        def swa_segs(q4, g=g):
            ks = sk_ref[pl.ds(start, win), 128 * g:128 * (g + 1)]
            vs = sv_ref[pl.ds(start, win), 128 * g:128 * (g + 1)]
            s_own = jnp.where(band, _dot_nt(q4, ks), MASK_VALUE)
            return [(_dot_nt(q4, cs_k[g]), cs_v[g]), (s_own, vs)]

        def gqa_segs(q4, g=g):
            return [(_dot_nt(q4, cg_k[g]), cg_v[g]),
                    (_dot_nt(q4, gk_ref[:, 128 * g:128 * (g + 1)]), gv_ref[:, 128 * g:128 * (g + 1)])]

        units += [_gqa_unit(sq_ref, g, tq, swa_segs, ob_ref, sink_ref), _gqa_unit(gq_ref, g, tq, gqa_segs, od_ref)]

    def mla_segs(h, q_h):
        cols = slice(128 * h, 128 * (h + 1))
        return [(_dot_nt(q_h, cm_k[:, cols]), cm_v[:, cols]), (_dot_nt(q_h, mk_ref[:, cols]), mv_ref[:, cols])]

    units += [_mla_pair_unit(mq_ref, hp, tq, mla_segs, oc_ref) for hp in range(MLA_HEADS // 2)]
    _run_units(units, lookahead=3)


def _attn_lat(lp, p, caches, layer):
    nq = DEC_SEQ // TQ_LAT

    def qspec(w):
        return pl.BlockSpec((TQ_LAT, w), lambda b, i: (b * nq + i, 0))

    def kspec(w):
        return pl.BlockSpec((DEC_SEQ, w), lambda b, i: (b, 0))

    def cspec(w):
        return pl.BlockSpec((1, 1, PAST_LEN, w), lambda b, i: (b, layer, 0, 0))

    is_q = dict(sq=True, mq=True, gq=True)
    in_specs = [_layer_spec((1, N_HEADS), layer), _layer_spec((256, 1024), layer),
                _layer_spec((MLA_KV_RANK, 1024), layer)]
    in_specs += [(qspec if is_q.get(n) else kspec)(_ATTN_WIDTH[n]) for n in _ATTN_IN]
    in_specs += [cspec(128), cspec(128), cspec(128), cspec(MLA_ROPE), cspec(128), cspec(128)]
    return pl.pallas_call(
        _attn_lat_kernel,
        out_shape=[jax.ShapeDtypeStruct((N_LAT, 512), BF16)] * 3,
        grid=(DEC_BATCH, nq),
        in_specs=in_specs,
        out_specs=[qspec(512)] * 3,
        scratch_shapes=[pltpu.VMEM((N_KV, PAST_LEN, LANES), BF16), pltpu.VMEM((N_KV, PAST_LEN, LANES), BF16),
                        pltpu.VMEM((PAST_LEN, 1024), BF16), pltpu.VMEM((PAST_LEN, 1024), BF16),
                        pltpu.VMEM((N_KV, PAST_LEN, LANES), BF16), pltpu.VMEM((N_KV, PAST_LEN, LANES), BF16)],
        compiler_params=_params(2),
        name="attn_lat",
    )(lp["sink"], lp["wk"], lp["wuv"], *[p[n] for n in _ATTN_IN], *caches)


MERGE_TILE = 512
MERGE_SUB = 256


def _merge_kernel(x_ref, mod_ref, oa_ref, ob_ref, oc_ref, od_ref, gates_ref, wb_ref, wo_ref, npost_ref,
                  nmpre_ref, nmpost_ref, w1_ref, w2_ref, y_ref):
    g1 = mod_ref[0, 2]
    sh2 = mod_ref[0, 3]
    sc2 = mod_ref[0, 4]
    g2 = mod_ref[0, 5]

    def sub_tile(r0):
        rows = slice(r0, r0 + MERGE_SUB)
        acc = None
        for n, o_ref in enumerate((oa_ref, ob_ref, oc_ref, od_ref)):
            br = _dot(o_ref[rows, :], wb_ref[n]) * gates_ref[rows, D_MODEL * n:D_MODEL * (n + 1)].astype(F32)
            acc = br if acc is None else acc + br
        y = _dot(acc.astype(BF16), wo_ref[...])
        yield
        x = x_ref[rows, :] + g1 * _rms(y, npost_ref[...])
        h = (_rms(x, nmpre_ref[...]) * (1.0 + sc2) + sh2).astype(BF16)
        a = jnp.maximum(_dot(h, w1_ref[...]), 0.0)
        yield
        y = _dot((a * a).astype(BF16), w2_ref[...])
        y_ref[rows, :] = x + g2 * _rms(y, nmpost_ref[...])

    _run_units([sub_tile(r0) for r0 in range(0, MERGE_TILE, MERGE_SUB)])


_MERGE_PARAMS = dict(w_branch=(N_BRANCH, BRANCH_W, D_MODEL), w_out=(D_MODEL, D_MODEL), npost=(1, D_MODEL),
                     nmpre=(1, D_MODEL), nmpost=(1, D_MODEL), w_mlp_in=(D_MODEL, D_FF), w_mlp_out=(D_FF, D_MODEL))


def _merge(x, mod4, o_a, o_b, o_c, o_d, gates, lp, is_ctx, layer):
    n_tok = x.shape[0]

    def tok(w):
        return pl.BlockSpec((MERGE_TILE, w), lambda i: (i, 0))

    in_specs = [tok(D_MODEL), _mod_spec(is_ctx, layer, MERGE_TILE), tok(512), tok(512), tok(512), tok(512), tok(4096)]
    in_specs += [_layer_spec(shape, layer) for shape in _MERGE_PARAMS.values()]
    return pl.pallas_call(
        _merge_kernel, out_shape=jax.ShapeDtypeStruct((n_tok, D_MODEL), F32), grid=(n_tok // MERGE_TILE,),
        in_specs=in_specs, out_specs=tok(D_MODEL), compiler_params=_params(1),
        name="merge_ctx" if is_ctx else "merge_lat",
    )(x, mod4, o_a, o_b, o_c, o_d, gates, *[lp[n] for n in _MERGE_PARAMS])


def _rope_tables():
    t = np.arange(DEC_SEQ)
    row, col = (t // GRID_W).astype(np.float32), (t % GRID_W).astype(np.float32)

    def pattern(d):
        nf = d // 4
        inv = (ROPE_BASE ** (-np.arange(nf, dtype=np.float32) / nf)).astype(np.float32)
        ang = np.concatenate([row[:, None] * inv, row[:, None] * inv, col[:, None] * inv, col[:, None] * inv], axis=1)
        cos, sin = np.cos(ang), np.sin(ang)
        second = (np.arange(d) % (2 * nf)) >= nf
        return cos, np.where(second, sin, 0.0), np.where(second, 0.0, -sin)

    out = np.zeros((9, DEC_SEQ, LANES), np.float32)
    out[[0, 3, 6]] = 1.0
    c, s1, s2 = pattern(HEAD_DIM)
    for a, tab in enumerate((c, s1, s2)):
        out[_TAB_64 + a] = np.tile(tab, (1, LANES // HEAD_DIM))
    c, s1, s2 = pattern(MLA_ROPE)
    for a, tab in enumerate((c, s1, s2)):
        out[_TAB_MLAQ + a][:, MLA_NOPE:MLA_NOPE + MLA_ROPE] = tab
        out[_TAB_KR + a][:, :MLA_ROPE] = tab
    return out


_ROPE_TABLES = _rope_tables()


def _block_mean_matrix(n):
    i = np.arange(n)
    return ((i[:, None] // HEAD_DIM) == (i[None, :] // HEAD_DIM)).astype(np.float32) / HEAD_DIM


def _mla_weight_layouts(w_uq, w_ukv):
    dq = MLA_NOPE + MLA_ROPE
    wq = w_uq.reshape(DEPTH, MLA_Q_RANK, MLA_HEADS, dq)
    wq = jnp.pad(wq, ((0, 0), (0, 0), (0, 0), (0, LANES - dq))).reshape(DEPTH, MLA_Q_RANK, MLA_HEADS * LANES)
    wkv = w_ukv.reshape(DEPTH, MLA_KV_RANK, MLA_HEADS, MLA_NOPE + MLA_V)
    wk = jnp.pad(wkv[..., :MLA_NOPE], ((0, 0), (0, 0), (0, 0), (0, LANES - MLA_NOPE)))
    wk = wk.reshape(DEPTH, MLA_KV_RANK, MLA_HEADS * LANES)
    place = np.zeros((LANES, MLA_HEADS, LANES), np.float32)
    for j in range(MLA_ROPE):
        place[j, :, MLA_NOPE + j] = 1.0
    place = jnp.broadcast_to(jnp.asarray(place.reshape(1, LANES, MLA_HEADS * LANES)), (DEPTH, LANES, MLA_HEADS * LANES))
    w_k = jnp.concatenate([wk, place], axis=1)
    w_uv = jnp.pad(wkv[..., MLA_NOPE:], ((0, 0), (0, 0), (0, 0), (0, LANES - MLA_V)))
    w_uv = w_uv.reshape(DEPTH, MLA_KV_RANK, MLA_HEADS * LANES)
    return wq.astype(BF16), w_k.astype(BF16), w_uv.astype(BF16)


def _lower_bounds(p):
    s = jax.nn.softmax(p.astype(F32), axis=0)
    return jnp.cumsum(s, axis=0) - s[0]


def kernel(x_prompt, x_sample, state_hgrn, cache_swa_k, cache_swa_v, cache_mla_ckv, cache_mla_kr, cache_gqa_k, cache_gqa_v, c, c_ctx, w_ada, b_ada, norm_mix_pre, norm_mix_post, norm_mlp_pre, norm_mlp_post, w_in, hgrn_lb_fwd, hgrn_lb_bwd, hgrn_norm, swa_sink, mla_q_norm, mla_kv_norm, mla_w_uq, mla_w_ukv, gqa_q_norm, gqa_k_norm, w_branch, w_out, w_mlp_in, w_mlp_out):
    tabs = jnp.asarray(_ROPE_TABLES)
    b512 = jnp.asarray(_block_mean_matrix(512)).astype(BF16)

    def rows(a):
        return a.reshape(DEPTH, 1, -1)

    w_uq, w_k, w_uv = _mla_weight_layouts(mla_w_uq, mla_w_ukv)
    w_a, w_kr, w_b = _split_w_in(w_in)
    lp = dict(
        npre=rows(norm_mix_pre), wa=w_a, wkr=w_kr, wb=w_b, mlaq_g=rows(mla_q_norm), mlakv_g=rows(mla_kv_norm),
        wuq=w_uq, wk=w_k, wuv=w_uv, gq_g=rows(jnp.tile(gqa_q_norm, (1, N_HEADS))),
        gk_g=rows(jnp.tile(gqa_k_norm, (1, N_KV))), b512=b512, b128=b512[:128, :128],
        lbf=rows(_lower_bounds(hgrn_lb_fwd)), lbb=rows(_lower_bounds(hgrn_lb_bwd)), gn=rows(hgrn_norm),
        sink=rows(swa_sink),
        w_branch=w_branch.astype(BF16), w_out=w_out.astype(BF16), npost=rows(norm_mix_post),
        nmpre=rows(norm_mlp_pre), nmpost=rows(norm_mlp_post),
        w_mlp_in=w_mlp_in.astype(BF16), w_mlp_out=w_mlp_out.astype(BF16))

    cond8 = jnp.zeros((SUBLANES, D_MODEL), F32).at[0].set(c_ctx).at[1:1 + DEC_BATCH].set(c)
    mod4 = _ada(cond8, w_ada, b_ada).reshape(DEPTH, SUBLANES, N_MOD, 1, D_MODEL)

    caches_in = (cache_swa_k.reshape(DEC_BATCH, DEPTH, PAST_LEN, 128),
                 cache_swa_v.reshape(DEC_BATCH, DEPTH, PAST_LEN, 128), cache_mla_ckv, cache_mla_kr,
                 cache_gqa_k.reshape(DEC_BATCH, DEPTH, PAST_LEN, 128),
                 cache_gqa_v.reshape(DEC_BATCH, DEPTH, PAST_LEN, 128))

    x_ctx = x_prompt.reshape(N_CTX, D_MODEL)
    x_lat = x_sample.reshape(N_LAT, D_MODEL)
    pc = None
    for l in range(DEPTH):
        pc = _proj(x_ctx, mod4, lp, True, layer=l, prev=pc)
        ob, oc, od = _attn_ctx(lp, pc, l)
        x_ctx = _merge(x_ctx, mod4, pc["oa"], ob, oc, od, pc["gates"], lp, True, l)

        pq = _proj(x_lat, mod4, lp, False, layer=l, tabs=tabs)
        oa = _hgrn_lat(pq, lp, l, state_hgrn)
        ob, oc, od = _attn_lat(lp, pq, caches_in, l)
        x_lat = _merge(x_lat, mod4, oa, ob, oc, od, pq["gates"], lp, False, l)

    kv_shape = (BATCH, DEPTH, SEQ, N_KV, HEAD_DIM)
    return (x_ctx.reshape(BATCH, SEQ, D_MODEL), x_lat.reshape(DEC_BATCH, DEC_SEQ, D_MODEL), pc["st"],
            pc["skc"].reshape(kv_shape), pc["svc"].reshape(kv_shape), pc["mckv"], pc["mkr"],
            pc["gkc"].reshape(kv_shape), pc["gvc"].reshape(kv_shape))
```

```python
import functools

import jax
import jax.numpy as jnp
import numpy as np
from jax import lax
from jax.experimental import pallas as pl
from jax.experimental.pallas import tpu as pltpu

D_MODEL = 1024
BATCH, SEQ = 32, 256
DEC_BATCH, DEC_SEQ = 2, 2048
DEPTH = 2
PAST_LEN = 512
GRID_W = 64
ROPE_BASE = 10000.0
NORM_EPS = 1e-6
MASK_VALUE = -1e30
LOG_FLOOR = 1e-30
N_MOD = 6
N_BRANCH = 4
BRANCH_W = 512
D_FF = 4 * D_MODEL
HGRN_HEADS, HGRN_DK = 4, 128
SWA_WINDOW = 128
MLA_HEADS, MLA_Q_RANK, MLA_KV_RANK, MLA_NOPE, MLA_ROPE, MLA_V = 8, 256, 128, 64, 32, 64
HEAD_DIM = 64
N_HEADS, N_KV = 8, 2

N_CTX = BATCH * SEQ
N_LAT = DEC_BATCH * DEC_SEQ

LANES = 128
SUBLANES = 8
VMEM_LIMIT_BYTES = 56 * 1024 * 1024

PROJ_TILE = 256
PROJ_SUB = 256
HGRN_CHUNK = 256
HGRN_UNROLL = 4
TQ_LAT = 128

BF16 = jnp.bfloat16
F32 = jnp.float32

_W_A_COLS = 3712
_W_B_COLS = 8608 - _W_A_COLS - MLA_ROPE
_C_HGRN = ("wa", 0)
_C_SWA_Q = ("wa", 2560)
_C_SWA_KV = ("wa", 3072)
_C_MLA_CQ = ("wa", 3328)
_C_MLA_CKV = ("wa", 3584)
_C_MLA_KR = ("wkr", 0)
_C_GQA_Q = ("wb", 0)
_C_GQA_KV = ("wb", 512)
_C_GATES = ("wb", 768)


def _split_w_in(w):
    w16 = w.astype(BF16)
    return w16, w16, w16[..., _W_A_COLS + MLA_ROPE:]


_W_IN_BLOCK_INDEX = dict(wkr=(0, _W_A_COLS // LANES))


def _sigmoid_pair(x):
    a = jnp.exp(-jnp.abs(x))
    r = 1.0 / (1.0 + a)
    ar = a * r
    pos = x >= 0
    return jnp.where(pos, r, ar), jnp.where(pos, ar, r)


def _sigmoid(x):
    return _sigmoid_pair(x)[0]


def _silu(x):
    return x * _sigmoid(x)


def _rms(x, gain):
    return x * lax.rsqrt(jnp.mean(x * x, axis=-1, keepdims=True) + NORM_EPS) * gain


def _dot(a, b):
    return jnp.dot(a, b, preferred_element_type=F32)


def _dot_nt(a, b):
    return lax.dot_general(a, b, (((1,), (1,)), ((), ())), preferred_element_type=F32)


def _dot_tn(a, b):
    return lax.dot_general(a, b, (((0,), (0,)), ((), ())), preferred_element_type=F32)


def _tile_lanes(t, width):
    reps = width // LANES
    return t if reps == 1 else jnp.concatenate([t] * reps, axis=1)


def _rope(x, tab_ref, base, shift, rows):
    w = x.shape[1]
    c = _tile_lanes(tab_ref[base, rows, :], w)
    s1 = _tile_lanes(tab_ref[base + 1, rows, :], w)
    s2 = _tile_lanes(tab_ref[base + 2, rows, :], w)
    return x * c + pltpu.roll(x, shift, 1) * s1 + pltpu.roll(x, w - shift, 1) * s2


def _head_rms(x, bmat_ref, gain):
    ms = _dot((x * x).astype(BF16), bmat_ref[...])
    return x * lax.rsqrt(ms + NORM_EPS) * gain


def _const_spec(shape):
    nd = len(shape)
    return pl.BlockSpec(shape, lambda *_: (0,) * nd, pipeline_mode=pl.Buffered(1))


def _layer_spec(shape, layer, block_index=None):
    idx = (layer,) + tuple(block_index or (0,) * len(shape))
    return pl.BlockSpec((None,) + tuple(shape), lambda *_: idx, pipeline_mode=pl.Buffered(1))


_ANY_SPEC = pl.BlockSpec(memory_space=pl.ANY)


def _params(n_grid_dims):
    return pltpu.CompilerParams(dimension_semantics=("arbitrary",) * n_grid_dims, vmem_limit_bytes=VMEM_LIMIT_BYTES)


_ADA_TN = 2048


def _ada_kernel(c_ref, w_ref, b_ref, o_ref):
    s = _silu(c_ref[...]).astype(BF16)
    o_ref[0] = _dot(s, w_ref[0].astype(BF16)) + b_ref[0]


def _ada(cond8, w_ada, b_ada):
    n = N_MOD * D_MODEL
    return pl.pallas_call(
        _ada_kernel,
        out_shape=jax.ShapeDtypeStruct((DEPTH, SUBLANES, n), F32),
        grid=(DEPTH, n // _ADA_TN),
        in_specs=[
            pl.BlockSpec((SUBLANES, D_MODEL), lambda l, j: (0, 0)),
            pl.BlockSpec((1, D_MODEL, _ADA_TN), lambda l, j: (l, 0, j)),
            pl.BlockSpec((1, 1, _ADA_TN), lambda l, j: (l, 0, j)),
        ],
        out_specs=pl.BlockSpec((1, SUBLANES, _ADA_TN), lambda l, j: (l, 0, j)),
        compiler_params=_params(2),
        name="ada",
    )(cond8, w_ada, b_ada.reshape(DEPTH, 1, n))


def _mod_spec(is_ctx, layer, tile):
    if is_ctx:
        return pl.BlockSpec((None, 1, N_MOD, 1, D_MODEL), lambda i: (layer, 0, 0, 0, 0))
    per_seq = DEC_SEQ // tile
    return pl.BlockSpec((None, 1, N_MOD, 1, D_MODEL), lambda i: (layer, 1 + i // per_seq, 0, 0, 0))


LOG2E = float(np.log2(np.e))
_SCALE_64 = HEAD_DIM ** -0.5 * LOG2E
_SCALE_MLA = (MLA_NOPE + MLA_ROPE) ** -0.5 * LOG2E
_TAB_64, _TAB_MLAQ, _TAB_KR = 0, 3, 6

_HGRN_PRE = ("hq", "hff", "hfb", "hi", "hg")
_ATTN_OUTS = (
    ("sq", 512, BF16), ("sk", 256, BF16), ("sv", 256, BF16),
    ("mq", 1024, BF16), ("mk", 1024, BF16), ("mv", 1024, BF16),
    ("gq", 512, BF16), ("gk", 256, BF16), ("gv", 256, BF16), ("gates", 4096, BF16))
_CACHE_OUTS = (("skc", 128), ("svc", 128), ("mckv", 128), ("mkr", MLA_ROPE), ("gkc", 128), ("gvc", 128))
_PROJ_PARAMS = dict(npre=(1, D_MODEL), wa=(D_MODEL, _W_A_COLS), wkr=(D_MODEL, LANES), wb=(D_MODEL, _W_B_COLS),
                    mlaq_g=(1, MLA_Q_RANK), mlakv_g=(1, MLA_KV_RANK),
                    wuq=(MLA_Q_RANK, 1024), wk=(256, 1024), wuv=(MLA_KV_RANK, 1024), gq_g=(1, 512), gk_g=(1, 128))
_HGRN_PARAMS = dict(lbf=(1, 512), lbb=(1, 512), gn=(1, 512))
_PROJ_CONSTS = tuple(_PROJ_PARAMS) + ("b512", "b128")
_HGRN_CONSTS = tuple(_HGRN_PARAMS)


def _proj_out_names(is_ctx):
    attn = [n for n, _, _ in _ATTN_OUTS]
    if is_ctx:
        return ["oa"] + attn + [n for n, _ in _CACHE_OUTS] + ["st"]
    return list(_HGRN_PRE) + attn


def _proj_kernel(*refs, is_ctx, n_alias):
    names = ["x", "mod"] + list(_PROJ_CONSTS) + (list(_HGRN_CONSTS) if is_ctx else ["tab"])
    r = dict(zip(names, refs))
    pos = len(names) + n_alias
    for name in _proj_out_names(is_ctx) + (["hbuf"] if is_ctx else []):
        r[name] = refs[pos]
        pos += 1

    sh1 = r["mod"][0, 0]
    sc1 = r["mod"][0, 1]
    for s in range(PROJ_TILE // PROJ_SUB):
        for _ in _proj_sub_tile(r, s, sh1, sc1, is_ctx):
            pass


def _proj_sub_tile(r, s, sh1, sc1, is_ctx):
    rows = slice(PROJ_SUB * s, PROJ_SUB * (s + 1))
    h = (_rms(r["x"][rows, :], r["npre"][...]) * (1.0 + sc1) + sh1).astype(BF16)

    def seg(where, w, skip=0):
        name, a = where
        return _dot(h, r[name][:, a + skip:a + skip + w])

    def rope(v, base, shift):
        return v if is_ctx else _rope(v, r["tab"], base, shift, rows)

    for j, name in enumerate(_HGRN_PRE):
        u = seg(_C_HGRN, 512, skip=512 * j)
        if is_ctx:
            r["hbuf"][s, j] = u
        else:
            r[name][rows, :] = u
    yield

    if is_ctx:
        assert HGRN_CHUNK == PROJ_SUB
        lvls = _hgrn_levels()
        for hd in range(HGRN_HEADS):
            cols = slice(LANES * hd, LANES * (hd + 1))
            q = _silu(r["hbuf"][s, 0, :, cols])
            v = r["hbuf"][s, 3, :, cols]
            o = None
            for d, lb in ((0, r["lbf"]), (1, r["lbb"])):
                k, lf2 = _hgrn_forget(r["hbuf"][s, 1 + d, :, cols], lb[:, cols])
                o_d, st = _hgrn_chunk(q, k, lf2, v, None, d == 0, lvls[d])
                r["st"][s, 0, d, hd] = st.T
                o = o_d if o is None else o + o_d
            r["oa"][rows, cols] = _hgrn_finish(o, r["gn"][:, cols], r["hbuf"][s, 4, :, cols])
        yield

    def dup16(k):
        return jnp.concatenate([d.astype(BF16) for d in _dup_heads(k)], axis=1)

    def vals16(v):
        return dup16(v) if is_ctx else jnp.concatenate([d.astype(BF16) for d in _heads_with_ones(v)], axis=1)

    def mla_vals16(x):
        if is_ctx:
            return jnp.where(_lane_lo(x.shape), x, pltpu.roll(x, HEAD_DIM, 1)).astype(BF16)
        return _with_ones(x).astype(BF16)

    r["sq"][rows, :] = (rope(seg(_C_SWA_Q, 512), _TAB_64, 16) * _SCALE_64).astype(BF16)
    kv = seg(_C_SWA_KV, 256)
    k, v = kv[:, :128], kv[:, 128:]
    r["sk"][rows, :] = dup16(rope(k, _TAB_64, 16))
    r["sv"][rows, :] = vals16(v)
    if is_ctx:
        r["skc"][s, 0] = k
        r["svc"][s, 0] = v

    cq = _rms(seg(_C_MLA_CQ, 256), r["mlaq_g"][...]).astype(BF16)
    q = _dot(cq, r["wuq"][...])
    r["mq"][rows, :] = (rope(q, _TAB_MLAQ, 8) * _SCALE_MLA).astype(BF16)
    ckv = _rms(seg(_C_MLA_CKV, 128), r["mlakv_g"][...])
    kr = seg(_C_MLA_KR, 128)
    if is_ctx:
        r["mckv"][s, 0] = ckv
        r["mkr"][s, 0] = kr[:, :MLA_ROPE]
    kr_used = rope(kr, _TAB_KR, 8)
    ckv16 = ckv.astype(BF16)
    r["mk"][rows, :] = _dot(jnp.concatenate([ckv16, kr_used.astype(BF16)], axis=1), r["wk"][...]).astype(BF16)
    r["mv"][rows, :] = mla_vals16(_dot(ckv16, r["wuv"][...]))

    gq = _head_rms(seg(_C_GQA_Q, 512), r["b512"], r["gq_g"][...])
    r["gq"][rows, :] = (rope(gq, _TAB_64, 16) * _SCALE_64).astype(BF16)
    kv = seg(_C_GQA_KV, 256)
    k, v = _head_rms(kv[:, :128], r["b128"], r["gk_g"][...]), kv[:, 128:]
    r["gk"][rows, :] = dup16(rope(k, _TAB_64, 16))
    r["gv"][rows, :] = vals16(v)
    if is_ctx:
        r["gkc"][s, 0] = k
        r["gvc"][s, 0] = v

    for j in range(N_BRANCH):
        g = _sigmoid(seg(_C_GATES, D_MODEL, skip=D_MODEL * j))
        r["gates"][rows, D_MODEL * j:D_MODEL * (j + 1)] = g.astype(BF16)


def _proj(x, mod4, lp, is_ctx, layer=0, tabs=None, prev=None):
    n_tok = x.shape[0]
    params = dict(_PROJ_PARAMS, **(_HGRN_PARAMS if is_ctx else {}))
    n_sub = PROJ_TILE // PROJ_SUB
    in_specs = [pl.BlockSpec((PROJ_TILE, D_MODEL), lambda i: (i, 0)), _mod_spec(is_ctx, layer, PROJ_TILE)]
    in_specs += [_layer_spec(params[n], layer, _W_IN_BLOCK_INDEX.get(n)) for n in _PROJ_PARAMS]
    in_specs += [_const_spec((512, 512)), _const_spec((128, 128))]
    args = [x, mod4] + [lp[n] for n in _PROJ_PARAMS] + [lp["b512"], lp["b128"]]
    if is_ctx:
        in_specs += [_layer_spec(params[n], layer) for n in _HGRN_PARAMS]
        args += [lp[n] for n in _HGRN_PARAMS]
    if not is_ctx:
        in_specs.append(pl.BlockSpec((9, PROJ_TILE, LANES), lambda i: (0, i % (DEC_SEQ // PROJ_TILE), 0)))
        args.append(tabs)

    def tok(w):
        return pl.BlockSpec((PROJ_TILE, w), lambda i: (i, 0))

    specs = {n: (tok(w), jax.ShapeDtypeStruct((n_tok, w), dt)) for n, w, dt in _ATTN_OUTS}
    if is_ctx:
        specs["oa"] = (tok(512), jax.ShapeDtypeStruct((n_tok, 512), BF16))
        for n, w in _CACHE_OUTS:
            specs[n] = (pl.BlockSpec((n_sub, 1, SEQ, w), lambda i: (i, layer, 0, 0)),
                        jax.ShapeDtypeStruct((BATCH, DEPTH, SEQ, w), F32))
        specs["st"] = (pl.BlockSpec((n_sub, 1, 2, HGRN_HEADS, LANES, LANES), lambda i: (i, layer, 0, 0, 0, 0)),
                       jax.ShapeDtypeStruct((BATCH, DEPTH, 2, HGRN_HEADS, LANES, LANES), F32))
    else:
        for n in _HGRN_PRE:
            specs[n] = (tok(512), jax.ShapeDtypeStruct((n_tok, 512), F32))
    out_names = _proj_out_names(is_ctx)
    aliases = {}
    n_alias = 0
    if prev is not None:
        for n in [n for n, _ in _CACHE_OUTS] + ["st"]:
            aliases[len(args)] = out_names.index(n)
            in_specs.append(_ANY_SPEC)
            args.append(prev[n])
            n_alias += 1
    outs = pl.pallas_call(
        functools.partial(_proj_kernel, is_ctx=is_ctx, n_alias=n_alias),
        out_shape=[specs[n][1] for n in out_names], grid=(n_tok // PROJ_TILE,), in_specs=in_specs,
        out_specs=[specs[n][0] for n in out_names],
        scratch_shapes=[pltpu.VMEM((n_sub, len(_HGRN_PRE), PROJ_SUB, 512), F32)] if is_ctx else [],
        input_output_aliases=aliases, compiler_params=_params(1),
        name="proj_ctx" if is_ctx else "proj_lat",
    )(*args)
    return dict(zip(out_names, outs))


def _pair_levels():
    C = HGRN_CHUNK
    ti = lax.broadcasted_iota(jnp.int32, (C, C), 0)
    si = lax.broadcasted_iota(jnp.int32, (C, C), 1)
    x = ti ^ si
    lvl = jnp.where(x == 0, -1, 0)
    b = 2
    while b < C:
        lvl = lvl + (x >= b).astype(jnp.int32)
        b *= 2
    return ti, si, lvl


def _hgrn_forget(f_pre, lb):
    s_pos, s_neg = _sigmoid_pair(f_pre)
    f = lb + (1.0 - lb) * s_pos
    return (1.0 - lb) * s_neg, jnp.log2(jnp.maximum(f, LOG_FLOOR))


def _hgrn_levels():
    ti, si, lvl = _pair_levels()
    return jnp.where(ti > si, lvl, -1), jnp.where(ti < si, lvl, -1)


def _hgrn_chunk(q, k, lf2, v, st, fwd, lvl):
    C = HGRN_CHUNK
    nv = C // SUBLANES
    sub = lax.broadcasted_iota(jnp.int32, (nv, SUBLANES, LANES), 1)

    p = lf2.reshape(nv, SUBLANES, LANES)
    for s in (1, 2, 4):
        if fwd:
            p = p + jnp.where(sub >= s, pltpu.roll(p, s, 1), 0.0)
        else:
            p = p + jnp.where(sub < SUBLANES - s, pltpu.roll(p, SUBLANES - s, 1), 0.0)

    a_mat = jnp.zeros((C, C), F32)
    q16, k16 = q.astype(BF16), k.astype(BF16)

    def add_level(a_mat, e, level):
        e16 = e.astype(BF16)
        return jnp.where(lvl == level, _dot_nt(q16 * e16, k16 * e16), a_mat)

    def row(r):
        return jnp.broadcast_to(p[:, r:r + 1, :], p.shape)

    odd = (sub & 1) == 1
    if fwd:
        bnd = (jnp.where(odd, pltpu.roll(p, 1, 1), p), jnp.where(sub < 4, row(1), row(5)), row(3))
    else:
        bnd = (jnp.where(odd, p, pltpu.roll(p, SUBLANES - 1, 1)), jnp.where(sub < 4, row(2), row(6)), row(4))
    for level, b in enumerate(bnd):
        a_mat = add_level(a_mat, jnp.exp2(-jnp.abs(p - b)).reshape(C, LANES), level)

    p = p.reshape(C, LANES)
    m = SUBLANES
    while m < C:
        nb = C // (2 * m)
        p4 = p.reshape(nb, 2, m, LANES)
        lo, hi = p4[:, 0], p4[:, 1]
        if fwd:
            tot = lo[:, m - 1:m]
            e_lo, e_hi = tot - lo, hi
            p_lo, p_hi = lo, hi + tot
        else:
            tot = hi[:, 0:1]
            e_lo, e_hi = lo, tot - hi
            p_lo, p_hi = lo + tot, hi
        e = jnp.exp2(jnp.concatenate([e_lo[:, None], e_hi[:, None]], axis=1).reshape(C, LANES))
        p = jnp.concatenate([p_lo[:, None], p_hi[:, None]], axis=1).reshape(C, LANES)
        a_mat = add_level(a_mat, e, int(np.log2(m)))
        m *= 2
    v16 = v.astype(BF16)
    o = jnp.sum(q * k, axis=-1, keepdims=True) * v + _dot(a_mat.astype(BF16), v16)

    if st is not None:
        o = o + _dot_nt((q * jnp.exp2(p)).astype(BF16), st.astype(BF16))
    tot = p[C - 1:C] if fwd else p[0:1]
    k_end = (k * jnp.exp2(tot - p)).astype(BF16)
    st_new = _dot_tn(v16, k_end)
    if st is not None:
        st_new = st_new + st * jnp.exp2(tot)
    return o, st_new


def _hgrn_finish(o, gn, g_pre):
    return (_rms(o, gn) * _silu(g_pre)).astype(BF16)


def _hgrn_lat_kernel(lbf_ref, lbb_ref, gn_ref, q_ref, ff_ref, fb_ref, v_ref, g_ref, s0_ref, o_ref,
                     of_ref, ob_ref, st_ref, *, n_chunks):
    C = HGRN_CHUNK
    for d in range(2):
        st_ref[d] = s0_ref[0, 0, d, 0].T
    lbf = lbf_ref[...]
    lbb = lbb_ref[...]
    lvls = _hgrn_levels()

    def body(c, carry):
        rf = pl.multiple_of(c * C, C)
        rb = pl.multiple_of((n_chunks - 1 - c) * C, C)
        for d, r0, f_ref, lb, out_ref in ((0, rf, ff_ref, lbf, of_ref), (1, rb, fb_ref, lbb, ob_ref)):
            sl = pl.ds(r0, C)
            k, lf2 = _hgrn_forget(f_ref[sl, :], lb)
            out_ref[sl, :], st_ref[d] = _hgrn_chunk(_silu(q_ref[sl, :]), k, lf2, v_ref[sl, :], st_ref[d], d == 0,
                                                    lvls[d])
        return carry

    lax.fori_loop(0, n_chunks, body, 0, unroll=HGRN_UNROLL)
    o_ref[...] = _hgrn_finish(of_ref[...] + ob_ref[...], gn_ref[...], g_ref[...])


def _hgrn_lat(p, lp, layer, state_in):
    tok_spec = pl.BlockSpec((DEC_SEQ, LANES), lambda b, h: (b, h))
    vec_spec = pl.BlockSpec((None, 1, LANES), lambda b, h: (layer, 0, h))
    state_spec = pl.BlockSpec((1, 1, 2, 1, LANES, LANES), lambda b, h: (b, layer, 0, h, 0, 0))
    return pl.pallas_call(
        functools.partial(_hgrn_lat_kernel, n_chunks=DEC_SEQ // HGRN_CHUNK),
        out_shape=jax.ShapeDtypeStruct((N_LAT, 512), BF16), grid=(DEC_BATCH, HGRN_HEADS),
        in_specs=[vec_spec] * 3 + [tok_spec] * 5 + [state_spec], out_specs=tok_spec,
        scratch_shapes=[pltpu.VMEM((DEC_SEQ, LANES), F32), pltpu.VMEM((DEC_SEQ, LANES), F32),
                        pltpu.VMEM((2, LANES, LANES), F32)],
        compiler_params=_params(2), name="hgrn_lat",
    )(lp["lbf"], lp["lbb"], lp["gn"], p["hq"], p["hff"], p["hfb"], p["hi"], p["hg"], state_in)


def _lane_lo(shape):
    return (lax.broadcasted_iota(jnp.int32, shape, len(shape) - 1) & (LANES - 1)) < HEAD_DIM


def _with_ones(v):
    return jnp.where(_lane_lo(v.shape), v, 1.0)


def _softmax_unit(make_segs, sink_col, finish, ones_col):
    segs = make_segs()
    mx = None
    for s, _ in segs:
        m = jnp.max(s, axis=-1, keepdims=True)
        mx = m if mx is None else jnp.maximum(mx, m)
    if sink_col is not None:
        mx = jnp.maximum(mx, sink_col)
    yield
    acc = den = None
    for s, v in segs:
        p = jnp.exp2(s - mx).astype(BF16)
        pv = _dot(p, v)
        acc = pv if acc is None else acc + pv
        if not ones_col:
            d = _dot(p, jnp.ones((v.shape[0], LANES), BF16))
            den = d if den is None else den + d
    if ones_col:
        den = jnp.where(_lane_lo(acc.shape), pltpu.roll(acc, HEAD_DIM, 1), acc)
    if sink_col is not None:
        den = den + jnp.exp2(sink_col - mx)
    finish(acc / den)


_DONE = object()


def _run_units(units, lookahead=1):
    pending = list(units)
    active = []
    for _ in range(lookahead - 1):
        if pending:
            u = pending.pop(0)
            next(u)
            active.insert(0, u)
    while pending or active:
        started = []
        if pending:
            u = pending.pop(0)
            next(u)
            started.append(u)
        n_adv = max(len(active) - (lookahead - 1), 1) if pending else len(active)
        keep = active[:len(active) - n_adv]
        active = started + keep + [a for a in active[len(active) - n_adv:] if next(a, _DONE) is not _DONE]


def _pair(lo, even, odd, ones_col):
    return jnp.where(lo, even, pltpu.roll(odd, HEAD_DIM, 1) if ones_col else odd)


def _gqa_unit(q_ref, g, tq, segs_fn, o_ref, ones_col, sink_ref=None):
    lo = _lane_lo((tq, LANES))
    parts = []
    for j in range(2):
        qp = q_ref[:, 256 * g + 128 * j:256 * g + 128 * (j + 1)]
        zero = jnp.zeros_like(qp)
        parts += [jnp.where(lo, qp, zero), jnp.where(lo, zero, qp)]
    q4 = jnp.concatenate(parts, axis=0)
    sink_col = None
    if sink_ref is not None:
        sink_col = jnp.concatenate(
            [jnp.broadcast_to(sink_ref[0:1, 4 * g + i:4 * g + i + 1] * LOG2E, (tq, 1)) for i in range(4)], axis=0)

    def finish(pv):
        o_ref[:, 256 * g:256 * g + 128] = _pair(lo, pv[0:tq], pv[tq:2 * tq], ones_col).astype(BF16)
        o_ref[:, 256 * g + 128:256 * (g + 1)] = _pair(lo, pv[2 * tq:3 * tq], pv[3 * tq:4 * tq], ones_col).astype(BF16)

    yield from _softmax_unit(lambda: segs_fn(q4), sink_col, finish, ones_col)


def _mla_pair_unit(q_ref, hp, tq, segs_fn, o_ref, ones_col):
    outs = []
    subs = [_softmax_unit(functools.partial(segs_fn, h, q_ref[:, 128 * h:128 * (h + 1)]), None, outs.append, ones_col)
            for h in (2 * hp, 2 * hp + 1)]
    while True:
        if [next(s, _DONE) for s in subs][0] is _DONE:
            break
        yield
    o_ref[:, 128 * hp:128 * (hp + 1)] = _pair(_lane_lo((tq, LANES)), outs[0], outs[1], ones_col).astype(BF16)


_ATTN_IN = ("sq", "sk", "sv", "mq", "mk", "mv", "gq", "gk", "gv")
_ATTN_WIDTH = dict(sq=512, sk=256, sv=256, mq=1024, mk=1024, mv=1024, gq=512, gk=256, gv=256)


def _attn_ctx_kernel(sink_ref, sq_ref, sk_ref, sv_ref, mq_ref, mk_ref, mv_ref, gq_ref, gk_ref, gv_ref,
                     ob_ref, oc_ref, od_ref):
    tq = SEQ
    units = []
    for g in range(N_KV):
        def swa_segs(q4, g=g):
            return [(_dot_nt(q4, sk_ref[:, 128 * g:128 * (g + 1)]), sv_ref[:, 128 * g:128 * (g + 1)])]

        def gqa_segs(q4, g=g):
            return [(_dot_nt(q4, gk_ref[:, 128 * g:128 * (g + 1)]), gv_ref[:, 128 * g:128 * (g + 1)])]

        units += [_gqa_unit(sq_ref, g, tq, swa_segs, ob_ref, False, sink_ref),
                  _gqa_unit(gq_ref, g, tq, gqa_segs, od_ref, False)]

    def mla_segs(h, q_h):
        return [(_dot_nt(q_h, mk_ref[:, 128 * h:128 * (h + 1)]), mv_ref[:, 128 * h:128 * (h + 1)])]

    units += [_mla_pair_unit(mq_ref, hp, tq, mla_segs, oc_ref, False) for hp in range(MLA_HEADS // 2)]
    _run_units(units)


def _attn_ctx(lp, p, layer):
    def spec(w):
        return pl.BlockSpec((SEQ, w), lambda b: (b, 0))

    return pl.pallas_call(
        _attn_ctx_kernel,
        out_shape=[jax.ShapeDtypeStruct((N_CTX, 512), BF16)] * 3,
        grid=(BATCH,),
        in_specs=[_layer_spec((1, N_HEADS), layer)] + [spec(_ATTN_WIDTH[n]) for n in _ATTN_IN],
        out_specs=[spec(512)] * 3,
        compiler_params=_params(1),
        name="attn_ctx",
    )(lp["sink"], *[p[n] for n in _ATTN_IN])


def _dup_heads(x):
    lo = _lane_lo(x.shape)
    r = pltpu.roll(x, HEAD_DIM, 1)
    return jnp.where(lo, x, r), jnp.where(lo, r, x)


def _heads_with_ones(x):
    return _with_ones(x), _with_ones(pltpu.roll(x, HEAD_DIM, 1))


def _attn_lat_kernel(sink_ref, wk_ref, wuv_ref,
                     sq_ref, sk_ref, sv_ref, mq_ref, mk_ref, mv_ref, gq_ref, gk_ref, gv_ref,
                     csk_ref, csv_ref, cckv_ref, ckr_ref, cgk_ref, cgv_ref,
                     ob_ref, oc_ref, od_ref,
                     cs_k, cs_v, cm_k, cm_v, cg_k, cg_v):
    tq = TQ_LAT
    qi = pl.program_id(1)

    @pl.when(qi == 0)
    def _():
        for src, dst in ((csk_ref, cs_k), (cgk_ref, cg_k)):
            d0, d1 = _dup_heads(src[0, 0])
            dst[0] = d0.astype(BF16)
            dst[1] = d1.astype(BF16)
        for src, dst in ((csv_ref, cs_v), (cgv_ref, cg_v)):
            d0, d1 = _heads_with_ones(src[0, 0])
            dst[0] = d0.astype(BF16)
            dst[1] = d1.astype(BF16)
        ckv16 = cckv_ref[0, 0].astype(BF16)
        k = _dot(ckv16, wk_ref[0:MLA_KV_RANK, :]) + _dot(ckr_ref[0, 0].astype(BF16),
                                                           wk_ref[MLA_KV_RANK:MLA_KV_RANK + MLA_ROPE, :])
        cm_k[...] = k.astype(BF16)
        cm_v[...] = _with_ones(_dot(ckv16, wuv_ref[...])).astype(BF16)

    q0 = qi * tq
    win = tq + 2 * SWA_WINDOW
    start = pl.multiple_of(jnp.clip(q0 - SWA_WINDOW, 0, DEC_SEQ - win), SWA_WINDOW)
    rows = lax.broadcasted_iota(jnp.int32, (4 * tq, win), 0)
    qpos = q0 + (rows & (tq - 1))
    kpos = start + lax.broadcasted_iota(jnp.int32, (4 * tq, win), 1)
    band = jnp.abs(qpos - kpos) <= SWA_WINDOW
    units = []
    for g in range(N_KV):
        def swa_segs(q4, g=g):
            ks = sk_ref[pl.ds(start, win), 128 * g:128 * (g + 1)]
            vs = sv_ref[pl.ds(start, win), 128 * g:128 * (g + 1)]
            s_own = jnp.where(band, _dot_nt(q4, ks), MASK_VALUE)
            return [(_dot_nt(q4, cs_k[g]), cs_v[g]), (s_own, vs)]

        def gqa_segs(q4, g=g):
            return [(_dot_nt(q4, cg_k[g]), cg_v[g]),
                    (_dot_nt(q4, gk_ref[:, 128 * g:128 * (g + 1)]), gv_ref[:, 128 * g:128 * (g + 1)])]

        units += [_gqa_unit(sq_ref, g, tq, swa_segs, ob_ref, True, sink_ref),
                  _gqa_unit(gq_ref, g, tq, gqa_segs, od_ref, True)]

    def mla_segs(h, q_h):
        cols = slice(128 * h, 128 * (h + 1))
        return [(_dot_nt(q_h, cm_k[:, cols]), cm_v[:, cols]), (_dot_nt(q_h, mk_ref[:, cols]), mv_ref[:, cols])]

    mla = [_mla_pair_unit(mq_ref, hp, tq, mla_segs, oc_ref, True) for hp in range(MLA_HEADS // 2)]
    _run_units([u for pair in zip(units, mla) for u in pair], lookahead=3)


def _attn_lat(lp, p, caches, layer):
    nq = DEC_SEQ // TQ_LAT

    def qspec(w):
        return pl.BlockSpec((TQ_LAT, w), lambda b, i: (b * nq + i, 0))

    def kspec(w):
        return pl.BlockSpec((DEC_SEQ, w), lambda b, i: (b, 0))

    def cspec(w):
        return pl.BlockSpec((1, 1, PAST_LEN, w), lambda b, i: (b, layer, 0, 0))

    is_q = dict(sq=True, mq=True, gq=True)
    in_specs = [_layer_spec((1, N_HEADS), layer), _layer_spec((256, 1024), layer),
                _layer_spec((MLA_KV_RANK, 1024), layer)]
    in_specs += [(qspec if is_q.get(n) else kspec)(_ATTN_WIDTH[n]) for n in _ATTN_IN]
    in_specs += [cspec(128), cspec(128), cspec(128), cspec(MLA_ROPE), cspec(128), cspec(128)]
    return pl.pallas_call(
        _attn_lat_kernel,
        out_shape=[jax.ShapeDtypeStruct((N_LAT, 512), BF16)] * 3,
        grid=(DEC_BATCH, nq),
        in_specs=in_specs,
        out_specs=[qspec(512)] * 3,
        scratch_shapes=[pltpu.VMEM((N_KV, PAST_LEN, LANES), BF16), pltpu.VMEM((N_KV, PAST_LEN, LANES), BF16),
                        pltpu.VMEM((PAST_LEN, 1024), BF16), pltpu.VMEM((PAST_LEN, 1024), BF16),
                        pltpu.VMEM((N_KV, PAST_LEN, LANES), BF16), pltpu.VMEM((N_KV, PAST_LEN, LANES), BF16)],
        compiler_params=_params(2),
        name="attn_lat",
    )(lp["sink"], lp["wk"], lp["wuv"], *[p[n] for n in _ATTN_IN], *caches)


MERGE_TILE = 512
MERGE_SUB = 256


def _merge_kernel(x_ref, mod_ref, oa_ref, ob_ref, oc_ref, od_ref, gates_ref, wb_ref, wo_ref, npost_ref,
                  nmpre_ref, nmpost_ref, w1_ref, w2_ref, y_ref):
    g1 = mod_ref[0, 2]
    sh2 = mod_ref[0, 3]
    sc2 = mod_ref[0, 4]
    g2 = mod_ref[0, 5]

    def sub_tile(r0):
        rows = slice(r0, r0 + MERGE_SUB)
        acc = None
        for n, o_ref in enumerate((oa_ref, ob_ref, oc_ref, od_ref)):
            br = _dot(o_ref[rows, :], wb_ref[n]) * gates_ref[rows, D_MODEL * n:D_MODEL * (n + 1)].astype(F32)
            acc = br if acc is None else acc + br
        y = _dot(acc.astype(BF16), wo_ref[...])
        yield
        x = x_ref[rows, :] + g1 * _rms(y, npost_ref[...])
        h = (_rms(x, nmpre_ref[...]) * (1.0 + sc2) + sh2).astype(BF16)
        a = jnp.maximum(_dot(h, w1_ref[...]), 0.0)
        yield
        y = _dot((a * a).astype(BF16), w2_ref[...])
        y_ref[rows, :] = x + g2 * _rms(y, nmpost_ref[...])

    _run_units([sub_tile(r0) for r0 in range(0, MERGE_TILE, MERGE_SUB)])


_MERGE_PARAMS = dict(w_branch=(N_BRANCH, BRANCH_W, D_MODEL), w_out=(D_MODEL, D_MODEL), npost=(1, D_MODEL),
                     nmpre=(1, D_MODEL), nmpost=(1, D_MODEL), w_mlp_in=(D_MODEL, D_FF), w_mlp_out=(D_FF, D_MODEL))


def _merge(x, mod4, o_a, o_b, o_c, o_d, gates, lp, is_ctx, layer):
    n_tok = x.shape[0]

    def tok(w):
        return pl.BlockSpec((MERGE_TILE, w), lambda i: (i, 0))

    in_specs = [tok(D_MODEL), _mod_spec(is_ctx, layer, MERGE_TILE), tok(512), tok(512), tok(512), tok(512), tok(4096)]
    in_specs += [_layer_spec(shape, layer) for shape in _MERGE_PARAMS.values()]
    return pl.pallas_call(
        _merge_kernel, out_shape=jax.ShapeDtypeStruct((n_tok, D_MODEL), F32), grid=(n_tok // MERGE_TILE,),
        in_specs=in_specs, out_specs=tok(D_MODEL), compiler_params=_params(1),
        name="merge_ctx" if is_ctx else "merge_lat",
    )(x, mod4, o_a, o_b, o_c, o_d, gates, *[lp[n] for n in _MERGE_PARAMS])


def _rope_tables():
    t = np.arange(DEC_SEQ)
    row, col = (t // GRID_W).astype(np.float32), (t % GRID_W).astype(np.float32)

    def pattern(d):
        nf = d // 4
        inv = (ROPE_BASE ** (-np.arange(nf, dtype=np.float32) / nf)).astype(np.float32)
        ang = np.concatenate([row[:, None] * inv, row[:, None] * inv, col[:, None] * inv, col[:, None] * inv], axis=1)
        cos, sin = np.cos(ang), np.sin(ang)
        second = (np.arange(d) % (2 * nf)) >= nf
        return cos, np.where(second, sin, 0.0), np.where(second, 0.0, -sin)

    out = np.zeros((9, DEC_SEQ, LANES), np.float32)
    out[[0, 3, 6]] = 1.0
    c, s1, s2 = pattern(HEAD_DIM)
    for a, tab in enumerate((c, s1, s2)):
        out[_TAB_64 + a] = np.tile(tab, (1, LANES // HEAD_DIM))
    c, s1, s2 = pattern(MLA_ROPE)
    for a, tab in enumerate((c, s1, s2)):
        out[_TAB_MLAQ + a][:, MLA_NOPE:MLA_NOPE + MLA_ROPE] = tab
        out[_TAB_KR + a][:, :MLA_ROPE] = tab
    return out


_ROPE_TABLES = _rope_tables()


def _block_mean_matrix(n):
    i = np.arange(n)
    return ((i[:, None] // HEAD_DIM) == (i[None, :] // HEAD_DIM)).astype(np.float32) / HEAD_DIM


def _mla_weight_layouts(w_uq, w_ukv):
    dq = MLA_NOPE + MLA_ROPE
    wq = w_uq.reshape(DEPTH, MLA_Q_RANK, MLA_HEADS, dq)
    wq = jnp.pad(wq, ((0, 0), (0, 0), (0, 0), (0, LANES - dq))).reshape(DEPTH, MLA_Q_RANK, MLA_HEADS * LANES)
    wkv = w_ukv.reshape(DEPTH, MLA_KV_RANK, MLA_HEADS, MLA_NOPE + MLA_V)
    wk = jnp.pad(wkv[..., :MLA_NOPE], ((0, 0), (0, 0), (0, 0), (0, LANES - MLA_NOPE)))
    wk = wk.reshape(DEPTH, MLA_KV_RANK, MLA_HEADS * LANES)
    place = np.zeros((LANES, MLA_HEADS, LANES), np.float32)
    for j in range(MLA_ROPE):
        place[j, :, MLA_NOPE + j] = 1.0
    place = jnp.broadcast_to(jnp.asarray(place.reshape(1, LANES, MLA_HEADS * LANES)), (DEPTH, LANES, MLA_HEADS * LANES))
    w_k = jnp.concatenate([wk, place], axis=1)
    w_uv = jnp.pad(wkv[..., MLA_NOPE:], ((0, 0), (0, 0), (0, 0), (0, LANES - MLA_V)))
    w_uv = w_uv.reshape(DEPTH, MLA_KV_RANK, MLA_HEADS * LANES)
    return wq.astype(BF16), w_k.astype(BF16), w_uv.astype(BF16)


def _lower_bounds(p):
    s = jax.nn.softmax(p.astype(F32), axis=0)
    return jnp.cumsum(s, axis=0) - s[0]


def kernel(x_prompt, x_sample, state_hgrn, cache_swa_k, cache_swa_v, cache_mla_ckv, cache_mla_kr, cache_gqa_k, cache_gqa_v, c, c_ctx, w_ada, b_ada, norm_mix_pre, norm_mix_post, norm_mlp_pre, norm_mlp_post, w_in, hgrn_lb_fwd, hgrn_lb_bwd, hgrn_norm, swa_sink, mla_q_norm, mla_kv_norm, mla_w_uq, mla_w_ukv, gqa_q_norm, gqa_k_norm, w_branch, w_out, w_mlp_in, w_mlp_out):
    tabs = jnp.asarray(_ROPE_TABLES)
    b512 = jnp.asarray(_block_mean_matrix(512)).astype(BF16)

    def rows(a):
        return a.reshape(DEPTH, 1, -1)

    w_uq, w_k, w_uv = _mla_weight_layouts(mla_w_uq, mla_w_ukv)
    w_a, w_kr, w_b = _split_w_in(w_in)
    lp = dict(
        npre=rows(norm_mix_pre), wa=w_a, wkr=w_kr, wb=w_b, mlaq_g=rows(mla_q_norm), mlakv_g=rows(mla_kv_norm),
        wuq=w_uq, wk=w_k, wuv=w_uv, gq_g=rows(jnp.tile(gqa_q_norm, (1, N_HEADS))),
        gk_g=rows(jnp.tile(gqa_k_norm, (1, N_KV))), b512=b512, b128=b512[:128, :128],
        lbf=rows(_lower_bounds(hgrn_lb_fwd)), lbb=rows(_lower_bounds(hgrn_lb_bwd)), gn=rows(hgrn_norm),
        sink=rows(swa_sink),
        w_branch=w_branch.astype(BF16), w_out=w_out.astype(BF16), npost=rows(norm_mix_post),
        nmpre=rows(norm_mlp_pre), nmpost=rows(norm_mlp_post),
        w_mlp_in=w_mlp_in.astype(BF16), w_mlp_out=w_mlp_out.astype(BF16))

    cond8 = jnp.zeros((SUBLANES, D_MODEL), F32).at[0].set(c_ctx).at[1:1 + DEC_BATCH].set(c)
    mod4 = _ada(cond8, w_ada, b_ada).reshape(DEPTH, SUBLANES, N_MOD, 1, D_MODEL)

    caches_in = (cache_swa_k.reshape(DEC_BATCH, DEPTH, PAST_LEN, 128),
                 cache_swa_v.reshape(DEC_BATCH, DEPTH, PAST_LEN, 128), cache_mla_ckv, cache_mla_kr,
                 cache_gqa_k.reshape(DEC_BATCH, DEPTH, PAST_LEN, 128),
                 cache_gqa_v.reshape(DEC_BATCH, DEPTH, PAST_LEN, 128))

    x_ctx = x_prompt.reshape(N_CTX, D_MODEL)
    x_lat = x_sample.reshape(N_LAT, D_MODEL)
    pc = None
    for l in range(DEPTH):
        pc = _proj(x_ctx, mod4, lp, True, layer=l, prev=pc)
        ob, oc, od = _attn_ctx(lp, pc, l)
        x_ctx = _merge(x_ctx, mod4, pc["oa"], ob, oc, od, pc["gates"], lp, True, l)

        pq = _proj(x_lat, mod4, lp, False, layer=l, tabs=tabs)
        oa = _hgrn_lat(pq, lp, l, state_hgrn)
        ob, oc, od = _attn_lat(lp, pq, caches_in, l)
        x_lat = _merge(x_lat, mod4, oa, ob, oc, od, pq["gates"], lp, False, l)

    kv_shape = (BATCH, DEPTH, SEQ, N_KV, HEAD_DIM)
    return (x_ctx.reshape(BATCH, SEQ, D_MODEL), x_lat.reshape(DEC_BATCH, DEC_SEQ, D_MODEL), pc["st"],
            pc["skc"].reshape(kv_shape), pc["svc"].reshape(kv_shape), pc["mckv"], pc["mkr"],
            pc["gkc"].reshape(kv_shape), pc["gvc"].reshape(kv_shape))
```

```python
import functools

import jax
import jax.numpy as jnp
import numpy as np
from jax import lax
from jax.experimental import pallas as pl
from jax.experimental.pallas import tpu as pltpu

D_MODEL = 1024
BATCH, SEQ = 32, 256
DEC_BATCH, DEC_SEQ = 2, 2048
DEPTH = 2
PAST_LEN = 512
GRID_W = 64
ROPE_BASE = 10000.0
NORM_EPS = 1e-6
MASK_VALUE = -1e30
LOG_FLOOR = 1e-30
N_MOD = 6
N_BRANCH = 4
BRANCH_W = 512
D_FF = 4 * D_MODEL
HGRN_HEADS, HGRN_DK = 4, 128
SWA_WINDOW = 128
MLA_HEADS, MLA_Q_RANK, MLA_KV_RANK, MLA_NOPE, MLA_ROPE, MLA_V = 8, 256, 128, 64, 32, 64
HEAD_DIM = 64
N_HEADS, N_KV = 8, 2

N_CTX = BATCH * SEQ
N_LAT = DEC_BATCH * DEC_SEQ

LANES = 128
SUBLANES = 8
VMEM_LIMIT_BYTES = 56 * 1024 * 1024

PROJ_TILE = 256
PROJ_SUB = 256
HGRN_CHUNK = 256
HGRN_UNROLL = 4
TQ_LAT = 128

BF16 = jnp.bfloat16
F32 = jnp.float32

_W_A_COLS = 3712
_W_B_COLS = 8608 - _W_A_COLS - MLA_ROPE
_C_HGRN = ("wa", 0)
_C_SWA_Q = ("wa", 2560)
_C_SWA_KV = ("wa", 3072)
_C_MLA_CQ = ("wa", 3328)
_C_MLA_CKV = ("wa", 3584)
_C_MLA_KR = ("wkr", 0)
_C_GQA_Q = ("wb", 0)
_C_GQA_KV = ("wb", 512)
_C_GATES = ("wb", 768)


def _split_w_in(w):
    w16 = w.astype(BF16)
    return w16, w16, w16[..., _W_A_COLS + MLA_ROPE:]


_W_IN_BLOCK_INDEX = dict(wkr=(0, _W_A_COLS // LANES))


def _sigmoid_pair(x):
    a = jnp.exp(-jnp.abs(x))
    r = 1.0 / (1.0 + a)
    ar = a * r
    pos = x >= 0
    return jnp.where(pos, r, ar), jnp.where(pos, ar, r)


def _sigmoid(x):
    return _sigmoid_pair(x)[0]


def _silu(x):
    return x * _sigmoid(x)


def _rms(x, gain):
    return x * lax.rsqrt(jnp.mean(x * x, axis=-1, keepdims=True) + NORM_EPS) * gain


def _dot(a, b):
    return jnp.dot(a, b, preferred_element_type=F32)


def _dot_nt(a, b):
    return lax.dot_general(a, b, (((1,), (1,)), ((), ())), preferred_element_type=F32)


def _dot_tn(a, b):
    return lax.dot_general(a, b, (((0,), (0,)), ((), ())), preferred_element_type=F32)


def _tile_lanes(t, width):
    reps = width // LANES
    return t if reps == 1 else jnp.concatenate([t] * reps, axis=1)


def _rope(x, tab_ref, base, shift, rows):
    w = x.shape[1]
    c = _tile_lanes(tab_ref[base, rows, :], w)
    s1 = _tile_lanes(tab_ref[base + 1, rows, :], w)
    s2 = _tile_lanes(tab_ref[base + 2, rows, :], w)
    return x * c + pltpu.roll(x, shift, 1) * s1 + pltpu.roll(x, w - shift, 1) * s2


def _head_rms(x, bmat_ref, gain):
    ms = _dot((x * x).astype(BF16), bmat_ref[...])
    return x * lax.rsqrt(ms + NORM_EPS) * gain


def _const_spec(shape):
    nd = len(shape)
    return pl.BlockSpec(shape, lambda *_: (0,) * nd, pipeline_mode=pl.Buffered(1))


def _layer_spec(shape, layer, block_index=None):
    idx = (layer,) + tuple(block_index or (0,) * len(shape))
    return pl.BlockSpec((None,) + tuple(shape), lambda *_: idx, pipeline_mode=pl.Buffered(1))


_ANY_SPEC = pl.BlockSpec(memory_space=pl.ANY)


def _params(n_grid_dims):
    return pltpu.CompilerParams(dimension_semantics=("arbitrary",) * n_grid_dims, vmem_limit_bytes=VMEM_LIMIT_BYTES)


_ADA_TN = 2048


def _ada_kernel(c_ref, w_ref, b_ref, o_ref):
    s = _silu(c_ref[...]).astype(BF16)
    o_ref[0] = _dot(s, w_ref[0].astype(BF16)) + b_ref[0]


def _ada(cond8, w_ada, b_ada):
    n = N_MOD * D_MODEL
    return pl.pallas_call(
        _ada_kernel,
        out_shape=jax.ShapeDtypeStruct((DEPTH, SUBLANES, n), F32),
        grid=(DEPTH, n // _ADA_TN),
        in_specs=[
            pl.BlockSpec((SUBLANES, D_MODEL), lambda l, j: (0, 0)),
            pl.BlockSpec((1, D_MODEL, _ADA_TN), lambda l, j: (l, 0, j)),
            pl.BlockSpec((1, 1, _ADA_TN), lambda l, j: (l, 0, j)),
        ],
        out_specs=pl.BlockSpec((1, SUBLANES, _ADA_TN), lambda l, j: (l, 0, j)),
        compiler_params=_params(2),
        name="ada",
    )(cond8, w_ada, b_ada.reshape(DEPTH, 1, n))


def _mod_spec(is_ctx, layer, tile):
    if is_ctx:
        return pl.BlockSpec((None, 1, N_MOD, 1, D_MODEL), lambda i: (layer, 0, 0, 0, 0))
    per_seq = DEC_SEQ // tile
    return pl.BlockSpec((None, 1, N_MOD, 1, D_MODEL), lambda i: (layer, 1 + i // per_seq, 0, 0, 0))


LOG2E = float(np.log2(np.e))
_SCALE_64 = HEAD_DIM ** -0.5 * LOG2E
_SCALE_MLA = (MLA_NOPE + MLA_ROPE) ** -0.5 * LOG2E
_TAB_64, _TAB_MLAQ, _TAB_KR = 0, 3, 6

_HGRN_PRE = ("hq", "hff", "hfb", "hi", "hg")
_ATTN_OUTS = (
    ("sq", 512, BF16), ("sk", 256, BF16), ("sv", 256, BF16),
    ("mq", 1024, BF16), ("mk", 1024, BF16), ("mv", 1024, BF16),
    ("gq", 512, BF16), ("gk", 256, BF16), ("gv", 256, BF16), ("gates", 4096, BF16))
_CACHE_OUTS = (("skc", 128), ("svc", 128), ("mckv", 128), ("mkr", MLA_ROPE), ("gkc", 128), ("gvc", 128))
_PROJ_PARAMS = dict(npre=(1, D_MODEL), wa=(D_MODEL, _W_A_COLS), wkr=(D_MODEL, LANES), wb=(D_MODEL, _W_B_COLS),
                    mlaq_g=(1, MLA_Q_RANK), mlakv_g=(1, MLA_KV_RANK),
                    wuq=(MLA_Q_RANK, 1024), wk=(256, 1024), wuv=(MLA_KV_RANK, 1024), gq_g=(1, 512), gk_g=(1, 128))
_HGRN_PARAMS = dict(lbf=(1, 512), lbb=(1, 512), gn=(1, 512))
_PROJ_CONSTS = tuple(_PROJ_PARAMS) + ("b512", "b128")
_HGRN_CONSTS = tuple(_HGRN_PARAMS)


def _proj_out_names(is_ctx):
    attn = [n for n, _, _ in _ATTN_OUTS]
    if is_ctx:
        return ["oa"] + attn + [n for n, _ in _CACHE_OUTS] + ["st"]
    return list(_HGRN_PRE) + attn


def _proj_kernel(*refs, is_ctx, n_alias):
    names = ["x", "mod"] + list(_PROJ_CONSTS) + (list(_HGRN_CONSTS) if is_ctx else ["tab"])
    r = dict(zip(names, refs))
    pos = len(names) + n_alias
    for name in _proj_out_names(is_ctx) + (["hbuf"] if is_ctx else []):
        r[name] = refs[pos]
        pos += 1

    sh1 = r["mod"][0, 0]
    sc1 = r["mod"][0, 1]
    for s in range(PROJ_TILE // PROJ_SUB):
        for _ in _proj_sub_tile(r, s, sh1, sc1, is_ctx):
            pass


def _proj_sub_tile(r, s, sh1, sc1, is_ctx):
    rows = slice(PROJ_SUB * s, PROJ_SUB * (s + 1))
    h = (_rms(r["x"][rows, :], r["npre"][...]) * (1.0 + sc1) + sh1).astype(BF16)

    def seg(where, w, skip=0):
        name, a = where
        return _dot(h, r[name][:, a + skip:a + skip + w])

    def rope(v, base, shift):
        return v if is_ctx else _rope(v, r["tab"], base, shift, rows)

    for j, name in enumerate(_HGRN_PRE):
        u = seg(_C_HGRN, 512, skip=512 * j)
        if is_ctx:
            r["hbuf"][s, j] = u
        else:
            r[name][rows, :] = u
    yield

    if is_ctx:
        assert HGRN_CHUNK == PROJ_SUB
        lvls = _hgrn_levels()
        for hd in range(HGRN_HEADS):
            cols = slice(LANES * hd, LANES * (hd + 1))
            q = _silu(r["hbuf"][s, 0, :, cols])
            v = r["hbuf"][s, 3, :, cols]
            o = None
            for d, lb in ((0, r["lbf"]), (1, r["lbb"])):
                k, lf2 = _hgrn_forget(r["hbuf"][s, 1 + d, :, cols], lb[:, cols])
                o_d, st = _hgrn_chunk(q, k, lf2, v, None, d == 0, lvls[d])
                r["st"][s, 0, d, hd] = st.T
                o = o_d if o is None else o + o_d
            r["oa"][rows, cols] = _hgrn_finish(o, r["gn"][:, cols], r["hbuf"][s, 4, :, cols])
        yield

    def dup16(k):
        return jnp.concatenate([d.astype(BF16) for d in _dup_heads(k)], axis=1)

    r["sq"][rows, :] = (rope(seg(_C_SWA_Q, 512), _TAB_64, 16) * _SCALE_64).astype(BF16)
    kv = seg(_C_SWA_KV, 256)
    k, v = kv[:, :128], kv[:, 128:]
    r["sk"][rows, :] = dup16(rope(k, _TAB_64, 16))
    r["sv"][rows, :] = dup16(v)
    if is_ctx:
        r["skc"][s, 0] = k
        r["svc"][s, 0] = v

    cq = _rms(seg(_C_MLA_CQ, 256), r["mlaq_g"][...]).astype(BF16)
    q = _dot(cq, r["wuq"][...])
    r["mq"][rows, :] = (rope(q, _TAB_MLAQ, 8) * _SCALE_MLA).astype(BF16)
    ckv = _rms(seg(_C_MLA_CKV, 128), r["mlakv_g"][...])
    kr = seg(_C_MLA_KR, 128)
    if is_ctx:
        r["mckv"][s, 0] = ckv
        r["mkr"][s, 0] = kr[:, :MLA_ROPE]
    kr_used = rope(kr, _TAB_KR, 8)
    ckv16 = ckv.astype(BF16)
    r["mk"][rows, :] = _dot(jnp.concatenate([ckv16, kr_used.astype(BF16)], axis=1), r["wk"][...]).astype(BF16)
    r["mv"][rows, :] = _dup_tiles(_dot(ckv16, r["wuv"][...])).astype(BF16)

    gq = _head_rms(seg(_C_GQA_Q, 512), r["b512"], r["gq_g"][...])
    r["gq"][rows, :] = (rope(gq, _TAB_64, 16) * _SCALE_64).astype(BF16)
    kv = seg(_C_GQA_KV, 256)
    k, v = _head_rms(kv[:, :128], r["b128"], r["gk_g"][...]), kv[:, 128:]
    r["gk"][rows, :] = dup16(rope(k, _TAB_64, 16))
    r["gv"][rows, :] = dup16(v)
    if is_ctx:
        r["gkc"][s, 0] = k
        r["gvc"][s, 0] = v

    for j in range(N_BRANCH):
        g = _sigmoid(seg(_C_GATES, D_MODEL, skip=D_MODEL * j))
        r["gates"][rows, D_MODEL * j:D_MODEL * (j + 1)] = g.astype(BF16)


def _proj(x, mod4, lp, is_ctx, layer=0, tabs=None, prev=None):
    n_tok = x.shape[0]
    params = dict(_PROJ_PARAMS, **(_HGRN_PARAMS if is_ctx else {}))
    n_sub = PROJ_TILE // PROJ_SUB
    in_specs = [pl.BlockSpec((PROJ_TILE, D_MODEL), lambda i: (i, 0)), _mod_spec(is_ctx, layer, PROJ_TILE)]
    in_specs += [_layer_spec(params[n], layer, _W_IN_BLOCK_INDEX.get(n)) for n in _PROJ_PARAMS]
    in_specs += [_const_spec((512, 512)), _const_spec((128, 128))]
    args = [x, mod4] + [lp[n] for n in _PROJ_PARAMS] + [lp["b512"], lp["b128"]]
    if is_ctx:
        in_specs += [_layer_spec(params[n], layer) for n in _HGRN_PARAMS]
        args += [lp[n] for n in _HGRN_PARAMS]
    if not is_ctx:
        in_specs.append(pl.BlockSpec((9, PROJ_TILE, LANES), lambda i: (0, i % (DEC_SEQ // PROJ_TILE), 0)))
        args.append(tabs)

    def tok(w):
        return pl.BlockSpec((PROJ_TILE, w), lambda i: (i, 0))

    specs = {n: (tok(w), jax.ShapeDtypeStruct((n_tok, w), dt)) for n, w, dt in _ATTN_OUTS}
    if is_ctx:
        specs["oa"] = (tok(512), jax.ShapeDtypeStruct((n_tok, 512), BF16))
        for n, w in _CACHE_OUTS:
            specs[n] = (pl.BlockSpec((n_sub, 1, SEQ, w), lambda i: (i, layer, 0, 0)),
                        jax.ShapeDtypeStruct((BATCH, DEPTH, SEQ, w), F32))
        specs["st"] = (pl.BlockSpec((n_sub, 1, 2, HGRN_HEADS, LANES, LANES), lambda i: (i, layer, 0, 0, 0, 0)),
                       jax.ShapeDtypeStruct((BATCH, DEPTH, 2, HGRN_HEADS, LANES, LANES), F32))
    else:
        for n in _HGRN_PRE:
            specs[n] = (tok(512), jax.ShapeDtypeStruct((n_tok, 512), F32))
    out_names = _proj_out_names(is_ctx)
    aliases = {}
    n_alias = 0
    if prev is not None:
        for n in [n for n, _ in _CACHE_OUTS] + ["st"]:
            aliases[len(args)] = out_names.index(n)
            in_specs.append(_ANY_SPEC)
            args.append(prev[n])
            n_alias += 1
    outs = pl.pallas_call(
        functools.partial(_proj_kernel, is_ctx=is_ctx, n_alias=n_alias),
        out_shape=[specs[n][1] for n in out_names], grid=(n_tok // PROJ_TILE,), in_specs=in_specs,
        out_specs=[specs[n][0] for n in out_names],
        scratch_shapes=[pltpu.VMEM((n_sub, len(_HGRN_PRE), PROJ_SUB, 512), F32)] if is_ctx else [],
        input_output_aliases=aliases, compiler_params=_params(1),
        name="proj_ctx" if is_ctx else "proj_lat",
    )(*args)
    return dict(zip(out_names, outs))


def _pair_levels():
    C = HGRN_CHUNK
    ti = lax.broadcasted_iota(jnp.int32, (C, C), 0)
    si = lax.broadcasted_iota(jnp.int32, (C, C), 1)
    x = ti ^ si
    lvl = jnp.where(x == 0, -1, 0)
    b = 2
    while b < C:
        lvl = lvl + (x >= b).astype(jnp.int32)
        b *= 2
    return ti, si, lvl


def _hgrn_forget(f_pre, lb):
    s_pos, s_neg = _sigmoid_pair(f_pre)
    f = lb + (1.0 - lb) * s_pos
    return (1.0 - lb) * s_neg, jnp.log2(jnp.maximum(f, LOG_FLOOR))


def _hgrn_levels():
    ti, si, lvl = _pair_levels()
    return jnp.where(ti > si, lvl, -1), jnp.where(ti < si, lvl, -1)


def _hgrn_chunk(q, k, lf2, v, st, fwd, lvl):
    C = HGRN_CHUNK
    nv = C // SUBLANES
    sub = lax.broadcasted_iota(jnp.int32, (nv, SUBLANES, LANES), 1)

    p = lf2.reshape(nv, SUBLANES, LANES)
    for s in (1, 2, 4):
        if fwd:
            p = p + jnp.where(sub >= s, pltpu.roll(p, s, 1), 0.0)
        else:
            p = p + jnp.where(sub < SUBLANES - s, pltpu.roll(p, SUBLANES - s, 1), 0.0)

    a_mat = jnp.zeros((C, C), F32)
    q16, k16 = q.astype(BF16), k.astype(BF16)

    def add_level(a_mat, e, level):
        e16 = e.astype(BF16)
        return jnp.where(lvl == level, _dot_nt(q16 * e16, k16 * e16), a_mat)

    def row(r):
        return jnp.broadcast_to(p[:, r:r + 1, :], p.shape)

    odd = (sub & 1) == 1
    if fwd:
        bnd = (jnp.where(odd, pltpu.roll(p, 1, 1), p), jnp.where(sub < 4, row(1), row(5)), row(3))
    else:
        bnd = (jnp.where(odd, p, pltpu.roll(p, SUBLANES - 1, 1)), jnp.where(sub < 4, row(2), row(6)), row(4))
    for level, b in enumerate(bnd):
        a_mat = add_level(a_mat, jnp.exp2(-jnp.abs(p - b)).reshape(C, LANES), level)

    p = p.reshape(C, LANES)
    m = SUBLANES
    while m < C:
        nb = C // (2 * m)
        p4 = p.reshape(nb, 2, m, LANES)
        lo, hi = p4[:, 0], p4[:, 1]
        if fwd:
            tot = lo[:, m - 1:m]
            e_lo, e_hi = tot - lo, hi
            p_lo, p_hi = lo, hi + tot
        else:
            tot = hi[:, 0:1]
            e_lo, e_hi = lo, tot - hi
            p_lo, p_hi = lo + tot, hi
        e = jnp.exp2(jnp.concatenate([e_lo[:, None], e_hi[:, None]], axis=1).reshape(C, LANES))
        p = jnp.concatenate([p_lo[:, None], p_hi[:, None]], axis=1).reshape(C, LANES)
        a_mat = add_level(a_mat, e, int(np.log2(m)))
        m *= 2
    v16 = v.astype(BF16)
    o = jnp.sum(q * k, axis=-1, keepdims=True) * v + _dot(a_mat.astype(BF16), v16)

    if st is not None:
        o = o + _dot_nt((q * jnp.exp2(p)).astype(BF16), st.astype(BF16))
    tot = p[C - 1:C] if fwd else p[0:1]
    k_end = (k * jnp.exp2(tot - p)).astype(BF16)
    st_new = _dot_tn(v16, k_end)
    if st is not None:
        st_new = st_new + st * jnp.exp2(tot)
    return o, st_new


def _hgrn_finish(o, gn, g_pre):
    return (_rms(o, gn) * _silu(g_pre)).astype(BF16)


def _hgrn_lat_kernel(lbf_ref, lbb_ref, gn_ref, q_ref, ff_ref, fb_ref, v_ref, g_ref, s0_ref, o_ref,
                     of_ref, ob_ref, st_ref, *, n_chunks):
    C = HGRN_CHUNK
    for d in range(2):
        st_ref[d] = s0_ref[0, 0, d, 0].T
    lbf = lbf_ref[...]
    lbb = lbb_ref[...]
    lvls = _hgrn_levels()

    def body(c, carry):
        rf = pl.multiple_of(c * C, C)
        rb = pl.multiple_of((n_chunks - 1 - c) * C, C)
        for d, r0, f_ref, lb, out_ref in ((0, rf, ff_ref, lbf, of_ref), (1, rb, fb_ref, lbb, ob_ref)):
            sl = pl.ds(r0, C)
            k, lf2 = _hgrn_forget(f_ref[sl, :], lb)
            out_ref[sl, :], st_ref[d] = _hgrn_chunk(_silu(q_ref[sl, :]), k, lf2, v_ref[sl, :], st_ref[d], d == 0,
                                                    lvls[d])
        return carry

    lax.fori_loop(0, n_chunks, body, 0, unroll=HGRN_UNROLL)
    o_ref[...] = _hgrn_finish(of_ref[...] + ob_ref[...], gn_ref[...], g_ref[...])


def _hgrn_lat(p, lp, layer, state_in):
    tok_spec = pl.BlockSpec((DEC_SEQ, LANES), lambda b, h: (b, h))
    vec_spec = pl.BlockSpec((None, 1, LANES), lambda b, h: (layer, 0, h))
    state_spec = pl.BlockSpec((1, 1, 2, 1, LANES, LANES), lambda b, h: (b, layer, 0, h, 0, 0))
    return pl.pallas_call(
        functools.partial(_hgrn_lat_kernel, n_chunks=DEC_SEQ // HGRN_CHUNK),
        out_shape=jax.ShapeDtypeStruct((N_LAT, 512), BF16), grid=(DEC_BATCH, HGRN_HEADS),
        in_specs=[vec_spec] * 3 + [tok_spec] * 5 + [state_spec], out_specs=tok_spec,
        scratch_shapes=[pltpu.VMEM((DEC_SEQ, LANES), F32), pltpu.VMEM((DEC_SEQ, LANES), F32),
                        pltpu.VMEM((2, LANES, LANES), F32)],
        compiler_params=_params(2), name="hgrn_lat",
    )(lp["lbf"], lp["lbb"], lp["gn"], p["hq"], p["hff"], p["hfb"], p["hi"], p["hg"], state_in)


def _lane_lo(shape):
    return (lax.broadcasted_iota(jnp.int32, shape, len(shape) - 1) & (LANES - 1)) < HEAD_DIM


def _softmax_unit(make_segs, sink_col, finish):
    segs = make_segs()
    mx = None
    for s, _ in segs:
        m = jnp.max(s, axis=-1, keepdims=True)
        mx = m if mx is None else jnp.maximum(mx, m)
    if sink_col is not None:
        mx = jnp.maximum(mx, sink_col)
    yield
    acc = None
    for s, v in segs:
        v_ones = jnp.concatenate([v, jnp.ones(v.shape, BF16)], axis=1)
        pv = _dot(jnp.exp2(s - mx).astype(BF16), v_ones)
        acc = pv if acc is None else acc + pv
    den = acc[:, LANES:]
    if sink_col is not None:
        den = den + jnp.exp2(sink_col - mx)
    finish(acc[:, :LANES] / den)


_DONE = object()


def _run_units(units, lookahead=1):
    pending = list(units)
    active = []
    for _ in range(lookahead - 1):
        if pending:
            u = pending.pop(0)
            next(u)
            active.insert(0, u)
    while pending or active:
        started = []
        if pending:
            u = pending.pop(0)
            next(u)
            started.append(u)
        n_adv = max(len(active) - (lookahead - 1), 1) if pending else len(active)
        keep = active[:len(active) - n_adv]
        active = started + keep + [a for a in active[len(active) - n_adv:] if next(a, _DONE) is not _DONE]


def _gqa_unit(q_ref, g, tq, segs_fn, o_ref, sink_ref=None):
    lo = _lane_lo((tq, LANES))
    parts = []
    for j in range(2):
        qp = q_ref[:, 256 * g + 128 * j:256 * g + 128 * (j + 1)]
        zero = jnp.zeros_like(qp)
        parts += [jnp.where(lo, qp, zero), jnp.where(lo, zero, qp)]
    q4 = jnp.concatenate(parts, axis=0)
    sink_col = None
    if sink_ref is not None:
        sink_col = jnp.concatenate(
            [jnp.broadcast_to(sink_ref[0:1, 4 * g + i:4 * g + i + 1] * LOG2E, (tq, 1)) for i in range(4)], axis=0)

    def finish(pv):
        o_ref[:, 256 * g:256 * g + 128] = jnp.where(lo, pv[0:tq], pv[tq:2 * tq]).astype(BF16)
        o_ref[:, 256 * g + 128:256 * (g + 1)] = jnp.where(lo, pv[2 * tq:3 * tq], pv[3 * tq:4 * tq]).astype(BF16)

    yield from _softmax_unit(lambda: segs_fn(q4), sink_col, finish)


def _mla_pair_unit(q_ref, hp, tq, segs_fn, o_ref):
    outs = []
    subs = [_softmax_unit(functools.partial(segs_fn, h, q_ref[:, 128 * h:128 * (h + 1)]), None, outs.append)
            for h in (2 * hp, 2 * hp + 1)]
    while True:
        if [next(s, _DONE) for s in subs][0] is _DONE:
            break
        yield
    o_ref[:, 128 * hp:128 * (hp + 1)] = jnp.where(_lane_lo((tq, LANES)), outs[0], outs[1]).astype(BF16)


_ATTN_IN = ("sq", "sk", "sv", "mq", "mk", "mv", "gq", "gk", "gv")
_ATTN_WIDTH = dict(sq=512, sk=256, sv=256, mq=1024, mk=1024, mv=1024, gq=512, gk=256, gv=256)


def _attn_ctx_kernel(sink_ref, sq_ref, sk_ref, sv_ref, mq_ref, mk_ref, mv_ref, gq_ref, gk_ref, gv_ref,
                     ob_ref, oc_ref, od_ref):
    tq = SEQ
    units = []
    for g in range(N_KV):
        def swa_segs(q4, g=g):
            return [(_dot_nt(q4, sk_ref[:, 128 * g:128 * (g + 1)]), sv_ref[:, 128 * g:128 * (g + 1)])]

        def gqa_segs(q4, g=g):
            return [(_dot_nt(q4, gk_ref[:, 128 * g:128 * (g + 1)]), gv_ref[:, 128 * g:128 * (g + 1)])]

        units += [_gqa_unit(sq_ref, g, tq, swa_segs, ob_ref, sink_ref), _gqa_unit(gq_ref, g, tq, gqa_segs, od_ref)]

    def mla_segs(h, q_h):
        return [(_dot_nt(q_h, mk_ref[:, 128 * h:128 * (h + 1)]), mv_ref[:, 128 * h:128 * (h + 1)])]

    units += [_mla_pair_unit(mq_ref, hp, tq, mla_segs, oc_ref) for hp in range(MLA_HEADS // 2)]
    _run_units(units)


def _attn_ctx(lp, p, layer):
    def spec(w):
        return pl.BlockSpec((SEQ, w), lambda b: (b, 0))

    return pl.pallas_call(
        _attn_ctx_kernel,
        out_shape=[jax.ShapeDtypeStruct((N_CTX, 512), BF16)] * 3,
        grid=(BATCH,),
        in_specs=[_layer_spec((1, N_HEADS), layer)] + [spec(_ATTN_WIDTH[n]) for n in _ATTN_IN],
        out_specs=[spec(512)] * 3,
        compiler_params=_params(1),
        name="attn_ctx",
    )(lp["sink"], *[p[n] for n in _ATTN_IN])


def _dup_heads(x):
    lo = _lane_lo(x.shape)
    r = pltpu.roll(x, HEAD_DIM, 1)
    return jnp.where(lo, x, r), jnp.where(lo, r, x)


def _dup_tiles(x):
    return jnp.where(_lane_lo(x.shape), x, pltpu.roll(x, HEAD_DIM, 1))


def _attn_lat_kernel(sink_ref, wk_ref, wuv_ref,
                     sq_ref, sk_ref, sv_ref, mq_ref, mk_ref, mv_ref, gq_ref, gk_ref, gv_ref,
                     csk_ref, csv_ref, cckv_ref, ckr_ref, cgk_ref, cgv_ref,
                     ob_ref, oc_ref, od_ref,
                     cs_k, cs_v, cm_k, cm_v, cg_k, cg_v):
    tq = TQ_LAT
    qi = pl.program_id(1)

    @pl.when(qi == 0)
    def _():
        for src, dst in ((csk_ref, cs_k), (csv_ref, cs_v), (cgk_ref, cg_k), (cgv_ref, cg_v)):
            d0, d1 = _dup_heads(src[0, 0])
            dst[0] = d0.astype(BF16)
            dst[1] = d1.astype(BF16)
        ckv16 = cckv_ref[0, 0].astype(BF16)
        k = _dot(ckv16, wk_ref[0:MLA_KV_RANK, :]) + _dot(ckr_ref[0, 0].astype(BF16),
                                                           wk_ref[MLA_KV_RANK:MLA_KV_RANK + MLA_ROPE, :])
        cm_k[...] = k.astype(BF16)
        cm_v[...] = _dup_tiles(_dot(ckv16, wuv_ref[...])).astype(BF16)

    q0 = qi * tq
    win = tq + 2 * SWA_WINDOW
    start = pl.multiple_of(jnp.clip(q0 - SWA_WINDOW, 0, DEC_SEQ - win), SWA_WINDOW)
    rows = lax.broadcasted_iota(jnp.int32, (4 * tq, win), 0)
    qpos = q0 + (rows & (tq - 1))
    kpos = start + lax.broadcasted_iota(jnp.int32, (4 * tq, win), 1)
    band = jnp.abs(qpos - kpos) <= SWA_WINDOW
    units = []
    for g in range(N_KV):
        def swa_segs(q4, g=g):
            ks = sk_ref[pl.ds(start, win), 128 * g:128 * (g + 1)]
            vs = sv_ref[pl.ds(start, win), 128 * g:128 * (g + 1)]
            s_own = jnp.where(band, _dot_nt(q4, ks), MASK_VALUE)
            return [(_dot_nt(q4, cs_k[g]), cs_v[g]), (s_own, vs)]

        def gqa_segs(q4, g=g):
            return [(_dot_nt(q4, cg_k[g]), cg_v[g]),
                    (_dot_nt(q4, gk_ref[:, 128 * g:128 * (g + 1)]), gv_ref[:, 128 * g:128 * (g + 1)])]

        units += [_gqa_unit(sq_ref, g, tq, swa_segs, ob_ref, sink_ref), _gqa_unit(gq_ref, g, tq, gqa_segs, od_ref)]

    def mla_segs(h, q_h):
        cols = slice(128 * h, 128 * (h + 1))
        return [(_dot_nt(q_h, cm_k[:, cols]), cm_v[:, cols]), (_dot_nt(q_h, mk_ref[:, cols]), mv_ref[:, cols])]

    mla = [_mla_pair_unit(mq_ref, hp, tq, mla_segs, oc_ref) for hp in range(MLA_HEADS // 2)]
    _run_units([u for pair in zip(units, mla) for u in pair], lookahead=3)


def _attn_lat(lp, p, caches, layer):
    nq = DEC_SEQ // TQ_LAT

    def qspec(w):
        return pl.BlockSpec((TQ_LAT, w), lambda b, i: (b * nq + i, 0))

    def kspec(w):
        return pl.BlockSpec((DEC_SEQ, w), lambda b, i: (b, 0))

    def cspec(w):
        return pl.BlockSpec((1, 1, PAST_LEN, w), lambda b, i: (b, layer, 0, 0))

    is_q = dict(sq=True, mq=True, gq=True)
    in_specs = [_layer_spec((1, N_HEADS), layer), _layer_spec((256, 1024), layer),
                _layer_spec((MLA_KV_RANK, 1024), layer)]
    in_specs += [(qspec if is_q.get(n) else kspec)(_ATTN_WIDTH[n]) for n in _ATTN_IN]
    in_specs += [cspec(128), cspec(128), cspec(128), cspec(MLA_ROPE), cspec(128), cspec(128)]
    return pl.pallas_call(
        _attn_lat_kernel,
        out_shape=[jax.ShapeDtypeStruct((N_LAT, 512), BF16)] * 3,
        grid=(DEC_BATCH, nq),
        in_specs=in_specs,
        out_specs=[qspec(512)] * 3,
        scratch_shapes=[pltpu.VMEM((N_KV, PAST_LEN, LANES), BF16), pltpu.VMEM((N_KV, PAST_LEN, LANES), BF16),
                        pltpu.VMEM((PAST_LEN, 1024), BF16), pltpu.VMEM((PAST_LEN, 1024), BF16),
                        pltpu.VMEM((N_KV, PAST_LEN, LANES), BF16), pltpu.VMEM((N_KV, PAST_LEN, LANES), BF16)],
        compiler_params=_params(2),
        name="attn_lat",
    )(lp["sink"], lp["wk"], lp["wuv"], *[p[n] for n in _ATTN_IN], *caches)


MERGE_TILE = 512
MERGE_SUB = 256


def _merge_kernel(x_ref, mod_ref, oa_ref, ob_ref, oc_ref, od_ref, gates_ref, wb_ref, wo_ref, npost_ref,
                  nmpre_ref, nmpost_ref, w1_ref, w2_ref, y_ref):
    g1 = mod_ref[0, 2]
    sh2 = mod_ref[0, 3]
    sc2 = mod_ref[0, 4]
    g2 = mod_ref[0, 5]

    def sub_tile(r0):
        rows = slice(r0, r0 + MERGE_SUB)
        acc = None
        for n, o_ref in enumerate((oa_ref, ob_ref, oc_ref, od_ref)):
            br = _dot(o_ref[rows, :], wb_ref[n]) * gates_ref[rows, D_MODEL * n:D_MODEL * (n + 1)].astype(F32)
            acc = br if acc is None else acc + br
        y = _dot(acc.astype(BF16), wo_ref[...])
        yield
        x = x_ref[rows, :] + g1 * _rms(y, npost_ref[...])
        h = (_rms(x, nmpre_ref[...]) * (1.0 + sc2) + sh2).astype(BF16)
        a = jnp.maximum(_dot(h, w1_ref[...]), 0.0)
        yield
        y = _dot((a * a).astype(BF16), w2_ref[...])
        y_ref[rows, :] = x + g2 * _rms(y, nmpost_ref[...])

    _run_units([sub_tile(r0) for r0 in range(0, MERGE_TILE, MERGE_SUB)])


_MERGE_PARAMS = dict(w_branch=(N_BRANCH, BRANCH_W, D_MODEL), w_out=(D_MODEL, D_MODEL), npost=(1, D_MODEL),
                     nmpre=(1, D_MODEL), nmpost=(1, D_MODEL), w_mlp_in=(D_MODEL, D_FF), w_mlp_out=(D_FF, D_MODEL))


def _merge(x, mod4, o_a, o_b, o_c, o_d, gates, lp, is_ctx, layer):
    n_tok = x.shape[0]

    def tok(w):
        return pl.BlockSpec((MERGE_TILE, w), lambda i: (i, 0))

    in_specs = [tok(D_MODEL), _mod_spec(is_ctx, layer, MERGE_TILE), tok(512), tok(512), tok(512), tok(512), tok(4096)]
    in_specs += [_layer_spec(shape, layer) for shape in _MERGE_PARAMS.values()]
    return pl.pallas_call(
        _merge_kernel, out_shape=jax.ShapeDtypeStruct((n_tok, D_MODEL), F32), grid=(n_tok // MERGE_TILE,),
        in_specs=in_specs, out_specs=tok(D_MODEL), compiler_params=_params(1),
        name="merge_ctx" if is_ctx else "merge_lat",
    )(x, mod4, o_a, o_b, o_c, o_d, gates, *[lp[n] for n in _MERGE_PARAMS])


def _rope_tables():
    t = np.arange(DEC_SEQ)
    row, col = (t // GRID_W).astype(np.float32), (t % GRID_W).astype(np.float32)

    def pattern(d):
        nf = d // 4
        inv = (ROPE_BASE ** (-np.arange(nf, dtype=np.float32) / nf)).astype(np.float32)
        ang = np.concatenate([row[:, None] * inv, row[:, None] * inv, col[:, None] * inv, col[:, None] * inv], axis=1)
        cos, sin = np.cos(ang), np.sin(ang)
        second = (np.arange(d) % (2 * nf)) >= nf
        return cos, np.where(second, sin, 0.0), np.where(second, 0.0, -sin)

    out = np.zeros((9, DEC_SEQ, LANES), np.float32)
    out[[0, 3, 6]] = 1.0
    c, s1, s2 = pattern(HEAD_DIM)
    for a, tab in enumerate((c, s1, s2)):
        out[_TAB_64 + a] = np.tile(tab, (1, LANES // HEAD_DIM))
    c, s1, s2 = pattern(MLA_ROPE)
    for a, tab in enumerate((c, s1, s2)):
        out[_TAB_MLAQ + a][:, MLA_NOPE:MLA_NOPE + MLA_ROPE] = tab
        out[_TAB_KR + a][:, :MLA_ROPE] = tab
    return out


_ROPE_TABLES = _rope_tables()


def _block_mean_matrix(n):
    i = np.arange(n)
    return ((i[:, None] // HEAD_DIM) == (i[None, :] // HEAD_DIM)).astype(np.float32) / HEAD_DIM


def _mla_weight_layouts(w_uq, w_ukv):
    dq = MLA_NOPE + MLA_ROPE
    wq = w_uq.reshape(DEPTH, MLA_Q_RANK, MLA_HEADS, dq)
    wq = jnp.pad(wq, ((0, 0), (0, 0), (0, 0), (0, LANES - dq))).reshape(DEPTH, MLA_Q_RANK, MLA_HEADS * LANES)
    wkv = w_ukv.reshape(DEPTH, MLA_KV_RANK, MLA_HEADS, MLA_NOPE + MLA_V)
    wk = jnp.pad(wkv[..., :MLA_NOPE], ((0, 0), (0, 0), (0, 0), (0, LANES - MLA_NOPE)))
    wk = wk.reshape(DEPTH, MLA_KV_RANK, MLA_HEADS * LANES)
    place = np.zeros((LANES, MLA_HEADS, LANES), np.float32)
    for j in range(MLA_ROPE):
        place[j, :, MLA_NOPE + j] = 1.0
    place = jnp.broadcast_to(jnp.asarray(place.reshape(1, LANES, MLA_HEADS * LANES)), (DEPTH, LANES, MLA_HEADS * LANES))
    w_k = jnp.concatenate([wk, place], axis=1)
    w_uv = jnp.pad(wkv[..., MLA_NOPE:], ((0, 0), (0, 0), (0, 0), (0, LANES - MLA_V)))
    w_uv = w_uv.reshape(DEPTH, MLA_KV_RANK, MLA_HEADS * LANES)
    return wq.astype(BF16), w_k.astype(BF16), w_uv.astype(BF16)


def _lower_bounds(p):
    s = jax.nn.softmax(p.astype(F32), axis=0)
    return jnp.cumsum(s, axis=0) - s[0]


def kernel(x_prompt, x_sample, state_hgrn, cache_swa_k, cache_swa_v, cache_mla_ckv, cache_mla_kr, cache_gqa_k, cache_gqa_v, c, c_ctx, w_ada, b_ada, norm_mix_pre, norm_mix_post, norm_mlp_pre, norm_mlp_post, w_in, hgrn_lb_fwd, hgrn_lb_bwd, hgrn_norm, swa_sink, mla_q_norm, mla_kv_norm, mla_w_uq, mla_w_ukv, gqa_q_norm, gqa_k_norm, w_branch, w_out, w_mlp_in, w_mlp_out):
    tabs = jnp.asarray(_ROPE_TABLES)
    b512 = jnp.asarray(_block_mean_matrix(512)).astype(BF16)

    def rows(a):
        return a.reshape(DEPTH, 1, -1)

    w_uq, w_k, w_uv = _mla_weight_layouts(mla_w_uq, mla_w_ukv)
    w_a, w_kr, w_b = _split_w_in(w_in)
    lp = dict(
        npre=rows(norm_mix_pre), wa=w_a, wkr=w_kr, wb=w_b, mlaq_g=rows(mla_q_norm), mlakv_g=rows(mla_kv_norm),
        wuq=w_uq, wk=w_k, wuv=w_uv, gq_g=rows(jnp.tile(gqa_q_norm, (1, N_HEADS))),
        gk_g=rows(jnp.tile(gqa_k_norm, (1, N_KV))), b512=b512, b128=b512[:128, :128],
        lbf=rows(_lower_bounds(hgrn_lb_fwd)), lbb=rows(_lower_bounds(hgrn_lb_bwd)), gn=rows(hgrn_norm),
        sink=rows(swa_sink),
        w_branch=w_branch.astype(BF16), w_out=w_out.astype(BF16), npost=rows(norm_mix_post),
        nmpre=rows(norm_mlp_pre), nmpost=rows(norm_mlp_post),
        w_mlp_in=w_mlp_in.astype(BF16), w_mlp_out=w_mlp_out.astype(BF16))

    cond8 = jnp.zeros((SUBLANES, D_MODEL), F32).at[0].set(c_ctx).at[1:1 + DEC_BATCH].set(c)
    mod4 = _ada(cond8, w_ada, b_ada).reshape(DEPTH, SUBLANES, N_MOD, 1, D_MODEL)

    caches_in = (cache_swa_k.reshape(DEC_BATCH, DEPTH, PAST_LEN, 128),
                 cache_swa_v.reshape(DEC_BATCH, DEPTH, PAST_LEN, 128), cache_mla_ckv, cache_mla_kr,
                 cache_gqa_k.reshape(DEC_BATCH, DEPTH, PAST_LEN, 128),
                 cache_gqa_v.reshape(DEC_BATCH, DEPTH, PAST_LEN, 128))

    x_ctx = x_prompt.reshape(N_CTX, D_MODEL)
    x_lat = x_sample.reshape(N_LAT, D_MODEL)
    pc = None
    for l in range(DEPTH):
        pc = _proj(x_ctx, mod4, lp, True, layer=l, prev=pc)
        ob, oc, od = _attn_ctx(lp, pc, l)
        x_ctx = _merge(x_ctx, mod4, pc["oa"], ob, oc, od, pc["gates"], lp, True, l)

        pq = _proj(x_lat, mod4, lp, False, layer=l, tabs=tabs)
        oa = _hgrn_lat(pq, lp, l, state_hgrn)
        ob, oc, od = _attn_lat(lp, pq, caches_in, l)
        x_lat = _merge(x_lat, mod4, oa, ob, oc, od, pq["gates"], lp, False, l)

    kv_shape = (BATCH, DEPTH, SEQ, N_KV, HEAD_DIM)
    return (x_ctx.reshape(BATCH, SEQ, D_MODEL), x_lat.reshape(DEC_BATCH, DEC_SEQ, D_MODEL), pc["st"],
            pc["skc"].reshape(kv_shape), pc["svc"].reshape(kv_shape), pc["mckv"], pc["mkr"],
            pc["gkc"].reshape(kv_shape), pc["gvc"].reshape(kv_shape))
```

```python
import functools

import jax
import jax.numpy as jnp
import numpy as np
from jax import lax
from jax.experimental import pallas as pl
from jax.experimental.pallas import tpu as pltpu

D_MODEL = 1024
BATCH, SEQ = 32, 256
DEC_BATCH, DEC_SEQ = 2, 2048
DEPTH = 2
PAST_LEN = 512
GRID_W = 64
ROPE_BASE = 10000.0
NORM_EPS = 1e-6
MASK_VALUE = -1e30
LOG_FLOOR = 1e-30
N_MOD = 6
N_BRANCH = 4
BRANCH_W = 512
D_FF = 4 * D_MODEL
HGRN_HEADS, HGRN_DK = 4, 128
SWA_WINDOW = 128
MLA_HEADS, MLA_Q_RANK, MLA_KV_RANK, MLA_NOPE, MLA_ROPE, MLA_V = 8, 256, 128, 64, 32, 64
HEAD_DIM = 64
N_HEADS, N_KV = 8, 2

N_CTX = BATCH * SEQ
N_LAT = DEC_BATCH * DEC_SEQ

LANES = 128
SUBLANES = 8
VMEM_LIMIT_BYTES = 56 * 1024 * 1024

PROJ_TILE = 256
PROJ_SUB = 256
HGRN_CHUNK = 256
HGRN_UNROLL = 4
TQ_LAT = 128

BF16 = jnp.bfloat16
F32 = jnp.float32

_W_A_COLS = 3712
_W_B_COLS = 8608 - _W_A_COLS - MLA_ROPE
_C_HGRN = ("wa", 0)
_C_SWA_Q = ("wa", 2560)
_C_SWA_KV = ("wa", 3072)
_C_MLA_CQ = ("wa", 3328)
_C_MLA_CKV = ("wa", 3584)
_C_MLA_KR = ("wkr", 0)
_C_GQA_Q = ("wb", 0)
_C_GQA_KV = ("wb", 512)
_C_GATES = ("wb", 768)


def _split_w_in(w):
    w16 = w.astype(BF16)
    return w16, w16, w16[..., _W_A_COLS + MLA_ROPE:]


_W_IN_BLOCK_INDEX = dict(wkr=(0, _W_A_COLS // LANES))


def _sigmoid_pair(x):
    a = jnp.exp(-jnp.abs(x))
    r = 1.0 / (1.0 + a)
    ar = a * r
    pos = x >= 0
    return jnp.where(pos, r, ar), jnp.where(pos, ar, r)


def _sigmoid(x):
    return 1.0 / (1.0 + jnp.exp2(x * -LOG2E))


def _silu(x):
    return x * _sigmoid(x)


def _rms(x, gain):
    return x * lax.rsqrt(jnp.mean(x * x, axis=-1, keepdims=True) + NORM_EPS) * gain


def _dot(a, b):
    return jnp.dot(a, b, preferred_element_type=F32)


def _dot_nt(a, b):
    return lax.dot_general(a, b, (((1,), (1,)), ((), ())), preferred_element_type=F32)


def _dot_tn(a, b):
    return lax.dot_general(a, b, (((0,), (0,)), ((), ())), preferred_element_type=F32)


def _tile_lanes(t, width):
    reps = width // LANES
    return t if reps == 1 else jnp.concatenate([t] * reps, axis=1)


def _rope(x, tab_ref, base, shift, rows):
    w = x.shape[1]
    c = _tile_lanes(tab_ref[base, rows, :], w)
    s1 = _tile_lanes(tab_ref[base + 1, rows, :], w)
    s2 = _tile_lanes(tab_ref[base + 2, rows, :], w)
    return x * c + pltpu.roll(x, shift, 1) * s1 + pltpu.roll(x, w - shift, 1) * s2


def _head_rms(x, bmat_ref, gain):
    ms = _dot((x * x).astype(BF16), bmat_ref[...])
    return x * lax.rsqrt(ms + NORM_EPS) * gain


def _const_spec(shape):
    nd = len(shape)
    return pl.BlockSpec(shape, lambda *_: (0,) * nd, pipeline_mode=pl.Buffered(1))


def _layer_spec(shape, layer, block_index=None):
    idx = (layer,) + tuple(block_index or (0,) * len(shape))
    return pl.BlockSpec((None,) + tuple(shape), lambda *_: idx, pipeline_mode=pl.Buffered(1))


_ANY_SPEC = pl.BlockSpec(memory_space=pl.ANY)


def _params(n_grid_dims):
    return pltpu.CompilerParams(dimension_semantics=("arbitrary",) * n_grid_dims, vmem_limit_bytes=VMEM_LIMIT_BYTES)


_ADA_TN = 2048


def _ada_kernel(c_ref, w_ref, b_ref, o_ref):
    s = _silu(c_ref[...]).astype(BF16)
    o_ref[0] = _dot(s, w_ref[0].astype(BF16)) + b_ref[0]


def _ada(cond8, w_ada, b_ada):
    n = N_MOD * D_MODEL
    return pl.pallas_call(
        _ada_kernel,
        out_shape=jax.ShapeDtypeStruct((DEPTH, SUBLANES, n), F32),
        grid=(DEPTH, n // _ADA_TN),
        in_specs=[
            pl.BlockSpec((SUBLANES, D_MODEL), lambda l, j: (0, 0)),
            pl.BlockSpec((1, D_MODEL, _ADA_TN), lambda l, j: (l, 0, j)),
            pl.BlockSpec((1, 1, _ADA_TN), lambda l, j: (l, 0, j)),
        ],
        out_specs=pl.BlockSpec((1, SUBLANES, _ADA_TN), lambda l, j: (l, 0, j)),
        compiler_params=_params(2),
        name="ada",
    )(cond8, w_ada, b_ada.reshape(DEPTH, 1, n))


def _mod_spec(is_ctx, layer, tile):
    if is_ctx:
        return pl.BlockSpec((None, 1, N_MOD, 1, D_MODEL), lambda i: (layer, 0, 0, 0, 0))
    per_seq = DEC_SEQ // tile
    return pl.BlockSpec((None, 1, N_MOD, 1, D_MODEL), lambda i: (layer, 1 + i // per_seq, 0, 0, 0))


LOG2E = float(np.log2(np.e))
_SCALE_64 = HEAD_DIM ** -0.5 * LOG2E
_SCALE_MLA = (MLA_NOPE + MLA_ROPE) ** -0.5 * LOG2E
_TAB_64, _TAB_MLAQ, _TAB_KR = 0, 3, 6

_HGRN_PRE = ("hq", "hff", "hfb", "hi", "hg")
_ATTN_OUTS = (
    ("sq", 512, BF16), ("sk", 256, BF16), ("sv", 256, BF16),
    ("mq", 1024, BF16), ("mk", 1024, BF16), ("mv", 1024, BF16),
    ("gq", 512, BF16), ("gk", 256, BF16), ("gv", 256, BF16), ("gates", 4096, BF16))
_CACHE_OUTS = (("skc", 128), ("svc", 128), ("mckv", 128), ("mkr", MLA_ROPE), ("gkc", 128), ("gvc", 128))
_PROJ_PARAMS = dict(npre=(1, D_MODEL), wa=(D_MODEL, _W_A_COLS), wkr=(D_MODEL, LANES), wb=(D_MODEL, _W_B_COLS),
                    mlaq_g=(1, MLA_Q_RANK), mlakv_g=(1, MLA_KV_RANK),
                    wuq=(MLA_Q_RANK, 1024), wk=(256, 1024), wuv=(MLA_KV_RANK, 1024), gq_g=(1, 512), gk_g=(1, 128))
_HGRN_PARAMS = dict(lbf=(1, 512), lbb=(1, 512), gn=(1, 512))
_PROJ_CONSTS = tuple(_PROJ_PARAMS) + ("b512", "b128")
_HGRN_CONSTS = tuple(_HGRN_PARAMS)


def _proj_out_names(is_ctx):
    attn = [n for n, _, _ in _ATTN_OUTS]
    if is_ctx:
        return ["oa"] + attn + [n for n, _ in _CACHE_OUTS] + ["st"]
    return list(_HGRN_PRE) + attn


def _proj_kernel(*refs, is_ctx, n_alias):
    names = ["x", "mod"] + list(_PROJ_CONSTS) + (list(_HGRN_CONSTS) if is_ctx else ["tab"])
    r = dict(zip(names, refs))
    pos = len(names) + n_alias
    for name in _proj_out_names(is_ctx) + (["hbuf"] if is_ctx else []):
        r[name] = refs[pos]
        pos += 1

    sh1 = r["mod"][0, 0]
    sc1 = r["mod"][0, 1]
    for s in range(PROJ_TILE // PROJ_SUB):
        for _ in _proj_sub_tile(r, s, sh1, sc1, is_ctx):
            pass


def _proj_sub_tile(r, s, sh1, sc1, is_ctx):
    rows = slice(PROJ_SUB * s, PROJ_SUB * (s + 1))
    h = (_rms(r["x"][rows, :], r["npre"][...]) * (1.0 + sc1) + sh1).astype(BF16)

    def seg(where, w, skip=0):
        name, a = where
        return _dot(h, r[name][:, a + skip:a + skip + w])

    def rope(v, base, shift):
        return v if is_ctx else _rope(v, r["tab"], base, shift, rows)

    for j, name in enumerate(_HGRN_PRE):
        u = seg(_C_HGRN, 512, skip=512 * j)
        if is_ctx:
            r["hbuf"][s, j] = u
        else:
            r[name][rows, :] = u
    yield

    if is_ctx:
        assert HGRN_CHUNK == PROJ_SUB
        lvl = _pair_levels()[2]
        for hd in range(HGRN_HEADS):
            cols = slice(LANES * hd, LANES * (hd + 1))
            q = _silu(r["hbuf"][s, 0, :, cols])
            v = r["hbuf"][s, 3, :, cols]
            kl = [_hgrn_forget(r["hbuf"][s, 1 + d, :, cols], lb[:, cols]) for d, lb in ((0, r["lbf"]), (1, r["lbb"]))]
            o, sts = _hgrn_chunk_bidir(q, v, kl[0], kl[1], lvl)
            for d in range(2):
                r["st"][s, 0, d, hd] = sts[d].T
            r["oa"][rows, cols] = _hgrn_finish(o, r["gn"][:, cols], r["hbuf"][s, 4, :, cols])
        yield

    def dup16(k):
        return jnp.concatenate([d.astype(BF16) for d in _dup_heads(k)], axis=1)

    r["sq"][rows, :] = (rope(seg(_C_SWA_Q, 512), _TAB_64, 16) * _SCALE_64).astype(BF16)
    kv = seg(_C_SWA_KV, 256)
    k, v = kv[:, :128], kv[:, 128:]
    r["sk"][rows, :] = dup16(rope(k, _TAB_64, 16))
    r["sv"][rows, :] = dup16(v)
    if is_ctx:
        r["skc"][s, 0] = k
        r["svc"][s, 0] = v

    cq = _rms(seg(_C_MLA_CQ, 256), r["mlaq_g"][...]).astype(BF16)
    q = _dot(cq, r["wuq"][...])
    r["mq"][rows, :] = (rope(q, _TAB_MLAQ, 8) * _SCALE_MLA).astype(BF16)
    ckv = _rms(seg(_C_MLA_CKV, 128), r["mlakv_g"][...])
    kr = seg(_C_MLA_KR, 128)
    if is_ctx:
        r["mckv"][s, 0] = ckv
        r["mkr"][s, 0] = kr[:, :MLA_ROPE]
    kr_used = rope(kr, _TAB_KR, 8)
    ckv16 = ckv.astype(BF16)
    r["mk"][rows, :] = _dot(jnp.concatenate([ckv16, kr_used.astype(BF16)], axis=1), r["wk"][...]).astype(BF16)
    r["mv"][rows, :] = _dup_tiles(_dot(ckv16, r["wuv"][...])).astype(BF16)

    gq = _head_rms(seg(_C_GQA_Q, 512), r["b512"], r["gq_g"][...])
    r["gq"][rows, :] = (rope(gq, _TAB_64, 16) * _SCALE_64).astype(BF16)
    kv = seg(_C_GQA_KV, 256)
    k, v = _head_rms(kv[:, :128], r["b128"], r["gk_g"][...]), kv[:, 128:]
    r["gk"][rows, :] = dup16(rope(k, _TAB_64, 16))
    r["gv"][rows, :] = dup16(v)
    if is_ctx:
        r["gkc"][s, 0] = k
        r["gvc"][s, 0] = v

    for j in range(N_BRANCH):
        g = _sigmoid(seg(_C_GATES, D_MODEL, skip=D_MODEL * j))
        r["gates"][rows, D_MODEL * j:D_MODEL * (j + 1)] = g.astype(BF16)


def _proj(x, mod4, lp, is_ctx, layer=0, tabs=None, prev=None):
    n_tok = x.shape[0]
    params = dict(_PROJ_PARAMS, **(_HGRN_PARAMS if is_ctx else {}))
    n_sub = PROJ_TILE // PROJ_SUB
    in_specs = [pl.BlockSpec((PROJ_TILE, D_MODEL), lambda i: (i, 0)), _mod_spec(is_ctx, layer, PROJ_TILE)]
    in_specs += [_layer_spec(params[n], layer, _W_IN_BLOCK_INDEX.get(n)) for n in _PROJ_PARAMS]
    in_specs += [_const_spec((512, 512)), _const_spec((128, 128))]
    args = [x, mod4] + [lp[n] for n in _PROJ_PARAMS] + [lp["b512"], lp["b128"]]
    if is_ctx:
        in_specs += [_layer_spec(params[n], layer) for n in _HGRN_PARAMS]
        args += [lp[n] for n in _HGRN_PARAMS]
    if not is_ctx:
        in_specs.append(pl.BlockSpec((9, PROJ_TILE, LANES), lambda i: (0, i % (DEC_SEQ // PROJ_TILE), 0)))
        args.append(tabs)

    def tok(w):
        return pl.BlockSpec((PROJ_TILE, w), lambda i: (i, 0))

    specs = {n: (tok(w), jax.ShapeDtypeStruct((n_tok, w), dt)) for n, w, dt in _ATTN_OUTS}
    if is_ctx:
        specs["oa"] = (tok(512), jax.ShapeDtypeStruct((n_tok, 512), BF16))
        for n, w in _CACHE_OUTS:
            specs[n] = (pl.BlockSpec((n_sub, 1, SEQ, w), lambda i: (i, layer, 0, 0)),
                        jax.ShapeDtypeStruct((BATCH, DEPTH, SEQ, w), F32))
        specs["st"] = (pl.BlockSpec((n_sub, 1, 2, HGRN_HEADS, LANES, LANES), lambda i: (i, layer, 0, 0, 0, 0)),
                       jax.ShapeDtypeStruct((BATCH, DEPTH, 2, HGRN_HEADS, LANES, LANES), F32))
    else:
        for n in _HGRN_PRE:
            specs[n] = (tok(512), jax.ShapeDtypeStruct((n_tok, 512), F32))
    out_names = _proj_out_names(is_ctx)
    aliases = {}
    n_alias = 0
    if prev is not None:
        for n in [n for n, _ in _CACHE_OUTS] + ["st"]:
            aliases[len(args)] = out_names.index(n)
            in_specs.append(_ANY_SPEC)
            args.append(prev[n])
            n_alias += 1
    outs = pl.pallas_call(
        functools.partial(_proj_kernel, is_ctx=is_ctx, n_alias=n_alias),
        out_shape=[specs[n][1] for n in out_names], grid=(n_tok // PROJ_TILE,), in_specs=in_specs,
        out_specs=[specs[n][0] for n in out_names],
        scratch_shapes=[pltpu.VMEM((n_sub, len(_HGRN_PRE), PROJ_SUB, 512), F32)] if is_ctx else [],
        input_output_aliases=aliases, compiler_params=_params(1),
        name="proj_ctx" if is_ctx else "proj_lat",
    )(*args)
    return dict(zip(out_names, outs))


def _pair_levels():
    C = HGRN_CHUNK
    ti = lax.broadcasted_iota(jnp.int32, (C, C), 0)
    si = lax.broadcasted_iota(jnp.int32, (C, C), 1)
    x = ti ^ si
    lvl = jnp.where(x == 0, -1, 0)
    b = 2
    while b < C:
        lvl = lvl + (x >= b).astype(jnp.int32)
        b *= 2
    return ti, si, lvl


def _hgrn_forget(f_pre, lb):
    s_pos, s_neg = _sigmoid_pair(f_pre)
    f = lb + (1.0 - lb) * s_pos
    return (1.0 - lb) * s_neg, jnp.log2(jnp.maximum(f, LOG_FLOOR))


def _hgrn_levels():
    ti, si, lvl = _pair_levels()
    return jnp.where(ti > si, lvl, -1), jnp.where(ti < si, lvl, -1)


def _hgrn_chunk(q, k, lf2, v, st, fwd, lvl):
    C = HGRN_CHUNK
    nv = C // SUBLANES
    sub = lax.broadcasted_iota(jnp.int32, (nv, SUBLANES, LANES), 1)

    p = lf2.reshape(nv, SUBLANES, LANES)
    for s in (1, 2, 4):
        if fwd:
            p = p + jnp.where(sub >= s, pltpu.roll(p, s, 1), 0.0)
        else:
            p = p + jnp.where(sub < SUBLANES - s, pltpu.roll(p, SUBLANES - s, 1), 0.0)

    a_mat = jnp.zeros((C, C), F32)
    q16, k16 = q.astype(BF16), k.astype(BF16)

    def add_level(a_mat, e, level):
        e16 = e.astype(BF16)
        return jnp.where(lvl == level, _dot_nt(q16 * e16, k16 * e16), a_mat)

    def row(r):
        return jnp.broadcast_to(p[:, r:r + 1, :], p.shape)

    odd = (sub & 1) == 1
    if fwd:
        bnd = (jnp.where(odd, pltpu.roll(p, 1, 1), p), jnp.where(sub < 4, row(1), row(5)), row(3))
    else:
        bnd = (jnp.where(odd, p, pltpu.roll(p, SUBLANES - 1, 1)), jnp.where(sub < 4, row(2), row(6)), row(4))
    for level, b in enumerate(bnd):
        a_mat = add_level(a_mat, jnp.exp2(-jnp.abs(p - b)).reshape(C, LANES), level)

    p = p.reshape(C, LANES)
    m = SUBLANES
    while m < C:
        nb = C // (2 * m)
        p4 = p.reshape(nb, 2, m, LANES)
        lo, hi = p4[:, 0], p4[:, 1]
        if fwd:
            tot = lo[:, m - 1:m]
            e_lo, e_hi = tot - lo, hi
            p_lo, p_hi = lo, hi + tot
        else:
            tot = hi[:, 0:1]
            e_lo, e_hi = lo, tot - hi
            p_lo, p_hi = lo + tot, hi
        e = jnp.exp2(jnp.concatenate([e_lo[:, None], e_hi[:, None]], axis=1).reshape(C, LANES))
        p = jnp.concatenate([p_lo[:, None], p_hi[:, None]], axis=1).reshape(C, LANES)
        a_mat = add_level(a_mat, e, int(np.log2(m)))
        m *= 2
    v16 = v.astype(BF16)
    o = jnp.sum(q * k, axis=-1, keepdims=True) * v + _dot(a_mat.astype(BF16), v16)

    if st is not None:
        o = o + _dot_nt((q * jnp.exp2(p)).astype(BF16), st.astype(BF16))
    tot = p[C - 1:C] if fwd else p[0:1]
    k_end = (k * jnp.exp2(tot - p)).astype(BF16)
    st_new = _dot_tn(v16, k_end)
    if st is not None:
        st_new = st_new + st * jnp.exp2(tot)
    return o, st_new


def _block_prefix(lf2, sub, fwd):
    p = lf2
    for s in (1, 2, 4):
        if fwd:
            p = p + jnp.where(sub >= s, pltpu.roll(p, s, 1), 0.0)
        else:
            p = p + jnp.where(sub < SUBLANES - s, pltpu.roll(p, SUBLANES - s, 1), 0.0)
    return p


def _hgrn_chunk_bidir(q, v, fwd_kl, bwd_kl, lvl):
    C = HGRN_CHUNK
    nv = C // SUBLANES
    sub = lax.broadcasted_iota(jnp.int32, (nv, SUBLANES, LANES), 1)
    (k_f, lf_f), (k_b, lf_b) = fwd_kl, bwd_kl
    p_f = _block_prefix(lf_f.reshape(nv, SUBLANES, LANES), sub, True)
    p_b = _block_prefix(lf_b.reshape(nv, SUBLANES, LANES), sub, False)
    q16, kf16, kb16 = q.astype(BF16), k_f.astype(BF16), k_b.astype(BF16)
    a_mat = jnp.zeros((C, C), F32)

    def add_level(a_mat, eq_f, ek_f, eq_b, ek_b, level):
        lhs = jnp.concatenate([q16 * eq_f.astype(BF16), q16 * eq_b.astype(BF16)], axis=1)
        rhs = jnp.concatenate([kf16 * ek_f.astype(BF16), kb16 * ek_b.astype(BF16)], axis=1)
        return jnp.where(lvl == level, _dot_nt(lhs, rhs), a_mat)

    def row(p, r):
        return jnp.broadcast_to(p[:, r:r + 1, :], p.shape)

    odd = (sub & 1) == 1
    bnd_f = (jnp.where(odd, pltpu.roll(p_f, 1, 1), p_f), jnp.where(sub < 4, row(p_f, 1), row(p_f, 5)), row(p_f, 3))
    bnd_b = (jnp.where(odd, p_b, pltpu.roll(p_b, SUBLANES - 1, 1)), jnp.where(sub < 4, row(p_b, 2), row(p_b, 6)),
             row(p_b, 4))
    for level in range(3):
        upper = (sub & (1 << level)) != 0
        e_f = jnp.exp2(-jnp.abs(p_f - bnd_f[level]))
        e_b = jnp.exp2(-jnp.abs(p_b - bnd_b[level]))
        a_mat = add_level(a_mat, *(jnp.where(m, e, 0.0).reshape(C, LANES) for m, e in
                                   ((upper, e_f), (~upper, e_f), (~upper, e_b), (upper, e_b))), level)

    p_f, p_b = p_f.reshape(C, LANES), p_b.reshape(C, LANES)
    m = SUBLANES
    while m < C:
        nb = C // (2 * m)

        def halves(lo, hi):
            return jnp.concatenate([lo[:, None], hi[:, None]], axis=1).reshape(C, LANES)

        f4, b4 = p_f.reshape(nb, 2, m, LANES), p_b.reshape(nb, 2, m, LANES)
        zero = jnp.zeros((nb, m, LANES), F32)
        tot_f = f4[:, 0, m - 1:m]
        tot_b = b4[:, 1, 0:1]
        a_mat = add_level(a_mat,
                          halves(zero, jnp.exp2(f4[:, 1])), halves(jnp.exp2(tot_f - f4[:, 0]), zero),
                          halves(jnp.exp2(b4[:, 0]), zero), halves(zero, jnp.exp2(tot_b - b4[:, 1])),
                          int(np.log2(m)))
        p_f = halves(f4[:, 0], f4[:, 1] + tot_f)
        p_b = halves(b4[:, 0] + tot_b, b4[:, 1])
        m *= 2
    v16 = v.astype(BF16)
    o = jnp.sum(q * (k_f + k_b), axis=-1, keepdims=True) * v + _dot(a_mat.astype(BF16), v16)
    st_f = _dot_tn(v16, (k_f * jnp.exp2(p_f[C - 1:C] - p_f)).astype(BF16))
    st_b = _dot_tn(v16, (k_b * jnp.exp2(p_b[0:1] - p_b)).astype(BF16))
    return o, (st_f, st_b)


def _hgrn_finish(o, gn, g_pre):
    return (_rms(o, gn) * _silu(g_pre)).astype(BF16)


def _hgrn_lat_kernel(lbf_ref, lbb_ref, gn_ref, q_ref, ff_ref, fb_ref, v_ref, g_ref, s0_ref, o_ref,
                     of_ref, ob_ref, st_ref, *, n_chunks):
    C = HGRN_CHUNK
    for d in range(2):
        st_ref[d] = s0_ref[0, 0, d, 0].T
    lbf = lbf_ref[...]
    lbb = lbb_ref[...]
    lvls = _hgrn_levels()

    def body(c, carry):
        rf = pl.multiple_of(c * C, C)
        rb = pl.multiple_of((n_chunks - 1 - c) * C, C)
        for d, r0, f_ref, lb, out_ref in ((0, rf, ff_ref, lbf, of_ref), (1, rb, fb_ref, lbb, ob_ref)):
            sl = pl.ds(r0, C)
            k, lf2 = _hgrn_forget(f_ref[sl, :], lb)
            out_ref[sl, :], st_ref[d] = _hgrn_chunk(_silu(q_ref[sl, :]), k, lf2, v_ref[sl, :], st_ref[d], d == 0,
                                                    lvls[d])
        return carry

    lax.fori_loop(0, n_chunks, body, 0, unroll=HGRN_UNROLL)
    o_ref[...] = _hgrn_finish(of_ref[...] + ob_ref[...], gn_ref[...], g_ref[...])


def _hgrn_lat(p, lp, layer, state_in):
    tok_spec = pl.BlockSpec((DEC_SEQ, LANES), lambda b, h: (b, h))
    vec_spec = pl.BlockSpec((None, 1, LANES), lambda b, h: (layer, 0, h))
    state_spec = pl.BlockSpec((1, 1, 2, 1, LANES, LANES), lambda b, h: (b, layer, 0, h, 0, 0))
    return pl.pallas_call(
        functools.partial(_hgrn_lat_kernel, n_chunks=DEC_SEQ // HGRN_CHUNK),
        out_shape=jax.ShapeDtypeStruct((N_LAT, 512), BF16), grid=(DEC_BATCH, HGRN_HEADS),
        in_specs=[vec_spec] * 3 + [tok_spec] * 5 + [state_spec], out_specs=tok_spec,
        scratch_shapes=[pltpu.VMEM((DEC_SEQ, LANES), F32), pltpu.VMEM((DEC_SEQ, LANES), F32),
                        pltpu.VMEM((2, LANES, LANES), F32)],
        compiler_params=_params(2), name="hgrn_lat",
    )(lp["lbf"], lp["lbb"], lp["gn"], p["hq"], p["hff"], p["hfb"], p["hi"], p["hg"], state_in)


def _lane_lo(shape):
    return (lax.broadcasted_iota(jnp.int32, shape, len(shape) - 1) & (LANES - 1)) < HEAD_DIM


def _softmax_unit(make_segs, sink_col, finish):
    segs = make_segs()
    mx = None
    for s, _ in segs:
        m = jnp.max(s, axis=-1, keepdims=True)
        mx = m if mx is None else jnp.maximum(mx, m)
    if sink_col is not None:
        mx = jnp.maximum(mx, sink_col)
    yield
    acc = None
    for s, v in segs:
        v_ones = jnp.concatenate([v, jnp.ones(v.shape, BF16)], axis=1)
        pv = _dot(jnp.exp2(s - mx).astype(BF16), v_ones)
        acc = pv if acc is None else acc + pv
    den = acc[:, LANES:]
    if sink_col is not None:
        den = den + jnp.exp2(sink_col - mx)
    finish(acc[:, :LANES] / den)


_DONE = object()


def _run_units(units, lookahead=1):
    pending = list(units)
    active = []
    for _ in range(lookahead - 1):
        if pending:
            u = pending.pop(0)
            next(u)
            active.insert(0, u)
    while pending or active:
        started = []
        if pending:
            u = pending.pop(0)
            next(u)
            started.append(u)
        n_adv = max(len(active) - (lookahead - 1), 1) if pending else len(active)
        keep = active[:len(active) - n_adv]
        active = started + keep + [a for a in active[len(active) - n_adv:] if next(a, _DONE) is not _DONE]


def _gqa_unit(q_ref, g, tq, segs_fn, o_ref, sink_ref=None):
    lo = _lane_lo((tq, LANES))
    parts = []
    for j in range(2):
        qp = q_ref[:, 256 * g + 128 * j:256 * g + 128 * (j + 1)]
        zero = jnp.zeros_like(qp)
        parts += [jnp.where(lo, qp, zero), jnp.where(lo, zero, qp)]
    q4 = jnp.concatenate(parts, axis=0)
    sink_col = None
    if sink_ref is not None:
        sink_col = jnp.concatenate(
            [jnp.broadcast_to(sink_ref[0:1, 4 * g + i:4 * g + i + 1] * LOG2E, (tq, 1)) for i in range(4)], axis=0)

    def finish(pv):
        o_ref[:, 256 * g:256 * g + 128] = jnp.where(lo, pv[0:tq], pv[tq:2 * tq]).astype(BF16)
        o_ref[:, 256 * g + 128:256 * (g + 1)] = jnp.where(lo, pv[2 * tq:3 * tq], pv[3 * tq:4 * tq]).astype(BF16)

    yield from _softmax_unit(lambda: segs_fn(q4), sink_col, finish)


def _mla_pair_unit(q_ref, hp, tq, segs_fn, o_ref):
    outs = []
    subs = [_softmax_unit(functools.partial(segs_fn, h, q_ref[:, 128 * h:128 * (h + 1)]), None, outs.append)
            for h in (2 * hp, 2 * hp + 1)]
    while True:
        if [next(s, _DONE) for s in subs][0] is _DONE:
            break
        yield
    o_ref[:, 128 * hp:128 * (hp + 1)] = jnp.where(_lane_lo((tq, LANES)), outs[0], outs[1]).astype(BF16)


_ATTN_IN = ("sq", "sk", "sv", "mq", "mk", "mv", "gq", "gk", "gv")
_ATTN_WIDTH = dict(sq=512, sk=256, sv=256, mq=1024, mk=1024, mv=1024, gq=512, gk=256, gv=256)


def _attn_ctx_kernel(sink_ref, sq_ref, sk_ref, sv_ref, mq_ref, mk_ref, mv_ref, gq_ref, gk_ref, gv_ref,
                     ob_ref, oc_ref, od_ref):
    tq = SEQ
    units = []
    for g in range(N_KV):
        def swa_segs(q4, g=g):
            return [(_dot_nt(q4, sk_ref[:, 128 * g:128 * (g + 1)]), sv_ref[:, 128 * g:128 * (g + 1)])]

        def gqa_segs(q4, g=g):
            return [(_dot_nt(q4, gk_ref[:, 128 * g:128 * (g + 1)]), gv_ref[:, 128 * g:128 * (g + 1)])]

        units += [_gqa_unit(sq_ref, g, tq, swa_segs, ob_ref, sink_ref), _gqa_unit(gq_ref, g, tq, gqa_segs, od_ref)]

    def mla_segs(h, q_h):
        return [(_dot_nt(q_h, mk_ref[:, 128 * h:128 * (h + 1)]), mv_ref[:, 128 * h:128 * (h + 1)])]

    units += [_mla_pair_unit(mq_ref, hp, tq, mla_segs, oc_ref) for hp in range(MLA_HEADS // 2)]
    _run_units(units)


def _attn_ctx(lp, p, layer):
    def spec(w):
        return pl.BlockSpec((SEQ, w), lambda b: (b, 0))

    return pl.pallas_call(
        _attn_ctx_kernel,
        out_shape=[jax.ShapeDtypeStruct((N_CTX, 512), BF16)] * 3,
        grid=(BATCH,),
        in_specs=[_layer_spec((1, N_HEADS), layer)] + [spec(_ATTN_WIDTH[n]) for n in _ATTN_IN],
        out_specs=[spec(512)] * 3,
        compiler_params=_params(1),
        name="attn_ctx",
    )(lp["sink"], *[p[n] for n in _ATTN_IN])


def _dup_heads(x):
    lo = _lane_lo(x.shape)
    r = pltpu.roll(x, HEAD_DIM, 1)
    return jnp.where(lo, x, r), jnp.where(lo, r, x)


def _dup_tiles(x):
    return jnp.where(_lane_lo(x.shape), x, pltpu.roll(x, HEAD_DIM, 1))


def _attn_lat_kernel(sink_ref, wk_ref, wuv_ref,
                     sq_ref, sk_ref, sv_ref, mq_ref, mk_ref, mv_ref, gq_ref, gk_ref, gv_ref,
                     csk_ref, csv_ref, cckv_ref, ckr_ref, cgk_ref, cgv_ref,
                     ob_ref, oc_ref, od_ref,
                     cs_k, cs_v, cm_k, cm_v, cg_k, cg_v):
    tq = TQ_LAT
    qi = pl.program_id(1)

    @pl.when(qi == 0)
    def _():
        for src, dst in ((csk_ref, cs_k), (csv_ref, cs_v), (cgk_ref, cg_k), (cgv_ref, cg_v)):
            d0, d1 = _dup_heads(src[0, 0])
            dst[0] = d0.astype(BF16)
            dst[1] = d1.astype(BF16)
        ckv16 = cckv_ref[0, 0].astype(BF16)
        k = _dot(ckv16, wk_ref[0:MLA_KV_RANK, :]) + _dot(ckr_ref[0, 0].astype(BF16),
                                                           wk_ref[MLA_KV_RANK:MLA_KV_RANK + MLA_ROPE, :])
        cm_k[...] = k.astype(BF16)
        cm_v[...] = _dup_tiles(_dot(ckv16, wuv_ref[...])).astype(BF16)

    q0 = qi * tq
    win = tq + 2 * SWA_WINDOW
    start = pl.multiple_of(jnp.clip(q0 - SWA_WINDOW, 0, DEC_SEQ - win), SWA_WINDOW)
    rows = lax.broadcasted_iota(jnp.int32, (4 * tq, win), 0)
    qpos = q0 + (rows & (tq - 1))
    kpos = start + lax.broadcasted_iota(jnp.int32, (4 * tq, win), 1)
    band = jnp.abs(qpos - kpos) <= SWA_WINDOW
    units = []
    for g in range(N_KV):
        def swa_segs(q4, g=g):
            ks = sk_ref[pl.ds(start, win), 128 * g:128 * (g + 1)]
            vs = sv_ref[pl.ds(start, win), 128 * g:128 * (g + 1)]
            s_own = jnp.where(band, _dot_nt(q4, ks), MASK_VALUE)
            return [(_dot_nt(q4, cs_k[g]), cs_v[g]), (s_own, vs)]

        def gqa_segs(q4, g=g):
            return [(_dot_nt(q4, cg_k[g]), cg_v[g]),
                    (_dot_nt(q4, gk_ref[:, 128 * g:128 * (g + 1)]), gv_ref[:, 128 * g:128 * (g + 1)])]

        units += [_gqa_unit(sq_ref, g, tq, swa_segs, ob_ref, sink_ref), _gqa_unit(gq_ref, g, tq, gqa_segs, od_ref)]

    def mla_segs(h, q_h):
        cols = slice(128 * h, 128 * (h + 1))
        return [(_dot_nt(q_h, cm_k[:, cols]), cm_v[:, cols]), (_dot_nt(q_h, mk_ref[:, cols]), mv_ref[:, cols])]

    mla = [_mla_pair_unit(mq_ref, hp, tq, mla_segs, oc_ref) for hp in range(MLA_HEADS // 2)]
    _run_units([u for pair in zip(units, mla) for u in pair], lookahead=3)


def _attn_lat(lp, p, caches, layer):
    nq = DEC_SEQ // TQ_LAT

    def qspec(w):
        return pl.BlockSpec((TQ_LAT, w), lambda b, i: (b * nq + i, 0))

    def kspec(w):
        return pl.BlockSpec((DEC_SEQ, w), lambda b, i: (b, 0))

    def cspec(w):
        return pl.BlockSpec((1, 1, PAST_LEN, w), lambda b, i: (b, layer, 0, 0))

    is_q = dict(sq=True, mq=True, gq=True)
    in_specs = [_layer_spec((1, N_HEADS), layer), _layer_spec((256, 1024), layer),
                _layer_spec((MLA_KV_RANK, 1024), layer)]
    in_specs += [(qspec if is_q.get(n) else kspec)(_ATTN_WIDTH[n]) for n in _ATTN_IN]
    in_specs += [cspec(128), cspec(128), cspec(128), cspec(MLA_ROPE), cspec(128), cspec(128)]
    return pl.pallas_call(
        _attn_lat_kernel,
        out_shape=[jax.ShapeDtypeStruct((N_LAT, 512), BF16)] * 3,
        grid=(DEC_BATCH, nq),
        in_specs=in_specs,
        out_specs=[qspec(512)] * 3,
        scratch_shapes=[pltpu.VMEM((N_KV, PAST_LEN, LANES), BF16), pltpu.VMEM((N_KV, PAST_LEN, LANES), BF16),
                        pltpu.VMEM((PAST_LEN, 1024), BF16), pltpu.VMEM((PAST_LEN, 1024), BF16),
                        pltpu.VMEM((N_KV, PAST_LEN, LANES), BF16), pltpu.VMEM((N_KV, PAST_LEN, LANES), BF16)],
        compiler_params=_params(2),
        name="attn_lat",
    )(lp["sink"], lp["wk"], lp["wuv"], *[p[n] for n in _ATTN_IN], *caches)


MERGE_TILE = 512
MERGE_SUB = 256


def _merge_kernel(x_ref, mod_ref, oa_ref, ob_ref, oc_ref, od_ref, gates_ref, wb_ref, wo_ref, npost_ref,
                  nmpre_ref, nmpost_ref, w1_ref, w2_ref, y_ref):
    g1 = mod_ref[0, 2]
    sh2 = mod_ref[0, 3]
    sc2 = mod_ref[0, 4]
    g2 = mod_ref[0, 5]

    def sub_tile(r0):
        rows = slice(r0, r0 + MERGE_SUB)
        acc = None
        for n, o_ref in enumerate((oa_ref, ob_ref, oc_ref, od_ref)):
            br = _dot(o_ref[rows, :], wb_ref[n]) * gates_ref[rows, D_MODEL * n:D_MODEL * (n + 1)].astype(F32)
            acc = br if acc is None else acc + br
        y = _dot(acc.astype(BF16), wo_ref[...])
        yield
        x = x_ref[rows, :] + g1 * _rms(y, npost_ref[...])
        h = (_rms(x, nmpre_ref[...]) * (1.0 + sc2) + sh2).astype(BF16)
        a = jnp.maximum(_dot(h, w1_ref[...]), 0.0)
        yield
        y = _dot((a * a).astype(BF16), w2_ref[...])
        y_ref[rows, :] = x + g2 * _rms(y, nmpost_ref[...])

    _run_units([sub_tile(r0) for r0 in range(0, MERGE_TILE, MERGE_SUB)])


_MERGE_PARAMS = dict(w_branch=(N_BRANCH, BRANCH_W, D_MODEL), w_out=(D_MODEL, D_MODEL), npost=(1, D_MODEL),
                     nmpre=(1, D_MODEL), nmpost=(1, D_MODEL), w_mlp_in=(D_MODEL, D_FF), w_mlp_out=(D_FF, D_MODEL))


def _merge(x, mod4, o_a, o_b, o_c, o_d, gates, lp, is_ctx, layer):
    n_tok = x.shape[0]

    def tok(w):
        return pl.BlockSpec((MERGE_TILE, w), lambda i: (i, 0))

    in_specs = [tok(D_MODEL), _mod_spec(is_ctx, layer, MERGE_TILE), tok(512), tok(512), tok(512), tok(512), tok(4096)]
    in_specs += [_layer_spec(shape, layer) for shape in _MERGE_PARAMS.values()]
    return pl.pallas_call(
        _merge_kernel, out_shape=jax.ShapeDtypeStruct((n_tok, D_MODEL), F32), grid=(n_tok // MERGE_TILE,),
        in_specs=in_specs, out_specs=tok(D_MODEL), compiler_params=_params(1),
        name="merge_ctx" if is_ctx else "merge_lat",
    )(x, mod4, o_a, o_b, o_c, o_d, gates, *[lp[n] for n in _MERGE_PARAMS])


def _rope_tables():
    t = np.arange(DEC_SEQ)
    row, col = (t // GRID_W).astype(np.float32), (t % GRID_W).astype(np.float32)

    def pattern(d):
        nf = d // 4
        inv = (ROPE_BASE ** (-np.arange(nf, dtype=np.float32) / nf)).astype(np.float32)
        ang = np.concatenate([row[:, None] * inv, row[:, None] * inv, col[:, None] * inv, col[:, None] * inv], axis=1)
        cos, sin = np.cos(ang), np.sin(ang)
        second = (np.arange(d) % (2 * nf)) >= nf
        return cos, np.where(second, sin, 0.0), np.where(second, 0.0, -sin)

    out = np.zeros((9, DEC_SEQ, LANES), np.float32)
    out[[0, 3, 6]] = 1.0
    c, s1, s2 = pattern(HEAD_DIM)
    for a, tab in enumerate((c, s1, s2)):
        out[_TAB_64 + a] = np.tile(tab, (1, LANES // HEAD_DIM))
    c, s1, s2 = pattern(MLA_ROPE)
    for a, tab in enumerate((c, s1, s2)):
        out[_TAB_MLAQ + a][:, MLA_NOPE:MLA_NOPE + MLA_ROPE] = tab
        out[_TAB_KR + a][:, :MLA_ROPE] = tab
    return out


_ROPE_TABLES = _rope_tables()


def _block_mean_matrix(n):
    i = np.arange(n)
    return ((i[:, None] // HEAD_DIM) == (i[None, :] // HEAD_DIM)).astype(np.float32) / HEAD_DIM


def _mla_weight_layouts(w_uq, w_ukv):
    dq = MLA_NOPE + MLA_ROPE
    wq = w_uq.reshape(DEPTH, MLA_Q_RANK, MLA_HEADS, dq)
    wq = jnp.pad(wq, ((0, 0), (0, 0), (0, 0), (0, LANES - dq))).reshape(DEPTH, MLA_Q_RANK, MLA_HEADS * LANES)
    wkv = w_ukv.reshape(DEPTH, MLA_KV_RANK, MLA_HEADS, MLA_NOPE + MLA_V)
    wk = jnp.pad(wkv[..., :MLA_NOPE], ((0, 0), (0, 0), (0, 0), (0, LANES - MLA_NOPE)))
    wk = wk.reshape(DEPTH, MLA_KV_RANK, MLA_HEADS * LANES)
    place = np.zeros((LANES, MLA_HEADS, LANES), np.float32)
    for j in range(MLA_ROPE):
        place[j, :, MLA_NOPE + j] = 1.0
    place = jnp.broadcast_to(jnp.asarray(place.reshape(1, LANES, MLA_HEADS * LANES)), (DEPTH, LANES, MLA_HEADS * LANES))
    w_k = jnp.concatenate([wk, place], axis=1)
    w_uv = jnp.pad(wkv[..., MLA_NOPE:], ((0, 0), (0, 0), (0, 0), (0, LANES - MLA_V)))
    w_uv = w_uv.reshape(DEPTH, MLA_KV_RANK, MLA_HEADS * LANES)
    return wq.astype(BF16), w_k.astype(BF16), w_uv.astype(BF16)


def _lower_bounds(p):
    s = jax.nn.softmax(p.astype(F32), axis=0)
    return jnp.cumsum(s, axis=0) - s[0]


def kernel(x_prompt, x_sample, state_hgrn, cache_swa_k, cache_swa_v, cache_mla_ckv, cache_mla_kr, cache_gqa_k, cache_gqa_v, c, c_ctx, w_ada, b_ada, norm_mix_pre, norm_mix_post, norm_mlp_pre, norm_mlp_post, w_in, hgrn_lb_fwd, hgrn_lb_bwd, hgrn_norm, swa_sink, mla_q_norm, mla_kv_norm, mla_w_uq, mla_w_ukv, gqa_q_norm, gqa_k_norm, w_branch, w_out, w_mlp_in, w_mlp_out):
    tabs = jnp.asarray(_ROPE_TABLES)
    b512 = jnp.asarray(_block_mean_matrix(512)).astype(BF16)

    def rows(a):
        return a.reshape(DEPTH, 1, -1)

    w_uq, w_k, w_uv = _mla_weight_layouts(mla_w_uq, mla_w_ukv)
    w_a, w_kr, w_b = _split_w_in(w_in)
    lp = dict(
        npre=rows(norm_mix_pre), wa=w_a, wkr=w_kr, wb=w_b, mlaq_g=rows(mla_q_norm), mlakv_g=rows(mla_kv_norm),
        wuq=w_uq, wk=w_k, wuv=w_uv, gq_g=rows(jnp.tile(gqa_q_norm, (1, N_HEADS))),
        gk_g=rows(jnp.tile(gqa_k_norm, (1, N_KV))), b512=b512, b128=b512[:128, :128],
        lbf=rows(_lower_bounds(hgrn_lb_fwd)), lbb=rows(_lower_bounds(hgrn_lb_bwd)), gn=rows(hgrn_norm),
        sink=rows(swa_sink),
        w_branch=w_branch.astype(BF16), w_out=w_out.astype(BF16), npost=rows(norm_mix_post),
        nmpre=rows(norm_mlp_pre), nmpost=rows(norm_mlp_post),
        w_mlp_in=w_mlp_in.astype(BF16), w_mlp_out=w_mlp_out.astype(BF16))

    cond8 = jnp.zeros((SUBLANES, D_MODEL), F32).at[0].set(c_ctx).at[1:1 + DEC_BATCH].set(c)
    mod4 = _ada(cond8, w_ada, b_ada).reshape(DEPTH, SUBLANES, N_MOD, 1, D_MODEL)

    caches_in = (cache_swa_k.reshape(DEC_BATCH, DEPTH, PAST_LEN, 128),
                 cache_swa_v.reshape(DEC_BATCH, DEPTH, PAST_LEN, 128), cache_mla_ckv, cache_mla_kr,
                 cache_gqa_k.reshape(DEC_BATCH, DEPTH, PAST_LEN, 128),
                 cache_gqa_v.reshape(DEC_BATCH, DEPTH, PAST_LEN, 128))

    x_ctx = x_prompt.reshape(N_CTX, D_MODEL)
    x_lat = x_sample.reshape(N_LAT, D_MODEL)
    pc = None
    for l in range(DEPTH):
        pc = _proj(x_ctx, mod4, lp, True, layer=l, prev=pc)
        ob, oc, od = _attn_ctx(lp, pc, l)
        x_ctx = _merge(x_ctx, mod4, pc["oa"], ob, oc, od, pc["gates"], lp, True, l)

        pq = _proj(x_lat, mod4, lp, False, layer=l, tabs=tabs)
        oa = _hgrn_lat(pq, lp, l, state_hgrn)
        ob, oc, od = _attn_lat(lp, pq, caches_in, l)
        x_lat = _merge(x_lat, mod4, oa, ob, oc, od, pq["gates"], lp, False, l)

    kv_shape = (BATCH, DEPTH, SEQ, N_KV, HEAD_DIM)
    return (x_ctx.reshape(BATCH, SEQ, D_MODEL), x_lat.reshape(DEC_BATCH, DEC_SEQ, D_MODEL), pc["st"],
            pc["skc"].reshape(kv_shape), pc["svc"].reshape(kv_shape), pc["mckv"], pc["mkr"],
            pc["gkc"].reshape(kv_shape), pc["gvc"].reshape(kv_shape))
```

```python
import functools

import jax
import jax.numpy as jnp
import numpy as np
from jax import lax
from jax.experimental import pallas as pl
from jax.experimental.pallas import tpu as pltpu

D_MODEL = 1024
BATCH, SEQ = 32, 256
DEC_BATCH, DEC_SEQ = 2, 2048
DEPTH = 2
PAST_LEN = 512
GRID_W = 64
ROPE_BASE = 10000.0
NORM_EPS = 1e-6
MASK_VALUE = -1e30
LOG_FLOOR = 1e-30
N_MOD = 6
N_BRANCH = 4
BRANCH_W = 512
D_FF = 4 * D_MODEL
HGRN_HEADS, HGRN_DK = 4, 128
SWA_WINDOW = 128
MLA_HEADS, MLA_Q_RANK, MLA_KV_RANK, MLA_NOPE, MLA_ROPE, MLA_V = 8, 256, 128, 64, 32, 64
HEAD_DIM = 64
N_HEADS, N_KV = 8, 2

N_CTX = BATCH * SEQ
N_LAT = DEC_BATCH * DEC_SEQ

LANES = 128
SUBLANES = 8
VMEM_LIMIT_BYTES = 56 * 1024 * 1024

PROJ_TILE = 256
PROJ_SUB = 256
HGRN_CHUNK = 256
HGRN_UNROLL = 4
TQ_LAT = 128

BF16 = jnp.bfloat16
F32 = jnp.float32

_W_A_COLS = 3712
_W_B_COLS = 8608 - _W_A_COLS - MLA_ROPE
_C_HGRN = ("wa", 0)
_C_SWA_Q = ("wa", 2560)
_C_SWA_KV = ("wa", 3072)
_C_MLA_CQ = ("wa", 3328)
_C_MLA_CKV = ("wa", 3584)
_C_MLA_KR = ("wkr", 0)
_C_GQA_Q = ("wb", 0)
_C_GQA_KV = ("wb", 512)
_C_GATES = ("wb", 768)


def _split_w_in(w):
    rows = 256
    n_cols = w.shape[-1]
    w16, tail = pl.pallas_call(
        _w_in_cast_kernel,
        out_shape=[jax.ShapeDtypeStruct(w.shape, BF16), jax.ShapeDtypeStruct((DEPTH, D_MODEL, _W_B_COLS), BF16)],
        grid=(DEPTH, D_MODEL // rows),
        in_specs=[pl.BlockSpec((1, rows, n_cols), lambda l, i: (l, i, 0))],
        out_specs=[pl.BlockSpec((1, rows, n_cols), lambda l, i: (l, i, 0)),
                   pl.BlockSpec((1, rows, _W_B_COLS), lambda l, i: (l, i, 0))],
        compiler_params=_params(2), name="w_in_cast",
    )(w)
    return w16, w16, tail


def _w_in_cast_kernel(w_ref, full_ref, tail_ref):
    x = w_ref[0]
    full_ref[0] = x.astype(BF16)
    tail_ref[0] = x[:, _W_A_COLS + MLA_ROPE:].astype(BF16)


_W_IN_BLOCK_INDEX = dict(wkr=(0, _W_A_COLS // LANES))


def _sigmoid_pair(x):
    a = jnp.exp(-jnp.abs(x))
    r = 1.0 / (1.0 + a)
    ar = a * r
    pos = x >= 0
    return jnp.where(pos, r, ar), jnp.where(pos, ar, r)


def _sigmoid(x):
    return 1.0 / (1.0 + jnp.exp2(x * -LOG2E))


def _silu(x):
    return x * _sigmoid(x)


def _rms(x, gain):
    return x * lax.rsqrt(jnp.mean(x * x, axis=-1, keepdims=True) + NORM_EPS) * gain


def _dot(a, b):
    return jnp.dot(a, b, preferred_element_type=F32)


def _dot_nt(a, b):
    return lax.dot_general(a, b, (((1,), (1,)), ((), ())), preferred_element_type=F32)


def _dot_tn(a, b):
    return lax.dot_general(a, b, (((0,), (0,)), ((), ())), preferred_element_type=F32)


def _tile_lanes(t, width):
    reps = width // LANES
    return t if reps == 1 else jnp.concatenate([t] * reps, axis=1)


def _rope(x, tab_ref, base, shift, rows):
    w = x.shape[1]
    c = _tile_lanes(tab_ref[base, rows, :], w)
    s1 = _tile_lanes(tab_ref[base + 1, rows, :], w)
    s2 = _tile_lanes(tab_ref[base + 2, rows, :], w)
    return x * c + pltpu.roll(x, shift, 1) * s1 + pltpu.roll(x, w - shift, 1) * s2


def _head_rms(x, bmat_ref, gain):
    ms = _dot((x * x).astype(BF16), bmat_ref[...])
    return x * lax.rsqrt(ms + NORM_EPS) * gain


def _const_spec(shape):
    nd = len(shape)
    return pl.BlockSpec(shape, lambda *_: (0,) * nd, pipeline_mode=pl.Buffered(1))


def _layer_spec(shape, layer, block_index=None):
    idx = (layer,) + tuple(block_index or (0,) * len(shape))
    return pl.BlockSpec((None,) + tuple(shape), lambda *_: idx, pipeline_mode=pl.Buffered(1))


_ANY_SPEC = pl.BlockSpec(memory_space=pl.ANY)


def _params(n_grid_dims):
    return pltpu.CompilerParams(dimension_semantics=("arbitrary",) * n_grid_dims, vmem_limit_bytes=VMEM_LIMIT_BYTES)


_ADA_TN = 2048


def _ada_kernel(c_ref, w_ref, b_ref, o_ref):
    s = _silu(c_ref[...]).astype(BF16)
    o_ref[0] = _dot(s, w_ref[0].astype(BF16)) + b_ref[0]


def _ada(cond8, w_ada, b_ada):
    n = N_MOD * D_MODEL
    return pl.pallas_call(
        _ada_kernel,
        out_shape=jax.ShapeDtypeStruct((DEPTH, SUBLANES, n), F32),
        grid=(DEPTH, n // _ADA_TN),
        in_specs=[
            pl.BlockSpec((SUBLANES, D_MODEL), lambda l, j: (0, 0)),
            pl.BlockSpec((1, D_MODEL, _ADA_TN), lambda l, j: (l, 0, j)),
            pl.BlockSpec((1, 1, _ADA_TN), lambda l, j: (l, 0, j)),
        ],
        out_specs=pl.BlockSpec((1, SUBLANES, _ADA_TN), lambda l, j: (l, 0, j)),
        compiler_params=_params(2),
        name="ada",
    )(cond8, w_ada, b_ada.reshape(DEPTH, 1, n))


def _mod_spec(is_ctx, layer, tile):
    if is_ctx:
        return pl.BlockSpec((None, 1, N_MOD, 1, D_MODEL), lambda i: (layer, 0, 0, 0, 0))
    per_seq = DEC_SEQ // tile
    return pl.BlockSpec((None, 1, N_MOD, 1, D_MODEL), lambda i: (layer, 1 + i // per_seq, 0, 0, 0))


LOG2E = float(np.log2(np.e))
_SCALE_64 = HEAD_DIM ** -0.5 * LOG2E
_SCALE_MLA = (MLA_NOPE + MLA_ROPE) ** -0.5 * LOG2E
_TAB_64, _TAB_MLAQ, _TAB_KR = 0, 3, 6

_HGRN_PRE = ("hq", "hff", "hfb", "hi", "hg")
_ATTN_OUTS = (
    ("sq", 512, BF16), ("sk", 256, BF16), ("sv", 256, BF16),
    ("mq", 1024, BF16), ("mk", 1024, BF16), ("mv", 1024, BF16),
    ("gq", 512, BF16), ("gk", 256, BF16), ("gv", 256, BF16), ("gates", 4096, BF16))
_CACHE_OUTS = (("skc", 128), ("svc", 128), ("mckv", 128), ("mkr", MLA_ROPE), ("gkc", 128), ("gvc", 128))
_PROJ_PARAMS = dict(npre=(1, D_MODEL), wa=(D_MODEL, _W_A_COLS), wkr=(D_MODEL, LANES), wb=(D_MODEL, _W_B_COLS),
                    mlaq_g=(1, MLA_Q_RANK), mlakv_g=(1, MLA_KV_RANK),
                    wuq=(MLA_Q_RANK, 1024), wk=(256, 1024), wuv=(MLA_KV_RANK, 1024), gq_g=(1, 512), gk_g=(1, 128))
_HGRN_PARAMS = dict(lbf=(1, 512), lbb=(1, 512), gn=(1, 512))
_PROJ_CONSTS = tuple(_PROJ_PARAMS) + ("b512", "b128")
_HGRN_CONSTS = tuple(_HGRN_PARAMS)


def _proj_out_names(is_ctx):
    attn = [n for n, _, _ in _ATTN_OUTS]
    if is_ctx:
        return ["oa"] + attn + [n for n, _ in _CACHE_OUTS] + ["st"]
    return list(_HGRN_PRE) + attn


def _proj_kernel(*refs, is_ctx, n_alias):
    names = ["x", "mod"] + list(_PROJ_CONSTS) + (list(_HGRN_CONSTS) if is_ctx else ["tab"])
    r = dict(zip(names, refs))
    pos = len(names) + n_alias
    for name in _proj_out_names(is_ctx) + (["hbuf"] if is_ctx else []):
        r[name] = refs[pos]
        pos += 1

    sh1 = r["mod"][0, 0]
    sc1 = r["mod"][0, 1]
    for s in range(PROJ_TILE // PROJ_SUB):
        for _ in _proj_sub_tile(r, s, sh1, sc1, is_ctx):
            pass


def _proj_sub_tile(r, s, sh1, sc1, is_ctx):
    rows = slice(PROJ_SUB * s, PROJ_SUB * (s + 1))
    h = (_rms(r["x"][rows, :], r["npre"][...]) * (1.0 + sc1) + sh1).astype(BF16)

    def seg(where, w, skip=0):
        name, a = where
        return _dot(h, r[name][:, a + skip:a + skip + w])

    def rope(v, base, shift):
        return v if is_ctx else _rope(v, r["tab"], base, shift, rows)

    for j, name in enumerate(_HGRN_PRE):
        u = seg(_C_HGRN, 512, skip=512 * j)
        if is_ctx:
            r["hbuf"][s, j] = u
        else:
            r[name][rows, :] = u
    yield

    if is_ctx:
        assert HGRN_CHUNK == PROJ_SUB
        lvl = _pair_levels()[2]
        for hd in range(HGRN_HEADS):
            cols = slice(LANES * hd, LANES * (hd + 1))
            q = _silu(r["hbuf"][s, 0, :, cols])
            v = r["hbuf"][s, 3, :, cols]
            kl = [_hgrn_forget(r["hbuf"][s, 1 + d, :, cols], lb[:, cols]) for d, lb in ((0, r["lbf"]), (1, r["lbb"]))]
            o, sts = _hgrn_chunk_bidir(q, v, kl[0], kl[1], lvl)
            for d in range(2):
                r["st"][s, 0, d, hd] = sts[d].T
            r["oa"][rows, cols] = _hgrn_finish(o, r["gn"][:, cols], r["hbuf"][s, 4, :, cols])
        yield

    def dup16(k):
        return jnp.concatenate([d.astype(BF16) for d in _dup_heads(k)], axis=1)

    r["sq"][rows, :] = (rope(seg(_C_SWA_Q, 512), _TAB_64, 16) * _SCALE_64).astype(BF16)
    kv = seg(_C_SWA_KV, 256)
    k, v = kv[:, :128], kv[:, 128:]
    r["sk"][rows, :] = dup16(rope(k, _TAB_64, 16))
    r["sv"][rows, :] = dup16(v)
    if is_ctx:
        r["skc"][s, 0] = k
        r["svc"][s, 0] = v

    cq = _rms(seg(_C_MLA_CQ, 256), r["mlaq_g"][...]).astype(BF16)
    q = _dot(cq, r["wuq"][...])
    r["mq"][rows, :] = (rope(q, _TAB_MLAQ, 8) * _SCALE_MLA).astype(BF16)
    ckv = _rms(seg(_C_MLA_CKV, 128), r["mlakv_g"][...])
    kr = seg(_C_MLA_KR, 128)
    if is_ctx:
        r["mckv"][s, 0] = ckv
        r["mkr"][s, 0] = kr[:, :MLA_ROPE]
    kr_used = rope(kr, _TAB_KR, 8)
    ckv16 = ckv.astype(BF16)
    r["mk"][rows, :] = _dot(jnp.concatenate([ckv16, kr_used.astype(BF16)], axis=1), r["wk"][...]).astype(BF16)
    r["mv"][rows, :] = _dup_tiles(_dot(ckv16, r["wuv"][...])).astype(BF16)

    gq = _head_rms(seg(_C_GQA_Q, 512), r["b512"], r["gq_g"][...])
    r["gq"][rows, :] = (rope(gq, _TAB_64, 16) * _SCALE_64).astype(BF16)
    kv = seg(_C_GQA_KV, 256)
    k, v = _head_rms(kv[:, :128], r["b128"], r["gk_g"][...]), kv[:, 128:]
    r["gk"][rows, :] = dup16(rope(k, _TAB_64, 16))
    r["gv"][rows, :] = dup16(v)
    if is_ctx:
        r["gkc"][s, 0] = k
        r["gvc"][s, 0] = v

    for j in range(N_BRANCH):
        g = _sigmoid(seg(_C_GATES, D_MODEL, skip=D_MODEL * j))
        r["gates"][rows, D_MODEL * j:D_MODEL * (j + 1)] = g.astype(BF16)


def _proj(x, mod4, lp, is_ctx, layer=0, tabs=None, prev=None):
    n_tok = x.shape[0]
    params = dict(_PROJ_PARAMS, **(_HGRN_PARAMS if is_ctx else {}))
    n_sub = PROJ_TILE // PROJ_SUB
    in_specs = [pl.BlockSpec((PROJ_TILE, D_MODEL), lambda i: (i, 0)), _mod_spec(is_ctx, layer, PROJ_TILE)]
    in_specs += [_layer_spec(params[n], layer, _W_IN_BLOCK_INDEX.get(n)) for n in _PROJ_PARAMS]
    in_specs += [_const_spec((512, 512)), _const_spec((128, 128))]
    args = [x, mod4] + [lp[n] for n in _PROJ_PARAMS] + [lp["b512"], lp["b128"]]
    if is_ctx:
        in_specs += [_layer_spec(params[n], layer) for n in _HGRN_PARAMS]
        args += [lp[n] for n in _HGRN_PARAMS]
    if not is_ctx:
        in_specs.append(pl.BlockSpec((9, PROJ_TILE, LANES), lambda i: (0, i % (DEC_SEQ // PROJ_TILE), 0)))
        args.append(tabs)

    def tok(w):
        return pl.BlockSpec((PROJ_TILE, w), lambda i: (i, 0))

    specs = {n: (tok(w), jax.ShapeDtypeStruct((n_tok, w), dt)) for n, w, dt in _ATTN_OUTS}
    if is_ctx:
        specs["oa"] = (tok(512), jax.ShapeDtypeStruct((n_tok, 512), BF16))
        for n, w in _CACHE_OUTS:
            specs[n] = (pl.BlockSpec((n_sub, 1, SEQ, w), lambda i: (i, layer, 0, 0)),
                        jax.ShapeDtypeStruct((BATCH, DEPTH, SEQ, w), F32))
        specs["st"] = (pl.BlockSpec((n_sub, 1, 2, HGRN_HEADS, LANES, LANES), lambda i: (i, layer, 0, 0, 0, 0)),
                       jax.ShapeDtypeStruct((BATCH, DEPTH, 2, HGRN_HEADS, LANES, LANES), F32))
    else:
        for n in _HGRN_PRE:
            specs[n] = (tok(512), jax.ShapeDtypeStruct((n_tok, 512), F32))
    out_names = _proj_out_names(is_ctx)
    aliases = {}
    n_alias = 0
    if prev is not None:
        for n in [n for n, _ in _CACHE_OUTS] + ["st"]:
            aliases[len(args)] = out_names.index(n)
            in_specs.append(_ANY_SPEC)
            args.append(prev[n])
            n_alias += 1
    outs = pl.pallas_call(
        functools.partial(_proj_kernel, is_ctx=is_ctx, n_alias=n_alias),
        out_shape=[specs[n][1] for n in out_names], grid=(n_tok // PROJ_TILE,), in_specs=in_specs,
        out_specs=[specs[n][0] for n in out_names],
        scratch_shapes=[pltpu.VMEM((n_sub, len(_HGRN_PRE), PROJ_SUB, 512), F32)] if is_ctx else [],
        input_output_aliases=aliases, compiler_params=_params(1),
        name="proj_ctx" if is_ctx else "proj_lat",
    )(*args)
    return dict(zip(out_names, outs))


def _pair_levels():
    C = HGRN_CHUNK
    ti = lax.broadcasted_iota(jnp.int32, (C, C), 0)
    si = lax.broadcasted_iota(jnp.int32, (C, C), 1)
    x = ti ^ si
    lvl = jnp.where(x == 0, -1, 0)
    b = 2
    while b < C:
        lvl = lvl + (x >= b).astype(jnp.int32)
        b *= 2
    return ti, si, lvl


def _hgrn_forget(f_pre, lb):
    s_pos, s_neg = _sigmoid_pair(f_pre)
    f = lb + (1.0 - lb) * s_pos
    return (1.0 - lb) * s_neg, jnp.log2(jnp.maximum(f, LOG_FLOOR))


def _hgrn_levels():
    ti, si, lvl = _pair_levels()
    return jnp.where(ti > si, lvl, -1), jnp.where(ti < si, lvl, -1)


def _hgrn_chunk(q, k, lf2, v, st, fwd, lvl):
    C = HGRN_CHUNK
    nv = C // SUBLANES
    sub = lax.broadcasted_iota(jnp.int32, (nv, SUBLANES, LANES), 1)

    p = lf2.reshape(nv, SUBLANES, LANES)
    for s in (1, 2, 4):
        if fwd:
            p = p + jnp.where(sub >= s, pltpu.roll(p, s, 1), 0.0)
        else:
            p = p + jnp.where(sub < SUBLANES - s, pltpu.roll(p, SUBLANES - s, 1), 0.0)

    a_mat = jnp.zeros((C, C), F32)
    q16, k16 = q.astype(BF16), k.astype(BF16)

    def add_level(a_mat, e, level):
        e16 = e.astype(BF16)
        return jnp.where(lvl == level, _dot_nt(q16 * e16, k16 * e16), a_mat)

    def row(r):
        return jnp.broadcast_to(p[:, r:r + 1, :], p.shape)

    odd = (sub & 1) == 1
    if fwd:
        bnd = (jnp.where(odd, pltpu.roll(p, 1, 1), p), jnp.where(sub < 4, row(1), row(5)), row(3))
    else:
        bnd = (jnp.where(odd, p, pltpu.roll(p, SUBLANES - 1, 1)), jnp.where(sub < 4, row(2), row(6)), row(4))
    for level, b in enumerate(bnd):
        a_mat = add_level(a_mat, jnp.exp2(-jnp.abs(p - b)).reshape(C, LANES), level)

    p = p.reshape(C, LANES)
    m = SUBLANES
    while m < C:
        nb = C // (2 * m)
        p4 = p.reshape(nb, 2, m, LANES)
        lo, hi = p4[:, 0], p4[:, 1]
        if fwd:
            tot = lo[:, m - 1:m]
            e_lo, e_hi = tot - lo, hi
            p_lo, p_hi = lo, hi + tot
        else:
            tot = hi[:, 0:1]
            e_lo, e_hi = lo, tot - hi
            p_lo, p_hi = lo + tot, hi
        e = jnp.exp2(jnp.concatenate([e_lo[:, None], e_hi[:, None]], axis=1).reshape(C, LANES))
        p = jnp.concatenate([p_lo[:, None], p_hi[:, None]], axis=1).reshape(C, LANES)
        a_mat = add_level(a_mat, e, int(np.log2(m)))
        m *= 2
    v16 = v.astype(BF16)
    o = jnp.sum(q * k, axis=-1, keepdims=True) * v + _dot(a_mat.astype(BF16), v16)

    if st is not None:
        o = o + _dot_nt((q * jnp.exp2(p)).astype(BF16), st.astype(BF16))
    tot = p[C - 1:C] if fwd else p[0:1]
    k_end = (k * jnp.exp2(tot - p)).astype(BF16)
    st_new = _dot_tn(v16, k_end)
    if st is not None:
        st_new = st_new + st * jnp.exp2(tot)
    return o, st_new


def _block_prefix(lf2, sub, fwd):
    p = lf2
    for s in (1, 2, 4):
        if fwd:
            p = p + jnp.where(sub >= s, pltpu.roll(p, s, 1), 0.0)
        else:
            p = p + jnp.where(sub < SUBLANES - s, pltpu.roll(p, SUBLANES - s, 1), 0.0)
    return p


def _hgrn_chunk_bidir(q, v, fwd_kl, bwd_kl, lvl):
    C = HGRN_CHUNK
    nv = C // SUBLANES
    sub = lax.broadcasted_iota(jnp.int32, (nv, SUBLANES, LANES), 1)
    (k_f, lf_f), (k_b, lf_b) = fwd_kl, bwd_kl
    p_f = _block_prefix(lf_f.reshape(nv, SUBLANES, LANES), sub, True)
    p_b = _block_prefix(lf_b.reshape(nv, SUBLANES, LANES), sub, False)
    q16, kf16, kb16 = q.astype(BF16), k_f.astype(BF16), k_b.astype(BF16)
    a_mat = jnp.zeros((C, C), F32)

    def add_level(a_mat, eq_f, ek_f, eq_b, ek_b, level):
        lhs = jnp.concatenate([q16 * eq_f.astype(BF16), q16 * eq_b.astype(BF16)], axis=1)
        rhs = jnp.concatenate([kf16 * ek_f.astype(BF16), kb16 * ek_b.astype(BF16)], axis=1)
        return jnp.where(lvl == level, _dot_nt(lhs, rhs), a_mat)

    def row(p, r):
        return jnp.broadcast_to(p[:, r:r + 1, :], p.shape)

    odd = (sub & 1) == 1
    bnd_f = (jnp.where(odd, pltpu.roll(p_f, 1, 1), p_f), jnp.where(sub < 4, row(p_f, 1), row(p_f, 5)), row(p_f, 3))
    bnd_b = (jnp.where(odd, p_b, pltpu.roll(p_b, SUBLANES - 1, 1)), jnp.where(sub < 4, row(p_b, 2), row(p_b, 6)),
             row(p_b, 4))
    for level in range(3):
        upper = (sub & (1 << level)) != 0
        e_f = jnp.exp2(-jnp.abs(p_f - bnd_f[level]))
        e_b = jnp.exp2(-jnp.abs(p_b - bnd_b[level]))
        a_mat = add_level(a_mat, *(jnp.where(m, e, 0.0).reshape(C, LANES) for m, e in
                                   ((upper, e_f), (~upper, e_f), (~upper, e_b), (upper, e_b))), level)

    p_f, p_b = p_f.reshape(C, LANES), p_b.reshape(C, LANES)
    m = SUBLANES
    while m < C:
        nb = C // (2 * m)

        def halves(lo, hi):
            return jnp.concatenate([lo[:, None], hi[:, None]], axis=1).reshape(C, LANES)

        f4, b4 = p_f.reshape(nb, 2, m, LANES), p_b.reshape(nb, 2, m, LANES)
        zero = jnp.zeros((nb, m, LANES), F32)
        tot_f = f4[:, 0, m - 1:m]
        tot_b = b4[:, 1, 0:1]
        a_mat = add_level(a_mat,
                          halves(zero, jnp.exp2(f4[:, 1])), halves(jnp.exp2(tot_f - f4[:, 0]), zero),
                          halves(jnp.exp2(b4[:, 0]), zero), halves(zero, jnp.exp2(tot_b - b4[:, 1])),
                          int(np.log2(m)))
        p_f = halves(f4[:, 0], f4[:, 1] + tot_f)
        p_b = halves(b4[:, 0] + tot_b, b4[:, 1])
        m *= 2
    v16 = v.astype(BF16)
    o = jnp.sum(q * (k_f + k_b), axis=-1, keepdims=True) * v + _dot(a_mat.astype(BF16), v16)
    st_f = _dot_tn(v16, (k_f * jnp.exp2(p_f[C - 1:C] - p_f)).astype(BF16))
    st_b = _dot_tn(v16, (k_b * jnp.exp2(p_b[0:1] - p_b)).astype(BF16))
    return o, (st_f, st_b)


def _hgrn_finish(o, gn, g_pre):
    return (_rms(o, gn) * _silu(g_pre)).astype(BF16)


def _hgrn_lat_kernel(lbf_ref, lbb_ref, gn_ref, q_ref, ff_ref, fb_ref, v_ref, g_ref, s0_ref, o_ref,
                     of_ref, ob_ref, st_ref, *, n_chunks):
    C = HGRN_CHUNK
    for d in range(2):
        st_ref[d] = s0_ref[0, 0, d, 0].T
    lbf = lbf_ref[...]
    lbb = lbb_ref[...]
    lvls = _hgrn_levels()

    def body(c, carry):
        rf = pl.multiple_of(c * C, C)
        rb = pl.multiple_of((n_chunks - 1 - c) * C, C)
        for d, r0, f_ref, lb, out_ref in ((0, rf, ff_ref, lbf, of_ref), (1, rb, fb_ref, lbb, ob_ref)):
            sl = pl.ds(r0, C)
            k, lf2 = _hgrn_forget(f_ref[sl, :], lb)
            out_ref[sl, :], st_ref[d] = _hgrn_chunk(_silu(q_ref[sl, :]), k, lf2, v_ref[sl, :], st_ref[d], d == 0,
                                                    lvls[d])
        return carry

    lax.fori_loop(0, n_chunks, body, 0, unroll=HGRN_UNROLL)
    o_ref[...] = _hgrn_finish(of_ref[...] + ob_ref[...], gn_ref[...], g_ref[...])


def _hgrn_lat(p, lp, layer, state_in):
    tok_spec = pl.BlockSpec((DEC_SEQ, LANES), lambda b, h: (b, h))
    vec_spec = pl.BlockSpec((None, 1, LANES), lambda b, h: (layer, 0, h))
    state_spec = pl.BlockSpec((1, 1, 2, 1, LANES, LANES), lambda b, h: (b, layer, 0, h, 0, 0))
    return pl.pallas_call(
        functools.partial(_hgrn_lat_kernel, n_chunks=DEC_SEQ // HGRN_CHUNK),
        out_shape=jax.ShapeDtypeStruct((N_LAT, 512), BF16), grid=(DEC_BATCH, HGRN_HEADS),
        in_specs=[vec_spec] * 3 + [tok_spec] * 5 + [state_spec], out_specs=tok_spec,
        scratch_shapes=[pltpu.VMEM((DEC_SEQ, LANES), F32), pltpu.VMEM((DEC_SEQ, LANES), F32),
                        pltpu.VMEM((2, LANES, LANES), F32)],
        compiler_params=_params(2), name="hgrn_lat",
    )(lp["lbf"], lp["lbb"], lp["gn"], p["hq"], p["hff"], p["hfb"], p["hi"], p["hg"], state_in)


def _lane_lo(shape):
    return (lax.broadcasted_iota(jnp.int32, shape, len(shape) - 1) & (LANES - 1)) < HEAD_DIM


def _softmax_unit(make_segs, sink_col, finish):
    segs = make_segs()
    mx = None
    for s, _ in segs:
        m = jnp.max(s, axis=-1, keepdims=True)
        mx = m if mx is None else jnp.maximum(mx, m)
    if sink_col is not None:
        mx = jnp.maximum(mx, sink_col)
    yield
    acc = None
    for s, v in segs:
        v_ones = jnp.concatenate([v, jnp.ones(v.shape, BF16)], axis=1)
        pv = _dot(jnp.exp2(s - mx).astype(BF16), v_ones)
        acc = pv if acc is None else acc + pv
    den = acc[:, LANES:]
    if sink_col is not None:
        den = den + jnp.exp2(sink_col - mx)
    finish(acc[:, :LANES] / den)


_DONE = object()


def _run_units(units, lookahead=1):
    pending = list(units)
    active = []
    for _ in range(lookahead - 1):
        if pending:
            u = pending.pop(0)
            next(u)
            active.insert(0, u)
    while pending or active:
        started = []
        if pending:
            u = pending.pop(0)
            next(u)
            started.append(u)
        n_adv = max(len(active) - (lookahead - 1), 1) if pending else len(active)
        keep = active[:len(active) - n_adv]
        active = started + keep + [a for a in active[len(active) - n_adv:] if next(a, _DONE) is not _DONE]


def _gqa_unit(q_ref, g, tq, segs_fn, o_ref, sink_ref=None):
    lo = _lane_lo((tq, LANES))
    parts = []
    for j in range(2):
        qp = q_ref[:, 256 * g + 128 * j:256 * g + 128 * (j + 1)]
        zero = jnp.zeros_like(qp)
        parts += [jnp.where(lo, qp, zero), jnp.where(lo, zero, qp)]
    q4 = jnp.concatenate(parts, axis=0)
    sink_col = None
    if sink_ref is not None:
        sink_col = jnp.concatenate(
            [jnp.broadcast_to(sink_ref[0:1, 4 * g + i:4 * g + i + 1] * LOG2E, (tq, 1)) for i in range(4)], axis=0)

    def finish(pv):
        o_ref[:, 256 * g:256 * g + 128] = jnp.where(lo, pv[0:tq], pv[tq:2 * tq]).astype(BF16)
        o_ref[:, 256 * g + 128:256 * (g + 1)] = jnp.where(lo, pv[2 * tq:3 * tq], pv[3 * tq:4 * tq]).astype(BF16)

    yield from _softmax_unit(lambda: segs_fn(q4), sink_col, finish)


def _mla_pair_unit(q_ref, hp, tq, segs_fn, o_ref):
    outs = []
    subs = [_softmax_unit(functools.partial(segs_fn, h, q_ref[:, 128 * h:128 * (h + 1)]), None, outs.append)
            for h in (2 * hp, 2 * hp + 1)]
    while True:
        if [next(s, _DONE) for s in subs][0] is _DONE:
            break
        yield
    o_ref[:, 128 * hp:128 * (hp + 1)] = jnp.where(_lane_lo((tq, LANES)), outs[0], outs[1]).astype(BF16)


_ATTN_IN = ("sq", "sk", "sv", "mq", "mk", "mv", "gq", "gk", "gv")
_ATTN_WIDTH = dict(sq=512, sk=256, sv=256, mq=1024, mk=1024, mv=1024, gq=512, gk=256, gv=256)


def _attn_ctx_kernel(sink_ref, sq_ref, sk_ref, sv_ref, mq_ref, mk_ref, mv_ref, gq_ref, gk_ref, gv_ref,
                     ob_ref, oc_ref, od_ref):
    tq = SEQ
    units = []
    for g in range(N_KV):
        def swa_segs(q4, g=g):
            return [(_dot_nt(q4, sk_ref[:, 128 * g:128 * (g + 1)]), sv_ref[:, 128 * g:128 * (g + 1)])]

        def gqa_segs(q4, g=g):
            return [(_dot_nt(q4, gk_ref[:, 128 * g:128 * (g + 1)]), gv_ref[:, 128 * g:128 * (g + 1)])]

        units += [_gqa_unit(sq_ref, g, tq, swa_segs, ob_ref, sink_ref), _gqa_unit(gq_ref, g, tq, gqa_segs, od_ref)]

    def mla_segs(h, q_h):
        return [(_dot_nt(q_h, mk_ref[:, 128 * h:128 * (h + 1)]), mv_ref[:, 128 * h:128 * (h + 1)])]

    units += [_mla_pair_unit(mq_ref, hp, tq, mla_segs, oc_ref) for hp in range(MLA_HEADS // 2)]
    _run_units(units)


def _attn_ctx(lp, p, layer):
    def spec(w):
        return pl.BlockSpec((SEQ, w), lambda b: (b, 0))

    return pl.pallas_call(
        _attn_ctx_kernel,
        out_shape=[jax.ShapeDtypeStruct((N_CTX, 512), BF16)] * 3,
        grid=(BATCH,),
        in_specs=[_layer_spec((1, N_HEADS), layer)] + [spec(_ATTN_WIDTH[n]) for n in _ATTN_IN],
        out_specs=[spec(512)] * 3,
        compiler_params=_params(1),
        name="attn_ctx",
    )(lp["sink"], *[p[n] for n in _ATTN_IN])


def _dup_heads(x):
    lo = _lane_lo(x.shape)
    r = pltpu.roll(x, HEAD_DIM, 1)
    return jnp.where(lo, x, r), jnp.where(lo, r, x)


def _dup_tiles(x):
    return jnp.where(_lane_lo(x.shape), x, pltpu.roll(x, HEAD_DIM, 1))


def _attn_lat_kernel(sink_ref, wk_ref, wuv_ref,
                     sq_ref, sk_ref, sv_ref, mq_ref, mk_ref, mv_ref, gq_ref, gk_ref, gv_ref,
                     csk_ref, csv_ref, cckv_ref, ckr_ref, cgk_ref, cgv_ref,
                     ob_ref, oc_ref, od_ref,
                     cs_k, cs_v, cm_k, cm_v, cg_k, cg_v):
    tq = TQ_LAT
    qi = pl.program_id(1)

    @pl.when(qi == 0)
    def _():
        for src, dst in ((csk_ref, cs_k), (csv_ref, cs_v), (cgk_ref, cg_k), (cgv_ref, cg_v)):
            d0, d1 = _dup_heads(src[0, 0])
            dst[0] = d0.astype(BF16)
            dst[1] = d1.astype(BF16)
        ckv16 = cckv_ref[0, 0].astype(BF16)
        k = _dot(ckv16, wk_ref[0:MLA_KV_RANK, :]) + _dot(ckr_ref[0, 0].astype(BF16),
                                                           wk_ref[MLA_KV_RANK:MLA_KV_RANK + MLA_ROPE, :])
        cm_k[...] = k.astype(BF16)
        cm_v[...] = _dup_tiles(_dot(ckv16, wuv_ref[...])).astype(BF16)

    q0 = qi * tq
    win = tq + 2 * SWA_WINDOW
    start = pl.multiple_of(jnp.clip(q0 - SWA_WINDOW, 0, DEC_SEQ - win), SWA_WINDOW)
    rows = lax.broadcasted_iota(jnp.int32, (4 * tq, win), 0)
    qpos = q0 + (rows & (tq - 1))
    kpos = start + lax.broadcasted_iota(jnp.int32, (4 * tq, win), 1)
    band = jnp.abs(qpos - kpos) <= SWA_WINDOW
    units = []
    for g in range(N_KV):
        def swa_segs(q4, g=g):
            ks = sk_ref[pl.ds(start, win), 128 * g:128 * (g + 1)]
            vs = sv_ref[pl.ds(start, win), 128 * g:128 * (g + 1)]
            s_own = jnp.where(band, _dot_nt(q4, ks), MASK_VALUE)
            return [(_dot_nt(q4, cs_k[g]), cs_v[g]), (s_own, vs)]

        def gqa_segs(q4, g=g):
            return [(_dot_nt(q4, cg_k[g]), cg_v[g]),
                    (_dot_nt(q4, gk_ref[:, 128 * g:128 * (g + 1)]), gv_ref[:, 128 * g:128 * (g + 1)])]

        units += [_gqa_unit(sq_ref, g, tq, swa_segs, ob_ref, sink_ref), _gqa_unit(gq_ref, g, tq, gqa_segs, od_ref)]

    def mla_segs(h, q_h):
        cols = slice(128 * h, 128 * (h + 1))
        return [(_dot_nt(q_h, cm_k[:, cols]), cm_v[:, cols]), (_dot_nt(q_h, mk_ref[:, cols]), mv_ref[:, cols])]

    mla = [_mla_pair_unit(mq_ref, hp, tq, mla_segs, oc_ref) for hp in range(MLA_HEADS // 2)]
    _run_units([u for pair in zip(units, mla) for u in pair], lookahead=3)


def _attn_lat(lp, p, caches, layer):
    nq = DEC_SEQ // TQ_LAT

    def qspec(w):
        return pl.BlockSpec((TQ_LAT, w), lambda b, i: (b * nq + i, 0))

    def kspec(w):
        return pl.BlockSpec((DEC_SEQ, w), lambda b, i: (b, 0))

    def cspec(w):
        return pl.BlockSpec((1, 1, PAST_LEN, w), lambda b, i: (b, layer, 0, 0))

    is_q = dict(sq=True, mq=True, gq=True)
    in_specs = [_layer_spec((1, N_HEADS), layer), _layer_spec((256, 1024), layer),
                _layer_spec((MLA_KV_RANK, 1024), layer)]
    in_specs += [(qspec if is_q.get(n) else kspec)(_ATTN_WIDTH[n]) for n in _ATTN_IN]
    in_specs += [cspec(128), cspec(128), cspec(128), cspec(MLA_ROPE), cspec(128), cspec(128)]
    return pl.pallas_call(
        _attn_lat_kernel,
        out_shape=[jax.ShapeDtypeStruct((N_LAT, 512), BF16)] * 3,
        grid=(DEC_BATCH, nq),
        in_specs=in_specs,
        out_specs=[qspec(512)] * 3,
        scratch_shapes=[pltpu.VMEM((N_KV, PAST_LEN, LANES), BF16), pltpu.VMEM((N_KV, PAST_LEN, LANES), BF16),
                        pltpu.VMEM((PAST_LEN, 1024), BF16), pltpu.VMEM((PAST_LEN, 1024), BF16),
                        pltpu.VMEM((N_KV, PAST_LEN, LANES), BF16), pltpu.VMEM((N_KV, PAST_LEN, LANES), BF16)],
        compiler_params=_params(2),
        name="attn_lat",
    )(lp["sink"], lp["wk"], lp["wuv"], *[p[n] for n in _ATTN_IN], *caches)


MERGE_TILE = 512
MERGE_SUB = 256


def _merge_kernel(x_ref, mod_ref, oa_ref, ob_ref, oc_ref, od_ref, gates_ref, wb_ref, wo_ref, npost_ref,
                  nmpre_ref, nmpost_ref, w1_ref, w2_ref, y_ref):
    g1 = mod_ref[0, 2]
    sh2 = mod_ref[0, 3]
    sc2 = mod_ref[0, 4]
    g2 = mod_ref[0, 5]

    def sub_tile(r0):
        rows = slice(r0, r0 + MERGE_SUB)
        acc = None
        for n, o_ref in enumerate((oa_ref, ob_ref, oc_ref, od_ref)):
            br = _dot(o_ref[rows, :], wb_ref[n]) * gates_ref[rows, D_MODEL * n:D_MODEL * (n + 1)].astype(F32)
            acc = br if acc is None else acc + br
        y = _dot(acc.astype(BF16), wo_ref[...])
        yield
        x = x_ref[rows, :] + g1 * _rms(y, npost_ref[...])
        h = (_rms(x, nmpre_ref[...]) * (1.0 + sc2) + sh2).astype(BF16)
        a = jnp.maximum(_dot(h, w1_ref[...]), 0.0)
        yield
        y = _dot((a * a).astype(BF16), w2_ref[...])
        y_ref[rows, :] = x + g2 * _rms(y, nmpost_ref[...])

    _run_units([sub_tile(r0) for r0 in range(0, MERGE_TILE, MERGE_SUB)])


_MERGE_PARAMS = dict(w_branch=(N_BRANCH, BRANCH_W, D_MODEL), w_out=(D_MODEL, D_MODEL), npost=(1, D_MODEL),
                     nmpre=(1, D_MODEL), nmpost=(1, D_MODEL), w_mlp_in=(D_MODEL, D_FF), w_mlp_out=(D_FF, D_MODEL))


def _merge(x, mod4, o_a, o_b, o_c, o_d, gates, lp, is_ctx, layer):
    n_tok = x.shape[0]

    def tok(w):
        return pl.BlockSpec((MERGE_TILE, w), lambda i: (i, 0))

    in_specs = [tok(D_MODEL), _mod_spec(is_ctx, layer, MERGE_TILE), tok(512), tok(512), tok(512), tok(512), tok(4096)]
    in_specs += [_layer_spec(shape, layer) for shape in _MERGE_PARAMS.values()]
    return pl.pallas_call(
        _merge_kernel, out_shape=jax.ShapeDtypeStruct((n_tok, D_MODEL), F32), grid=(n_tok // MERGE_TILE,),
        in_specs=in_specs, out_specs=tok(D_MODEL), compiler_params=_params(1),
        name="merge_ctx" if is_ctx else "merge_lat",
    )(x, mod4, o_a, o_b, o_c, o_d, gates, *[lp[n] for n in _MERGE_PARAMS])


def _rope_tables():
    t = np.arange(DEC_SEQ)
    row, col = (t // GRID_W).astype(np.float32), (t % GRID_W).astype(np.float32)

    def pattern(d):
        nf = d // 4
        inv = (ROPE_BASE ** (-np.arange(nf, dtype=np.float32) / nf)).astype(np.float32)
        ang = np.concatenate([row[:, None] * inv, row[:, None] * inv, col[:, None] * inv, col[:, None] * inv], axis=1)
        cos, sin = np.cos(ang), np.sin(ang)
        second = (np.arange(d) % (2 * nf)) >= nf
        return cos, np.where(second, sin, 0.0), np.where(second, 0.0, -sin)

    out = np.zeros((9, DEC_SEQ, LANES), np.float32)
    out[[0, 3, 6]] = 1.0
    c, s1, s2 = pattern(HEAD_DIM)
    for a, tab in enumerate((c, s1, s2)):
        out[_TAB_64 + a] = np.tile(tab, (1, LANES // HEAD_DIM))
    c, s1, s2 = pattern(MLA_ROPE)
    for a, tab in enumerate((c, s1, s2)):
        out[_TAB_MLAQ + a][:, MLA_NOPE:MLA_NOPE + MLA_ROPE] = tab
        out[_TAB_KR + a][:, :MLA_ROPE] = tab
    return out


_ROPE_TABLES = _rope_tables()


def _block_mean_matrix(n):
    i = np.arange(n)
    return ((i[:, None] // HEAD_DIM) == (i[None, :] // HEAD_DIM)).astype(np.float32) / HEAD_DIM


def _mla_weight_layouts(w_uq, w_ukv):
    dq = MLA_NOPE + MLA_ROPE
    wq = w_uq.reshape(DEPTH, MLA_Q_RANK, MLA_HEADS, dq)
    wq = jnp.pad(wq, ((0, 0), (0, 0), (0, 0), (0, LANES - dq))).reshape(DEPTH, MLA_Q_RANK, MLA_HEADS * LANES)
    wkv = w_ukv.reshape(DEPTH, MLA_KV_RANK, MLA_HEADS, MLA_NOPE + MLA_V)
    wk = jnp.pad(wkv[..., :MLA_NOPE], ((0, 0), (0, 0), (0, 0), (0, LANES - MLA_NOPE)))
    wk = wk.reshape(DEPTH, MLA_KV_RANK, MLA_HEADS * LANES)
    place = np.zeros((LANES, MLA_HEADS, LANES), np.float32)
    for j in range(MLA_ROPE):
        place[j, :, MLA_NOPE + j] = 1.0
    place = jnp.broadcast_to(jnp.asarray(place.reshape(1, LANES, MLA_HEADS * LANES)), (DEPTH, LANES, MLA_HEADS * LANES))
    w_k = jnp.concatenate([wk, place], axis=1)
    w_uv = jnp.pad(wkv[..., MLA_NOPE:], ((0, 0), (0, 0), (0, 0), (0, LANES - MLA_V)))
    w_uv = w_uv.reshape(DEPTH, MLA_KV_RANK, MLA_HEADS * LANES)
    return wq.astype(BF16), w_k.astype(BF16), w_uv.astype(BF16)


def _lower_bounds(p):
    s = jax.nn.softmax(p.astype(F32), axis=0)
    return jnp.cumsum(s, axis=0) - s[0]


def kernel(x_prompt, x_sample, state_hgrn, cache_swa_k, cache_swa_v, cache_mla_ckv, cache_mla_kr, cache_gqa_k, cache_gqa_v, c, c_ctx, w_ada, b_ada, norm_mix_pre, norm_mix_post, norm_mlp_pre, norm_mlp_post, w_in, hgrn_lb_fwd, hgrn_lb_bwd, hgrn_norm, swa_sink, mla_q_norm, mla_kv_norm, mla_w_uq, mla_w_ukv, gqa_q_norm, gqa_k_norm, w_branch, w_out, w_mlp_in, w_mlp_out):
    tabs = jnp.asarray(_ROPE_TABLES)
    b512 = jnp.asarray(_block_mean_matrix(512)).astype(BF16)

    def rows(a):
        return a.reshape(DEPTH, 1, -1)

    w_uq, w_k, w_uv = _mla_weight_layouts(mla_w_uq, mla_w_ukv)
    w_a, w_kr, w_b = _split_w_in(w_in)
    lp = dict(
        npre=rows(norm_mix_pre), wa=w_a, wkr=w_kr, wb=w_b, mlaq_g=rows(mla_q_norm), mlakv_g=rows(mla_kv_norm),
        wuq=w_uq, wk=w_k, wuv=w_uv, gq_g=rows(jnp.tile(gqa_q_norm, (1, N_HEADS))),
        gk_g=rows(jnp.tile(gqa_k_norm, (1, N_KV))), b512=b512, b128=b512[:128, :128],
        lbf=rows(_lower_bounds(hgrn_lb_fwd)), lbb=rows(_lower_bounds(hgrn_lb_bwd)), gn=rows(hgrn_norm),
        sink=rows(swa_sink),
        w_branch=w_branch.astype(BF16), w_out=w_out.astype(BF16), npost=rows(norm_mix_post),
        nmpre=rows(norm_mlp_pre), nmpost=rows(norm_mlp_post),
        w_mlp_in=w_mlp_in.astype(BF16), w_mlp_out=w_mlp_out.astype(BF16))

    cond8 = jnp.zeros((SUBLANES, D_MODEL), F32).at[0].set(c_ctx).at[1:1 + DEC_BATCH].set(c)
    mod4 = _ada(cond8, w_ada, b_ada).reshape(DEPTH, SUBLANES, N_MOD, 1, D_MODEL)

    caches_in = (cache_swa_k.reshape(DEC_BATCH, DEPTH, PAST_LEN, 128),
                 cache_swa_v.reshape(DEC_BATCH, DEPTH, PAST_LEN, 128), cache_mla_ckv, cache_mla_kr,
                 cache_gqa_k.reshape(DEC_BATCH, DEPTH, PAST_LEN, 128),
                 cache_gqa_v.reshape(DEC_BATCH, DEPTH, PAST_LEN, 128))

    x_ctx = x_prompt.reshape(N_CTX, D_MODEL)
    x_lat = x_sample.reshape(N_LAT, D_MODEL)
    pc = None
    for l in range(DEPTH):
        pc = _proj(x_ctx, mod4, lp, True, layer=l, prev=pc)
        ob, oc, od = _attn_ctx(lp, pc, l)
        x_ctx = _merge(x_ctx, mod4, pc["oa"], ob, oc, od, pc["gates"], lp, True, l)

        pq = _proj(x_lat, mod4, lp, False, layer=l, tabs=tabs)
        oa = _hgrn_lat(pq, lp, l, state_hgrn)
        ob, oc, od = _attn_lat(lp, pq, caches_in, l)
        x_lat = _merge(x_lat, mod4, oa, ob, oc, od, pq["gates"], lp, False, l)

    kv_shape = (BATCH, DEPTH, SEQ, N_KV, HEAD_DIM)
    return (x_ctx.reshape(BATCH, SEQ, D_MODEL), x_lat.reshape(DEC_BATCH, DEC_SEQ, D_MODEL), pc["st"],
            pc["skc"].reshape(kv_shape), pc["svc"].reshape(kv_shape), pc["mckv"], pc["mkr"],
            pc["gkc"].reshape(kv_shape), pc["gvc"].reshape(kv_shape))
```

```python
import functools

import jax
import jax.numpy as jnp
import numpy as np
from jax import lax
from jax.experimental import pallas as pl
from jax.experimental.pallas import tpu as pltpu

D_MODEL = 1024
BATCH, SEQ = 32, 256
DEC_BATCH, DEC_SEQ = 2, 2048
DEPTH = 2
PAST_LEN = 512
GRID_W = 64
ROPE_BASE = 10000.0
NORM_EPS = 1e-6
MASK_VALUE = -1e30
LOG_FLOOR = 1e-30
N_MOD = 6
N_BRANCH = 4
BRANCH_W = 512
D_FF = 4 * D_MODEL
HGRN_HEADS, HGRN_DK = 4, 128
SWA_WINDOW = 128
MLA_HEADS, MLA_Q_RANK, MLA_KV_RANK, MLA_NOPE, MLA_ROPE, MLA_V = 8, 256, 128, 64, 32, 64
HEAD_DIM = 64
N_HEADS, N_KV = 8, 2

N_CTX = BATCH * SEQ
N_LAT = DEC_BATCH * DEC_SEQ

LANES = 128
SUBLANES = 8
VMEM_LIMIT_BYTES = 56 * 1024 * 1024

PROJ_TILE = 256
PROJ_SUB = 256
HGRN_CHUNK = 256
HGRN_UNROLL = 4
TQ_LAT = 128

BF16 = jnp.bfloat16
F32 = jnp.float32

_W_A_COLS = 3712
_W_B_COLS = 8608 - _W_A_COLS - MLA_ROPE
_C_HGRN = ("wa", 0)
_C_SWA_Q = ("wa", 2560)
_C_SWA_KV = ("wa", 3072)
_C_MLA_CQ = ("wa", 3328)
_C_MLA_CKV = ("wa", 3584)
_C_MLA_KR = ("wkr", 0)
_C_GQA_Q = ("wb", 0)
_C_GQA_KV = ("wb", 512)
_C_GATES = ("wb", 768)


def _split_w_in(w):
    w16 = w.astype(BF16)
    return w16, w16, w16[..., _W_A_COLS + MLA_ROPE:]


_W_IN_BLOCK_INDEX = dict(wkr=(0, _W_A_COLS // LANES))


def _sigmoid_pair(x):
    a = jnp.exp(-jnp.abs(x))
    r = 1.0 / (1.0 + a)
    ar = a * r
    pos = x >= 0
    return jnp.where(pos, r, ar), jnp.where(pos, ar, r)


def _sigmoid(x):
    return 1.0 / (1.0 + jnp.exp2(x * -LOG2E))


def _silu(x):
    return x * _sigmoid(x)


def _rms(x, gain):
    return x * lax.rsqrt(jnp.mean(x * x, axis=-1, keepdims=True) + NORM_EPS) * gain


def _dot(a, b):
    return jnp.dot(a, b, preferred_element_type=F32)


def _dot_nt(a, b):
    return lax.dot_general(a, b, (((1,), (1,)), ((), ())), preferred_element_type=F32)


def _dot_tn(a, b):
    return lax.dot_general(a, b, (((0,), (0,)), ((), ())), preferred_element_type=F32)


def _tile_lanes(t, width):
    reps = width // LANES
    return t if reps == 1 else jnp.concatenate([t] * reps, axis=1)


def _rope(x, tab_ref, base, shift, rows):
    w = x.shape[1]
    c = _tile_lanes(tab_ref[base, rows, :], w)
    s1 = _tile_lanes(tab_ref[base + 1, rows, :], w)
    s2 = _tile_lanes(tab_ref[base + 2, rows, :], w)
    return x * c + pltpu.roll(x, shift, 1) * s1 + pltpu.roll(x, w - shift, 1) * s2


def _head_rms(x, bmat_ref, gain):
    ms = _dot((x * x).astype(BF16), bmat_ref[...])
    return x * lax.rsqrt(ms + NORM_EPS) * gain


def _const_spec(shape):
    nd = len(shape)
    return pl.BlockSpec(shape, lambda *_: (0,) * nd, pipeline_mode=pl.Buffered(1))


def _layer_spec(shape, layer, block_index=None):
    idx = (layer,) + tuple(block_index or (0,) * len(shape))
    return pl.BlockSpec((None,) + tuple(shape), lambda *_: idx, pipeline_mode=pl.Buffered(1))


_ANY_SPEC = pl.BlockSpec(memory_space=pl.ANY)


def _params(n_grid_dims):
    return pltpu.CompilerParams(dimension_semantics=("arbitrary",) * n_grid_dims, vmem_limit_bytes=VMEM_LIMIT_BYTES)


_ADA_TN = 2048


def _ada_kernel(c_ref, w_ref, b_ref, o_ref):
    s = _silu(c_ref[...]).astype(BF16)
    o_ref[0] = _dot(s, w_ref[0].astype(BF16)) + b_ref[0]


def _ada(cond8, w_ada, b_ada):
    n = N_MOD * D_MODEL
    return pl.pallas_call(
        _ada_kernel,
        out_shape=jax.ShapeDtypeStruct((DEPTH, SUBLANES, n), F32),
        grid=(DEPTH, n // _ADA_TN),
        in_specs=[
            pl.BlockSpec((SUBLANES, D_MODEL), lambda l, j: (0, 0)),
            pl.BlockSpec((1, D_MODEL, _ADA_TN), lambda l, j: (l, 0, j)),
            pl.BlockSpec((1, 1, _ADA_TN), lambda l, j: (l, 0, j)),
        ],
        out_specs=pl.BlockSpec((1, SUBLANES, _ADA_TN), lambda l, j: (l, 0, j)),
        compiler_params=_params(2),
        name="ada",
    )(cond8, w_ada, b_ada.reshape(DEPTH, 1, n))


def _mod_spec(is_ctx, layer, tile):
    if is_ctx:
        return pl.BlockSpec((None, 1, N_MOD, 1, D_MODEL), lambda i: (layer, 0, 0, 0, 0))
    per_seq = DEC_SEQ // tile
    return pl.BlockSpec((None, 1, N_MOD, 1, D_MODEL), lambda i: (layer, 1 + i // per_seq, 0, 0, 0))


LOG2E = float(np.log2(np.e))
_SCALE_64 = HEAD_DIM ** -0.5 * LOG2E
_SCALE_MLA = (MLA_NOPE + MLA_ROPE) ** -0.5 * LOG2E
_TAB_64, _TAB_MLAQ, _TAB_KR = 0, 3, 6

_HGRN_PRE = ("hq", "hff", "hfb", "hi", "hg")
_ATTN_OUTS = (
    ("sq", 512, BF16), ("sk", 256, BF16), ("sv", 256, BF16),
    ("mq", 1024, BF16), ("mk", 1024, BF16), ("mv", 1024, BF16),
    ("gq", 512, BF16), ("gk", 256, BF16), ("gv", 256, BF16), ("gates", 4096, BF16))
_CACHE_OUTS = (("skc", 128), ("svc", 128), ("mckv", 128), ("mkr", MLA_ROPE), ("gkc", 128), ("gvc", 128))
_PROJ_PARAMS = dict(npre=(1, D_MODEL), wa=(D_MODEL, _W_A_COLS), wkr=(D_MODEL, LANES), wb=(D_MODEL, _W_B_COLS),
                    mlaq_g=(1, MLA_Q_RANK), mlakv_g=(1, MLA_KV_RANK),
                    wuq=(MLA_Q_RANK, 1024), wk=(256, 1024), wuv=(MLA_KV_RANK, 1024), gq_g=(1, 512), gk_g=(1, 128))
_HGRN_PARAMS = dict(lbf=(1, 512), lbb=(1, 512), gn=(1, 512))
_PROJ_CONSTS = tuple(_PROJ_PARAMS) + ("b512", "b128")
_HGRN_CONSTS = tuple(_HGRN_PARAMS)


def _proj_out_names(is_ctx):
    attn = [n for n, _, _ in _ATTN_OUTS]
    if is_ctx:
        return ["oa"] + attn + [n for n, _ in _CACHE_OUTS] + ["st"]
    return list(_HGRN_PRE) + attn


def _proj_kernel(*refs, is_ctx, n_alias):
    names = ["x", "mod"] + list(_PROJ_CONSTS) + (list(_HGRN_CONSTS) if is_ctx else ["tab"])
    r = dict(zip(names, refs))
    pos = len(names) + n_alias
    for name in _proj_out_names(is_ctx) + (["hbuf"] if is_ctx else []):
        r[name] = refs[pos]
        pos += 1

    sh1 = r["mod"][0, 0]
    sc1 = r["mod"][0, 1]
    for s in range(PROJ_TILE // PROJ_SUB):
        for _ in _proj_sub_tile(r, s, sh1, sc1, is_ctx):
            pass


def _proj_sub_tile(r, s, sh1, sc1, is_ctx):
    rows = slice(PROJ_SUB * s, PROJ_SUB * (s + 1))
    h = (_rms(r["x"][rows, :], r["npre"][...]) * (1.0 + sc1) + sh1).astype(BF16)

    def seg(where, w, skip=0):
        name, a = where
        return _dot(h, r[name][:, a + skip:a + skip + w])

    def rope(v, base, shift):
        return v if is_ctx else _rope(v, r["tab"], base, shift, rows)

    for j, name in enumerate(_HGRN_PRE):
        u = seg(_C_HGRN, 512, skip=512 * j)
        if is_ctx:
            r["hbuf"][s, j] = u
        else:
            r[name][rows, :] = u
    yield

    if is_ctx:
        assert HGRN_CHUNK == PROJ_SUB
        lvl = _pair_levels()[2]
        for hd in range(HGRN_HEADS):
            cols = slice(LANES * hd, LANES * (hd + 1))
            q = _silu(r["hbuf"][s, 0, :, cols])
            v = r["hbuf"][s, 3, :, cols]
            kl = [_hgrn_forget(r["hbuf"][s, 1 + d, :, cols], lb[:, cols]) for d, lb in ((0, r["lbf"]), (1, r["lbb"]))]
            o, sts = _hgrn_chunk_bidir(q, v, kl[0], kl[1], lvl)
            for d in range(2):
                r["st"][s, 0, d, hd] = sts[d].T
            r["oa"][rows, cols] = _hgrn_finish(o, r["gn"][:, cols], r["hbuf"][s, 4, :, cols])
        yield

    def dup16(k):
        return jnp.concatenate([d.astype(BF16) for d in _dup_heads(k)], axis=1)

    r["sq"][rows, :] = (rope(seg(_C_SWA_Q, 512), _TAB_64, 16) * _SCALE_64).astype(BF16)
    kv = seg(_C_SWA_KV, 256)
    k, v = kv[:, :128], kv[:, 128:]
    r["sk"][rows, :] = dup16(rope(k, _TAB_64, 16))
    r["sv"][rows, :] = dup16(v)
    if is_ctx:
        r["skc"][s, 0] = k
        r["svc"][s, 0] = v

    cq = _rms(seg(_C_MLA_CQ, 256), r["mlaq_g"][...]).astype(BF16)
    q = _dot(cq, r["wuq"][...])
    r["mq"][rows, :] = (rope(q, _TAB_MLAQ, 8) * _SCALE_MLA).astype(BF16)
    ckv = _rms(seg(_C_MLA_CKV, 128), r["mlakv_g"][...])
    kr = seg(_C_MLA_KR, 128)
    if is_ctx:
        r["mckv"][s, 0] = ckv
        r["mkr"][s, 0] = kr[:, :MLA_ROPE]
    kr_used = rope(kr, _TAB_KR, 8)
    ckv16 = ckv.astype(BF16)
    r["mk"][rows, :] = _dot(jnp.concatenate([ckv16, kr_used.astype(BF16)], axis=1), r["wk"][...]).astype(BF16)
    r["mv"][rows, :] = _dup_tiles(_dot(ckv16, r["wuv"][...])).astype(BF16)

    gq = _head_rms(seg(_C_GQA_Q, 512), r["b512"], r["gq_g"][...])
    r["gq"][rows, :] = (rope(gq, _TAB_64, 16) * _SCALE_64).astype(BF16)
    kv = seg(_C_GQA_KV, 256)
    k, v = _head_rms(kv[:, :128], r["b128"], r["gk_g"][...]), kv[:, 128:]
    r["gk"][rows, :] = dup16(rope(k, _TAB_64, 16))
    r["gv"][rows, :] = dup16(v)
    if is_ctx:
        r["gkc"][s, 0] = k
        r["gvc"][s, 0] = v

    for j in range(N_BRANCH):
        g = _sigmoid(seg(_C_GATES, D_MODEL, skip=D_MODEL * j))
        r["gates"][rows, D_MODEL * j:D_MODEL * (j + 1)] = g.astype(BF16)


def _proj(x, mod4, lp, is_ctx, layer=0, tabs=None, prev=None):
    n_tok = x.shape[0]
    params = dict(_PROJ_PARAMS, **(_HGRN_PARAMS if is_ctx else {}))
    n_sub = PROJ_TILE // PROJ_SUB
    in_specs = [pl.BlockSpec((PROJ_TILE, D_MODEL), lambda i: (i, 0)), _mod_spec(is_ctx, layer, PROJ_TILE)]
    in_specs += [_layer_spec(params[n], layer, _W_IN_BLOCK_INDEX.get(n)) for n in _PROJ_PARAMS]
    in_specs += [_const_spec((512, 512)), _const_spec((128, 128))]
    args = [x, mod4] + [lp[n] for n in _PROJ_PARAMS] + [lp["b512"], lp["b128"]]
    if is_ctx:
        in_specs += [_layer_spec(params[n], layer) for n in _HGRN_PARAMS]
        args += [lp[n] for n in _HGRN_PARAMS]
    if not is_ctx:
        in_specs.append(pl.BlockSpec((9, PROJ_TILE, LANES), lambda i: (0, i % (DEC_SEQ // PROJ_TILE), 0)))
        args.append(tabs)

    def tok(w):
        return pl.BlockSpec((PROJ_TILE, w), lambda i: (i, 0))

    specs = {n: (tok(w), jax.ShapeDtypeStruct((n_tok, w), dt)) for n, w, dt in _ATTN_OUTS}
    if is_ctx:
        specs["oa"] = (tok(512), jax.ShapeDtypeStruct((n_tok, 512), BF16))
        for n, w in _CACHE_OUTS:
            specs[n] = (pl.BlockSpec((n_sub, 1, SEQ, w), lambda i: (i, layer, 0, 0)),
                        jax.ShapeDtypeStruct((BATCH, DEPTH, SEQ, w), F32))
        specs["st"] = (pl.BlockSpec((n_sub, 1, 2, HGRN_HEADS, LANES, LANES), lambda i: (i, layer, 0, 0, 0, 0)),
                       jax.ShapeDtypeStruct((BATCH, DEPTH, 2, HGRN_HEADS, LANES, LANES), F32))
    else:
        for n in _HGRN_PRE:
            specs[n] = (tok(512), jax.ShapeDtypeStruct((n_tok, 512), F32))
    out_names = _proj_out_names(is_ctx)
    aliases = {}
    n_alias = 0
    if prev is not None:
        for n in [n for n, _ in _CACHE_OUTS] + ["st"]:
            aliases[len(args)] = out_names.index(n)
            in_specs.append(_ANY_SPEC)
            args.append(prev[n])
            n_alias += 1
    outs = pl.pallas_call(
        functools.partial(_proj_kernel, is_ctx=is_ctx, n_alias=n_alias),
        out_shape=[specs[n][1] for n in out_names], grid=(n_tok // PROJ_TILE,), in_specs=in_specs,
        out_specs=[specs[n][0] for n in out_names],
        scratch_shapes=[pltpu.VMEM((n_sub, len(_HGRN_PRE), PROJ_SUB, 512), F32)] if is_ctx else [],
        input_output_aliases=aliases, compiler_params=_params(1),
        name="proj_ctx" if is_ctx else "proj_lat",
    )(*args)
    return dict(zip(out_names, outs))


def _pair_levels():
    C = HGRN_CHUNK
    ti = lax.broadcasted_iota(jnp.int32, (C, C), 0)
    si = lax.broadcasted_iota(jnp.int32, (C, C), 1)
    x = ti ^ si
    lvl = jnp.where(x == 0, -1, 0)
    b = 2
    while b < C:
        lvl = lvl + (x >= b).astype(jnp.int32)
        b *= 2
    return ti, si, lvl


def _hgrn_forget(f_pre, lb):
    s_pos, s_neg = _sigmoid_pair(f_pre)
    f = lb + (1.0 - lb) * s_pos
    return (1.0 - lb) * s_neg, jnp.log2(jnp.maximum(f, LOG_FLOOR))


def _hgrn_levels():
    ti, si, lvl = _pair_levels()
    return jnp.where(ti > si, lvl, -1), jnp.where(ti < si, lvl, -1)


def _hgrn_chunk(q, k, lf2, v, st, fwd, lvl):
    C = HGRN_CHUNK
    nv = C // SUBLANES
    sub = lax.broadcasted_iota(jnp.int32, (nv, SUBLANES, LANES), 1)

    p = lf2.reshape(nv, SUBLANES, LANES)
    for s in (1, 2, 4):
        if fwd:
            p = p + jnp.where(sub >= s, pltpu.roll(p, s, 1), 0.0)
        else:
            p = p + jnp.where(sub < SUBLANES - s, pltpu.roll(p, SUBLANES - s, 1), 0.0)

    a_mat = jnp.zeros((C, C), F32)
    q16, k16 = q.astype(BF16), k.astype(BF16)

    def add_level(a_mat, e, level):
        e16 = e.astype(BF16)
        return jnp.where(lvl == level, _dot_nt(q16 * e16, k16 * e16), a_mat)

    def row(r):
        return jnp.broadcast_to(p[:, r:r + 1, :], p.shape)

    odd = (sub & 1) == 1
    if fwd:
        bnd = (jnp.where(odd, pltpu.roll(p, 1, 1), p), jnp.where(sub < 4, row(1), row(5)), row(3))
    else:
        bnd = (jnp.where(odd, p, pltpu.roll(p, SUBLANES - 1, 1)), jnp.where(sub < 4, row(2), row(6)), row(4))
    for level, b in enumerate(bnd):
        a_mat = add_level(a_mat, jnp.exp2(-jnp.abs(p - b)).reshape(C, LANES), level)

    p = p.reshape(C, LANES)
    m = SUBLANES
    while m < C:
        nb = C // (2 * m)
        p4 = p.reshape(nb, 2, m, LANES)
        lo, hi = p4[:, 0], p4[:, 1]
        if fwd:
            tot = lo[:, m - 1:m]
            e_lo, e_hi = tot - lo, hi
            p_lo, p_hi = lo, hi + tot
        else:
            tot = hi[:, 0:1]
            e_lo, e_hi = lo, tot - hi
            p_lo, p_hi = lo + tot, hi
        e = jnp.exp2(jnp.concatenate([e_lo[:, None], e_hi[:, None]], axis=1).reshape(C, LANES))
        p = jnp.concatenate([p_lo[:, None], p_hi[:, None]], axis=1).reshape(C, LANES)
        a_mat = add_level(a_mat, e, int(np.log2(m)))
        m *= 2
    v16 = v.astype(BF16)
    o = jnp.sum(q * k, axis=-1, keepdims=True) * v + _dot(a_mat.astype(BF16), v16)

    if st is not None:
        o = o + _dot_nt((q * jnp.exp2(p)).astype(BF16), st.astype(BF16))
    tot = p[C - 1:C] if fwd else p[0:1]
    k_end = (k * jnp.exp2(tot - p)).astype(BF16)
    st_new = _dot_tn(v16, k_end)
    if st is not None:
        st_new = st_new + st * jnp.exp2(tot)
    return o, st_new


def _block_prefix(lf2, sub, fwd):
    p = lf2
    for s in (1, 2, 4):
        if fwd:
            p = p + jnp.where(sub >= s, pltpu.roll(p, s, 1), 0.0)
        else:
            p = p + jnp.where(sub < SUBLANES - s, pltpu.roll(p, SUBLANES - s, 1), 0.0)
    return p


def _hgrn_chunk_bidir(q, v, fwd_kl, bwd_kl, lvl):
    C = HGRN_CHUNK
    nv = C // SUBLANES
    sub = lax.broadcasted_iota(jnp.int32, (nv, SUBLANES, LANES), 1)
    (k_f, lf_f), (k_b, lf_b) = fwd_kl, bwd_kl
    p_f = _block_prefix(lf_f.reshape(nv, SUBLANES, LANES), sub, True)
    p_b = _block_prefix(lf_b.reshape(nv, SUBLANES, LANES), sub, False)
    q16, kf16, kb16 = q.astype(BF16), k_f.astype(BF16), k_b.astype(BF16)
    a_mat = jnp.zeros((C, C), F32)

    def add_level(a_mat, eq_f, ek_f, eq_b, ek_b, level):
        lhs = jnp.concatenate([q16 * eq_f.astype(BF16), q16 * eq_b.astype(BF16)], axis=1)
        rhs = jnp.concatenate([kf16 * ek_f.astype(BF16), kb16 * ek_b.astype(BF16)], axis=1)
        return jnp.where(lvl == level, _dot_nt(lhs, rhs), a_mat)

    def row(p, r):
        return jnp.broadcast_to(p[:, r:r + 1, :], p.shape)

    odd = (sub & 1) == 1
    bnd_f = (jnp.where(odd, pltpu.roll(p_f, 1, 1), p_f), jnp.where(sub < 4, row(p_f, 1), row(p_f, 5)), row(p_f, 3))
    bnd_b = (jnp.where(odd, p_b, pltpu.roll(p_b, SUBLANES - 1, 1)), jnp.where(sub < 4, row(p_b, 2), row(p_b, 6)),
             row(p_b, 4))
    for level in range(3):
        upper = (sub & (1 << level)) != 0
        e_f = jnp.exp2(-jnp.abs(p_f - bnd_f[level]))
        e_b = jnp.exp2(-jnp.abs(p_b - bnd_b[level]))
        a_mat = add_level(a_mat, *(jnp.where(m, e, 0.0).reshape(C, LANES) for m, e in
                                   ((upper, e_f), (~upper, e_f), (~upper, e_b), (upper, e_b))), level)

    p_f, p_b = p_f.reshape(C, LANES), p_b.reshape(C, LANES)
    m = SUBLANES
    while m < C:
        nb = C // (2 * m)

        def halves(lo, hi):
            return jnp.concatenate([lo[:, None], hi[:, None]], axis=1).reshape(C, LANES)

        f4, b4 = p_f.reshape(nb, 2, m, LANES), p_b.reshape(nb, 2, m, LANES)
        zero = jnp.zeros((nb, m, LANES), F32)
        tot_f = f4[:, 0, m - 1:m]
        tot_b = b4[:, 1, 0:1]
        a_mat = add_level(a_mat,
                          halves(zero, jnp.exp2(f4[:, 1])), halves(jnp.exp2(tot_f - f4[:, 0]), zero),
                          halves(jnp.exp2(b4[:, 0]), zero), halves(zero, jnp.exp2(tot_b - b4[:, 1])),
                          int(np.log2(m)))
        p_f = halves(f4[:, 0], f4[:, 1] + tot_f)
        p_b = halves(b4[:, 0] + tot_b, b4[:, 1])
        m *= 2
    v16 = v.astype(BF16)
    o = jnp.sum(q * (k_f + k_b), axis=-1, keepdims=True) * v + _dot(a_mat.astype(BF16), v16)
    st_f = _dot_tn(v16, (k_f * jnp.exp2(p_f[C - 1:C] - p_f)).astype(BF16))
    st_b = _dot_tn(v16, (k_b * jnp.exp2(p_b[0:1] - p_b)).astype(BF16))
    return o, (st_f, st_b)


def _hgrn_finish(o, gn, g_pre):
    return (_rms(o, gn) * _silu(g_pre)).astype(BF16)


def _hgrn_lat_kernel(lbf_ref, lbb_ref, gn_ref, q_ref, ff_ref, fb_ref, v_ref, g_ref, s0_ref, o_ref,
                     of_ref, ob_ref, st_ref, *, n_chunks):
    C = HGRN_CHUNK
    for d in range(2):
        st_ref[d] = s0_ref[0, 0, d, 0].T
    lbf = lbf_ref[...]
    lbb = lbb_ref[...]
    lvls = _hgrn_levels()

    def body(c, carry):
        rf = pl.multiple_of(c * C, C)
        rb = pl.multiple_of((n_chunks - 1 - c) * C, C)
        for d, r0, f_ref, lb, out_ref in ((0, rf, ff_ref, lbf, of_ref), (1, rb, fb_ref, lbb, ob_ref)):
            sl = pl.ds(r0, C)
            k, lf2 = _hgrn_forget(f_ref[sl, :], lb)
            out_ref[sl, :], st_ref[d] = _hgrn_chunk(_silu(q_ref[sl, :]), k, lf2, v_ref[sl, :], st_ref[d], d == 0,
                                                    lvls[d])
        return carry

    lax.fori_loop(0, n_chunks, body, 0, unroll=HGRN_UNROLL)
    o_ref[...] = _hgrn_finish(of_ref[...] + ob_ref[...], gn_ref[...], g_ref[...])


def _hgrn_lat(p, lp, layer, state_in):
    tok_spec = pl.BlockSpec((DEC_SEQ, LANES), lambda b, h: (b, h))
    vec_spec = pl.BlockSpec((None, 1, LANES), lambda b, h: (layer, 0, h))
    state_spec = pl.BlockSpec((1, 1, 2, 1, LANES, LANES), lambda b, h: (b, layer, 0, h, 0, 0))
    return pl.pallas_call(
        functools.partial(_hgrn_lat_kernel, n_chunks=DEC_SEQ // HGRN_CHUNK),
        out_shape=jax.ShapeDtypeStruct((N_LAT, 512), BF16), grid=(DEC_BATCH, HGRN_HEADS),
        in_specs=[vec_spec] * 3 + [tok_spec] * 5 + [state_spec], out_specs=tok_spec,
        scratch_shapes=[pltpu.VMEM((DEC_SEQ, LANES), F32), pltpu.VMEM((DEC_SEQ, LANES), F32),
                        pltpu.VMEM((2, LANES, LANES), F32)],
        compiler_params=_params(2), name="hgrn_lat",
    )(lp["lbf"], lp["lbb"], lp["gn"], p["hq"], p["hff"], p["hfb"], p["hi"], p["hg"], state_in)


def _lane_lo(shape):
    return (lax.broadcasted_iota(jnp.int32, shape, len(shape) - 1) & (LANES - 1)) < HEAD_DIM


def _softmax_unit(make_segs, sink_col, finish):
    segs = make_segs()
    mx = None
    for s, _ in segs:
        m = jnp.max(s, axis=-1, keepdims=True)
        mx = m if mx is None else jnp.maximum(mx, m)
    if sink_col is not None:
        mx = jnp.maximum(mx, sink_col)
    yield
    acc = None
    for s, v in segs:
        v_ones = jnp.concatenate([v, jnp.ones(v.shape, BF16)], axis=1)
        pv = _dot(jnp.exp2(s - mx).astype(BF16), v_ones)
        acc = pv if acc is None else acc + pv
    den = acc[:, LANES:]
    if sink_col is not None:
        den = den + jnp.exp2(sink_col - mx)
    finish(acc[:, :LANES] / den)


_DONE = object()


def _run_units(units, lookahead=1):
    pending = list(units)
    active = []
    for _ in range(lookahead - 1):
        if pending:
            u = pending.pop(0)
            next(u)
            active.insert(0, u)
    while pending or active:
        started = []
        if pending:
            u = pending.pop(0)
            next(u)
            started.append(u)
        n_adv = max(len(active) - (lookahead - 1), 1) if pending else len(active)
        keep = active[:len(active) - n_adv]
        active = started + keep + [a for a in active[len(active) - n_adv:] if next(a, _DONE) is not _DONE]


def _gqa_unit(q_ref, g, tq, segs_fn, o_ref, sink_ref=None):
    lo = _lane_lo((tq, LANES))
    parts = []
    for j in range(2):
        qp = q_ref[:, 256 * g + 128 * j:256 * g + 128 * (j + 1)]
        zero = jnp.zeros_like(qp)
        parts += [jnp.where(lo, qp, zero), jnp.where(lo, zero, qp)]
    q4 = jnp.concatenate(parts, axis=0)
    sink_col = None
    if sink_ref is not None:
        sink_col = jnp.concatenate(
            [jnp.broadcast_to(sink_ref[0:1, 4 * g + i:4 * g + i + 1] * LOG2E, (tq, 1)) for i in range(4)], axis=0)

    def finish(pv):
        o_ref[:, 256 * g:256 * g + 128] = jnp.where(lo, pv[0:tq], pv[tq:2 * tq]).astype(BF16)
        o_ref[:, 256 * g + 128:256 * (g + 1)] = jnp.where(lo, pv[2 * tq:3 * tq], pv[3 * tq:4 * tq]).astype(BF16)

    yield from _softmax_unit(lambda: segs_fn(q4), sink_col, finish)


def _mla_pair_unit(q_ref, hp, tq, segs_fn, o_ref):
    outs = []
    subs = [_softmax_unit(functools.partial(segs_fn, h, q_ref[:, 128 * h:128 * (h + 1)]), None, outs.append)
            for h in (2 * hp, 2 * hp + 1)]
    while True:
        if [next(s, _DONE) for s in subs][0] is _DONE:
            break
        yield
    o_ref[:, 128 * hp:128 * (hp + 1)] = jnp.where(_lane_lo((tq, LANES)), outs[0], outs[1]).astype(BF16)


_ATTN_IN = ("sq", "sk", "sv", "mq", "mk", "mv", "gq", "gk", "gv")
_ATTN_WIDTH = dict(sq=512, sk=256, sv=256, mq=1024, mk=1024, mv=1024, gq=512, gk=256, gv=256)


def _attn_ctx_kernel(sink_ref, sq_ref, sk_ref, sv_ref, mq_ref, mk_ref, mv_ref, gq_ref, gk_ref, gv_ref,
                     ob_ref, oc_ref, od_ref):
    tq = SEQ
    units = []
    for g in range(N_KV):
        def swa_segs(q4, g=g):
            return [(_dot_nt(q4, sk_ref[:, 128 * g:128 * (g + 1)]), sv_ref[:, 128 * g:128 * (g + 1)])]

        def gqa_segs(q4, g=g):
            return [(_dot_nt(q4, gk_ref[:, 128 * g:128 * (g + 1)]), gv_ref[:, 128 * g:128 * (g + 1)])]

        units += [_gqa_unit(sq_ref, g, tq, swa_segs, ob_ref, sink_ref), _gqa_unit(gq_ref, g, tq, gqa_segs, od_ref)]

    def mla_segs(h, q_h):
        return [(_dot_nt(q_h, mk_ref[:, 128 * h:128 * (h + 1)]), mv_ref[:, 128 * h:128 * (h + 1)])]

    mla = [_mla_pair_unit(mq_ref, hp, tq, mla_segs, oc_ref) for hp in range(MLA_HEADS // 2)]
    _run_units([u for pair in zip(units, mla) for u in pair])


def _attn_ctx(lp, p, layer):
    def spec(w):
        return pl.BlockSpec((SEQ, w), lambda b: (b, 0))

    return pl.pallas_call(
        _attn_ctx_kernel,
        out_shape=[jax.ShapeDtypeStruct((N_CTX, 512), BF16)] * 3,
        grid=(BATCH,),
        in_specs=[_layer_spec((1, N_HEADS), layer)] + [spec(_ATTN_WIDTH[n]) for n in _ATTN_IN],
        out_specs=[spec(512)] * 3,
        compiler_params=_params(1),
        name="attn_ctx",
    )(lp["sink"], *[p[n] for n in _ATTN_IN])


def _dup_heads(x):
    lo = _lane_lo(x.shape)
    r = pltpu.roll(x, HEAD_DIM, 1)
    return jnp.where(lo, x, r), jnp.where(lo, r, x)


def _dup_tiles(x):
    return jnp.where(_lane_lo(x.shape), x, pltpu.roll(x, HEAD_DIM, 1))


def _attn_lat_kernel(sink_ref, wk_ref, wuv_ref,
                     sq_ref, sk_ref, sv_ref, mq_ref, mk_ref, mv_ref, gq_ref, gk_ref, gv_ref,
                     csk_ref, csv_ref, cckv_ref, ckr_ref, cgk_ref, cgv_ref,
                     ob_ref, oc_ref, od_ref,
                     cs_k, cs_v, cm_k, cm_v, cg_k, cg_v):
    tq = TQ_LAT
    qi = pl.program_id(1)

    @pl.when(qi == 0)
    def _():
        for src, dst in ((csk_ref, cs_k), (csv_ref, cs_v), (cgk_ref, cg_k), (cgv_ref, cg_v)):
            d0, d1 = _dup_heads(src[0, 0])
            dst[0] = d0.astype(BF16)
            dst[1] = d1.astype(BF16)
        ckv16 = cckv_ref[0, 0].astype(BF16)
        k = _dot(ckv16, wk_ref[0:MLA_KV_RANK, :]) + _dot(ckr_ref[0, 0].astype(BF16),
                                                           wk_ref[MLA_KV_RANK:MLA_KV_RANK + MLA_ROPE, :])
        cm_k[...] = k.astype(BF16)
        cm_v[...] = _dup_tiles(_dot(ckv16, wuv_ref[...])).astype(BF16)

    q0 = qi * tq
    win = tq + 2 * SWA_WINDOW
    start = pl.multiple_of(jnp.clip(q0 - SWA_WINDOW, 0, DEC_SEQ - win), SWA_WINDOW)
    rows = lax.broadcasted_iota(jnp.int32, (4 * tq, win), 0)
    qpos = q0 + (rows & (tq - 1))
    kpos = start + lax.broadcasted_iota(jnp.int32, (4 * tq, win), 1)
    band = jnp.abs(qpos - kpos) <= SWA_WINDOW
    units = []
    for g in range(N_KV):
        def swa_segs(q4, g=g):
            ks = sk_ref[pl.ds(start, win), 128 * g:128 * (g + 1)]
            vs = sv_ref[pl.ds(start, win), 128 * g:128 * (g + 1)]
            s_own = jnp.where(band, _dot_nt(q4, ks), MASK_VALUE)
            return [(_dot_nt(q4, cs_k[g]), cs_v[g]), (s_own, vs)]

        def gqa_segs(q4, g=g):
            return [(_dot_nt(q4, cg_k[g]), cg_v[g]),
                    (_dot_nt(q4, gk_ref[:, 128 * g:128 * (g + 1)]), gv_ref[:, 128 * g:128 * (g + 1)])]

        units += [_gqa_unit(sq_ref, g, tq, swa_segs, ob_ref, sink_ref), _gqa_unit(gq_ref, g, tq, gqa_segs, od_ref)]

    def mla_segs(h, q_h):
        cols = slice(128 * h, 128 * (h + 1))
        return [(_dot_nt(q_h, cm_k[:, cols]), cm_v[:, cols]), (_dot_nt(q_h, mk_ref[:, cols]), mv_ref[:, cols])]

    mla = [_mla_pair_unit(mq_ref, hp, tq, mla_segs, oc_ref) for hp in range(MLA_HEADS // 2)]
    _run_units([u for pair in zip(units, mla) for u in pair], lookahead=3)


def _attn_lat(lp, p, caches, layer):
    nq = DEC_SEQ // TQ_LAT

    def qspec(w):
        return pl.BlockSpec((TQ_LAT, w), lambda b, i: (b * nq + i, 0))

    def kspec(w):
        return pl.BlockSpec((DEC_SEQ, w), lambda b, i: (b, 0))

    def cspec(w):
        return pl.BlockSpec((1, 1, PAST_LEN, w), lambda b, i: (b, layer, 0, 0))

    is_q = dict(sq=True, mq=True, gq=True)
    in_specs = [_layer_spec((1, N_HEADS), layer), _layer_spec((256, 1024), layer),
                _layer_spec((MLA_KV_RANK, 1024), layer)]
    in_specs += [(qspec if is_q.get(n) else kspec)(_ATTN_WIDTH[n]) for n in _ATTN_IN]
    in_specs += [cspec(128), cspec(128), cspec(128), cspec(MLA_ROPE), cspec(128), cspec(128)]
    return pl.pallas_call(
        _attn_lat_kernel,
        out_shape=[jax.ShapeDtypeStruct((N_LAT, 512), BF16)] * 3,
        grid=(DEC_BATCH, nq),
        in_specs=in_specs,
        out_specs=[qspec(512)] * 3,
        scratch_shapes=[pltpu.VMEM((N_KV, PAST_LEN, LANES), BF16), pltpu.VMEM((N_KV, PAST_LEN, LANES), BF16),
                        pltpu.VMEM((PAST_LEN, 1024), BF16), pltpu.VMEM((PAST_LEN, 1024), BF16),
                        pltpu.VMEM((N_KV, PAST_LEN, LANES), BF16), pltpu.VMEM((N_KV, PAST_LEN, LANES), BF16)],
        compiler_params=_params(2),
        name="attn_lat",
    )(lp["sink"], lp["wk"], lp["wuv"], *[p[n] for n in _ATTN_IN], *caches)


MERGE_TILE = 512
MERGE_SUB = 256


def _merge_kernel(x_ref, mod_ref, oa_ref, ob_ref, oc_ref, od_ref, gates_ref, wb_ref, wo_ref, npost_ref,
                  nmpre_ref, nmpost_ref, w1_ref, w2_ref, y_ref):
    g1 = mod_ref[0, 2]
    sh2 = mod_ref[0, 3]
    sc2 = mod_ref[0, 4]
    g2 = mod_ref[0, 5]

    def sub_tile(r0):
        rows = slice(r0, r0 + MERGE_SUB)
        acc = None
        for n, o_ref in enumerate((oa_ref, ob_ref, oc_ref, od_ref)):
            br = _dot(o_ref[rows, :], wb_ref[n]) * gates_ref[rows, D_MODEL * n:D_MODEL * (n + 1)].astype(F32)
            acc = br if acc is None else acc + br
        y = _dot(acc.astype(BF16), wo_ref[...])
        yield
        x = x_ref[rows, :] + g1 * _rms(y, npost_ref[...])
        h = (_rms(x, nmpre_ref[...]) * (1.0 + sc2) + sh2).astype(BF16)
        a = jnp.maximum(_dot(h, w1_ref[...]), 0.0)
        yield
        y = _dot((a * a).astype(BF16), w2_ref[...])
        y_ref[rows, :] = x + g2 * _rms(y, nmpost_ref[...])

    _run_units([sub_tile(r0) for r0 in range(0, MERGE_TILE, MERGE_SUB)])


_MERGE_PARAMS = dict(w_branch=(N_BRANCH, BRANCH_W, D_MODEL), w_out=(D_MODEL, D_MODEL), npost=(1, D_MODEL),
                     nmpre=(1, D_MODEL), nmpost=(1, D_MODEL), w_mlp_in=(D_MODEL, D_FF), w_mlp_out=(D_FF, D_MODEL))


def _merge(x, mod4, o_a, o_b, o_c, o_d, gates, lp, is_ctx, layer):
    n_tok = x.shape[0]

    def tok(w):
        return pl.BlockSpec((MERGE_TILE, w), lambda i: (i, 0))

    in_specs = [tok(D_MODEL), _mod_spec(is_ctx, layer, MERGE_TILE), tok(512), tok(512), tok(512), tok(512), tok(4096)]
    in_specs += [_layer_spec(shape, layer) for shape in _MERGE_PARAMS.values()]
    return pl.pallas_call(
        _merge_kernel, out_shape=jax.ShapeDtypeStruct((n_tok, D_MODEL), F32), grid=(n_tok // MERGE_TILE,),
        in_specs=in_specs, out_specs=tok(D_MODEL), compiler_params=_params(1),
        name="merge_ctx" if is_ctx else "merge_lat",
    )(x, mod4, o_a, o_b, o_c, o_d, gates, *[lp[n] for n in _MERGE_PARAMS])


def _rope_tables():
    t = np.arange(DEC_SEQ)
    row, col = (t // GRID_W).astype(np.float32), (t % GRID_W).astype(np.float32)

    def pattern(d):
        nf = d // 4
        inv = (ROPE_BASE ** (-np.arange(nf, dtype=np.float32) / nf)).astype(np.float32)
        ang = np.concatenate([row[:, None] * inv, row[:, None] * inv, col[:, None] * inv, col[:, None] * inv], axis=1)
        cos, sin = np.cos(ang), np.sin(ang)
        second = (np.arange(d) % (2 * nf)) >= nf
        return cos, np.where(second, sin, 0.0), np.where(second, 0.0, -sin)

    out = np.zeros((9, DEC_SEQ, LANES), np.float32)
    out[[0, 3, 6]] = 1.0
    c, s1, s2 = pattern(HEAD_DIM)
    for a, tab in enumerate((c, s1, s2)):
        out[_TAB_64 + a] = np.tile(tab, (1, LANES // HEAD_DIM))
    c, s1, s2 = pattern(MLA_ROPE)
    for a, tab in enumerate((c, s1, s2)):
        out[_TAB_MLAQ + a][:, MLA_NOPE:MLA_NOPE + MLA_ROPE] = tab
        out[_TAB_KR + a][:, :MLA_ROPE] = tab
    return out


_ROPE_TABLES = _rope_tables()


def _block_mean_matrix(n):
    i = np.arange(n)
    return ((i[:, None] // HEAD_DIM) == (i[None, :] // HEAD_DIM)).astype(np.float32) / HEAD_DIM


def _mla_weight_layouts(w_uq, w_ukv):
    dq = MLA_NOPE + MLA_ROPE
    wq = w_uq.reshape(DEPTH, MLA_Q_RANK, MLA_HEADS, dq)
    wq = jnp.pad(wq, ((0, 0), (0, 0), (0, 0), (0, LANES - dq))).reshape(DEPTH, MLA_Q_RANK, MLA_HEADS * LANES)
    wkv = w_ukv.reshape(DEPTH, MLA_KV_RANK, MLA_HEADS, MLA_NOPE + MLA_V)
    wk = jnp.pad(wkv[..., :MLA_NOPE], ((0, 0), (0, 0), (0, 0), (0, LANES - MLA_NOPE)))
    wk = wk.reshape(DEPTH, MLA_KV_RANK, MLA_HEADS * LANES)
    place = np.zeros((LANES, MLA_HEADS, LANES), np.float32)
    for j in range(MLA_ROPE):
        place[j, :, MLA_NOPE + j] = 1.0
    place = jnp.broadcast_to(jnp.asarray(place.reshape(1, LANES, MLA_HEADS * LANES)), (DEPTH, LANES, MLA_HEADS * LANES))
    w_k = jnp.concatenate([wk, place], axis=1)
    w_uv = jnp.pad(wkv[..., MLA_NOPE:], ((0, 0), (0, 0), (0, 0), (0, LANES - MLA_V)))
    w_uv = w_uv.reshape(DEPTH, MLA_KV_RANK, MLA_HEADS * LANES)
    return wq.astype(BF16), w_k.astype(BF16), w_uv.astype(BF16)


def _lower_bounds(p):
    s = jax.nn.softmax(p.astype(F32), axis=0)
    return jnp.cumsum(s, axis=0) - s[0]


def kernel(x_prompt, x_sample, state_hgrn, cache_swa_k, cache_swa_v, cache_mla_ckv, cache_mla_kr, cache_gqa_k, cache_gqa_v, c, c_ctx, w_ada, b_ada, norm_mix_pre, norm_mix_post, norm_mlp_pre, norm_mlp_post, w_in, hgrn_lb_fwd, hgrn_lb_bwd, hgrn_norm, swa_sink, mla_q_norm, mla_kv_norm, mla_w_uq, mla_w_ukv, gqa_q_norm, gqa_k_norm, w_branch, w_out, w_mlp_in, w_mlp_out):
    tabs = jnp.asarray(_ROPE_TABLES)
    b512 = jnp.asarray(_block_mean_matrix(512)).astype(BF16)

    def rows(a):
        return a.reshape(DEPTH, 1, -1)

    w_uq, w_k, w_uv = _mla_weight_layouts(mla_w_uq, mla_w_ukv)
    w_a, w_kr, w_b = _split_w_in(w_in)
    lp = dict(
        npre=rows(norm_mix_pre), wa=w_a, wkr=w_kr, wb=w_b, mlaq_g=rows(mla_q_norm), mlakv_g=rows(mla_kv_norm),
        wuq=w_uq, wk=w_k, wuv=w_uv, gq_g=rows(jnp.tile(gqa_q_norm, (1, N_HEADS))),
        gk_g=rows(jnp.tile(gqa_k_norm, (1, N_KV))), b512=b512, b128=b512[:128, :128],
        lbf=rows(_lower_bounds(hgrn_lb_fwd)), lbb=rows(_lower_bounds(hgrn_lb_bwd)), gn=rows(hgrn_norm),
        sink=rows(swa_sink),
        w_branch=w_branch.astype(BF16), w_out=w_out.astype(BF16), npost=rows(norm_mix_post),
        nmpre=rows(norm_mlp_pre), nmpost=rows(norm_mlp_post),
        w_mlp_in=w_mlp_in.astype(BF16), w_mlp_out=w_mlp_out.astype(BF16))

    cond8 = jnp.zeros((SUBLANES, D_MODEL), F32).at[0].set(c_ctx).at[1:1 + DEC_BATCH].set(c)
    mod4 = _ada(cond8, w_ada, b_ada).reshape(DEPTH, SUBLANES, N_MOD, 1, D_MODEL)

    caches_in = (cache_swa_k.reshape(DEC_BATCH, DEPTH, PAST_LEN, 128),
                 cache_swa_v.reshape(DEC_BATCH, DEPTH, PAST_LEN, 128), cache_mla_ckv, cache_mla_kr,
                 cache_gqa_k.reshape(DEC_BATCH, DEPTH, PAST_LEN, 128),
                 cache_gqa_v.reshape(DEC_BATCH, DEPTH, PAST_LEN, 128))

    x_ctx = x_prompt.reshape(N_CTX, D_MODEL)
    x_lat = x_sample.reshape(N_LAT, D_MODEL)
    pc = None
    for l in range(DEPTH):
        pc = _proj(x_ctx, mod4, lp, True, layer=l, prev=pc)
        ob, oc, od = _attn_ctx(lp, pc, l)
        x_ctx = _merge(x_ctx, mod4, pc["oa"], ob, oc, od, pc["gates"], lp, True, l)

        pq = _proj(x_lat, mod4, lp, False, layer=l, tabs=tabs)
        oa = _hgrn_lat(pq, lp, l, state_hgrn)
        ob, oc, od = _attn_lat(lp, pq, caches_in, l)
        x_lat = _merge(x_lat, mod4, oa, ob, oc, od, pq["gates"], lp, False, l)

    kv_shape = (BATCH, DEPTH, SEQ, N_KV, HEAD_DIM)
    return (x_ctx.reshape(BATCH, SEQ, D_MODEL), x_lat.reshape(DEC_BATCH, DEC_SEQ, D_MODEL), pc["st"],
            pc["skc"].reshape(kv_shape), pc["svc"].reshape(kv_shape), pc["mckv"], pc["mkr"],
            pc["gkc"].reshape(kv_shape), pc["gvc"].reshape(kv_shape))
```
